```python
import jax
import jax.numpy as jnp
from jax import lax
import numpy as np

D_MODEL = 1024
BATCH = 4
SEQ = 4096
DEPTH = 1

GRID_W = 64
CTX_LEN = 256
HEAD_DIM = 64
ATTN_WIDTH = D_MODEL // 2
N_Q_HEADS = ATTN_WIDTH // HEAD_DIM
N_KV_HEADS = N_Q_HEADS // 4
Q_PER_KV = N_Q_HEADS // N_KV_HEADS
KV_WIDTH = N_KV_HEADS * HEAD_DIM
WINDOW = 128
BLOCK = 128
ROPE_BASE = 10000.0
RWKV_WIDTH = D_MODEL // 2
RWKV_HEAD_DIM = 64
RWKV_HEADS = RWKV_WIDTH // RWKV_HEAD_DIM
DECAY_LORA = 32
ICLR_LORA = 32
GATE_LORA = 96
RWKV_IN_WIDTH = 3 * RWKV_WIDTH + DECAY_LORA + ICLR_LORA + GATE_LORA
SHORT_CONV = 3
N_BRANCHES = 2
IN_WIDTH = ATTN_WIDTH + 2 * KV_WIDTH + RWKV_IN_WIDTH + N_BRANCHES * D_MODEL
D_FF = 4 * D_MODEL
N_MOD = 6
NORM_EPS = 1e-6
LNX_EPS = 1e-5 * RWKV_HEAD_DIM
IN_SPLITS = [ATTN_WIDTH, ATTN_WIDTH + KV_WIDTH, ATTN_WIDTH + 2 * KV_WIDTH,
             ATTN_WIDTH + 2 * KV_WIDTH + RWKV_IN_WIDTH]
RW_SPLITS = [RWKV_WIDTH, 2 * RWKV_WIDTH, 3 * RWKV_WIDTH,
             3 * RWKV_WIDTH + DECAY_LORA, 3 * RWKV_WIDTH + DECAY_LORA + ICLR_LORA]

kernel_name = "hybrid_gqa_rwkv7_dit_block"


def rmsnorm(x, g):
    xf = x.astype(jnp.float32)
    y = xf * lax.rsqrt(jnp.mean(xf * xf, axis=-1, keepdims=True) + NORM_EPS)
    return (y * g.astype(jnp.float32)).astype(x.dtype)


def modulate(h, shift, scale):
    return h * (1 + scale) + shift


def axial_rope_tables(rows):
    n_freq = HEAD_DIM // 4
    inv_freq = jnp.power(ROPE_BASE, -jnp.arange(n_freq, dtype=jnp.float32) / n_freq)
    row = jnp.repeat(jnp.arange(rows, dtype=jnp.float32), GRID_W)
    col = jnp.tile(jnp.arange(GRID_W, dtype=jnp.float32), rows)
    ang = jnp.concatenate([row[:, None] * inv_freq, col[:, None] * inv_freq], axis=-1)
    return jnp.cos(ang), jnp.sin(ang)


def apply_rope(x, cos, sin):
    half = x.shape[-1] // 2
    x1, x2 = x[..., :half], x[..., half:]
    c = cos[:, None, :].astype(x.dtype)
    s = sin[:, None, :].astype(x.dtype)
    return jnp.concatenate([x1 * c - x2 * s, x1 * s + x2 * c], axis=-1)


def sink_column(sink, lead_shape):
    s = sink.astype(jnp.float32).reshape(N_KV_HEADS, Q_PER_KV)
    return jnp.broadcast_to(s[:, :, None, None], lead_shape + (1,))


def latent_window_attention(q, k, v, kc, vc, sink):
    B, L, _, dh = q.shape
    Lc = kc.shape[1]
    nb = L // BLOCK
    scale = dh ** -0.5
    qb = q.reshape(B, nb, BLOCK, N_KV_HEADS, Q_PER_KV, dh)
    pad = ((0, 0), (BLOCK, BLOCK), (0, 0), (0, 0))

    def band(t):
        tb = jnp.pad(t, pad).reshape(B, nb + 2, BLOCK, N_KV_HEADS, dh)
        return jnp.concatenate([tb[:, :-2], tb[:, 1:-1], tb[:, 2:]], axis=2)

    kb, vb = band(k), band(v)
    s_loc = jnp.einsum('bnqgrd,bnkgd->bngrqk', qb, kb).astype(jnp.float32) * scale
    s_ctx = jnp.einsum('bnqgrd,bcgd->bngrqc', qb, kc).astype(jnp.float32) * scale
    blk = jnp.arange(nb)[:, None, None]
    qi = jnp.arange(BLOCK)[None, :, None]
    kj = jnp.arange(3 * BLOCK)[None, None, :]
    i_pos = blk * BLOCK + qi
    j_pos = blk * BLOCK - BLOCK + kj
    valid = (jnp.abs(i_pos - j_pos) <= WINDOW) & (j_pos >= 0) & (j_pos < L)
    s_loc = jnp.where(valid[None, :, None, None], s_loc, -jnp.inf)
    s_sink = jnp.broadcast_to(sink.astype(jnp.float32).reshape(N_KV_HEADS, Q_PER_KV)[None, None, :, :, None, None],
                              s_loc.shape[:-1] + (1,))
    p = jax.nn.softmax(jnp.concatenate([s_loc, s_ctx, s_sink], axis=-1), axis=-1)
    p_loc = p[..., :3 * BLOCK].astype(v.dtype)
    p_ctx = p[..., 3 * BLOCK:3 * BLOCK + Lc].astype(v.dtype)
    o = (jnp.einsum('bngrqk,bnkgd->bnqgrd', p_loc, vb)
         + jnp.einsum('bngrqc,bcgd->bnqgrd', p_ctx, vc))
    return o.reshape(B, L, N_Q_HEADS * dh)


def context_attention(qc, kc, vc, sink):
    B, Lc, _, dh = qc.shape
    qg = qc.reshape(B, Lc, N_KV_HEADS, Q_PER_KV, dh)
    s = jnp.einsum('bqgrd,bkgd->bgrqk', qg, kc).astype(jnp.float32) * dh ** -0.5
    s_sink = sink_column(sink, s.shape[1:-1])[None]
    s_sink = jnp.broadcast_to(s_sink, s.shape[:-1] + (1,))
    p = jax.nn.softmax(jnp.concatenate([s, s_sink], axis=-1), axis=-1)
    o = jnp.einsum('bgrqk,bkgd->bqgrd', p[..., :Lc].astype(vc.dtype), vc)
    return o.reshape(B, Lc, N_Q_HEADS * dh)


def short_conv(u, w):
    L = u.shape[1]
    half = SHORT_CONV // 2
    up = jnp.pad(u, ((0, 0), (half, half), (0, 0)))
    return sum(up[:, t:t + L] * w[t] for t in range(SHORT_CONV))


def rwkv_prepare(u, decay_w0, decay_w2, iclr_a0, iclr_a2, k_k, k_a):
    B, L, _ = u.shape
    r, k, v, hw, ha, hg = jnp.split(u, RW_SPLITS, axis=-1)

    def heads(t):
        return t.astype(jnp.float32).reshape(B, L, RWKV_HEADS, RWKV_HEAD_DIM)

    kk = heads(k * k_k)
    kk = kk * lax.rsqrt(jnp.maximum(jnp.sum(kk * kk, axis=-1, keepdims=True), 1e-24))
    dirs = []
    for d in range(2):
        w_log = -jax.nn.softplus(-(decay_w0[d] + jnp.tanh(hw) @ decay_w2[d])) - 0.5
        a = jax.nn.sigmoid(iclr_a0[d] + ha @ iclr_a2[d])
        decay = jnp.exp(-jnp.exp(heads(w_log)))
        k_d = heads(k * (1 + (a - 1) * k_a))
        dirs.append((decay, kk * heads(a), k_d))
    return heads(r), heads(v), hg, kk, dirs


def wkv7_scan(S0, r, decay, kk, b, k, v, reverse, emit):
    xs = tuple(jnp.moveaxis(t, 1, 0) for t in (r, decay, kk, b, k, v))

    def step(S, inp):
        r_t, w_t, kk_t, b_t, k_t, v_t = inp
        sa = jnp.einsum('bhvk,bhk->bhv', S, -kk_t)
        S = S * w_t[:, :, None, :] + sa[..., None] * b_t[:, :, None, :] + v_t[..., None] * k_t[:, :, None, :]
        y = jnp.einsum('bhvk,bhk->bhv', S, r_t) if emit else None
        return S, y

    S, ys = lax.scan(step, S0, xs, reverse=reverse)
    return S, (jnp.moveaxis(ys, 0, 1) if emit else None)


def rwkv_output(y, r, v, hg, dirs, gate_g2, r_k, lnx_w, lnx_b):
    B, L = y.shape[:2]
    mu = jnp.mean(y, axis=-1, keepdims=True)
    var = jnp.mean(jnp.square(y - mu), axis=-1, keepdims=True)
    yn = (y - mu) * lax.rsqrt(var + LNX_EPS)
    k_sum = dirs[0][2] + dirs[1][2]
    bonus = jnp.sum(r * k_sum * r_k.astype(jnp.float32), axis=-1, keepdims=True) * v
    out = (yn.reshape(B, L, RWKV_WIDTH) * lnx_w + lnx_b + bonus.reshape(B, L, RWKV_WIDTH)).astype(hg.dtype)
    g = jax.nn.sigmoid(hg) @ gate_g2
    return out * g


def merge_branches(ya, yr, gate_logits, w_branch_attn, w_branch_rwkv, w_out):
    ga, gr = jnp.split(jax.nn.sigmoid(gate_logits), N_BRANCHES, axis=-1)
    return (ga * (ya @ w_branch_attn) + gr * (yr @ w_branch_rwkv)) @ w_out


def sq_relu_mlp(h, w_up, w_down):
    return jnp.square(jax.nn.relu(h @ w_up)) @ w_down


def trunk_layer(x, xc, c, c_ctx, cos, sin, w_ada, b_ada, norm1_g, w_in, sink, conv_w,
                decay_w0, decay_w2, iclr_a0, iclr_a2, gate_g2, k_k, k_a, r_k, lnx_w, lnx_b,
                w_branch_attn, w_branch_rwkv, w_out, norm2_g, w_mlp_up, w_mlp_down, update_ctx):
    B, L, D = x.shape
    Lc = xc.shape[1]
    mod = (jax.nn.silu(c) @ w_ada + b_ada).reshape(B, 1, N_MOD, D)
    shift1, scale1, gate1, shift2, scale2, gate2 = [mod[:, :, i] for i in range(N_MOD)]
    modc = (jax.nn.silu(c_ctx) @ w_ada + b_ada).reshape(N_MOD, D)

    h = modulate(rmsnorm(x, norm1_g), shift1, scale1)
    hc = modulate(rmsnorm(xc, norm1_g), modc[0], modc[1])
    q, k, v, rw, gl = jnp.split(h @ w_in, IN_SPLITS, axis=-1)
    qc, kc, vc, rwc, glc = jnp.split(hc @ w_in, IN_SPLITS, axis=-1)

    q = apply_rope(q.reshape(B, L, N_Q_HEADS, HEAD_DIM), cos, sin)
    k = apply_rope(k.reshape(B, L, N_KV_HEADS, HEAD_DIM), cos, sin)
    v = v.reshape(B, L, N_KV_HEADS, HEAD_DIM)
    kc = kc.reshape(B, Lc, N_KV_HEADS, HEAD_DIM)
    vc = vc.reshape(B, Lc, N_KV_HEADS, HEAD_DIM)
    ya = latent_window_attention(q, k, v, kc, vc, sink)

    r_l, v_l, hg_l, kk_l, dirs_l = rwkv_prepare(short_conv(rw, conv_w), decay_w0, decay_w2,
                                               iclr_a0, iclr_a2, k_k, k_a)
    r_c, v_c, hg_c, kk_c, dirs_c = rwkv_prepare(short_conv(rwc, conv_w), decay_w0, decay_w2,
                                               iclr_a0, iclr_a2, k_k, k_a)
    S0 = jnp.zeros((B, RWKV_HEADS, RWKV_HEAD_DIM, RWKV_HEAD_DIM), jnp.float32)
    y_lat = []
    y_ctx = []
    for d, rev in enumerate((False, True)):
        dec_c, b_c, k_c = dirs_c[d]
        S_ctx, yc_d = wkv7_scan(S0, r_c, dec_c, kk_c, b_c, k_c, v_c, rev, update_ctx)
        dec_l, b_l, k_l = dirs_l[d]
        _, y_d = wkv7_scan(S_ctx, r_l, dec_l, kk_l, b_l, k_l, v_l, rev, True)
        y_lat.append(y_d)
        y_ctx.append(yc_d)
    yr = rwkv_output(y_lat[0] + y_lat[1], r_l, v_l, hg_l, dirs_l, gate_g2, r_k, lnx_w, lnx_b)

    x = x + gate1 * merge_branches(ya, yr, gl, w_branch_attn, w_branch_rwkv, w_out)
    x = x + gate2 * sq_relu_mlp(modulate(rmsnorm(x, norm2_g), shift2, scale2), w_mlp_up, w_mlp_down)

    if update_ctx:
        yac = context_attention(qc.reshape(B, Lc, N_Q_HEADS, HEAD_DIM), kc, vc, sink)
        yrc = rwkv_output(y_ctx[0] + y_ctx[1], r_c, v_c, hg_c, dirs_c, gate_g2, r_k, lnx_w, lnx_b)
        xc = xc + modc[2] * merge_branches(yac, yrc, glc, w_branch_attn, w_branch_rwkv, w_out)
        xc = xc + modc[5] * sq_relu_mlp(modulate(rmsnorm(xc, norm2_g), modc[3], modc[4]), w_mlp_up, w_mlp_down)
    return x, xc


def setup_inputs(seed: int = 0) -> dict:
    key = jax.random.key(seed)
    ks = iter(jax.random.split(key, 32))

    def nrm(shape, s):
        return jax.random.normal(next(ks), shape, jnp.float32) * s

    D = D_MODEL
    L = DEPTH
    conv_base = jnp.asarray([0.25, 1.0, 0.25], jnp.float32)[None, :, None]
    return {
        "x": nrm((BATCH, SEQ, D), 1.0),
        "c": nrm((BATCH, D), 1.0),
        "ctx": nrm((BATCH, CTX_LEN, D), 1.0),
        "c_ctx": nrm((D,), 1.0),
        "w_ada": nrm((L, D, N_MOD * D), 0.5 * D ** -0.5),
        "b_ada": nrm((L, N_MOD * D), 0.02),
        "norm1_g": 1.0 + nrm((L, D), 0.05),
        "w_in": nrm((L, D, IN_WIDTH), D ** -0.5),
        "sink": nrm((L, N_Q_HEADS), 0.5),
        "conv_w": conv_base + nrm((L, SHORT_CONV, RWKV_IN_WIDTH), 0.05),
        "decay_w0": nrm((L, 2, RWKV_WIDTH), 0.5),
        "decay_w2": nrm((L, 2, DECAY_LORA, RWKV_WIDTH), DECAY_LORA ** -0.5),
        "iclr_a0": nrm((L, 2, RWKV_WIDTH), 0.5),
        "iclr_a2": nrm((L, 2, ICLR_LORA, RWKV_WIDTH), ICLR_LORA ** -0.5),
        "gate_g2": nrm((L, GATE_LORA, RWKV_WIDTH), GATE_LORA ** -0.5),
        "k_k": 1.0 + nrm((L, RWKV_WIDTH), 0.1),
        "k_a": 1.0 + nrm((L, RWKV_WIDTH), 0.1),
        "r_k": nrm((L, RWKV_HEADS, RWKV_HEAD_DIM), 0.1),
        "lnx_w": 1.0 + nrm((L, RWKV_WIDTH), 0.05),
        "lnx_b": nrm((L, RWKV_WIDTH), 0.02),
        "w_branch_attn": nrm((L, ATTN_WIDTH, D), ATTN_WIDTH ** -0.5),
        "w_branch_rwkv": nrm((L, RWKV_WIDTH, D), RWKV_WIDTH ** -0.5),
        "w_out": nrm((L, D, D), D ** -0.5),
        "norm2_g": 1.0 + nrm((L, D), 0.05),
        "w_mlp_up": nrm((L, D, D_FF), D ** -0.5),
        "w_mlp_down": nrm((L, D_FF, D), D_FF ** -0.5),
        "norm_f_g": 1.0 + nrm((D,), 0.05),
    }


def reference(x, c, ctx, c_ctx, w_ada, b_ada, norm1_g, w_in, sink, conv_w, decay_w0, decay_w2,
              iclr_a0, iclr_a2, gate_g2, k_k, k_a, r_k, lnx_w, lnx_b, w_branch_attn, w_branch_rwkv,
              w_out, norm2_g, w_mlp_up, w_mlp_down, norm_f_g):
    rows = x.shape[1] // GRID_W
    cos, sin = axial_rope_tables(rows)
    xc = ctx
    for l in range(DEPTH):
        x, xc = trunk_layer(x, xc, c, c_ctx, cos, sin, w_ada[l], b_ada[l], norm1_g[l], w_in[l], sink[l],
                            conv_w[l], decay_w0[l], decay_w2[l], iclr_a0[l], iclr_a2[l], gate_g2[l],
                            k_k[l], k_a[l], r_k[l], lnx_w[l], lnx_b[l], w_branch_attn[l], w_branch_rwkv[l],
                            w_out[l], norm2_g[l], w_mlp_up[l], w_mlp_down[l], update_ctx=(l < DEPTH - 1))
    return rmsnorm(x, norm_f_g)
```

```python
import functools
import math

import jax
import jax.numpy as jnp
from jax import lax
from jax.experimental import pallas as pl
from jax.experimental.pallas import tpu as pltpu

F32 = jnp.float32
BF16 = jnp.bfloat16

GRID_W = 64
HEAD_DIM = 64
Q_PER_KV = 4
ATTN_BLOCK = 128
ROPE_BASE = 10000.0
NORM_EPS = 1e-6
LNX_EPS = 1e-5 * HEAD_DIM
DECAY_LORA, ICLR_LORA, GATE_LORA = 32, 32, 96
LORA_PAD = 256
CHUNK = 64
LANES = 128
PAIR = LANES // HEAD_DIM
NEG = -1e30
VMEM_LIMIT = 56 * 1024 * 1024


def _dot(a, b):
    return jnp.dot(a, b, preferred_element_type=F32)


def _dot_nt(a, b):
    return lax.dot_general(a, b, (((1,), (1,)), ((), ())), preferred_element_type=F32)


def _dot_tn(a, b):
    return lax.dot_general(a, b, (((0,), (0,)), ((), ())), preferred_element_type=F32)


def _iota(shape, dim):
    return lax.broadcasted_iota(jnp.int32, shape, dim)


def _head_ones(width):
    r = _iota((width, width), 0) // HEAD_DIM
    c = _iota((width, width), 1) // HEAD_DIM
    return jnp.where(r == c, 1.0, 0.0).astype(BF16)


def _head_sum(x, ones_bd):
    hi = x.astype(BF16)
    lo = (x - hi.astype(F32)).astype(BF16)
    return _dot(hi, ones_bd) + _dot(lo, ones_bd)


def _rmsnorm(x, g):
    ms = jnp.mean(x * x, axis=-1, keepdims=True)
    return x * lax.rsqrt(ms + NORM_EPS) * g


def _params(*sem):
    return pltpu.CompilerParams(dimension_semantics=sem, vmem_limit_bytes=VMEM_LIMIT)


def _const_spec(shape):
    nd = len(shape)
    return pl.BlockSpec(shape, lambda *_: (0,) * nd, pipeline_mode=pl.Buffered(1))


def _ada_kernel(c_ref, w_ref, b_ref, o_ref):
    c = c_ref[...]
    s = c * jax.nn.sigmoid(c)
    o_ref[...] = _dot(s.astype(BF16), w_ref[...].astype(BF16)) + b_ref[...]


def _ada_mod(cc, w_ada, b_ada):
    rows, d = cc.shape
    n = w_ada.shape[1]
    return pl.pallas_call(
        _ada_kernel,
        grid=(n // d,),
        in_specs=[pl.BlockSpec((rows, d), lambda j: (0, 0)),
                  pl.BlockSpec((d, d), lambda j: (0, j)),
                  pl.BlockSpec((1, d), lambda j: (0, j))],
        out_specs=pl.BlockSpec((rows, d), lambda j: (0, j)),
        out_shape=jax.ShapeDtypeStruct((rows, n), F32),
        compiler_params=_params("arbitrary"),
        name="ada_mod",
    )(cc, w_ada, b_ada.reshape(1, n))


def _rope(x, cos_t, sin_t):
    w = x.shape[1]
    half = HEAD_DIM // 2
    first = (_iota(x.shape, 1) % HEAD_DIM) < half
    swapped = jnp.where(first, pltpu.roll(x, w - half, 1), pltpu.roll(x, half, 1))
    reps = w // LANES
    c = jnp.concatenate([cos_t] * reps, axis=1)
    s = jnp.concatenate([sin_t] * reps, axis=1)
    return x * c + swapped * s


def _inproj_kernel(*refs, latent, widths):
    if latent:
        (x_ref, mod_ref, g_ref, w_ref, cos_ref, sin_ref,
         q_ref, k_ref, v_ref, rkv_ref, lora_ref, gate_ref) = refs
    else:
        x_ref, mod_ref, g_ref, w_ref, k_ref, v_ref, rkv_ref, lora_ref = refs
    x = x_ref[0]
    h = _rmsnorm(x, g_ref[...]) * (1.0 + mod_ref[0, 1:2, :]) + mod_ref[0, 0:1, :]
    hb = h.astype(BF16)
    off = 0

    def seg(name):
        nonlocal off
        lo = off
        off += widths[name]
        return _dot(hb, w_ref[:, lo:off])

    if latent:
        cos_t, sin_t = cos_ref[...], sin_ref[...]
        q = seg("q")
        q_ref[0] = (_rope(q, cos_t, sin_t) * (HEAD_DIM ** -0.5)).astype(BF16)
        k_ref[0] = _rope(seg("k"), cos_t, sin_t).astype(BF16)
    else:
        k_ref[0] = seg("k").astype(BF16)
    v_ref[0] = seg("v").astype(BF16)
    rkv_ref[0] = seg("rkv")
    lora_ref[0] = seg("lora")
    if latent:
        gate_ref[0] = jax.nn.sigmoid(seg("gate")).astype(BF16)


def _in_proj(x, mod, mod_row, norm_g, w_packed, widths, tables, *, latent, tile):
    b, l, d = x.shape
    nt = l // tile
    n = w_packed.shape[1]
    if latent:
        mod_map = lambda i, t: (i, 0, 0)
    else:
        mod_map = lambda i, t: (mod_row, 0, 0)
    tok = lambda w: pl.BlockSpec((1, tile, w), lambda i, t: (i, t, 0))
    in_specs = [tok(d),
                pl.BlockSpec((1,) + mod.shape[1:], mod_map),
                _const_spec((1, d)),
                _const_spec((d, n))]
    args = [x, mod, norm_g.reshape(1, d), w_packed]
    out_specs, out_shape = [], []

    def out(w, dt):
        out_specs.append(tok(w))
        out_shape.append(jax.ShapeDtypeStruct((b, l, w), dt))

    if latent:
        in_specs += [pl.BlockSpec((tile, LANES), lambda i, t: (t, 0))] * 2
        args += list(tables)
        out(widths["q"], BF16)
    out(widths["k"], BF16)
    out(widths["v"], BF16)
    out(widths["rkv"], F32)
    out(widths["lora"], F32)
    if latent:
        out(widths["gate"], BF16)
    return pl.pallas_call(
        functools.partial(_inproj_kernel, latent=latent, widths=widths),
        grid=(b, nt),
        in_specs=in_specs,
        out_specs=out_specs,
        out_shape=out_shape,
        compiler_params=_params("parallel", "parallel"),
        name="in_proj_latent" if latent else "in_proj_context",
    )(*args)


def _attn_kernel(sink_ref, q_ref, kp_ref, kc_ref, kn_ref, vp_ref, vc_ref, vn_ref,
                 kx_ref, vx_ref, o_ref, *, n_kv):
    i = pl.program_id(1)
    nb = pl.num_programs(1)
    blk = ATTN_BLOCK
    lc = kx_ref.shape[1]
    qi = _iota((blk, blk), 0)
    kj = _iota((blk, blk), 1)
    zero = jnp.zeros((blk, blk), F32)
    bias = jnp.concatenate(
        [jnp.where((kj >= qi) & (i > 0), 0.0, NEG),
         zero,
         jnp.where((kj <= qi) & (i < nb - 1), 0.0, NEG),
         jnp.zeros((blk, lc), F32)], axis=1)
    bias = jnp.concatenate([bias] * Q_PER_KV, axis=0)
    left = _iota((blk, LANES), 1) < HEAD_DIM
    q = q_ref[0].astype(F32)
    for g in range(n_kv):
        gs = slice(g * LANES, (g + 1) * LANES)
        kcat = jnp.concatenate([kp_ref[0, :, gs], kc_ref[0, :, gs], kn_ref[0, :, gs], kx_ref[0, :, gs]], axis=0)
        vcat = jnp.concatenate([vp_ref[0, :, gs], vc_ref[0, :, gs], vn_ref[0, :, gs], vx_ref[0, :, gs]], axis=0)
        heads = range(g * Q_PER_KV, (g + 1) * Q_PER_KV)
        qs, sinks = [], []
        for hd in heads:
            qp = q[:, (hd // PAIR) * LANES:(hd // PAIR + 1) * LANES]
            keep = left if hd % PAIR == 0 else jnp.logical_not(left)
            qs.append(jnp.where(keep, qp, 0.0).astype(BF16))
            sinks.append(jnp.full((blk, 1), sink_ref[hd], F32))
        qs = jnp.concatenate(qs, axis=0)
        sink = jnp.concatenate(sinks, axis=0)
        s = _dot_nt(qs, kcat) + bias
        m = jnp.maximum(jnp.max(s, axis=1, keepdims=True), sink)
        p = jnp.exp(s - m)
        den = jnp.sum(p, axis=1, keepdims=True) + jnp.exp(sink - m)
        o = _dot(p.astype(BF16), vcat) / den
        for hd in heads[::PAIR]:
            j = hd - g * Q_PER_KV
            pair = jnp.where(left, o[j * blk:(j + 1) * blk], o[(j + 1) * blk:(j + 2) * blk])
            col = (hd // PAIR) * LANES
            o_ref[0, :, col:col + LANES] = pair.astype(BF16)


def _attention(sink, q, kd, vd, kxd, vxd):
    b, l, wq = q.shape
    wk = kd.shape[2]
    lc = kxd.shape[1]
    nb = l // ATTN_BLOCK
    blk = lambda w, f: pl.BlockSpec((1, ATTN_BLOCK, w), f)
    prev = lambda bi, i: (bi, jnp.maximum(i - 1, 0), 0)
    cur = lambda bi, i: (bi, i, 0)
    nxt = lambda bi, i: (bi, jnp.minimum(i + 1, nb - 1), 0)
    ctx = pl.BlockSpec((1, lc, wk), lambda bi, i: (bi, 0, 0))
    return pl.pallas_call(
        functools.partial(_attn_kernel, n_kv=wk // LANES),
        grid=(b, nb),
        in_specs=[pl.BlockSpec(memory_space=pltpu.SMEM),
                  blk(wq, cur),
                  blk(wk, prev), blk(wk, cur), blk(wk, nxt),
                  blk(wk, prev), blk(wk, cur), blk(wk, nxt),
                  ctx, ctx],
        out_specs=blk(wq, cur),
        out_shape=jax.ShapeDtypeStruct((b, l, wq), BF16),
        compiler_params=_params("parallel", "parallel"),
        name="attention",
    )(sink, q, kd, kd, kd, vd, vd, vd, kxd, vxd)


def _conv3(x, prev_row, next_row, w):
    n = x.shape[0]
    row = _iota(x.shape, 0)
    xm = jnp.where(row == 0, prev_row, pltpu.roll(x, 1, 0))
    xp = jnp.where(row == n - 1, next_row, pltpu.roll(x, n - 1, 0))
    return xm * w[0:1] + x * w[1:2] + xp * w[2:3]


def _prep_kernel(*refs, latent, width):
    (rkv_ref, rkv_p, rkv_n, lora_ref, lora_p, lora_n, cw_ref, cwl_ref, kk_ref, ka_ref,
     w0_ref, a0_ref, wl_ref, rk_ref) = refs[:14]
    outs = refs[14:]
    if latent:
        at_ref, rt_ref, bt_ref, kt_ref, v_ref, wc_ref, bonus_ref, g_ref = outs
    else:
        at_ref, rt_ref, bt_ref, kt_ref, v_ref, wc_ref = outs
    t = pl.program_id(1)
    nt = pl.num_programs(1)
    tile = rkv_ref.shape[1]
    w = width
    has_prev = (t > 0).astype(F32)
    has_next = (t < nt - 1).astype(F32)
    u = _conv3(rkv_ref[0], rkv_p[0, 7:8, :] * has_prev, rkv_n[0, 0:1, :] * has_next, cw_ref[...])
    ul = _conv3(lora_ref[0], lora_p[0, 7:8, :] * has_prev, lora_n[0, 0:1, :] * has_next, cwl_ref[...])
    r, k, v = u[:, :w], u[:, w:2 * w], u[:, 2 * w:]
    ones_bd = _head_ones(w)

    kk = k * kk_ref[...]
    kk = kk * lax.rsqrt(jnp.maximum(_head_sum(kk * kk, ones_bd), 1e-24))

    lane = _iota(ul.shape, 1)
    lin = jnp.where(lane < DECAY_LORA, jnp.tanh(ul),
                    jnp.where(lane < DECAY_LORA + ICLR_LORA, ul, jax.nn.sigmoid(ul)))
    proj = _dot(lin.astype(BF16), wl_ref[...])

    tr = _iota((tile, tile), 0)
    tc = _iota((tile, tile), 1)
    same = (tr // CHUNK) == (tc // CHUNK)
    tri = (jnp.where(same & (tc <= tr), 1.0, 0.0).astype(BF16),
           jnp.where(same & (tc >= tr), 1.0, 0.0).astype(BF16))

    def exact_dot(m, x):
        h1 = x.astype(BF16)
        r1 = x - h1.astype(F32)
        h2 = r1.astype(BF16)
        h3 = (r1 - h2.astype(F32)).astype(BF16)
        return _dot(m, h1) + _dot(m, h2) + _dot(m, h3)

    k_sum = None
    for d in range(2):
        z = w0_ref[d:d + 1, :] + proj[:, d * w:(d + 1) * w]
        lw = -math.exp(-0.5) * jax.nn.sigmoid(z)
        a = jax.nn.sigmoid(a0_ref[d:d + 1, :] + proj[:, (2 + d) * w:(3 + d) * w])
        kd = k * (1.0 + (a - 1.0) * ka_ref[...])
        k_sum = kd if k_sum is None else k_sum + kd
        cum = exact_dot(tri[d], lw)
        e_neg = jnp.exp(-cum)
        at_ref[d, 0] = (-kk * jnp.exp(cum - lw)).astype(BF16)
        rt_ref[d, 0] = (r * jnp.exp(cum)).astype(BF16)
        bt_ref[d, 0] = (kk * a * e_neg).astype(BF16)
        kt_ref[d, 0] = (kd * e_neg).astype(BF16)
        for j in range(tile // CHUNK):
            edge = (j + 1) * CHUNK - 1 if d == 0 else j * CHUNK
            wc_ref[0, j, d:d + 1, :] = jnp.exp(cum[edge:edge + 1, :])
    v_ref[0] = v.astype(BF16)
    if latent:
        bonus_ref[0] = _head_sum(r * k_sum * rk_ref[...], ones_bd) * v
        g_ref[0] = proj[:, 4 * w:5 * w].astype(BF16)


def _wkv_prep(rkv, lora, prm, *, latent, tile):
    b, l, w3 = rkv.shape
    w = w3 // 3
    nt = l // tile
    n8 = l // 8
    tok = lambda wd: pl.BlockSpec((1, tile, wd), lambda i, t: (i, t, 0))
    prev = lambda wd: pl.BlockSpec((1, 8, wd), lambda i, t: (i, jnp.maximum(t * (tile // 8) - 1, 0), 0))
    nxt = lambda wd: pl.BlockSpec((1, 8, wd), lambda i, t: (i, jnp.minimum((t + 1) * (tile // 8), n8 - 1), 0))
    wl = lora.shape[2]
    in_specs = [tok(w3), prev(w3), nxt(w3), tok(wl), prev(wl), nxt(wl)]
    consts = [prm["conv_rkv"], prm["conv_lora"], prm["k_k"], prm["k_a"], prm["decay_w0"],
              prm["iclr_a0"], prm["lora_w"], prm["r_k"]]
    in_specs += [_const_spec(c.shape) for c in consts]
    dirtok = pl.BlockSpec((2, 1, tile, w), lambda i, t: (0, i, t, 0))
    out_specs = [dirtok] * 4 + [tok(w), pl.BlockSpec((1, tile // CHUNK, 2, w), lambda i, t: (i, t, 0, 0))]
    out_shape = [jax.ShapeDtypeStruct((2, b, l, w), BF16)] * 4 + [
        jax.ShapeDtypeStruct((b, l, w), BF16), jax.ShapeDtypeStruct((b, l // CHUNK, 2, w), F32)]
    if latent:
        out_specs += [tok(w), tok(w)]
        out_shape += [jax.ShapeDtypeStruct((b, l, w), F32), jax.ShapeDtypeStruct((b, l, w), BF16)]
    return pl.pallas_call(
        functools.partial(_prep_kernel, latent=latent, width=w),
        grid=(b, nt),
        in_specs=in_specs,
        out_specs=out_specs,
        out_shape=out_shape,
        compiler_params=_params("parallel", "parallel"),
        name="wkv_prep_latent" if latent else "wkv_prep_context",
    )(rkv, rkv, rkv, lora, lora, lora, *consts)


def _wkv_kernel(*refs, emit_y, n_pairs):
    dir_refs = (refs[0:6], refs[6:12])
    z0_ref = refs[12]
    if emit_y:
        y_refs = refs[13:15]
        z_scr = refs[15]
    else:
        zfin_ref = refs[13]
        z_scr = refs[14]
    c = pl.program_id(1)

    @pl.when(c == 0)
    def _():
        z_scr[...] = z0_ref[0]

    n = 2 * CHUNK
    row = _iota((n, n), 0)
    lane = _iota((n, n), 1)
    tt, ss = row % CHUNK, lane % CHUNK
    top, left = row < CHUNK, lane < CHUNK
    bd = top == left
    left_h = _iota((CHUNK, n), 1) < CHUNK
    for d in range(2):
        at_ref, rt_ref, bt_ref, kt_ref, v_ref, wc_ref = dir_refs[d]
        before = (ss < tt) if d == 0 else (ss > tt)
        mask_a = before | (jnp.logical_not(top) & (ss == tt))
        for p in range(n_pairs):
            sl = slice(p * LANES, (p + 1) * LANES)
            at, rt = at_ref[0, 0, :, sl], rt_ref[0, 0, :, sl]
            bt, kt, v = bt_ref[0, 0, :, sl], kt_ref[0, 0, :, sl], v_ref[0, :, sl]
            s_old = z_scr[d, p]
            sb = s_old.astype(BF16)
            lhs = jnp.concatenate([at, rt], axis=0).astype(F32)
            rhs0 = jnp.concatenate([bt, kt], axis=0)
            rhs1 = jnp.concatenate([kt, bt], axis=0)
            a0 = jnp.where(mask_a, _dot_nt(jnp.where(left, lhs, 0.0).astype(BF16), rhs0), 0.0)
            a1 = jnp.where(mask_a, _dot_nt(jnp.where(left, 0.0, lhs).astype(BF16), rhs1), 0.0)
            a_top = jnp.concatenate([a0[:CHUNK], a1[:CHUNK]], axis=0)
            nb_ = jnp.where(bd, a_top, 0.0).astype(BF16)
            a_ak = jnp.where(bd, 0.0, a_top).astype(BF16)
            av = _dot(a_ak, jnp.concatenate([v, v], axis=0))
            x = jnp.concatenate([jnp.concatenate([lhs[:CHUNK], lhs[:CHUNK]], axis=0), av], axis=1)
            for it in range(6):
                x = x + _dot(nb_, x.astype(BF16))
                if it < 5:
                    nb_ = _dot(nb_, nb_).astype(BF16)
            pm = jnp.where(left_h, x[:CHUNK, :LANES], x[CHUNK:, :LANES])
            qm = jnp.where(left_h, x[:CHUNK, LANES:], x[CHUNK:, LANES:])
            u = _dot_nt(pm.astype(BF16), sb) + qm
            ub = u.astype(BF16)
            if emit_y:
                a_bot = jnp.concatenate([a0[CHUNK:], a1[CHUNK:]], axis=0)
                lhs_y = jnp.concatenate([jnp.where(bd, a_bot, 0.0), jnp.where(bd, 0.0, a_bot)], axis=1)
                rhs_y = jnp.concatenate([ub, ub, v, v], axis=0)
                yst = _dot(lhs_y.astype(BF16), rhs_y)
                y = _dot_nt(rt, sb) + jnp.where(left_h, yst[:CHUNK], yst[CHUNK:])
                y_refs[d][0, :, sl] = y
            s_inc = _dot_tn(jnp.concatenate([ub, v], axis=0), rhs0)
            z_scr[d, p] = (s_old + jnp.where(bd, s_inc, 0.0)) * wc_ref[0, 0, d:d + 1, sl]

    if not emit_y:
        @pl.when(c == pl.num_programs(1) - 1)
        def _():
            zfin_ref[0] = z_scr[...]


def _wkv_scan(prep, z0, *, emit_y):
    at, rt, bt, kt, v, wc = prep[:6]
    _, b, l, w = at.shape
    nc = l // CHUNK
    n_pairs = w // LANES
    fwd = lambda i, c: c
    rev = lambda i, c: nc - 1 - c
    in_specs, args = [], []
    for d, cm in enumerate((fwd, rev)):
        for arr in (at, rt, bt, kt):
            in_specs.append(pl.BlockSpec((1, 1, CHUNK, w), lambda i, c, d=d, cm=cm: (d, i, cm(i, c), 0)))
            args.append(arr)
        in_specs.append(pl.BlockSpec((1, CHUNK, w), lambda i, c, cm=cm: (i, cm(i, c), 0)))
        args.append(v)
        in_specs.append(pl.BlockSpec((1, 1, 2, w), lambda i, c, cm=cm: (i, cm(i, c), 0, 0)))
        args.append(wc)
    zshape = (2, n_pairs, LANES, LANES)
    zspec = pl.BlockSpec((1,) + zshape, lambda i, c: (i, 0, 0, 0, 0))
    in_specs.append(zspec)
    args.append(z0)
    if emit_y:
        out_specs = [pl.BlockSpec((1, CHUNK, w), lambda i, c: (i, c, 0)),
                     pl.BlockSpec((1, CHUNK, w), lambda i, c: (i, nc - 1 - c, 0))]
        out_shape = [jax.ShapeDtypeStruct((b, l, w), F32)] * 2
    else:
        out_specs = zspec
        out_shape = jax.ShapeDtypeStruct((b,) + zshape, F32)
    return pl.pallas_call(
        functools.partial(_wkv_kernel, emit_y=emit_y, n_pairs=n_pairs),
        grid=(b, nc),
        in_specs=in_specs,
        out_specs=out_specs,
        out_shape=out_shape,
        scratch_shapes=[pltpu.VMEM(zshape, F32)],
        compiler_params=_params("parallel", "arbitrary"),
        name="wkv_scan_latent" if emit_y else "wkv_scan_context",
    )(*args)


def _merge_kernel(x_ref, mod_ref, ya_ref, yf_ref, yr_ref, bonus_ref, g_ref, gate_ref,
                  lnw_ref, lnb_ref, wba_ref, wbr_ref, wo_ref, n2_ref, wu_ref, wd_ref, nf_ref,
                  o_ref, *, ff_chunk):
    x = x_ref[0]
    d = x.shape[1]
    mod = lambda j: mod_ref[0, j:j + 1, :]
    y = yf_ref[0] + yr_ref[0]
    ones_bd = _head_ones(y.shape[1])
    inv = 1.0 / HEAD_DIM
    mu = _head_sum(y, ones_bd) * inv
    yc = y - mu
    var = _head_sum(yc * yc, ones_bd) * inv
    yn = yc * lax.rsqrt(var + LNX_EPS)
    yr = (yn * lnw_ref[...] + lnb_ref[...] + bonus_ref[0]) * g_ref[0].astype(F32)
    gate = gate_ref[0].astype(F32)
    merged = gate[:, :d] * _dot(ya_ref[0], wba_ref[...]) + gate[:, d:] * _dot(yr.astype(BF16), wbr_ref[...])
    x1 = x + mod(2) * _dot(merged.astype(BF16), wo_ref[...])
    h2 = (_rmsnorm(x1, n2_ref[...]) * (1.0 + mod(4)) + mod(3)).astype(BF16)
    acc = jnp.zeros_like(x1)
    for j in range(wu_ref.shape[1] // ff_chunk):
        cs = slice(j * ff_chunk, (j + 1) * ff_chunk)
        up = jnp.maximum(_dot(h2, wu_ref[:, cs]), 0.0)
        acc = acc + _dot((up * up).astype(BF16), wd_ref[cs, :])
    x2 = x1 + mod(5) * acc
    o_ref[0] = _rmsnorm(x2, nf_ref[...])


def _merge_mlp(x, mod, ya, yf, yr, bonus, g, gate, prm, *, tile):
    b, l, d = x.shape
    tok = lambda arr: pl.BlockSpec((1, tile, arr.shape[2]), lambda i, t: (i, t, 0))
    consts = [prm["lnx_w"], prm["lnx_b"], prm["w_branch_attn"], prm["w_branch_rwkv"], prm["w_out"],
              prm["norm2_g"], prm["w_mlp_up"], prm["w_mlp_down"], prm["norm_f_g"]]
    toks = [ya, yf, yr, bonus, g, gate]
    return pl.pallas_call(
        functools.partial(_merge_kernel, ff_chunk=min(1024, prm["w_mlp_up"].shape[1])),
        grid=(b, l // tile),
        in_specs=[tok(x), pl.BlockSpec((1,) + mod.shape[1:], lambda i, t: (i, 0, 0))]
        + [tok(a) for a in toks] + [_const_spec(c.shape) for c in consts],
        out_specs=tok(x),
        out_shape=jax.ShapeDtypeStruct(x.shape, x.dtype),
        compiler_params=_params("parallel", "parallel"),
        name="merge_mlp",
    )(x, mod, *toks, *consts)


def _rope_tables(l):
    n_freq = HEAD_DIM // 4
    inv_freq = jnp.power(ROPE_BASE, -jnp.arange(n_freq, dtype=F32) / n_freq)
    rows = l // GRID_W
    row = jnp.repeat(jnp.arange(rows, dtype=F32), GRID_W)
    col = jnp.tile(jnp.arange(GRID_W, dtype=F32), rows)
    ang = jnp.concatenate([row[:, None] * inv_freq, col[:, None] * inv_freq], axis=-1)
    cos, sin = jnp.cos(ang), jnp.sin(ang)
    reps = LANES // HEAD_DIM
    return (jnp.tile(jnp.concatenate([cos, cos], axis=1), (1, reps)),
            jnp.tile(jnp.concatenate([-sin, sin], axis=1), (1, reps)))


def _dup_heads(w, n_heads):
    cols = [w[:, h * HEAD_DIM:(h + 1) * HEAD_DIM] for h in range(n_heads)]
    return jnp.concatenate([c for c in cols for _ in range(PAIR)], axis=1)


def _pad_cols(w, width):
    return jnp.pad(w, ((0, 0), (0, width - w.shape[1])))


def kernel(x, c, ctx, c_ctx, w_ada, b_ada, norm1_g, w_in, sink, conv_w, decay_w0, decay_w2, iclr_a0, iclr_a2, gate_g2, k_k, k_a, r_k, lnx_w, lnx_b, w_branch_attn, w_branch_rwkv, w_out, norm2_g, w_mlp_up, w_mlp_down, norm_f_g):
    assert w_in.shape[0] == 1, "single-layer block: context tokens are read, never updated"
    b, l, d = x.shape
    attn_w = w_branch_attn.shape[1]
    rw = w_branch_rwkv.shape[1]
    n_q = attn_w // HEAD_DIM
    n_kv = n_q // Q_PER_KV
    kv_w = n_kv * HEAD_DIM
    assert n_kv * PAIR * HEAD_DIM == n_kv * LANES and rw % LANES == 0 and l % 256 == 0 and ctx.shape[1] % 256 == 0

    w = w_in[0]
    o_k, o_v, o_r = attn_w, attn_w + kv_w, attn_w + 2 * kv_w
    o_l = o_r + 3 * rw
    o_g = o_l + DECAY_LORA + ICLR_LORA + GATE_LORA
    seg_k = _dup_heads(w[:, o_k:o_v], n_kv)
    seg_v = _dup_heads(w[:, o_v:o_r], n_kv)
    seg_rkv = w[:, o_r:o_l]
    seg_lora = _pad_cols(w[:, o_l:o_g], LORA_PAD)
    w_ctx = jnp.concatenate([seg_k, seg_v, seg_rkv, seg_lora], axis=1).astype(BF16)
    w_lat = jnp.concatenate([w[:, :o_k].astype(BF16), w_ctx, w[:, o_g:].astype(BF16)], axis=1)
    widths_ctx = {"k": seg_k.shape[1], "v": seg_v.shape[1], "rkv": 3 * rw, "lora": LORA_PAD}
    widths_lat = {"q": attn_w, **widths_ctx, "gate": 2 * d}

    cw = conv_w[0]
    lora_w = jnp.zeros((LORA_PAD, 5 * rw), F32)
    lora_w = lora_w.at[:DECAY_LORA, :rw].set(decay_w2[0, 0]).at[:DECAY_LORA, rw:2 * rw].set(decay_w2[0, 1])
    r1 = DECAY_LORA + ICLR_LORA
    lora_w = lora_w.at[DECAY_LORA:r1, 2 * rw:3 * rw].set(iclr_a2[0, 0]).at[DECAY_LORA:r1, 3 * rw:4 * rw].set(iclr_a2[0, 1])
    lora_w = lora_w.at[r1:r1 + GATE_LORA, 4 * rw:].set(gate_g2[0])
    prm = {
        "conv_rkv": cw[:, :3 * rw], "conv_lora": _pad_cols(cw[:, 3 * rw:], LORA_PAD),
        "k_k": k_k[0].reshape(1, rw), "k_a": k_a[0].reshape(1, rw),
        "decay_w0": decay_w0[0], "iclr_a0": iclr_a0[0], "lora_w": lora_w.astype(BF16),
        "r_k": r_k[0].reshape(1, rw),
        "lnx_w": lnx_w[0].reshape(1, rw), "lnx_b": lnx_b[0].reshape(1, rw),
        "w_branch_attn": w_branch_attn[0].astype(BF16), "w_branch_rwkv": w_branch_rwkv[0].astype(BF16),
        "w_out": w_out[0].astype(BF16), "norm2_g": norm2_g[0].reshape(1, d),
        "w_mlp_up": w_mlp_up[0].astype(BF16), "w_mlp_down": w_mlp_down[0].astype(BF16),
        "norm_f_g": norm_f_g.reshape(1, d),
    }

    rows = -(-(b + 1) // 8) * 8
    cc = jnp.zeros((rows, d), F32).at[:b].set(c).at[b].set(c_ctx)
    mod = _ada_mod(cc, w_ada[0], b_ada[0]).reshape(rows, -1, d)

    q, kd, vd, rkv, lora, gate = _in_proj(x, mod, b, norm1_g[0], w_lat, widths_lat, _rope_tables(l),
                                          latent=True, tile=256)
    kxd, vxd, rkv_c, lora_c = _in_proj(ctx, mod, b, norm1_g[0], w_ctx, widths_ctx, None,
                                       latent=False, tile=256)
    ya = _attention(sink[0], q, kd, vd, kxd, vxd)

    prep_c = _wkv_prep(rkv_c, lora_c, prm, latent=False, tile=256)
    z_ctx = _wkv_scan(prep_c, jnp.zeros((b, 2, rw // LANES, LANES, LANES), F32), emit_y=False)
    prep = _wkv_prep(rkv, lora, prm, latent=True, tile=256)
    yf, yr = _wkv_scan(prep, z_ctx, emit_y=True)
    bonus, g = prep[6], prep[7]

    return _merge_mlp(x, mod, ya, yf, yr, bonus, g, gate, prm, tile=256)
```

```python
import functools
import math

import jax
import jax.numpy as jnp
from jax import lax
from jax.experimental import pallas as pl
from jax.experimental.pallas import tpu as pltpu

F32 = jnp.float32
BF16 = jnp.bfloat16

GRID_W = 64
HEAD_DIM = 64
Q_PER_KV = 4
ATTN_BLOCK = 128
ROPE_BASE = 10000.0
NORM_EPS = 1e-6
LNX_EPS = 1e-5 * HEAD_DIM
DECAY_LORA, ICLR_LORA, GATE_LORA = 32, 32, 96
LORA_PAD = 256
CHUNK = 64
LANES = 128
PAIR = LANES // HEAD_DIM
NEG = -1e30
VMEM_LIMIT = 56 * 1024 * 1024


def _dot(a, b):
    return jnp.dot(a, b, preferred_element_type=F32)


def _dot_nt(a, b):
    return lax.dot_general(a, b, (((1,), (1,)), ((), ())), preferred_element_type=F32)


def _dot_tn(a, b):
    return lax.dot_general(a, b, (((0,), (0,)), ((), ())), preferred_element_type=F32)


def _iota(shape, dim):
    return lax.broadcasted_iota(jnp.int32, shape, dim)


def _head_ones(width):
    r = _iota((width, width), 0) // HEAD_DIM
    c = _iota((width, width), 1) // HEAD_DIM
    return jnp.where(r == c, 1.0, 0.0).astype(BF16)


def _head_sum(x, ones_bd):
    hi = x.astype(BF16)
    lo = (x - hi.astype(F32)).astype(BF16)
    return _dot(hi, ones_bd) + _dot(lo, ones_bd)


def _rmsnorm(x, g):
    ms = jnp.mean(x * x, axis=-1, keepdims=True)
    return x * lax.rsqrt(ms + NORM_EPS) * g


def _params(*sem):
    return pltpu.CompilerParams(dimension_semantics=sem, vmem_limit_bytes=VMEM_LIMIT)


def _const_spec(shape):
    nd = len(shape)
    return pl.BlockSpec(shape, lambda *_: (0,) * nd, pipeline_mode=pl.Buffered(1))


def _ada_kernel(c_ref, w_ref, b_ref, o_ref):
    c = c_ref[...]
    s = c * jax.nn.sigmoid(c)
    o_ref[...] = _dot(s.astype(BF16), w_ref[...].astype(BF16)) + b_ref[...]


def _ada_mod(cc, w_ada, b_ada):
    rows, d = cc.shape
    n = w_ada.shape[1]
    return pl.pallas_call(
        _ada_kernel,
        grid=(n // d,),
        in_specs=[pl.BlockSpec((rows, d), lambda j: (0, 0)),
                  pl.BlockSpec((d, d), lambda j: (0, j)),
                  pl.BlockSpec((1, d), lambda j: (0, j))],
        out_specs=pl.BlockSpec((rows, d), lambda j: (0, j)),
        out_shape=jax.ShapeDtypeStruct((rows, n), F32),
        compiler_params=_params("arbitrary"),
        name="ada_mod",
    )(cc, w_ada, b_ada.reshape(1, n))


def _rope(x, cos_t, sin_t):
    w = x.shape[1]
    half = HEAD_DIM // 2
    first = (_iota(x.shape, 1) % HEAD_DIM) < half
    swapped = jnp.where(first, pltpu.roll(x, w - half, 1), pltpu.roll(x, half, 1))
    reps = w // LANES
    c = jnp.concatenate([cos_t] * reps, axis=1)
    s = jnp.concatenate([sin_t] * reps, axis=1)
    return x * c + swapped * s


def _inproj_kernel(*refs, latent, widths):
    if latent:
        (x_ref, mod_ref, g_ref, w_ref, cos_ref, sin_ref,
         q_ref, k_ref, v_ref, rkv_ref, lora_ref, gate_ref) = refs
    else:
        x_ref, mod_ref, g_ref, w_ref, k_ref, v_ref, rkv_ref, lora_ref = refs
    x = x_ref[0]
    h = _rmsnorm(x, g_ref[...]) * (1.0 + mod_ref[0, 1:2, :]) + mod_ref[0, 0:1, :]
    hb = h.astype(BF16)
    off = 0

    def seg(name):
        nonlocal off
        lo = off
        off += widths[name]
        return _dot(hb, w_ref[:, lo:off])

    if latent:
        cos_t, sin_t = cos_ref[...], sin_ref[...]
        q = seg("q")
        q_ref[0] = (_rope(q, cos_t, sin_t) * (HEAD_DIM ** -0.5)).astype(BF16)
        k_ref[0] = _rope(seg("k"), cos_t, sin_t).astype(BF16)
    else:
        k_ref[0] = seg("k").astype(BF16)
    v_ref[0] = seg("v").astype(BF16)
    rkv_ref[0] = seg("rkv")
    lora_ref[0] = seg("lora")
    if latent:
        gate_ref[0] = jax.nn.sigmoid(seg("gate")).astype(BF16)


def _in_proj(x, mod, mod_row, norm_g, w_packed, widths, tables, *, latent, tile):
    b, l, d = x.shape
    nt = l // tile
    n = w_packed.shape[1]
    if latent:
        mod_map = lambda i, t: (i, 0, 0)
    else:
        mod_map = lambda i, t: (mod_row, 0, 0)
    tok = lambda w: pl.BlockSpec((1, tile, w), lambda i, t: (i, t, 0))
    in_specs = [tok(d),
                pl.BlockSpec((1,) + mod.shape[1:], mod_map),
                _const_spec((1, d)),
                _const_spec((d, n))]
    args = [x, mod, norm_g.reshape(1, d), w_packed]
    out_specs, out_shape = [], []

    def out(w, dt):
        out_specs.append(tok(w))
        out_shape.append(jax.ShapeDtypeStruct((b, l, w), dt))

    if latent:
        in_specs += [pl.BlockSpec((tile, LANES), lambda i, t: (t, 0))] * 2
        args += list(tables)
        out(widths["q"], BF16)
    out(widths["k"], BF16)
    out(widths["v"], BF16)
    out(widths["rkv"], F32)
    out(widths["lora"], F32)
    if latent:
        out(widths["gate"], BF16)
    return pl.pallas_call(
        functools.partial(_inproj_kernel, latent=latent, widths=widths),
        grid=(b, nt),
        in_specs=in_specs,
        out_specs=out_specs,
        out_shape=out_shape,
        compiler_params=_params("parallel", "parallel"),
        name="in_proj_latent" if latent else "in_proj_context",
    )(*args)


def _attn_kernel(sink_ref, q_ref, kp_ref, kc_ref, kn_ref, vp_ref, vc_ref, vn_ref,
                 kx_ref, vx_ref, o_ref, *, n_kv):
    i = pl.program_id(1)
    nb = pl.num_programs(1)
    blk = ATTN_BLOCK
    lc = kx_ref.shape[1]
    qi = _iota((blk, blk), 0)
    kj = _iota((blk, blk), 1)
    zero = jnp.zeros((blk, blk), F32)
    bias = jnp.concatenate(
        [jnp.where((kj >= qi) & (i > 0), 0.0, NEG),
         zero,
         jnp.where((kj <= qi) & (i < nb - 1), 0.0, NEG),
         jnp.zeros((blk, lc), F32)], axis=1)
    bias = jnp.concatenate([bias] * Q_PER_KV, axis=0)
    left = _iota((blk, LANES), 1) < HEAD_DIM
    q = q_ref[0].astype(F32)
    for g in range(n_kv):
        gs = slice(g * LANES, (g + 1) * LANES)
        kcat = jnp.concatenate([kp_ref[0, :, gs], kc_ref[0, :, gs], kn_ref[0, :, gs], kx_ref[0, :, gs]], axis=0)
        vcat = jnp.concatenate([vp_ref[0, :, gs], vc_ref[0, :, gs], vn_ref[0, :, gs], vx_ref[0, :, gs]], axis=0)
        heads = range(g * Q_PER_KV, (g + 1) * Q_PER_KV)
        qs, sinks = [], []
        for hd in heads:
            qp = q[:, (hd // PAIR) * LANES:(hd // PAIR + 1) * LANES]
            keep = left if hd % PAIR == 0 else jnp.logical_not(left)
            qs.append(jnp.where(keep, qp, 0.0).astype(BF16))
            sinks.append(jnp.full((blk, 1), sink_ref[hd], F32))
        qs = jnp.concatenate(qs, axis=0)
        sink = jnp.concatenate(sinks, axis=0)
        s = _dot_nt(qs, kcat) + bias
        m = jnp.maximum(jnp.max(s, axis=1, keepdims=True), sink)
        p = jnp.exp(s - m)
        den = jnp.sum(p, axis=1, keepdims=True) + jnp.exp(sink - m)
        o = _dot(p.astype(BF16), vcat) / den
        for hd in heads[::PAIR]:
            j = hd - g * Q_PER_KV
            pair = jnp.where(left, o[j * blk:(j + 1) * blk], o[(j + 1) * blk:(j + 2) * blk])
            col = (hd // PAIR) * LANES
            o_ref[0, :, col:col + LANES] = pair.astype(BF16)


def _attention(sink, q, kd, vd, kxd, vxd):
    b, l, wq = q.shape
    wk = kd.shape[2]
    lc = kxd.shape[1]
    nb = l // ATTN_BLOCK
    blk = lambda w, f: pl.BlockSpec((1, ATTN_BLOCK, w), f)
    prev = lambda bi, i: (bi, jnp.maximum(i - 1, 0), 0)
    cur = lambda bi, i: (bi, i, 0)
    nxt = lambda bi, i: (bi, jnp.minimum(i + 1, nb - 1), 0)
    ctx = pl.BlockSpec((1, lc, wk), lambda bi, i: (bi, 0, 0))
    return pl.pallas_call(
        functools.partial(_attn_kernel, n_kv=wk // LANES),
        grid=(b, nb),
        in_specs=[pl.BlockSpec(memory_space=pltpu.SMEM),
                  blk(wq, cur),
                  blk(wk, prev), blk(wk, cur), blk(wk, nxt),
                  blk(wk, prev), blk(wk, cur), blk(wk, nxt),
                  ctx, ctx],
        out_specs=blk(wq, cur),
        out_shape=jax.ShapeDtypeStruct((b, l, wq), BF16),
        compiler_params=_params("parallel", "parallel"),
        name="attention",
    )(sink, q, kd, kd, kd, vd, vd, vd, kxd, vxd)


def _conv3(x, prev_row, next_row, w):
    n = x.shape[0]
    row = _iota(x.shape, 0)
    xm = jnp.where(row == 0, prev_row, pltpu.roll(x, 1, 0))
    xp = jnp.where(row == n - 1, next_row, pltpu.roll(x, n - 1, 0))
    return xm * w[0:1] + x * w[1:2] + xp * w[2:3]


def _pair_masks():
    n = 2 * CHUNK
    row = _iota((n, n), 0)
    lane = _iota((n, n), 1)
    top, left = row < CHUNK, lane < CHUNK
    return {"row": row % CHUNK, "lane": lane % CHUNK, "top": top, "left": left, "bd": top == left,
            "left_h": _iota((CHUNK, n), 1) < CHUNK}


def _stack(a, b):
    return jnp.concatenate([a, b], axis=0)


def _fold(x, m):
    return jnp.where(m["left_h"], x[:CHUNK], x[CHUNK:])


def _chunk_local(inst, m, reverse):
    bf = lambda x: x.astype(BF16)
    before = (m["lane"] < m["row"]) if not reverse else (m["lane"] > m["row"])
    mask_a = before | (jnp.logical_not(m["top"]) & (m["lane"] == m["row"]))
    left, bd = m["left"], m["bd"]
    lhs = [_stack(i["at"], i["rt"]) for i in inst]
    a0 = [jnp.where(mask_a, _dot_nt(bf(jnp.where(left, l, 0.0)), bf(_stack(i["bt"], i["kt"]))), 0.0)
          for l, i in zip(lhs, inst)]
    a1 = [jnp.where(mask_a, _dot_nt(bf(jnp.where(left, 0.0, l)), bf(_stack(i["kt"], i["bt"]))), 0.0)
          for l, i in zip(lhs, inst)]
    a_top = [_stack(x[:CHUNK], y[:CHUNK]) for x, y in zip(a0, a1)]
    a_bot = [_stack(x[CHUNK:], y[CHUNK:]) for x, y in zip(a0, a1)]
    nmat = [bf(jnp.where(bd, t, 0.0)) for t in a_top]
    v_sw = [bf(pltpu.roll(i["v"], CHUNK, 1)) for i in inst]
    av = [_dot(bf(jnp.where(bd, 0.0, t)), _stack(s, s)) for t, s in zip(a_top, v_sw)]
    x = [jnp.where(bd, _stack(i["at"], i["at"]), a) for i, a in zip(inst, av)]
    steps = CHUNK.bit_length() - 1
    for it in range(steps):
        x = [xi + _dot(n, bf(xi)) for xi, n in zip(x, nmat)]
        if it < steps - 1:
            nmat = [bf(_dot(n, n)) for n in nmat]
    vb = [bf(i["v"]) for i in inst]
    hst = [_dot(bf(jnp.where(bd, 0.0, b)), _stack(s, s)) for b, s in zip(a_bot, vb)]
    kvf = [_dot_tn(s, bf(i["kt"])) for s, i in zip(vb, inst)]
    return [{"pm": _fold(xi, m),
             "qm": pltpu.roll(jnp.where(m["left_h"], xi[CHUNK:], xi[:CHUNK]), CHUNK, 1),
             "arb": _fold(b, m), "hm": _fold(h, m), "kv": _fold(k, m)}
            for xi, b, h, k in zip(x, a_bot, hst, kvf)]


def _prep_kernel(*refs, latent, width):
    (rkv_ref, rkv_p, rkv_n, lora_ref, lora_p, lora_n, cw_ref, cwl_ref, kk_ref, ka_ref,
     w0_ref, a0_ref, wl_ref, rk_ref) = refs[:14]
    outs = refs[14:]
    out_refs = dict(zip(("pm", "qm", "rt", "bt", "arb", "hm", "kv"), outs[:7]))
    wc_ref = outs[7]
    if latent:
        bonus_ref, g_ref = outs[8:]
    t = pl.program_id(1)
    nt = pl.num_programs(1)
    tile = rkv_ref.shape[1]
    w = width
    has_prev = (t > 0).astype(F32)
    has_next = (t < nt - 1).astype(F32)
    u = _conv3(rkv_ref[0], rkv_p[0, 7:8, :] * has_prev, rkv_n[0, 0:1, :] * has_next, cw_ref[...])
    ul = _conv3(lora_ref[0], lora_p[0, 7:8, :] * has_prev, lora_n[0, 0:1, :] * has_next, cwl_ref[...])
    r, k, v = u[:, :w], u[:, w:2 * w], u[:, 2 * w:]
    ones_bd = _head_ones(w)

    kk = k * kk_ref[...]
    kk = kk * lax.rsqrt(jnp.maximum(_head_sum(kk * kk, ones_bd), 1e-24))

    lane = _iota(ul.shape, 1)
    lin = jnp.where(lane < DECAY_LORA, jnp.tanh(ul),
                    jnp.where(lane < DECAY_LORA + ICLR_LORA, ul, jax.nn.sigmoid(ul)))
    proj = _dot(lin.astype(BF16), wl_ref[...])

    tr = _iota((tile, tile), 0)
    tc = _iota((tile, tile), 1)
    same = (tr // CHUNK) == (tc // CHUNK)
    tri = (jnp.where(same & (tc <= tr), 1.0, 0.0).astype(BF16),
           jnp.where(same & (tc >= tr), 1.0, 0.0).astype(BF16))

    def exact_dot(m, x):
        h1 = x.astype(BF16)
        r1 = x - h1.astype(F32)
        h2 = r1.astype(BF16)
        h3 = (r1 - h2.astype(F32)).astype(BF16)
        return _dot(m, h1) + _dot(m, h2) + _dot(m, h3)

    masks = _pair_masks()
    k_sum = None
    for d in range(2):
        z = w0_ref[d:d + 1, :] + proj[:, d * w:(d + 1) * w]
        lw = -math.exp(-0.5) * jax.nn.sigmoid(z)
        a = jax.nn.sigmoid(a0_ref[d:d + 1, :] + proj[:, (2 + d) * w:(3 + d) * w])
        kd = k * (1.0 + (a - 1.0) * ka_ref[...])
        k_sum = kd if k_sum is None else k_sum + kd
        cum = exact_dot(tri[d], lw)
        e_neg = jnp.exp(-cum)
        full = {"at": -kk * jnp.exp(cum - lw), "rt": r * jnp.exp(cum), "bt": kk * a * e_neg,
                "kt": kd * e_neg, "v": v}
        out_refs["rt"][d, 0] = full["rt"].astype(BF16)
        out_refs["bt"][d, 0] = full["bt"].astype(BF16)
        for j in range(tile // CHUNK):
            edge = (j + 1) * CHUNK - 1 if d == 0 else j * CHUNK
            wc_ref[0, j, d:d + 1, :] = jnp.exp(cum[edge:edge + 1, :])
        where = [(slice(j * CHUNK, (j + 1) * CHUNK), slice(p * LANES, (p + 1) * LANES))
                 for j in range(tile // CHUNK) for p in range(w // LANES)]
        inst = [{name: val[rs, ls] for name, val in full.items()} for rs, ls in where]
        for (rs, ls), res in zip(where, _chunk_local(inst, masks, reverse=(d == 1))):
            for name, val in res.items():
                out_refs[name][d, 0, rs, ls] = val.astype(BF16)
    if latent:
        bonus_ref[0] = _head_sum(r * k_sum * rk_ref[...], ones_bd) * v
        g_ref[0] = proj[:, 4 * w:5 * w].astype(BF16)


def _wkv_prep(rkv, lora, prm, *, latent, tile):
    b, l, w3 = rkv.shape
    w = w3 // 3
    nt = l // tile
    n8 = l // 8
    tok = lambda wd: pl.BlockSpec((1, tile, wd), lambda i, t: (i, t, 0))
    prev = lambda wd: pl.BlockSpec((1, 8, wd), lambda i, t: (i, jnp.maximum(t * (tile // 8) - 1, 0), 0))
    nxt = lambda wd: pl.BlockSpec((1, 8, wd), lambda i, t: (i, jnp.minimum((t + 1) * (tile // 8), n8 - 1), 0))
    wl = lora.shape[2]
    in_specs = [tok(w3), prev(w3), nxt(w3), tok(wl), prev(wl), nxt(wl)]
    consts = [prm["conv_rkv"], prm["conv_lora"], prm["k_k"], prm["k_a"], prm["decay_w0"],
              prm["iclr_a0"], prm["lora_w"], prm["r_k"]]
    in_specs += [_const_spec(c.shape) for c in consts]
    dirtok = pl.BlockSpec((2, 1, tile, w), lambda i, t: (0, i, t, 0))
    out_specs = [dirtok] * 7 + [pl.BlockSpec((1, tile // CHUNK, 2, w), lambda i, t: (i, t, 0, 0))]
    out_shape = [jax.ShapeDtypeStruct((2, b, l, w), BF16)] * 7 + [
        jax.ShapeDtypeStruct((b, l // CHUNK, 2, w), F32)]
    if latent:
        out_specs += [tok(w), tok(w)]
        out_shape += [jax.ShapeDtypeStruct((b, l, w), F32), jax.ShapeDtypeStruct((b, l, w), BF16)]
    return pl.pallas_call(
        functools.partial(_prep_kernel, latent=latent, width=w),
        grid=(b, nt),
        in_specs=in_specs,
        out_specs=out_specs,
        out_shape=out_shape,
        compiler_params=_params("parallel", "parallel"),
        name="wkv_prep_latent" if latent else "wkv_prep_context",
    )(rkv, rkv, rkv, lora, lora, lora, *consts)


def _wkv_kernel(*refs, emit_y, n_pairs):
    names = ("pm", "qm", "rt", "bt", "arb", "hm", "kv", "wc")
    dir_refs = (dict(zip(names, refs[0:8])), dict(zip(names, refs[8:16])))
    z0_ref = refs[16]
    if emit_y:
        y_refs = refs[17:19]
        z_scr = refs[19]
    else:
        zfin_ref = refs[17]
        z_scr = refs[18]
    c = pl.program_id(1)

    @pl.when(c == 0)
    def _():
        z_scr[...] = z0_ref[0]

    m = _pair_masks()
    bd, left_h = m["bd"], m["left_h"]

    def unfold(x):
        x = x.astype(F32)
        return _stack(jnp.where(left_h, x, 0.0), jnp.where(left_h, 0.0, x))

    tiles = [(d, p, slice(p * LANES, (p + 1) * LANES)) for d in range(2) for p in range(n_pairs)]
    ld = lambda name: [dir_refs[d][name][0, 0, :, sl] for d, _, sl in tiles]
    s_old = [z_scr[d, p] for d, p, _ in tiles]
    sb = [s.astype(BF16) for s in s_old]
    if emit_y:
        ur = [_dot_nt(_stack(pm, rt), s) for pm, rt, s in zip(ld("pm"), ld("rt"), sb)]
    else:
        ur = [_dot_nt(pm, s) for pm, s in zip(ld("pm"), sb)]
    ub = [(x[:CHUNK] + q.astype(F32)).astype(BF16) for x, q in zip(ur, ld("qm"))]
    inc = [_dot_tn(u, bt) for u, bt in zip(ub, ld("bt"))]
    if emit_y:
        yst = [_dot(unfold(a).astype(BF16), _stack(u, u)) for a, u in zip(ld("arb"), ub)]
        for (d, _, sl), x, ys, h in zip(tiles, ur, yst, ld("hm")):
            y_refs[d][0, :, sl] = x[CHUNK:] + _fold(ys, m) + h.astype(F32)
    for (d, p, sl), s, i, kv in zip(tiles, s_old, inc, ld("kv")):
        z_scr[d, p] = (s + jnp.where(bd, i, 0.0) + unfold(kv)) * dir_refs[d]["wc"][0, 0, d:d + 1, sl]

    if not emit_y:
        @pl.when(c == pl.num_programs(1) - 1)
        def _():
            zfin_ref[0] = z_scr[...]


def _wkv_scan(prep, z0, *, emit_y):
    wc = prep[7]
    _, b, l, w = prep[0].shape
    nc = l // CHUNK
    n_pairs = w // LANES
    fwd = lambda i, c: c
    rev = lambda i, c: nc - 1 - c
    in_specs, args = [], []
    for d, cm in enumerate((fwd, rev)):
        for arr in prep[:7]:
            in_specs.append(pl.BlockSpec((1, 1, CHUNK, w), lambda i, c, d=d, cm=cm: (d, i, cm(i, c), 0)))
            args.append(arr)
        in_specs.append(pl.BlockSpec((1, 1, 2, w), lambda i, c, cm=cm: (i, cm(i, c), 0, 0)))
        args.append(wc)
    zshape = (2, n_pairs, LANES, LANES)
    zspec = pl.BlockSpec((1,) + zshape, lambda i, c: (i, 0, 0, 0, 0))
    in_specs.append(zspec)
    args.append(z0)
    if emit_y:
        out_specs = [pl.BlockSpec((1, CHUNK, w), lambda i, c: (i, c, 0)),
                     pl.BlockSpec((1, CHUNK, w), lambda i, c: (i, nc - 1 - c, 0))]
        out_shape = [jax.ShapeDtypeStruct((b, l, w), F32)] * 2
    else:
        out_specs = zspec
        out_shape = jax.ShapeDtypeStruct((b,) + zshape, F32)
    return pl.pallas_call(
        functools.partial(_wkv_kernel, emit_y=emit_y, n_pairs=n_pairs),
        grid=(b, nc),
        in_specs=in_specs,
        out_specs=out_specs,
        out_shape=out_shape,
        scratch_shapes=[pltpu.VMEM(zshape, F32)],
        compiler_params=_params("parallel", "arbitrary"),
        name="wkv_scan_latent" if emit_y else "wkv_scan_context",
    )(*args)


def _merge_kernel(x_ref, mod_ref, ya_ref, yf_ref, yr_ref, bonus_ref, g_ref, gate_ref,
                  lnw_ref, lnb_ref, wba_ref, wbr_ref, wo_ref, n2_ref, wu_ref, wd_ref, nf_ref,
                  o_ref, *, ff_chunk):
    x = x_ref[0]
    d = x.shape[1]
    mod = lambda j: mod_ref[0, j:j + 1, :]
    y = yf_ref[0] + yr_ref[0]
    ones_bd = _head_ones(y.shape[1])
    inv = 1.0 / HEAD_DIM
    mu = _head_sum(y, ones_bd) * inv
    yc = y - mu
    var = _head_sum(yc * yc, ones_bd) * inv
    yn = yc * lax.rsqrt(var + LNX_EPS)
    yr = (yn * lnw_ref[...] + lnb_ref[...] + bonus_ref[0]) * g_ref[0].astype(F32)
    gate = gate_ref[0].astype(F32)
    merged = gate[:, :d] * _dot(ya_ref[0], wba_ref[...]) + gate[:, d:] * _dot(yr.astype(BF16), wbr_ref[...])
    x1 = x + mod(2) * _dot(merged.astype(BF16), wo_ref[...])
    h2 = (_rmsnorm(x1, n2_ref[...]) * (1.0 + mod(4)) + mod(3)).astype(BF16)
    acc = jnp.zeros_like(x1)
    for j in range(wu_ref.shape[1] // ff_chunk):
        cs = slice(j * ff_chunk, (j + 1) * ff_chunk)
        up = jnp.maximum(_dot(h2, wu_ref[:, cs]), 0.0)
        acc = acc + _dot((up * up).astype(BF16), wd_ref[cs, :])
    x2 = x1 + mod(5) * acc
    o_ref[0] = _rmsnorm(x2, nf_ref[...])


def _merge_mlp(x, mod, ya, yf, yr, bonus, g, gate, prm, *, tile):
    b, l, d = x.shape
    tok = lambda arr: pl.BlockSpec((1, tile, arr.shape[2]), lambda i, t: (i, t, 0))
    consts = [prm["lnx_w"], prm["lnx_b"], prm["w_branch_attn"], prm["w_branch_rwkv"], prm["w_out"],
              prm["norm2_g"], prm["w_mlp_up"], prm["w_mlp_down"], prm["norm_f_g"]]
    toks = [ya, yf, yr, bonus, g, gate]
    return pl.pallas_call(
        functools.partial(_merge_kernel, ff_chunk=min(1024, prm["w_mlp_up"].shape[1])),
        grid=(b, l // tile),
        in_specs=[tok(x), pl.BlockSpec((1,) + mod.shape[1:], lambda i, t: (i, 0, 0))]
        + [tok(a) for a in toks] + [_const_spec(c.shape) for c in consts],
        out_specs=tok(x),
        out_shape=jax.ShapeDtypeStruct(x.shape, x.dtype),
        compiler_params=_params("parallel", "parallel"),
        name="merge_mlp",
    )(x, mod, *toks, *consts)


def _rope_tables(l):
    n_freq = HEAD_DIM // 4
    inv_freq = jnp.power(ROPE_BASE, -jnp.arange(n_freq, dtype=F32) / n_freq)
    rows = l // GRID_W
    row = jnp.repeat(jnp.arange(rows, dtype=F32), GRID_W)
    col = jnp.tile(jnp.arange(GRID_W, dtype=F32), rows)
    ang = jnp.concatenate([row[:, None] * inv_freq, col[:, None] * inv_freq], axis=-1)
    cos, sin = jnp.cos(ang), jnp.sin(ang)
    reps = LANES // HEAD_DIM
    return (jnp.tile(jnp.concatenate([cos, cos], axis=1), (1, reps)),
            jnp.tile(jnp.concatenate([-sin, sin], axis=1), (1, reps)))


def _dup_heads(w, n_heads):
    cols = [w[:, h * HEAD_DIM:(h + 1) * HEAD_DIM] for h in range(n_heads)]
    return jnp.concatenate([c for c in cols for _ in range(PAIR)], axis=1)


def _pad_cols(w, width):
    return jnp.pad(w, ((0, 0), (0, width - w.shape[1])))


def kernel(x, c, ctx, c_ctx, w_ada, b_ada, norm1_g, w_in, sink, conv_w, decay_w0, decay_w2, iclr_a0, iclr_a2, gate_g2, k_k, k_a, r_k, lnx_w, lnx_b, w_branch_attn, w_branch_rwkv, w_out, norm2_g, w_mlp_up, w_mlp_down, norm_f_g):
    assert w_in.shape[0] == 1, "single-layer block: context tokens are read, never updated"
    b, l, d = x.shape
    attn_w = w_branch_attn.shape[1]
    rw = w_branch_rwkv.shape[1]
    n_q = attn_w // HEAD_DIM
    n_kv = n_q // Q_PER_KV
    kv_w = n_kv * HEAD_DIM
    assert n_kv * PAIR * HEAD_DIM == n_kv * LANES and rw % LANES == 0 and l % 256 == 0 and ctx.shape[1] % 256 == 0

    w = w_in[0]
    o_k, o_v, o_r = attn_w, attn_w + kv_w, attn_w + 2 * kv_w
    o_l = o_r + 3 * rw
    o_g = o_l + DECAY_LORA + ICLR_LORA + GATE_LORA
    seg_k = _dup_heads(w[:, o_k:o_v], n_kv)
    seg_v = _dup_heads(w[:, o_v:o_r], n_kv)
    seg_rkv = w[:, o_r:o_l]
    seg_lora = _pad_cols(w[:, o_l:o_g], LORA_PAD)
    w_ctx = jnp.concatenate([seg_k, seg_v, seg_rkv, seg_lora], axis=1).astype(BF16)
    w_lat = jnp.concatenate([w[:, :o_k].astype(BF16), w_ctx, w[:, o_g:].astype(BF16)], axis=1)
    widths_ctx = {"k": seg_k.shape[1], "v": seg_v.shape[1], "rkv": 3 * rw, "lora": LORA_PAD}
    widths_lat = {"q": attn_w, **widths_ctx, "gate": 2 * d}

    cw = conv_w[0]
    lora_w = jnp.zeros((LORA_PAD, 5 * rw), F32)
    lora_w = lora_w.at[:DECAY_LORA, :rw].set(decay_w2[0, 0]).at[:DECAY_LORA, rw:2 * rw].set(decay_w2[0, 1])
    r1 = DECAY_LORA + ICLR_LORA
    lora_w = lora_w.at[DECAY_LORA:r1, 2 * rw:3 * rw].set(iclr_a2[0, 0]).at[DECAY_LORA:r1, 3 * rw:4 * rw].set(iclr_a2[0, 1])
    lora_w = lora_w.at[r1:r1 + GATE_LORA, 4 * rw:].set(gate_g2[0])
    prm = {
        "conv_rkv": cw[:, :3 * rw], "conv_lora": _pad_cols(cw[:, 3 * rw:], LORA_PAD),
        "k_k": k_k[0].reshape(1, rw), "k_a": k_a[0].reshape(1, rw),
        "decay_w0": decay_w0[0], "iclr_a0": iclr_a0[0], "lora_w": lora_w.astype(BF16),
        "r_k": r_k[0].reshape(1, rw),
        "lnx_w": lnx_w[0].reshape(1, rw), "lnx_b": lnx_b[0].reshape(1, rw),
        "w_branch_attn": w_branch_attn[0].astype(BF16), "w_branch_rwkv": w_branch_rwkv[0].astype(BF16),
        "w_out": w_out[0].astype(BF16), "norm2_g": norm2_g[0].reshape(1, d),
        "w_mlp_up": w_mlp_up[0].astype(BF16), "w_mlp_down": w_mlp_down[0].astype(BF16),
        "norm_f_g": norm_f_g.reshape(1, d),
    }

    rows = -(-(b + 1) // 8) * 8
    cc = jnp.zeros((rows, d), F32).at[:b].set(c).at[b].set(c_ctx)
    mod = _ada_mod(cc, w_ada[0], b_ada[0]).reshape(rows, -1, d)

    q, kd, vd, rkv, lora, gate = _in_proj(x, mod, b, norm1_g[0], w_lat, widths_lat, _rope_tables(l),
                                          latent=True, tile=256)
    kxd, vxd, rkv_c, lora_c = _in_proj(ctx, mod, b, norm1_g[0], w_ctx, widths_ctx, None,
                                       latent=False, tile=256)
    ya = _attention(sink[0], q, kd, vd, kxd, vxd)

    prep_c = _wkv_prep(rkv_c, lora_c, prm, latent=False, tile=256)
    z_ctx = _wkv_scan(prep_c, jnp.zeros((b, 2, rw // LANES, LANES, LANES), F32), emit_y=False)
    prep = _wkv_prep(rkv, lora, prm, latent=True, tile=256)
    yf, yr = _wkv_scan(prep, z_ctx, emit_y=True)
    bonus, g = prep[8], prep[9]

    return _merge_mlp(x, mod, ya, yf, yr, bonus, g, gate, prm, tile=256)
```

```python
import functools
import math

import jax
import jax.numpy as jnp
from jax import lax
from jax.experimental import pallas as pl
from jax.experimental.pallas import tpu as pltpu

F32 = jnp.float32
BF16 = jnp.bfloat16

GRID_W = 64
HEAD_DIM = 64
Q_PER_KV = 4
ATTN_BLOCK = 128
ATTN_QB = 2
ROPE_BASE = 10000.0
NORM_EPS = 1e-6
LNX_EPS = 1e-5 * HEAD_DIM
DECAY_LORA, ICLR_LORA, GATE_LORA = 32, 32, 96
LORA_PAD = 256
CHUNK = 64
LANES = 128
PAIR = LANES // HEAD_DIM
NEG = -1e30
VMEM_LIMIT = 56 * 1024 * 1024


def _dot(a, b):
    return jnp.dot(a, b, preferred_element_type=F32)


def _dot_nt(a, b):
    return lax.dot_general(a, b, (((1,), (1,)), ((), ())), preferred_element_type=F32)


def _dot_tn(a, b):
    return lax.dot_general(a, b, (((0,), (0,)), ((), ())), preferred_element_type=F32)


def _iota(shape, dim):
    return lax.broadcasted_iota(jnp.int32, shape, dim)


def _head_ones(width):
    r = _iota((width, width), 0) // HEAD_DIM
    c = _iota((width, width), 1) // HEAD_DIM
    return jnp.where(r == c, 1.0, 0.0).astype(BF16)


def _head_sum(x, ones_bd):
    hi = x.astype(BF16)
    lo = (x - hi.astype(F32)).astype(BF16)
    return _dot(hi, ones_bd) + _dot(lo, ones_bd)


def _rmsnorm(x, g):
    ms = jnp.mean(x * x, axis=-1, keepdims=True)
    return x * lax.rsqrt(ms + NORM_EPS) * g


def _params(*sem):
    return pltpu.CompilerParams(dimension_semantics=sem, vmem_limit_bytes=VMEM_LIMIT)


def _const_spec(shape):
    nd = len(shape)
    return pl.BlockSpec(shape, lambda *_: (0,) * nd, pipeline_mode=pl.Buffered(1))


def _ada_kernel(c_ref, w_ref, b_ref, o_ref):
    c = c_ref[...]
    s = c * jax.nn.sigmoid(c)
    o_ref[...] = _dot(s.astype(BF16), w_ref[...].astype(BF16)) + b_ref[...]


def _ada_mod(cc, w_ada, b_ada):
    rows, d = cc.shape
    n = w_ada.shape[1]
    return pl.pallas_call(
        _ada_kernel,
        grid=(n // d,),
        in_specs=[pl.BlockSpec((rows, d), lambda j: (0, 0)),
                  pl.BlockSpec((d, d), lambda j: (0, j)),
                  pl.BlockSpec((1, d), lambda j: (0, j))],
        out_specs=pl.BlockSpec((rows, d), lambda j: (0, j)),
        out_shape=jax.ShapeDtypeStruct((rows, n), F32),
        compiler_params=_params("arbitrary"),
        name="ada_mod",
    )(cc, w_ada, b_ada.reshape(1, n))


def _rope(x, cos_t, sin_t):
    w = x.shape[1]
    half = HEAD_DIM // 2
    first = (_iota(x.shape, 1) % HEAD_DIM) < half
    swapped = jnp.where(first, pltpu.roll(x, w - half, 1), pltpu.roll(x, half, 1))
    reps = w // LANES
    c = jnp.concatenate([cos_t] * reps, axis=1)
    s = jnp.concatenate([sin_t] * reps, axis=1)
    return x * c + swapped * s


def _inproj_kernel(*refs, latent, widths):
    if latent:
        (x_ref, mod_ref, g_ref, w_ref, cos_ref, sin_ref,
         q_ref, k_ref, v_ref, rkv_ref, lora_ref, gate_ref) = refs
    else:
        x_ref, mod_ref, g_ref, w_ref, k_ref, v_ref, rkv_ref, lora_ref = refs
    x = x_ref[0]
    h = _rmsnorm(x, g_ref[...]) * (1.0 + mod_ref[0, 1:2, :]) + mod_ref[0, 0:1, :]
    hb = h.astype(BF16)
    off = 0

    def seg(name):
        nonlocal off
        lo = off
        off += widths[name]
        return _dot(hb, w_ref[:, lo:off])

    if latent:
        cos_t, sin_t = cos_ref[...], sin_ref[...]
        q = seg("q")
        q_ref[0] = (_rope(q, cos_t, sin_t) * (HEAD_DIM ** -0.5)).astype(BF16)
        k_ref[0] = _rope(seg("k"), cos_t, sin_t).astype(BF16)
    else:
        k_ref[0] = seg("k").astype(BF16)
    v_ref[0] = seg("v").astype(BF16)
    rkv_ref[0] = seg("rkv")
    lora_ref[0] = seg("lora")
    if latent:
        gate_ref[0] = jax.nn.sigmoid(seg("gate")).astype(BF16)


def _in_proj(x, mod, mod_row, norm_g, w_packed, widths, tables, *, latent, tile):
    b, l, d = x.shape
    nt = l // tile
    n = w_packed.shape[1]
    if latent:
        mod_map = lambda i, t: (i, 0, 0)
    else:
        mod_map = lambda i, t: (mod_row, 0, 0)
    tok = lambda w: pl.BlockSpec((1, tile, w), lambda i, t: (i, t, 0))
    in_specs = [tok(d),
                pl.BlockSpec((1,) + mod.shape[1:], mod_map),
                _const_spec((1, d)),
                _const_spec((d, n))]
    args = [x, mod, norm_g.reshape(1, d), w_packed]
    out_specs, out_shape = [], []

    def out(w, dt):
        out_specs.append(tok(w))
        out_shape.append(jax.ShapeDtypeStruct((b, l, w), dt))

    if latent:
        in_specs += [pl.BlockSpec((tile, LANES), lambda i, t: (t, 0))] * 2
        args += list(tables)
        out(widths["q"], BF16)
    out(widths["k"], BF16)
    out(widths["v"], BF16)
    out(widths["rkv"], F32)
    out(widths["lora"], F32)
    if latent:
        out(widths["gate"], BF16)
    return pl.pallas_call(
        functools.partial(_inproj_kernel, latent=latent, widths=widths),
        grid=(b, nt),
        in_specs=in_specs,
        out_specs=out_specs,
        out_shape=out_shape,
        compiler_params=_params("parallel", "parallel"),
        name="in_proj_latent" if latent else "in_proj_context",
    )(*args)


def _attn_kernel(sink_ref, q_ref, kp_ref, kc_ref, kn_ref, vp_ref, vc_ref, vn_ref,
                 kx_ref, vx_ref, o_ref, *, n_kv):
    i = pl.program_id(1)
    last = pl.num_programs(1) - 1
    blk = ATTN_BLOCK
    qi = _iota((blk, blk), 0)
    kj = _iota((blk, blk), 1)
    left = _iota((blk, LANES), 1) < HEAD_DIM

    def key_block(refs, j, gs):
        ref_p, ref_c, ref_n = refs
        if j < 0:
            return ref_p[0, :, gs]
        if j >= ATTN_QB:
            return ref_n[0, :, gs]
        return ref_c[0, j * blk:(j + 1) * blk, gs]

    for qb in range(ATTN_QB):
        lo_ok = kj >= qi
        hi_ok = kj <= qi
        if qb == 0:
            lo_ok = lo_ok & (i > 0)
        if qb == ATTN_QB - 1:
            hi_ok = hi_ok & (i < last)
        bias_lo = jnp.concatenate([jnp.where(lo_ok, 0.0, NEG)] * Q_PER_KV, axis=0)
        bias_hi = jnp.concatenate([jnp.where(hi_ok, 0.0, NEG)] * Q_PER_KV, axis=0)
        rows = slice(qb * blk, (qb + 1) * blk)
        q = q_ref[0, rows, :].astype(F32)
        for g in range(n_kv):
            gs = slice(g * LANES, (g + 1) * LANES)
            kcat = jnp.concatenate([key_block((kp_ref, kc_ref, kn_ref), qb + j, gs) for j in (-1, 0, 1)]
                                   + [kx_ref[0, :, gs]], axis=0)
            vcat = jnp.concatenate([key_block((vp_ref, vc_ref, vn_ref), qb + j, gs) for j in (-1, 0, 1)]
                                   + [vx_ref[0, :, gs]], axis=0)
            heads = range(g * Q_PER_KV, (g + 1) * Q_PER_KV)
            qs, sinks = [], []
            for hd in heads:
                qp = q[:, (hd // PAIR) * LANES:(hd // PAIR + 1) * LANES]
                keep = left if hd % PAIR == 0 else jnp.logical_not(left)
                qs.append(jnp.where(keep, qp, 0.0).astype(BF16))
                sinks.append(jnp.full((blk, 1), sink_ref[hd], F32))
            qs = jnp.concatenate(qs, axis=0)
            sink = jnp.concatenate(sinks, axis=0)
            s = _dot_nt(qs, kcat)
            s = jnp.concatenate([s[:, :blk] + bias_lo, s[:, blk:2 * blk], s[:, 2 * blk:3 * blk] + bias_hi,
                                 s[:, 3 * blk:]], axis=1)
            m = jnp.maximum(jnp.max(s, axis=1, keepdims=True), sink)
            p = jnp.exp(s - m)
            den = jnp.sum(p, axis=1, keepdims=True) + jnp.exp(sink - m)
            o = _dot(p.astype(BF16), vcat) / den
            for hd in heads[::PAIR]:
                j = hd - g * Q_PER_KV
                pair = jnp.where(left, o[j * blk:(j + 1) * blk], o[(j + 1) * blk:(j + 2) * blk])
                col = (hd // PAIR) * LANES
                o_ref[0, rows, col:col + LANES] = pair.astype(BF16)


def _attention(sink, q, kd, vd, kxd, vxd):
    b, l, wq = q.shape
    wk = kd.shape[2]
    lc = kxd.shape[1]
    nb = l // ATTN_BLOCK
    span = ATTN_QB * ATTN_BLOCK
    blk = lambda w, f: pl.BlockSpec((1, ATTN_BLOCK, w), f)
    prev = lambda bi, i: (bi, jnp.maximum(i * ATTN_QB - 1, 0), 0)
    nxt = lambda bi, i: (bi, jnp.minimum((i + 1) * ATTN_QB, nb - 1), 0)
    cur = lambda w: pl.BlockSpec((1, span, w), lambda bi, i: (bi, i, 0))
    ctx = pl.BlockSpec((1, lc, wk), lambda bi, i: (bi, 0, 0))
    return pl.pallas_call(
        functools.partial(_attn_kernel, n_kv=wk // LANES),
        grid=(b, l // span),
        in_specs=[pl.BlockSpec(memory_space=pltpu.SMEM),
                  cur(wq),
                  blk(wk, prev), cur(wk), blk(wk, nxt),
                  blk(wk, prev), cur(wk), blk(wk, nxt),
                  ctx, ctx],
        out_specs=cur(wq),
        out_shape=jax.ShapeDtypeStruct((b, l, wq), BF16),
        compiler_params=_params("parallel", "parallel"),
        name="attention",
    )(sink, q, kd, kd, kd, vd, vd, vd, kxd, vxd)


def _conv3(x, prev_row, next_row, w):
    n = x.shape[0]
    row = _iota(x.shape, 0)
    xm = jnp.where(row == 0, prev_row, pltpu.roll(x, 1, 0))
    xp = jnp.where(row == n - 1, next_row, pltpu.roll(x, n - 1, 0))
    return xm * w[0:1] + x * w[1:2] + xp * w[2:3]


def _pair_masks():
    n = 2 * CHUNK
    row = _iota((n, n), 0)
    lane = _iota((n, n), 1)
    top, left = row < CHUNK, lane < CHUNK
    return {"row": row % CHUNK, "lane": lane % CHUNK, "top": top, "left": left, "bd": top == left,
            "left_h": _iota((CHUNK, n), 1) < CHUNK}


def _stack(a, b):
    return jnp.concatenate([a, b], axis=0)


def _fold(x, m):
    return jnp.where(m["left_h"], x[:CHUNK], x[CHUNK:])


def _chunk_local(inst, m, reverse):
    bf = lambda x: x.astype(BF16)
    before = (m["lane"] < m["row"]) if not reverse else (m["lane"] > m["row"])
    mask_a = before | (jnp.logical_not(m["top"]) & (m["lane"] == m["row"]))
    left, bd = m["left"], m["bd"]
    lhs = [_stack(i["at"], i["rt"]) for i in inst]
    a0 = [jnp.where(mask_a, _dot_nt(bf(jnp.where(left, l, 0.0)), bf(_stack(i["bt"], i["kt"]))), 0.0)
          for l, i in zip(lhs, inst)]
    a1 = [jnp.where(mask_a, _dot_nt(bf(jnp.where(left, 0.0, l)), bf(_stack(i["kt"], i["bt"]))), 0.0)
          for l, i in zip(lhs, inst)]
    a_top = [_stack(x[:CHUNK], y[:CHUNK]) for x, y in zip(a0, a1)]
    a_bot = [_stack(x[CHUNK:], y[CHUNK:]) for x, y in zip(a0, a1)]
    nmat = [bf(jnp.where(bd, t, 0.0)) for t in a_top]
    v_sw = [bf(pltpu.roll(i["v"], CHUNK, 1)) for i in inst]
    av = [_dot(bf(jnp.where(bd, 0.0, t)), _stack(s, s)) for t, s in zip(a_top, v_sw)]
    x = [jnp.where(bd, _stack(i["at"], i["at"]), a) for i, a in zip(inst, av)]
    steps = CHUNK.bit_length() - 1
    for it in range(steps):
        x = [xi + _dot(n, bf(xi)) for xi, n in zip(x, nmat)]
        if it < steps - 1:
            nmat = [bf(_dot(n, n)) for n in nmat]
    vb = [bf(i["v"]) for i in inst]
    hst = [_dot(bf(jnp.where(bd, 0.0, b)), _stack(s, s)) for b, s in zip(a_bot, vb)]
    kvf = [_dot_tn(s, bf(i["kt"])) for s, i in zip(vb, inst)]
    return [{"pm": _fold(xi, m),
             "qm": pltpu.roll(jnp.where(m["left_h"], xi[CHUNK:], xi[:CHUNK]), CHUNK, 1),
             "arb": _fold(b, m), "hm": _fold(h, m), "kv": _fold(k, m)}
            for xi, b, h, k in zip(x, a_bot, hst, kvf)]


def _prep_kernel(*refs, latent, width):
    (rkv_ref, rkv_p, rkv_n, lora_ref, lora_p, lora_n, cw_ref, cwl_ref, kk_ref, ka_ref,
     w0_ref, a0_ref, wl_ref, rk_ref) = refs[:14]
    outs = refs[14:]
    out_refs = dict(zip(("pm", "qm", "rt", "bt", "arb", "hm", "kv"), outs[:7]))
    wc_ref = outs[7]
    if latent:
        bonus_ref, g_ref = outs[8:]
    t = pl.program_id(1)
    nt = pl.num_programs(1)
    tile = rkv_ref.shape[1]
    w = width
    has_prev = (t > 0).astype(F32)
    has_next = (t < nt - 1).astype(F32)
    u = _conv3(rkv_ref[0], rkv_p[0, 7:8, :] * has_prev, rkv_n[0, 0:1, :] * has_next, cw_ref[...])
    ul = _conv3(lora_ref[0], lora_p[0, 7:8, :] * has_prev, lora_n[0, 0:1, :] * has_next, cwl_ref[...])
    r, k, v = u[:, :w], u[:, w:2 * w], u[:, 2 * w:]
    ones_bd = _head_ones(w)

    kk = k * kk_ref[...]
    kk = kk * lax.rsqrt(jnp.maximum(_head_sum(kk * kk, ones_bd), 1e-24))

    lane = _iota(ul.shape, 1)
    lin = jnp.where(lane < DECAY_LORA, jnp.tanh(ul),
                    jnp.where(lane < DECAY_LORA + ICLR_LORA, ul, jax.nn.sigmoid(ul)))
    proj = _dot(lin.astype(BF16), wl_ref[...])

    tr = _iota((tile, tile), 0)
    tc = _iota((tile, tile), 1)
    same = (tr // CHUNK) == (tc // CHUNK)
    tri = (jnp.where(same & (tc <= tr), 1.0, 0.0).astype(BF16),
           jnp.where(same & (tc >= tr), 1.0, 0.0).astype(BF16))

    def exact_dot(m, x):
        h1 = x.astype(BF16)
        r1 = x - h1.astype(F32)
        h2 = r1.astype(BF16)
        h3 = (r1 - h2.astype(F32)).astype(BF16)
        return _dot(m, h1) + _dot(m, h2) + _dot(m, h3)

    masks = _pair_masks()
    k_sum = None
    for d in range(2):
        z = w0_ref[d:d + 1, :] + proj[:, d * w:(d + 1) * w]
        lw = -math.exp(-0.5) * jax.nn.sigmoid(z)
        a = jax.nn.sigmoid(a0_ref[d:d + 1, :] + proj[:, (2 + d) * w:(3 + d) * w])
        kd = k * (1.0 + (a - 1.0) * ka_ref[...])
        k_sum = kd if k_sum is None else k_sum + kd
        cum = exact_dot(tri[d], lw)
        e_neg = jnp.exp(-cum)
        full = {"at": -kk * jnp.exp(cum - lw), "rt": r * jnp.exp(cum), "bt": kk * a * e_neg,
                "kt": kd * e_neg, "v": v}
        out_refs["rt"][d, 0] = full["rt"].astype(BF16)
        out_refs["bt"][d, 0] = full["bt"].astype(BF16)
        for j in range(tile // CHUNK):
            edge = (j + 1) * CHUNK - 1 if d == 0 else j * CHUNK
            wc_ref[0, j, d:d + 1, :] = jnp.exp(cum[edge:edge + 1, :])
        where = [(slice(j * CHUNK, (j + 1) * CHUNK), slice(p * LANES, (p + 1) * LANES))
                 for j in range(tile // CHUNK) for p in range(w // LANES)]
        inst = [{name: val[rs, ls] for name, val in full.items()} for rs, ls in where]
        for (rs, ls), res in zip(where, _chunk_local(inst, masks, reverse=(d == 1))):
            for name, val in res.items():
                out_refs[name][d, 0, rs, ls] = val.astype(BF16)
    if latent:
        bonus_ref[0] = _head_sum(r * k_sum * rk_ref[...], ones_bd) * v
        g_ref[0] = proj[:, 4 * w:5 * w].astype(BF16)


def _wkv_prep(rkv, lora, prm, *, latent, tile):
    b, l, w3 = rkv.shape
    w = w3 // 3
    nt = l // tile
    n8 = l // 8
    tok = lambda wd: pl.BlockSpec((1, tile, wd), lambda i, t: (i, t, 0))
    prev = lambda wd: pl.BlockSpec((1, 8, wd), lambda i, t: (i, jnp.maximum(t * (tile // 8) - 1, 0), 0))
    nxt = lambda wd: pl.BlockSpec((1, 8, wd), lambda i, t: (i, jnp.minimum((t + 1) * (tile // 8), n8 - 1), 0))
    wl = lora.shape[2]
    in_specs = [tok(w3), prev(w3), nxt(w3), tok(wl), prev(wl), nxt(wl)]
    consts = [prm["conv_rkv"], prm["conv_lora"], prm["k_k"], prm["k_a"], prm["decay_w0"],
              prm["iclr_a0"], prm["lora_w"], prm["r_k"]]
    in_specs += [_const_spec(c.shape) for c in consts]
    dirtok = pl.BlockSpec((2, 1, tile, w), lambda i, t: (0, i, t, 0))
    out_specs = [dirtok] * 7 + [pl.BlockSpec((1, tile // CHUNK, 2, w), lambda i, t: (i, t, 0, 0))]
    out_shape = [jax.ShapeDtypeStruct((2, b, l, w), BF16)] * 7 + [
        jax.ShapeDtypeStruct((b, l // CHUNK, 2, w), F32)]
    if latent:
        out_specs += [tok(w), tok(w)]
        out_shape += [jax.ShapeDtypeStruct((b, l, w), F32), jax.ShapeDtypeStruct((b, l, w), BF16)]
    return pl.pallas_call(
        functools.partial(_prep_kernel, latent=latent, width=w),
        grid=(b, nt),
        in_specs=in_specs,
        out_specs=out_specs,
        out_shape=out_shape,
        compiler_params=_params("parallel", "parallel"),
        name="wkv_prep_latent" if latent else "wkv_prep_context",
    )(rkv, rkv, rkv, lora, lora, lora, *consts)


def _wkv_kernel(*refs, emit_y, n_pairs):
    names = ("pm", "qm", "rt", "bt", "arb", "hm", "kv", "wc")
    dir_refs = (dict(zip(names, refs[0:8])), dict(zip(names, refs[8:16])))
    z0_ref = refs[16]
    if emit_y:
        y_refs = refs[17:19]
        z_scr = refs[19]
    else:
        zfin_ref = refs[17]
        z_scr = refs[18]
    c = pl.program_id(0)

    @pl.when(c == 0)
    def _():
        z_scr[...] = z0_ref[...]

    m = _pair_masks()
    bd, left_h = m["bd"], m["left_h"]

    def unfold(x):
        x = x.astype(F32)
        return _stack(jnp.where(left_h, x, 0.0), jnp.where(left_h, 0.0, x))

    tiles = [(i, d, p, slice(p * LANES, (p + 1) * LANES))
             for i in range(z_scr.shape[0]) for d in range(2) for p in range(n_pairs)]
    ld = lambda name: [dir_refs[d][name][0, i, :, sl] for i, d, _, sl in tiles]
    s_old = [z_scr[i, d, p] for i, d, p, _ in tiles]
    sb = [s.astype(BF16) for s in s_old]
    if emit_y:
        ur = [_dot_nt(_stack(pm, rt), s) for pm, rt, s in zip(ld("pm"), ld("rt"), sb)]
    else:
        ur = [_dot_nt(pm, s) for pm, s in zip(ld("pm"), sb)]
    ub = [(x[:CHUNK] + q.astype(F32)).astype(BF16) for x, q in zip(ur, ld("qm"))]
    inc = [_dot_tn(u, bt) for u, bt in zip(ub, ld("bt"))]
    if emit_y:
        yst = [_dot(unfold(a).astype(BF16), _stack(u, u)) for a, u in zip(ld("arb"), ub)]
        for (i, d, _, sl), x, ys, h in zip(tiles, ur, yst, ld("hm")):
            y_refs[d][i, :, sl] = x[CHUNK:] + _fold(ys, m) + h.astype(F32)
    for (i, d, p, sl), s, dz, kv in zip(tiles, s_old, inc, ld("kv")):
        z_scr[i, d, p] = (s + jnp.where(bd, dz, 0.0) + unfold(kv)) * dir_refs[d]["wc"][i, 0, d:d + 1, sl]

    if not emit_y:
        @pl.when(c == pl.num_programs(0) - 1)
        def _():
            zfin_ref[...] = z_scr[...]


def _wkv_scan(prep, z0, *, emit_y):
    wc = prep[7]
    _, b, l, w = prep[0].shape
    nc = l // CHUNK
    n_pairs = w // LANES
    fwd = lambda c: c
    rev = lambda c: nc - 1 - c
    in_specs, args = [], []
    for d, cm in enumerate((fwd, rev)):
        for arr in prep[:7]:
            in_specs.append(pl.BlockSpec((1, b, CHUNK, w), lambda c, d=d, cm=cm: (d, 0, cm(c), 0)))
            args.append(arr)
        in_specs.append(pl.BlockSpec((b, 1, 2, w), lambda c, cm=cm: (0, cm(c), 0, 0)))
        args.append(wc)
    zshape = (b, 2, n_pairs, LANES, LANES)
    zspec = pl.BlockSpec(zshape, lambda c: (0, 0, 0, 0, 0))
    in_specs.append(zspec)
    args.append(z0)
    if emit_y:
        out_specs = [pl.BlockSpec((b, CHUNK, w), lambda c: (0, c, 0)),
                     pl.BlockSpec((b, CHUNK, w), lambda c: (0, nc - 1 - c, 0))]
        out_shape = [jax.ShapeDtypeStruct((b, l, w), F32)] * 2
    else:
        out_specs = zspec
        out_shape = jax.ShapeDtypeStruct(zshape, F32)
    return pl.pallas_call(
        functools.partial(_wkv_kernel, emit_y=emit_y, n_pairs=n_pairs),
        grid=(nc,),
        in_specs=in_specs,
        out_specs=out_specs,
        out_shape=out_shape,
        scratch_shapes=[pltpu.VMEM(zshape, F32)],
        compiler_params=_params("arbitrary"),
        name="wkv_scan_latent" if emit_y else "wkv_scan_context",
    )(*args)


def _merge_kernel(x_ref, mod_ref, ya_ref, yf_ref, yr_ref, bonus_ref, g_ref, gate_ref,
                  lnw_ref, lnb_ref, wba_ref, wbr_ref, wo_ref, n2_ref, wu_ref, wd_ref, nf_ref,
                  o_ref, *, ff_chunk):
    x = x_ref[0]
    d = x.shape[1]
    mod = lambda j: mod_ref[0, j:j + 1, :]
    y = yf_ref[0] + yr_ref[0]
    ones_bd = _head_ones(y.shape[1])
    inv = 1.0 / HEAD_DIM
    mu = _head_sum(y, ones_bd) * inv
    yc = y - mu
    var = _head_sum(yc * yc, ones_bd) * inv
    yn = yc * lax.rsqrt(var + LNX_EPS)
    yr = (yn * lnw_ref[...] + lnb_ref[...] + bonus_ref[0]) * g_ref[0].astype(F32)
    gate = gate_ref[0].astype(F32)
    merged = gate[:, :d] * _dot(ya_ref[0], wba_ref[...]) + gate[:, d:] * _dot(yr.astype(BF16), wbr_ref[...])
    x1 = x + mod(2) * _dot(merged.astype(BF16), wo_ref[...])
    h2 = (_rmsnorm(x1, n2_ref[...]) * (1.0 + mod(4)) + mod(3)).astype(BF16)
    acc = jnp.zeros_like(x1)
    for j in range(wu_ref.shape[1] // ff_chunk):
        cs = slice(j * ff_chunk, (j + 1) * ff_chunk)
        up = jnp.maximum(_dot(h2, wu_ref[:, cs]), 0.0)
        acc = acc + _dot((up * up).astype(BF16), wd_ref[cs, :])
    x2 = x1 + mod(5) * acc
    o_ref[0] = _rmsnorm(x2, nf_ref[...])


def _merge_mlp(x, mod, ya, yf, yr, bonus, g, gate, prm, *, tile):
    b, l, d = x.shape
    tok = lambda arr: pl.BlockSpec((1, tile, arr.shape[2]), lambda i, t: (i, t, 0))
    consts = [prm["lnx_w"], prm["lnx_b"], prm["w_branch_attn"], prm["w_branch_rwkv"], prm["w_out"],
              prm["norm2_g"], prm["w_mlp_up"], prm["w_mlp_down"], prm["norm_f_g"]]
    toks = [ya, yf, yr, bonus, g, gate]
    return pl.pallas_call(
        functools.partial(_merge_kernel, ff_chunk=min(1024, prm["w_mlp_up"].shape[1])),
        grid=(b, l // tile),
        in_specs=[tok(x), pl.BlockSpec((1,) + mod.shape[1:], lambda i, t: (i, 0, 0))]
        + [tok(a) for a in toks] + [_const_spec(c.shape) for c in consts],
        out_specs=tok(x),
        out_shape=jax.ShapeDtypeStruct(x.shape, x.dtype),
        compiler_params=_params("parallel", "parallel"),
        name="merge_mlp",
    )(x, mod, *toks, *consts)


def _rope_tables(l):
    n_freq = HEAD_DIM // 4
    inv_freq = jnp.power(ROPE_BASE, -jnp.arange(n_freq, dtype=F32) / n_freq)
    rows = l // GRID_W
    row = jnp.repeat(jnp.arange(rows, dtype=F32), GRID_W)
    col = jnp.tile(jnp.arange(GRID_W, dtype=F32), rows)
    ang = jnp.concatenate([row[:, None] * inv_freq, col[:, None] * inv_freq], axis=-1)
    cos, sin = jnp.cos(ang), jnp.sin(ang)
    reps = LANES // HEAD_DIM
    return (jnp.tile(jnp.concatenate([cos, cos], axis=1), (1, reps)),
            jnp.tile(jnp.concatenate([-sin, sin], axis=1), (1, reps)))


def _dup_heads(w, n_heads):
    cols = [w[:, h * HEAD_DIM:(h + 1) * HEAD_DIM] for h in range(n_heads)]
    return jnp.concatenate([c for c in cols for _ in range(PAIR)], axis=1)


def _pad_cols(w, width):
    return jnp.pad(w, ((0, 0), (0, width - w.shape[1])))


def kernel(x, c, ctx, c_ctx, w_ada, b_ada, norm1_g, w_in, sink, conv_w, decay_w0, decay_w2, iclr_a0, iclr_a2, gate_g2, k_k, k_a, r_k, lnx_w, lnx_b, w_branch_attn, w_branch_rwkv, w_out, norm2_g, w_mlp_up, w_mlp_down, norm_f_g):
    assert w_in.shape[0] == 1, "single-layer block: context tokens are read, never updated"
    b, l, d = x.shape
    attn_w = w_branch_attn.shape[1]
    rw = w_branch_rwkv.shape[1]
    n_q = attn_w // HEAD_DIM
    n_kv = n_q // Q_PER_KV
    kv_w = n_kv * HEAD_DIM
    assert n_kv * PAIR * HEAD_DIM == n_kv * LANES and rw % LANES == 0 and l % 256 == 0 and ctx.shape[1] % 256 == 0

    w = w_in[0]
    o_k, o_v, o_r = attn_w, attn_w + kv_w, attn_w + 2 * kv_w
    o_l = o_r + 3 * rw
    o_g = o_l + DECAY_LORA + ICLR_LORA + GATE_LORA
    seg_k = _dup_heads(w[:, o_k:o_v], n_kv)
    seg_v = _dup_heads(w[:, o_v:o_r], n_kv)
    seg_rkv = w[:, o_r:o_l]
    seg_lora = _pad_cols(w[:, o_l:o_g], LORA_PAD)
    w_ctx = jnp.concatenate([seg_k, seg_v, seg_rkv, seg_lora], axis=1).astype(BF16)
    w_lat = jnp.concatenate([w[:, :o_k].astype(BF16), w_ctx, w[:, o_g:].astype(BF16)], axis=1)
    widths_ctx = {"k": seg_k.shape[1], "v": seg_v.shape[1], "rkv": 3 * rw, "lora": LORA_PAD}
    widths_lat = {"q": attn_w, **widths_ctx, "gate": 2 * d}

    cw = conv_w[0]
    lora_w = jnp.zeros((LORA_PAD, 5 * rw), F32)
    lora_w = lora_w.at[:DECAY_LORA, :rw].set(decay_w2[0, 0]).at[:DECAY_LORA, rw:2 * rw].set(decay_w2[0, 1])
    r1 = DECAY_LORA + ICLR_LORA
    lora_w = lora_w.at[DECAY_LORA:r1, 2 * rw:3 * rw].set(iclr_a2[0, 0]).at[DECAY_LORA:r1, 3 * rw:4 * rw].set(iclr_a2[0, 1])
    lora_w = lora_w.at[r1:r1 + GATE_LORA, 4 * rw:].set(gate_g2[0])
    prm = {
        "conv_rkv": cw[:, :3 * rw], "conv_lora": _pad_cols(cw[:, 3 * rw:], LORA_PAD),
        "k_k": k_k[0].reshape(1, rw), "k_a": k_a[0].reshape(1, rw),
        "decay_w0": decay_w0[0], "iclr_a0": iclr_a0[0], "lora_w": lora_w.astype(BF16),
        "r_k": r_k[0].reshape(1, rw),
        "lnx_w": lnx_w[0].reshape(1, rw), "lnx_b": lnx_b[0].reshape(1, rw),
        "w_branch_attn": w_branch_attn[0].astype(BF16), "w_branch_rwkv": w_branch_rwkv[0].astype(BF16),
        "w_out": w_out[0].astype(BF16), "norm2_g": norm2_g[0].reshape(1, d),
        "w_mlp_up": w_mlp_up[0].astype(BF16), "w_mlp_down": w_mlp_down[0].astype(BF16),
        "norm_f_g": norm_f_g.reshape(1, d),
    }

    rows = -(-(b + 1) // 8) * 8
    cc = jnp.zeros((rows, d), F32).at[:b].set(c).at[b].set(c_ctx)
    mod = _ada_mod(cc, w_ada[0], b_ada[0]).reshape(rows, -1, d)

    q, kd, vd, rkv, lora, gate = _in_proj(x, mod, b, norm1_g[0], w_lat, widths_lat, _rope_tables(l),
                                          latent=True, tile=512)
    kxd, vxd, rkv_c, lora_c = _in_proj(ctx, mod, b, norm1_g[0], w_ctx, widths_ctx, None,
                                       latent=False, tile=256)
    ya = _attention(sink[0], q, kd, vd, kxd, vxd)

    prep_c = _wkv_prep(rkv_c, lora_c, prm, latent=False, tile=256)
    z_ctx = _wkv_scan(prep_c, jnp.zeros((b, 2, rw // LANES, LANES, LANES), F32), emit_y=False)
    prep = _wkv_prep(rkv, lora, prm, latent=True, tile=256)
    yf, yr = _wkv_scan(prep, z_ctx, emit_y=True)
    bonus, g = prep[8], prep[9]

    return _merge_mlp(x, mod, ya, yf, yr, bonus, g, gate, prm, tile=512)
```

```python
import functools
import math

import jax
import jax.numpy as jnp
from jax import lax
from jax.experimental import pallas as pl
from jax.experimental.pallas import tpu as pltpu

F32 = jnp.float32
BF16 = jnp.bfloat16

GRID_W = 64
HEAD_DIM = 64
Q_PER_KV = 4
ATTN_BLOCK = 128
ATTN_QB = 4
ROPE_BASE = 10000.0
NORM_EPS = 1e-6
LNX_EPS = 1e-5 * HEAD_DIM
DECAY_LORA, ICLR_LORA, GATE_LORA = 32, 32, 96
LORA_PAD = 256
CHUNK = 64
PREP_GROUP = 32
LANES = 128
PAIR = LANES // HEAD_DIM
NEG = -1e30
VMEM_LIMIT = 56 * 1024 * 1024


def _dot(a, b):
    return jnp.dot(a, b, preferred_element_type=F32)


def _dot_nt(a, b):
    return lax.dot_general(a, b, (((1,), (1,)), ((), ())), preferred_element_type=F32)


def _dot_tn(a, b):
    return lax.dot_general(a, b, (((0,), (0,)), ((), ())), preferred_element_type=F32)


def _iota(shape, dim):
    return lax.broadcasted_iota(jnp.int32, shape, dim)


def _head_ones(width):
    r = _iota((width, width), 0) // HEAD_DIM
    c = _iota((width, width), 1) // HEAD_DIM
    return jnp.where(r == c, 1.0, 0.0).astype(BF16)


def _head_sum(x, ones_bd):
    hi = x.astype(BF16)
    lo = (x - hi.astype(F32)).astype(BF16)
    return _dot(hi, ones_bd) + _dot(lo, ones_bd)


def _rmsnorm(x, g):
    ms = jnp.mean(x * x, axis=-1, keepdims=True)
    return x * lax.rsqrt(ms + NORM_EPS) * g


def _params(*sem):
    return pltpu.CompilerParams(dimension_semantics=sem, vmem_limit_bytes=VMEM_LIMIT)


def _const_spec(shape):
    nd = len(shape)
    return pl.BlockSpec(shape, lambda *_: (0,) * nd, pipeline_mode=pl.Buffered(1))


def _ada_kernel(c_ref, w_ref, b_ref, o_ref):
    c = c_ref[...]
    s = c * jax.nn.sigmoid(c)
    o_ref[...] = _dot(s.astype(BF16), w_ref[...].astype(BF16)) + b_ref[...]


def _ada_mod(cc, w_ada, b_ada):
    rows, d = cc.shape
    n = w_ada.shape[1]
    return pl.pallas_call(
        _ada_kernel,
        grid=(n // d,),
        in_specs=[pl.BlockSpec((rows, d), lambda j: (0, 0)),
                  pl.BlockSpec((d, d), lambda j: (0, j)),
                  pl.BlockSpec((1, d), lambda j: (0, j))],
        out_specs=pl.BlockSpec((rows, d), lambda j: (0, j)),
        out_shape=jax.ShapeDtypeStruct((rows, n), F32),
        compiler_params=_params("arbitrary"),
        name="ada_mod",
    )(cc, w_ada, b_ada.reshape(1, n))


def _rope(x, cos_t, sin_t):
    w = x.shape[1]
    half = HEAD_DIM // 2
    first = (_iota(x.shape, 1) % HEAD_DIM) < half
    swapped = jnp.where(first, pltpu.roll(x, w - half, 1), pltpu.roll(x, half, 1))
    reps = w // LANES
    c = jnp.concatenate([cos_t] * reps, axis=1)
    s = jnp.concatenate([sin_t] * reps, axis=1)
    return x * c + swapped * s


def _inproj_kernel(*refs, latent, widths):
    if latent:
        (x_ref, mod_ref, g_ref, w_ref, cos_ref, sin_ref,
         q_ref, k_ref, v_ref, rkv_ref, lora_ref, gate_ref) = refs
    else:
        x_ref, mod_ref, g_ref, w_ref, k_ref, v_ref, rkv_ref, lora_ref = refs
    x = x_ref[0]
    h = _rmsnorm(x, g_ref[...]) * (1.0 + mod_ref[0, 1:2, :]) + mod_ref[0, 0:1, :]
    hb = h.astype(BF16)
    off = 0

    def seg(name):
        nonlocal off
        lo = off
        off += widths[name]
        return _dot(hb, w_ref[:, lo:off])

    if latent:
        cos_t, sin_t = cos_ref[...], sin_ref[...]
        q = seg("q")
        q_ref[0] = (_rope(q, cos_t, sin_t) * (HEAD_DIM ** -0.5)).astype(BF16)
        k_ref[0] = _rope(seg("k"), cos_t, sin_t).astype(BF16)
    else:
        k_ref[0] = seg("k").astype(BF16)
    v_ref[0] = seg("v").astype(BF16)
    rkv_ref[0] = seg("rkv")
    lora_ref[0] = seg("lora")
    if latent:
        gate_ref[0] = jax.nn.sigmoid(seg("gate")).astype(BF16)


def _in_proj(x, mod, mod_row, norm_g, w_packed, widths, tables, *, latent, tile):
    b, l, d = x.shape
    nt = l // tile
    n = w_packed.shape[1]
    if latent:
        mod_map = lambda i, t: (i, 0, 0)
    else:
        mod_map = lambda i, t: (mod_row, 0, 0)
    tok = lambda w: pl.BlockSpec((1, tile, w), lambda i, t: (i, t, 0))
    in_specs = [tok(d),
                pl.BlockSpec((1,) + mod.shape[1:], mod_map),
                _const_spec((1, d)),
                _const_spec((d, n))]
    args = [x, mod, norm_g.reshape(1, d), w_packed]
    out_specs, out_shape = [], []

    def out(w, dt):
        out_specs.append(tok(w))
        out_shape.append(jax.ShapeDtypeStruct((b, l, w), dt))

    if latent:
        in_specs += [pl.BlockSpec((tile, LANES), lambda i, t: (t, 0))] * 2
        args += list(tables)
        out(widths["q"], BF16)
    out(widths["k"], BF16)
    out(widths["v"], BF16)
    out(widths["rkv"], F32)
    out(widths["lora"], F32)
    if latent:
        out(widths["gate"], BF16)
    return pl.pallas_call(
        functools.partial(_inproj_kernel, latent=latent, widths=widths),
        grid=(b, nt),
        in_specs=in_specs,
        out_specs=out_specs,
        out_shape=out_shape,
        compiler_params=_params("parallel", "parallel"),
        name="in_proj_latent" if latent else "in_proj_context",
    )(*args)


def _attn_kernel(sink_ref, q_ref, kp_ref, kc_ref, kn_ref, vp_ref, vc_ref, vn_ref,
                 kx_ref, vx_ref, o_ref, *, n_kv):
    i = pl.program_id(1)
    last = pl.num_programs(1) - 1
    blk = ATTN_BLOCK
    qi = _iota((blk, blk), 0)
    kj = _iota((blk, blk), 1)
    left = _iota((blk, LANES), 1) < HEAD_DIM

    def key_block(refs, j, gs):
        ref_p, ref_c, ref_n = refs[:3]
        if j < 0:
            return ref_p[0, :, gs]
        if j >= ATTN_QB:
            return ref_n[0, :, gs]
        return ref_c[0, j * blk:(j + 1) * blk, gs]

    items = []
    for qb in range(ATTN_QB):
        lo_ok = kj >= qi
        hi_ok = kj <= qi
        if qb == 0:
            lo_ok = lo_ok & (i > 0)
        if qb == ATTN_QB - 1:
            hi_ok = hi_ok & (i < last)
        bias_lo = jnp.concatenate([jnp.where(lo_ok, 0.0, NEG)] * Q_PER_KV, axis=0)
        bias_hi = jnp.concatenate([jnp.where(hi_ok, 0.0, NEG)] * Q_PER_KV, axis=0)
        rows = slice(qb * blk, (qb + 1) * blk)
        q = q_ref[0, rows, :].astype(F32)
        for g in range(n_kv):
            gs = slice(g * LANES, (g + 1) * LANES)
            heads = range(g * Q_PER_KV, (g + 1) * Q_PER_KV)
            qs, sinks = [], []
            for hd in heads:
                qp = q[:, (hd // PAIR) * LANES:(hd // PAIR + 1) * LANES]
                keep = left if hd % PAIR == 0 else jnp.logical_not(left)
                qs.append(jnp.where(keep, qp, 0.0).astype(BF16))
                sinks.append(jnp.full((blk, 1), sink_ref[hd], F32))
            items.append({
                "qb": qb, "gs": gs, "rows": rows, "heads": heads, "bias": (bias_lo, bias_hi),
                "qs": jnp.concatenate(qs, axis=0), "sink": jnp.concatenate(sinks, axis=0)})

    def cat(refs, it):
        return jnp.concatenate([key_block(refs, it["qb"] + j, it["gs"]) for j in (-1, 0, 1)]
                               + [refs[3][0, :, it["gs"]]], axis=0)

    s = [_dot_nt(it["qs"], cat((kp_ref, kc_ref, kn_ref, kx_ref), it)) for it in items]
    s = [jnp.concatenate([x[:, :blk] + it["bias"][0], x[:, blk:2 * blk], x[:, 2 * blk:3 * blk] + it["bias"][1],
                          x[:, 3 * blk:]], axis=1) for x, it in zip(s, items)]
    m = [jnp.maximum(jnp.max(x, axis=1, keepdims=True), it["sink"]) for x, it in zip(s, items)]
    p = [jnp.exp(x - mx) for x, mx in zip(s, m)]
    den = [jnp.sum(x, axis=1, keepdims=True) + jnp.exp(it["sink"] - mx) for x, mx, it in zip(p, m, items)]
    o = [_dot(x.astype(BF16), cat((vp_ref, vc_ref, vn_ref, vx_ref), it)) / dn for x, dn, it in zip(p, den, items)]
    for x, it in zip(o, items):
        for hd in it["heads"][::PAIR]:
            j = hd - it["heads"][0]
            pair = jnp.where(left, x[j * blk:(j + 1) * blk], x[(j + 1) * blk:(j + 2) * blk])
            col = (hd // PAIR) * LANES
            o_ref[0, it["rows"], col:col + LANES] = pair.astype(BF16)


def _attention(sink, q, kd, vd, kxd, vxd):
    b, l, wq = q.shape
    wk = kd.shape[2]
    lc = kxd.shape[1]
    nb = l // ATTN_BLOCK
    span = ATTN_QB * ATTN_BLOCK
    blk = lambda w, f: pl.BlockSpec((1, ATTN_BLOCK, w), f)
    prev = lambda bi, i: (bi, jnp.maximum(i * ATTN_QB - 1, 0), 0)
    nxt = lambda bi, i: (bi, jnp.minimum((i + 1) * ATTN_QB, nb - 1), 0)
    cur = lambda w: pl.BlockSpec((1, span, w), lambda bi, i: (bi, i, 0))
    ctx = pl.BlockSpec((1, lc, wk), lambda bi, i: (bi, 0, 0))
    return pl.pallas_call(
        functools.partial(_attn_kernel, n_kv=wk // LANES),
        grid=(b, l // span),
        in_specs=[pl.BlockSpec(memory_space=pltpu.SMEM),
                  cur(wq),
                  blk(wk, prev), cur(wk), blk(wk, nxt),
                  blk(wk, prev), cur(wk), blk(wk, nxt),
                  ctx, ctx],
        out_specs=cur(wq),
        out_shape=jax.ShapeDtypeStruct((b, l, wq), BF16),
        compiler_params=_params("parallel", "parallel"),
        name="attention",
    )(sink, q, kd, kd, kd, vd, vd, vd, kxd, vxd)


def _conv3(x, prev_row, next_row, w):
    n = x.shape[0]
    row = _iota(x.shape, 0)
    xm = jnp.where(row == 0, prev_row, pltpu.roll(x, 1, 0))
    xp = jnp.where(row == n - 1, next_row, pltpu.roll(x, n - 1, 0))
    return xm * w[0:1] + x * w[1:2] + xp * w[2:3]


def _pair_masks():
    n = 2 * CHUNK
    row = _iota((n, n), 0)
    lane = _iota((n, n), 1)
    top, left = row < CHUNK, lane < CHUNK
    return {"row": row % CHUNK, "lane": lane % CHUNK, "top": top, "left": left, "bd": top == left,
            "left_h": _iota((CHUNK, n), 1) < CHUNK, "left_q": _iota((CHUNK // 2, n), 1) < CHUNK}


def _stack(a, b):
    return jnp.concatenate([a, b], axis=0)


def _fold(x, m):
    return jnp.where(m["left_h"], x[:CHUNK], x[CHUNK:])


def _unfold(x, left, anti=False):
    a, b = jnp.where(left, x, 0.0), jnp.where(left, 0.0, x)
    return _stack(b, a) if anti else _stack(a, b)


def _chunk_local(inst, m):
    bf = lambda x: x.astype(BF16)
    diag = jnp.logical_not(m["top"]) & (m["lane"] == m["row"])
    masks = ((m["lane"] < m["row"]) | diag, (m["lane"] > m["row"]) | diag)
    mask_a = [masks[i["rev"]] for i in inst]
    left, lh, lq = m["left"], m["left_h"], m["left_q"]
    half = CHUNK // 2
    lhs = [_stack(i["at"], i["rt"]) for i in inst]
    a0 = [jnp.where(ma, _dot_nt(bf(jnp.where(left, l, 0.0)), bf(_stack(i["bt"], i["kt"]))), 0.0)
          for ma, l, i in zip(mask_a, lhs, inst)]
    a1 = [jnp.where(ma, _dot_nt(bf(jnp.where(left, 0.0, l)), bf(_stack(i["kt"], i["bt"]))), 0.0)
          for ma, l, i in zip(mask_a, lhs, inst)]
    nc = [jnp.where(lh, x[:CHUNK], y[:CHUNK]) for x, y in zip(a0, a1)]
    ak_sw = [jnp.where(lh, y[:CHUNK], x[:CHUNK]) for x, y in zip(a0, a1)]
    arb = [jnp.where(lh, x[CHUNK:], y[CHUNK:]) for x, y in zip(a0, a1)]
    ark_sw = [jnp.where(lh, y[CHUNK:], x[CHUNK:]) for x, y in zip(a0, a1)]
    v_sw = [pltpu.roll(i["v"], CHUNK, 1) for i in inst]
    av_sw = [_dot(bf(a), bf(_unfold(s, lh))) for a, s in zip(ak_sw, v_sw)]
    x = [_stack(jnp.where(lh, i["at"], a), jnp.where(lh, a, i["at"])) for i, a in zip(inst, av_sw)]
    steps = CHUNK.bit_length() - 1
    for it in range(steps - 1):
        nbd = [bf(_unfold(n, lh)) for n in nc]
        x = [xi + _dot(n, bf(xi)) for xi, n in zip(x, nbd)]
        if it < steps - 2:
            nc = [_dot(bf(n), b) for n, b in zip(nc, nbd)]
        else:
            nq = [_dot(bf(n[half:]), b) for n, b in zip(nc, nbd)]
    inc = [_dot(bf(_unfold(n, lq)), bf(xi)) for n, xi in zip(nq, x)]
    x = [jnp.concatenate([xi[:half], xi[half:CHUNK] + d[:half], xi[CHUNK:CHUNK + half],
                          xi[CHUNK + half:] + d[half:]], axis=0) for xi, d in zip(x, inc)]
    hm = [_dot(bf(a), bf(_unfold(i["v"], lh, anti=True))) for a, i in zip(ark_sw, inst)]
    kvf = [_dot_tn(bf(i["v"]), bf(i["kt"])) for i in inst]
    return [{"pm": _fold(xi, m),
             "qm": pltpu.roll(jnp.where(lh, xi[CHUNK:], xi[:CHUNK]), CHUNK, 1),
             "arb": b, "hm": h, "kv": _fold(k, m)}
            for xi, b, h, k in zip(x, arb, hm, kvf)]


def _prep_kernel(*refs, latent, width):
    (rkv_ref, rkv_p, rkv_n, lora_ref, lora_p, lora_n, cw_ref, cwl_ref, kk_ref, ka_ref,
     w0_ref, a0_ref, wl_ref, rk_ref) = refs[:14]
    outs = refs[14:]
    out_refs = dict(zip(("pm", "qm", "rt", "bt", "arb", "hm", "kv"), outs[:7]))
    wc_ref = outs[7]
    if latent:
        bonus_ref, g_ref = outs[8:]
    t = pl.program_id(1)
    nt = pl.num_programs(1)
    tile = rkv_ref.shape[1]
    w = width
    has_prev = (t > 0).astype(F32)
    has_next = (t < nt - 1).astype(F32)
    u = _conv3(rkv_ref[0], rkv_p[0, 7:8, :] * has_prev, rkv_n[0, 0:1, :] * has_next, cw_ref[...])
    ul = _conv3(lora_ref[0], lora_p[0, 7:8, :] * has_prev, lora_n[0, 0:1, :] * has_next, cwl_ref[...])
    r, k, v = u[:, :w], u[:, w:2 * w], u[:, 2 * w:]
    ones_bd = _head_ones(w)

    kk = k * kk_ref[...]
    kk = kk * lax.rsqrt(jnp.maximum(_head_sum(kk * kk, ones_bd), 1e-24))

    lane = _iota(ul.shape, 1)
    lin = jnp.where(lane < DECAY_LORA, jnp.tanh(ul),
                    jnp.where(lane < DECAY_LORA + ICLR_LORA, ul, jax.nn.sigmoid(ul)))
    proj = _dot(lin.astype(BF16), wl_ref[...])

    tr = _iota((tile, tile), 0)
    tc = _iota((tile, tile), 1)
    same = (tr // CHUNK) == (tc // CHUNK)
    tri = (jnp.where(same & (tc <= tr), 1.0, 0.0).astype(BF16),
           jnp.where(same & (tc >= tr), 1.0, 0.0).astype(BF16))

    def exact_dot(m, x):
        h1 = x.astype(BF16)
        r1 = x - h1.astype(F32)
        h2 = r1.astype(BF16)
        h3 = (r1 - h2.astype(F32)).astype(BF16)
        return _dot(m, h1) + _dot(m, h2) + _dot(m, h3)

    masks = _pair_masks()
    k_sum = None
    work = []
    for d in range(2):
        z = w0_ref[d:d + 1, :] + proj[:, d * w:(d + 1) * w]
        lw = -math.exp(-0.5) * jax.nn.sigmoid(z)
        a = jax.nn.sigmoid(a0_ref[d:d + 1, :] + proj[:, (2 + d) * w:(3 + d) * w])
        kd = k * (1.0 + (a - 1.0) * ka_ref[...])
        k_sum = kd if k_sum is None else k_sum + kd
        cum = exact_dot(tri[d], lw)
        e_neg = jnp.exp(-cum)
        full = {"at": -kk * jnp.exp(cum - lw), "rt": r * jnp.exp(cum), "bt": kk * a * e_neg,
                "kt": kd * e_neg, "v": v}
        out_refs["rt"][d, 0] = full["rt"].astype(BF16)
        out_refs["bt"][d, 0] = full["bt"].astype(BF16)
        for j in range(tile // CHUNK):
            edge = (j + 1) * CHUNK - 1 if d == 0 else j * CHUNK
            wc_ref[0, j, d:d + 1, :] = jnp.exp(cum[edge:edge + 1, :])
        work += [(d, slice(j * CHUNK, (j + 1) * CHUNK), slice(p * LANES, (p + 1) * LANES), full)
                 for j in range(tile // CHUNK) for p in range(w // LANES)]
    for g0 in range(0, len(work), PREP_GROUP):
        group = work[g0:g0 + PREP_GROUP]
        inst = [dict({name: val[rs, ls] for name, val in full.items()}, rev=d) for d, rs, ls, full in group]
        for (d, rs, ls, _), res in zip(group, _chunk_local(inst, masks)):
            for name, val in res.items():
                out_refs[name][d, 0, rs, ls] = val.astype(BF16)
    if latent:
        bonus_ref[0] = _head_sum(r * k_sum * rk_ref[...], ones_bd) * v
        g_ref[0] = proj[:, 4 * w:5 * w].astype(BF16)


def _wkv_prep(rkv, lora, prm, *, latent, tile):
    b, l, w3 = rkv.shape
    w = w3 // 3
    nt = l // tile
    n8 = l // 8
    tok = lambda wd: pl.BlockSpec((1, tile, wd), lambda i, t: (i, t, 0))
    prev = lambda wd: pl.BlockSpec((1, 8, wd), lambda i, t: (i, jnp.maximum(t * (tile // 8) - 1, 0), 0))
    nxt = lambda wd: pl.BlockSpec((1, 8, wd), lambda i, t: (i, jnp.minimum((t + 1) * (tile // 8), n8 - 1), 0))
    wl = lora.shape[2]
    in_specs = [tok(w3), prev(w3), nxt(w3), tok(wl), prev(wl), nxt(wl)]
    consts = [prm["conv_rkv"], prm["conv_lora"], prm["k_k"], prm["k_a"], prm["decay_w0"],
              prm["iclr_a0"], prm["lora_w"], prm["r_k"]]
    in_specs += [_const_spec(c.shape) for c in consts]
    dirtok = pl.BlockSpec((2, 1, tile, w), lambda i, t: (0, i, t, 0))
    out_specs = [dirtok] * 7 + [pl.BlockSpec((1, tile // CHUNK, 2, w), lambda i, t: (i, t, 0, 0))]
    out_shape = [jax.ShapeDtypeStruct((2, b, l, w), BF16)] * 7 + [
        jax.ShapeDtypeStruct((b, l // CHUNK, 2, w), F32)]
    if latent:
        out_specs += [tok(w), tok(w)]
        out_shape += [jax.ShapeDtypeStruct((b, l, w), F32), jax.ShapeDtypeStruct((b, l, w), BF16)]
    return pl.pallas_call(
        functools.partial(_prep_kernel, latent=latent, width=w),
        grid=(b, nt),
        in_specs=in_specs,
        out_specs=out_specs,
        out_shape=out_shape,
        compiler_params=_params("parallel", "parallel"),
        name="wkv_prep_latent" if latent else "wkv_prep_context",
    )(rkv, rkv, rkv, lora, lora, lora, *consts)


def _wkv_kernel(*refs, emit_y, n_pairs):
    names = ("pm", "qm", "rt", "bt", "arb", "hm", "kv", "wc")
    dir_refs = (dict(zip(names, refs[0:8])), dict(zip(names, refs[8:16])))
    z0_ref = refs[16]
    if emit_y:
        y_refs = refs[17:19]
        z_scr = refs[19]
    else:
        zfin_ref = refs[17]
        z_scr = refs[18]
    c = pl.program_id(0)

    @pl.when(c == 0)
    def _():
        z_scr[...] = z0_ref[...]

    m = _pair_masks()
    bd, left_h = m["bd"], m["left_h"]

    def unfold(x):
        x = x.astype(F32)
        return _stack(jnp.where(left_h, x, 0.0), jnp.where(left_h, 0.0, x))

    tiles = [(i, d, p, slice(p * LANES, (p + 1) * LANES))
             for i in range(z_scr.shape[0]) for d in range(2) for p in range(n_pairs)]
    ld = lambda name: [dir_refs[d][name][0, i, :, sl] for i, d, _, sl in tiles]
    s_old = [z_scr[i, d, p] for i, d, p, _ in tiles]
    sb = [s.astype(BF16) for s in s_old]
    if emit_y:
        ur = [_dot_nt(_stack(pm, rt), s) for pm, rt, s in zip(ld("pm"), ld("rt"), sb)]
    else:
        ur = [_dot_nt(pm, s) for pm, s in zip(ld("pm"), sb)]
    u = [x[:CHUNK] + q.astype(F32) for x, q in zip(ur, ld("qm"))]
    inc = [_dot_tn(ui.astype(BF16), bt) for ui, bt in zip(u, ld("bt"))]
    if emit_y:
        yc = [_dot(a, unfold(ui).astype(BF16)) for a, ui in zip(ld("arb"), u)]
        for (i, d, _, sl), x, ys, h in zip(tiles, ur, yc, ld("hm")):
            y_refs[d][i, :, sl] = x[CHUNK:] + ys + h.astype(F32)
    for (i, d, p, sl), s, dz, kv in zip(tiles, s_old, inc, ld("kv")):
        z_scr[i, d, p] = (s + jnp.where(bd, dz, 0.0) + unfold(kv)) * dir_refs[d]["wc"][i, 0, d:d + 1, sl]

    if not emit_y:
        @pl.when(c == pl.num_programs(0) - 1)
        def _():
            zfin_ref[...] = z_scr[...]


def _wkv_scan(prep, z0, *, emit_y):
    wc = prep[7]
    _, b, l, w = prep[0].shape
    nc = l // CHUNK
    n_pairs = w // LANES
    fwd = lambda c: c
    rev = lambda c: nc - 1 - c
    in_specs, args = [], []
    for d, cm in enumerate((fwd, rev)):
        for arr in prep[:7]:
            in_specs.append(pl.BlockSpec((1, b, CHUNK, w), lambda c, d=d, cm=cm: (d, 0, cm(c), 0)))
            args.append(arr)
        in_specs.append(pl.BlockSpec((b, 1, 2, w), lambda c, cm=cm: (0, cm(c), 0, 0)))
        args.append(wc)
    zshape = (b, 2, n_pairs, LANES, LANES)
    zspec = pl.BlockSpec(zshape, lambda c: (0, 0, 0, 0, 0))
    in_specs.append(zspec)
    args.append(z0)
    if emit_y:
        out_specs = [pl.BlockSpec((b, CHUNK, w), lambda c: (0, c, 0)),
                     pl.BlockSpec((b, CHUNK, w), lambda c: (0, nc - 1 - c, 0))]
        out_shape = [jax.ShapeDtypeStruct((b, l, w), F32)] * 2
    else:
        out_specs = zspec
        out_shape = jax.ShapeDtypeStruct(zshape, F32)
    return pl.pallas_call(
        functools.partial(_wkv_kernel, emit_y=emit_y, n_pairs=n_pairs),
        grid=(nc,),
        in_specs=in_specs,
        out_specs=out_specs,
        out_shape=out_shape,
        scratch_shapes=[pltpu.VMEM(zshape, F32)],
        compiler_params=_params("arbitrary"),
        name="wkv_scan_latent" if emit_y else "wkv_scan_context",
    )(*args)


def _merge_kernel(x_ref, mod_ref, ya_ref, yf_ref, yr_ref, bonus_ref, g_ref, gate_ref,
                  lnw_ref, lnb_ref, wba_ref, wbr_ref, wo_ref, n2_ref, wu_ref, wd_ref, nf_ref,
                  o_ref, *, ff_chunk):
    x = x_ref[0]
    d = x.shape[1]
    mod = lambda j: mod_ref[0, j:j + 1, :]
    y = yf_ref[0] + yr_ref[0]
    ones_bd = _head_ones(y.shape[1])
    inv = 1.0 / HEAD_DIM
    mu = _head_sum(y, ones_bd) * inv
    yc = y - mu
    var = _head_sum(yc * yc, ones_bd) * inv
    yn = yc * lax.rsqrt(var + LNX_EPS)
    yr = (yn * lnw_ref[...] + lnb_ref[...] + bonus_ref[0]) * g_ref[0].astype(F32)
    gate = gate_ref[0].astype(F32)
    merged = gate[:, :d] * _dot(ya_ref[0], wba_ref[...]) + gate[:, d:] * _dot(yr.astype(BF16), wbr_ref[...])
    x1 = x + mod(2) * _dot(merged.astype(BF16), wo_ref[...])
    h2 = (_rmsnorm(x1, n2_ref[...]) * (1.0 + mod(4)) + mod(3)).astype(BF16)
    acc = jnp.zeros_like(x1)
    for j in range(wu_ref.shape[1] // ff_chunk):
        cs = slice(j * ff_chunk, (j + 1) * ff_chunk)
        up = jnp.maximum(_dot(h2, wu_ref[:, cs]), 0.0)
        acc = acc + _dot((up * up).astype(BF16), wd_ref[cs, :])
    x2 = x1 + mod(5) * acc
    o_ref[0] = _rmsnorm(x2, nf_ref[...])


def _merge_mlp(x, mod, ya, yf, yr, bonus, g, gate, prm, *, tile):
    b, l, d = x.shape
    tok = lambda arr: pl.BlockSpec((1, tile, arr.shape[2]), lambda i, t: (i, t, 0))
    consts = [prm["lnx_w"], prm["lnx_b"], prm["w_branch_attn"], prm["w_branch_rwkv"], prm["w_out"],
              prm["norm2_g"], prm["w_mlp_up"], prm["w_mlp_down"], prm["norm_f_g"]]
    toks = [ya, yf, yr, bonus, g, gate]
    return pl.pallas_call(
        functools.partial(_merge_kernel, ff_chunk=min(1024, prm["w_mlp_up"].shape[1])),
        grid=(b, l // tile),
        in_specs=[tok(x), pl.BlockSpec((1,) + mod.shape[1:], lambda i, t: (i, 0, 0))]
        + [tok(a) for a in toks] + [_const_spec(c.shape) for c in consts],
        out_specs=tok(x),
        out_shape=jax.ShapeDtypeStruct(x.shape, x.dtype),
        compiler_params=_params("parallel", "parallel"),
        name="merge_mlp",
    )(x, mod, *toks, *consts)


def _rope_tables(l):
    n_freq = HEAD_DIM // 4
    inv_freq = jnp.power(ROPE_BASE, -jnp.arange(n_freq, dtype=F32) / n_freq)
    rows = l // GRID_W
    row = jnp.repeat(jnp.arange(rows, dtype=F32), GRID_W)
    col = jnp.tile(jnp.arange(GRID_W, dtype=F32), rows)
    ang = jnp.concatenate([row[:, None] * inv_freq, col[:, None] * inv_freq], axis=-1)
    cos, sin = jnp.cos(ang), jnp.sin(ang)
    reps = LANES // HEAD_DIM
    return (jnp.tile(jnp.concatenate([cos, cos], axis=1), (1, reps)),
            jnp.tile(jnp.concatenate([-sin, sin], axis=1), (1, reps)))


def _dup_heads(w, n_heads):
    cols = [w[:, h * HEAD_DIM:(h + 1) * HEAD_DIM] for h in range(n_heads)]
    return jnp.concatenate([c for c in cols for _ in range(PAIR)], axis=1)


def _pad_cols(w, width):
    return jnp.pad(w, ((0, 0), (0, width - w.shape[1])))


def kernel(x, c, ctx, c_ctx, w_ada, b_ada, norm1_g, w_in, sink, conv_w, decay_w0, decay_w2, iclr_a0, iclr_a2, gate_g2, k_k, k_a, r_k, lnx_w, lnx_b, w_branch_attn, w_branch_rwkv, w_out, norm2_g, w_mlp_up, w_mlp_down, norm_f_g):
    assert w_in.shape[0] == 1, "single-layer block: context tokens are read, never updated"
    b, l, d = x.shape
    attn_w = w_branch_attn.shape[1]
    rw = w_branch_rwkv.shape[1]
    n_q = attn_w // HEAD_DIM
    n_kv = n_q // Q_PER_KV
    kv_w = n_kv * HEAD_DIM
    assert n_kv * PAIR * HEAD_DIM == n_kv * LANES and rw % LANES == 0 and l % 256 == 0 and ctx.shape[1] % 256 == 0

    w = w_in[0]
    o_k, o_v, o_r = attn_w, attn_w + kv_w, attn_w + 2 * kv_w
    o_l = o_r + 3 * rw
    o_g = o_l + DECAY_LORA + ICLR_LORA + GATE_LORA
    seg_k = _dup_heads(w[:, o_k:o_v], n_kv)
    seg_v = _dup_heads(w[:, o_v:o_r], n_kv)
    seg_rkv = w[:, o_r:o_l]
    seg_lora = _pad_cols(w[:, o_l:o_g], LORA_PAD)
    w_ctx = jnp.concatenate([seg_k, seg_v, seg_rkv, seg_lora], axis=1).astype(BF16)
    w_lat = jnp.concatenate([w[:, :o_k].astype(BF16), w_ctx, w[:, o_g:].astype(BF16)], axis=1)
    widths_ctx = {"k": seg_k.shape[1], "v": seg_v.shape[1], "rkv": 3 * rw, "lora": LORA_PAD}
    widths_lat = {"q": attn_w, **widths_ctx, "gate": 2 * d}

    cw = conv_w[0]
    lora_w = jnp.zeros((LORA_PAD, 5 * rw), F32)
    lora_w = lora_w.at[:DECAY_LORA, :rw].set(decay_w2[0, 0]).at[:DECAY_LORA, rw:2 * rw].set(decay_w2[0, 1])
    r1 = DECAY_LORA + ICLR_LORA
    lora_w = lora_w.at[DECAY_LORA:r1, 2 * rw:3 * rw].set(iclr_a2[0, 0]).at[DECAY_LORA:r1, 3 * rw:4 * rw].set(iclr_a2[0, 1])
    lora_w = lora_w.at[r1:r1 + GATE_LORA, 4 * rw:].set(gate_g2[0])
    prm = {
        "conv_rkv": cw[:, :3 * rw], "conv_lora": _pad_cols(cw[:, 3 * rw:], LORA_PAD),
        "k_k": k_k[0].reshape(1, rw), "k_a": k_a[0].reshape(1, rw),
        "decay_w0": decay_w0[0], "iclr_a0": iclr_a0[0], "lora_w": lora_w.astype(BF16),
        "r_k": r_k[0].reshape(1, rw),
        "lnx_w": lnx_w[0].reshape(1, rw), "lnx_b": lnx_b[0].reshape(1, rw),
        "w_branch_attn": w_branch_attn[0].astype(BF16), "w_branch_rwkv": w_branch_rwkv[0].astype(BF16),
        "w_out": w_out[0].astype(BF16), "norm2_g": norm2_g[0].reshape(1, d),
        "w_mlp_up": w_mlp_up[0].astype(BF16), "w_mlp_down": w_mlp_down[0].astype(BF16),
        "norm_f_g": norm_f_g.reshape(1, d),
    }

    rows = -(-(b + 1) // 8) * 8
    cc = jnp.zeros((rows, d), F32).at[:b].set(c).at[b].set(c_ctx)
    mod = _ada_mod(cc, w_ada[0], b_ada[0]).reshape(rows, -1, d)

    q, kd, vd, rkv, lora, gate = _in_proj(x, mod, b, norm1_g[0], w_lat, widths_lat, _rope_tables(l),
                                          latent=True, tile=512)
    kxd, vxd, rkv_c, lora_c = _in_proj(ctx, mod, b, norm1_g[0], w_ctx, widths_ctx, None,
                                       latent=False, tile=256)
    ya = _attention(sink[0], q, kd, vd, kxd, vxd)

    prep_c = _wkv_prep(rkv_c, lora_c, prm, latent=False, tile=256)
    z_ctx = _wkv_scan(prep_c, jnp.zeros((b, 2, rw // LANES, LANES, LANES), F32), emit_y=False)
    prep = _wkv_prep(rkv, lora, prm, latent=True, tile=256)
    yf, yr = _wkv_scan(prep, z_ctx, emit_y=True)
    bonus, g = prep[8], prep[9]

    return _merge_mlp(x, mod, ya, yf, yr, bonus, g, gate, prm, tile=512)
```

```python
import functools
import math

import jax
import jax.numpy as jnp
from jax import lax
from jax.experimental import pallas as pl
from jax.experimental.pallas import tpu as pltpu

F32 = jnp.float32
BF16 = jnp.bfloat16

GRID_W = 64
HEAD_DIM = 64
Q_PER_KV = 4
ATTN_BLOCK = 128
ATTN_QB = 4
ROPE_BASE = 10000.0
NORM_EPS = 1e-6
LNX_EPS = 1e-5 * HEAD_DIM
DECAY_LORA, ICLR_LORA, GATE_LORA = 32, 32, 96
LORA_PAD = 256
CHUNK = 64
PREP_GROUP = 32
LANES = 128
PAIR = LANES // HEAD_DIM
NEG = -1e30
VMEM_LIMIT = 56 * 1024 * 1024


def _dot(a, b):
    return jnp.dot(a, b, preferred_element_type=F32)


def _dot_nt(a, b):
    return lax.dot_general(a, b, (((1,), (1,)), ((), ())), preferred_element_type=F32)


def _dot_tn(a, b):
    return lax.dot_general(a, b, (((0,), (0,)), ((), ())), preferred_element_type=F32)


def _iota(shape, dim):
    return lax.broadcasted_iota(jnp.int32, shape, dim)


def _head_ones(width):
    r = _iota((width, width), 0) // HEAD_DIM
    c = _iota((width, width), 1) // HEAD_DIM
    return jnp.where(r == c, 1.0, 0.0).astype(BF16)


def _head_sum(x, ones_bd):
    hi = x.astype(BF16)
    lo = (x - hi.astype(F32)).astype(BF16)
    return _dot(hi, ones_bd) + _dot(lo, ones_bd)


def _rmsnorm(x, g):
    ms = jnp.mean(x * x, axis=-1, keepdims=True)
    return x * lax.rsqrt(ms + NORM_EPS) * g


def _params(*sem):
    return pltpu.CompilerParams(dimension_semantics=sem, vmem_limit_bytes=VMEM_LIMIT)


def _const_spec(shape):
    nd = len(shape)
    return pl.BlockSpec(shape, lambda *_: (0,) * nd, pipeline_mode=pl.Buffered(1))


def _ada_kernel(c_ref, w_ref, b_ref, o_ref):
    c = c_ref[...]
    s = c * jax.nn.sigmoid(c)
    o_ref[...] = _dot(s.astype(BF16), w_ref[...].astype(BF16)) + b_ref[...]


def _ada_mod(cc, w_ada, b_ada):
    rows, d = cc.shape
    n = w_ada.shape[1]
    return pl.pallas_call(
        _ada_kernel,
        grid=(n // d,),
        in_specs=[pl.BlockSpec((rows, d), lambda j: (0, 0)),
                  pl.BlockSpec((d, d), lambda j: (0, j)),
                  pl.BlockSpec((1, d), lambda j: (0, j))],
        out_specs=pl.BlockSpec((rows, d), lambda j: (0, j)),
        out_shape=jax.ShapeDtypeStruct((rows, n), F32),
        compiler_params=_params("arbitrary"),
        name="ada_mod",
    )(cc, w_ada, b_ada.reshape(1, n))


def _rope(x, cos_t, sin_t):
    w = x.shape[1]
    half = HEAD_DIM // 2
    first = (_iota(x.shape, 1) % HEAD_DIM) < half
    swapped = jnp.where(first, pltpu.roll(x, w - half, 1), pltpu.roll(x, half, 1))
    reps = w // LANES
    c = jnp.concatenate([cos_t] * reps, axis=1)
    s = jnp.concatenate([sin_t] * reps, axis=1)
    return x * c + swapped * s


def _inproj_kernel(*refs, latent, widths):
    if latent:
        (x_ref, mod_ref, g_ref, w_ref, cos_ref, sin_ref,
         q_ref, k_ref, v_ref, rkv_ref, lora_ref, gate_ref) = refs
    else:
        x_ref, mod_ref, g_ref, w_ref, k_ref, v_ref, rkv_ref, lora_ref = refs
    x = x_ref[0]
    h = _rmsnorm(x, g_ref[...]) * (1.0 + mod_ref[0, 1:2, :]) + mod_ref[0, 0:1, :]
    hb = h.astype(BF16)
    off = 0

    def seg(name):
        nonlocal off
        lo = off
        off += widths[name]
        return _dot(hb, w_ref[:, lo:off])

    if latent:
        cos_t, sin_t = cos_ref[...], sin_ref[...]
        q = seg("q")
        q_ref[0] = (_rope(q, cos_t, sin_t) * (HEAD_DIM ** -0.5)).astype(BF16)
        k_ref[0] = _rope(seg("k"), cos_t, sin_t).astype(BF16)
    else:
        k_ref[0] = seg("k").astype(BF16)
    v_ref[0] = seg("v").astype(BF16)
    rkv_ref[0] = seg("rkv")
    lora_ref[0] = seg("lora")
    if latent:
        gate_ref[0] = jax.nn.sigmoid(seg("gate")).astype(BF16)


def _in_proj(x, mod, mod_row, norm_g, w_packed, widths, tables, *, latent, tile):
    b, l, d = x.shape
    nt = l // tile
    n = w_packed.shape[1]
    if latent:
        mod_map = lambda i, t: (i, 0, 0)
    else:
        mod_map = lambda i, t: (mod_row, 0, 0)
    tok = lambda w: pl.BlockSpec((1, tile, w), lambda i, t: (i, t, 0))
    in_specs = [tok(d),
                pl.BlockSpec((1,) + mod.shape[1:], mod_map),
                _const_spec((1, d)),
                _const_spec((d, n))]
    args = [x, mod, norm_g.reshape(1, d), w_packed]
    out_specs, out_shape = [], []

    def out(w, dt):
        out_specs.append(tok(w))
        out_shape.append(jax.ShapeDtypeStruct((b, l, w), dt))

    if latent:
        in_specs += [pl.BlockSpec((tile, LANES), lambda i, t: (t, 0))] * 2
        args += list(tables)
        out(widths["q"], BF16)
    out(widths["k"], BF16)
    out(widths["v"], BF16)
    out(widths["rkv"], F32)
    out(widths["lora"], F32)
    if latent:
        out(widths["gate"], BF16)
    return pl.pallas_call(
        functools.partial(_inproj_kernel, latent=latent, widths=widths),
        grid=(b, nt),
        in_specs=in_specs,
        out_specs=out_specs,
        out_shape=out_shape,
        compiler_params=_params("parallel", "parallel"),
        name="in_proj_latent" if latent else "in_proj_context",
    )(*args)


def _attn_kernel(sink_ref, q_ref, kp_ref, kc_ref, kn_ref, vp_ref, vc_ref, vn_ref,
                 kx_ref, vx_ref, o_ref, *, n_kv):
    i = pl.program_id(1)
    last = pl.num_programs(1) - 1
    blk = ATTN_BLOCK
    qi = _iota((blk, blk), 0)
    kj = _iota((blk, blk), 1)
    left = _iota((blk, LANES), 1) < HEAD_DIM

    def key_block(refs, j, gs):
        ref_p, ref_c, ref_n = refs[:3]
        if j < 0:
            return ref_p[0, :, gs]
        if j >= ATTN_QB:
            return ref_n[0, :, gs]
        return ref_c[0, j * blk:(j + 1) * blk, gs]

    items = []
    for qb in range(ATTN_QB):
        lo_ok = kj >= qi
        hi_ok = kj <= qi
        if qb == 0:
            lo_ok = lo_ok & (i > 0)
        if qb == ATTN_QB - 1:
            hi_ok = hi_ok & (i < last)
        bias_lo = jnp.concatenate([jnp.where(lo_ok, 0.0, NEG)] * Q_PER_KV, axis=0)
        bias_hi = jnp.concatenate([jnp.where(hi_ok, 0.0, NEG)] * Q_PER_KV, axis=0)
        rows = slice(qb * blk, (qb + 1) * blk)
        q = q_ref[0, rows, :].astype(F32)
        for g in range(n_kv):
            gs = slice(g * LANES, (g + 1) * LANES)
            heads = range(g * Q_PER_KV, (g + 1) * Q_PER_KV)
            qs, sinks = [], []
            for hd in heads:
                qp = q[:, (hd // PAIR) * LANES:(hd // PAIR + 1) * LANES]
                keep = left if hd % PAIR == 0 else jnp.logical_not(left)
                qs.append(jnp.where(keep, qp, 0.0).astype(BF16))
                sinks.append(jnp.full((blk, 1), sink_ref[hd], F32))
            items.append({
                "qb": qb, "gs": gs, "rows": rows, "heads": heads, "bias": (bias_lo, bias_hi),
                "qs": jnp.concatenate(qs, axis=0), "sink": jnp.concatenate(sinks, axis=0)})

    def cat(refs, it):
        return jnp.concatenate([key_block(refs, it["qb"] + j, it["gs"]) for j in (-1, 0, 1)]
                               + [refs[3][0, :, it["gs"]]], axis=0)

    s = [_dot_nt(it["qs"], cat((kp_ref, kc_ref, kn_ref, kx_ref), it)) for it in items]
    s = [jnp.concatenate([x[:, :blk] + it["bias"][0], x[:, blk:2 * blk], x[:, 2 * blk:3 * blk] + it["bias"][1],
                          x[:, 3 * blk:]], axis=1) for x, it in zip(s, items)]
    m = [jnp.maximum(jnp.max(x, axis=1, keepdims=True), it["sink"]) for x, it in zip(s, items)]
    p = [jnp.exp(x - mx) for x, mx in zip(s, m)]
    den = [jnp.sum(x, axis=1, keepdims=True) + jnp.exp(it["sink"] - mx) for x, mx, it in zip(p, m, items)]
    o = [_dot(x.astype(BF16), cat((vp_ref, vc_ref, vn_ref, vx_ref), it)) / dn for x, dn, it in zip(p, den, items)]
    for x, it in zip(o, items):
        for hd in it["heads"][::PAIR]:
            j = hd - it["heads"][0]
            pair = jnp.where(left, x[j * blk:(j + 1) * blk], x[(j + 1) * blk:(j + 2) * blk])
            col = (hd // PAIR) * LANES
            o_ref[0, it["rows"], col:col + LANES] = pair.astype(BF16)


def _attention(sink, q, kd, vd, kxd, vxd):
    b, l, wq = q.shape
    wk = kd.shape[2]
    lc = kxd.shape[1]
    nb = l // ATTN_BLOCK
    span = ATTN_QB * ATTN_BLOCK
    blk = lambda w, f: pl.BlockSpec((1, ATTN_BLOCK, w), f)
    prev = lambda bi, i: (bi, jnp.maximum(i * ATTN_QB - 1, 0), 0)
    nxt = lambda bi, i: (bi, jnp.minimum((i + 1) * ATTN_QB, nb - 1), 0)
    cur = lambda w: pl.BlockSpec((1, span, w), lambda bi, i: (bi, i, 0))
    ctx = pl.BlockSpec((1, lc, wk), lambda bi, i: (bi, 0, 0))
    return pl.pallas_call(
        functools.partial(_attn_kernel, n_kv=wk // LANES),
        grid=(b, l // span),
        in_specs=[pl.BlockSpec(memory_space=pltpu.SMEM),
                  cur(wq),
                  blk(wk, prev), cur(wk), blk(wk, nxt),
                  blk(wk, prev), cur(wk), blk(wk, nxt),
                  ctx, ctx],
        out_specs=cur(wq),
        out_shape=jax.ShapeDtypeStruct((b, l, wq), BF16),
        compiler_params=_params("parallel", "parallel"),
        name="attention",
    )(sink, q, kd, kd, kd, vd, vd, vd, kxd, vxd)


def _conv3(x, prev_row, next_row, w):
    n = x.shape[0]
    row = _iota(x.shape, 0)
    xm = jnp.where(row == 0, prev_row, pltpu.roll(x, 1, 0))
    xp = jnp.where(row == n - 1, next_row, pltpu.roll(x, n - 1, 0))
    return xm * w[0:1] + x * w[1:2] + xp * w[2:3]


def _pair_masks():
    n = 2 * CHUNK
    row = _iota((n, n), 0)
    lane = _iota((n, n), 1)
    top, left = row < CHUNK, lane < CHUNK
    return {"row": row % CHUNK, "lane": lane % CHUNK, "top": top, "left": left, "bd": top == left,
            "left_h": _iota((CHUNK, n), 1) < CHUNK}


def _stack(a, b):
    return jnp.concatenate([a, b], axis=0)


def _fold(x, m):
    return jnp.where(m["left_h"], x[:CHUNK], x[CHUNK:])


def _chunk_local(inst, m):
    bf = lambda x: x.astype(BF16)
    diag = jnp.logical_not(m["top"]) & (m["lane"] == m["row"])
    masks = ((m["lane"] < m["row"]) | diag, (m["lane"] > m["row"]) | diag)
    mask_a = [masks[i["rev"]] for i in inst]
    lh = m["left_h"]
    half = CHUNK // 2
    keep_l = jnp.where(m["left"], 1.0, 0.0).astype(BF16)
    keep_r = jnp.where(m["left"], 0.0, 1.0).astype(BF16)
    keep_lh = jnp.where(lh, 1.0, 0.0).astype(BF16)
    keep_rh = jnp.where(lh, 0.0, 1.0).astype(BF16)

    def unfold(xb, anti=False):
        a, b = xb * keep_lh, xb * keep_rh
        return _stack(b, a) if anti else _stack(a, b)

    lhs = [bf(_stack(i["at"], i["rt"])) for i in inst]
    a0 = [jnp.where(ma, _dot_nt(l * keep_l, bf(_stack(i["bt"], i["kt"]))), 0.0) for ma, l, i in zip(mask_a, lhs, inst)]
    a1 = [jnp.where(ma, _dot_nt(l * keep_r, bf(_stack(i["kt"], i["bt"]))), 0.0) for ma, l, i in zip(mask_a, lhs, inst)]
    nc = [jnp.where(lh, x[:CHUNK], y[:CHUNK]) for x, y in zip(a0, a1)]
    arb = [jnp.where(lh, x[CHUNK:], y[CHUNK:]) for x, y in zip(a0, a1)]
    ak_ark_sw = [bf(jnp.where(m["left"], y, x)) for x, y in zip(a0, a1)]
    vh = [_dot(a, unfold(bf(i["v"]), anti=True)) for a, i in zip(ak_ark_sw, inst)]
    eye = jnp.where(_iota((CHUNK, LANES), 1) % CHUNK == _iota((CHUNK, LANES), 0), 1.0, 0.0)
    tc = [eye + n for n in nc]
    ncb = [bf(n) for n in nc]
    nc = [_dot(n, unfold(n)) for n in ncb]
    steps = CHUNK.bit_length() - 1
    for _ in range(steps - 2):
        ncb = [bf(n) for n in nc]
        both = [_dot(n, jnp.concatenate([unfold(bf(t)), unfold(n)], axis=1)) for n, t in zip(ncb, tc)]
        tc = [t + r[:, :LANES] for t, r in zip(tc, both)]
        nc = [r[:, LANES:] for r in both]
    inc = [_dot(bf(n[half:]), unfold(bf(t))) for n, t in zip(nc, tc)]
    tc = [jnp.concatenate([t[:half], t[half:] + d], axis=0) for t, d in zip(tc, inc)]
    pq = [_dot(bf(t), jnp.concatenate([unfold(l[:CHUNK]), unfold(bf(x[:CHUNK]))], axis=1))
          for t, l, x in zip(tc, lhs, vh)]
    kvf = [_dot_tn(bf(i["v"]), bf(i["kt"])) for i in inst]
    return [{"pm": r[:, :LANES], "qm": r[:, LANES:], "arb": b, "hm": x[CHUNK:], "kv": _fold(k, m)}
            for r, b, x, k in zip(pq, arb, vh, kvf)]


def _prep_kernel(*refs, latent, width):
    (rkv_ref, rkv_p, rkv_n, lora_ref, lora_p, lora_n, cw_ref, cwl_ref, kk_ref, ka_ref,
     w0_ref, a0_ref, wl_ref, rk_ref) = refs[:14]
    outs = refs[14:]
    out_refs = dict(zip(("pm", "qm", "rt", "bt", "arb", "hm", "kv"), outs[:7]))
    wc_ref = outs[7]
    if latent:
        bonus_ref, g_ref = outs[8:]
    t = pl.program_id(1)
    nt = pl.num_programs(1)
    tile = rkv_ref.shape[1]
    w = width
    has_prev = (t > 0).astype(F32)
    has_next = (t < nt - 1).astype(F32)
    u = _conv3(rkv_ref[0], rkv_p[0, 7:8, :] * has_prev, rkv_n[0, 0:1, :] * has_next, cw_ref[...])
    ul = _conv3(lora_ref[0], lora_p[0, 7:8, :] * has_prev, lora_n[0, 0:1, :] * has_next, cwl_ref[...])
    r, k, v = u[:, :w], u[:, w:2 * w], u[:, 2 * w:]
    ones_bd = _head_ones(w)

    kk = k * kk_ref[...]
    kk = kk * lax.rsqrt(jnp.maximum(_head_sum(kk * kk, ones_bd), 1e-24))

    lane = _iota(ul.shape, 1)
    lin = jnp.where(lane < DECAY_LORA, jnp.tanh(ul),
                    jnp.where(lane < DECAY_LORA + ICLR_LORA, ul, jax.nn.sigmoid(ul)))
    proj = _dot(lin.astype(BF16), wl_ref[...])

    tr = _iota((tile, tile), 0)
    tc = _iota((tile, tile), 1)
    same = (tr // CHUNK) == (tc // CHUNK)
    tri = (jnp.where(same & (tc <= tr), 1.0, 0.0).astype(BF16),
           jnp.where(same & (tc >= tr), 1.0, 0.0).astype(BF16))

    def exact_dot(m, x):
        h1 = x.astype(BF16)
        r1 = x - h1.astype(F32)
        h2 = r1.astype(BF16)
        h3 = (r1 - h2.astype(F32)).astype(BF16)
        return _dot(m, h1) + _dot(m, h2) + _dot(m, h3)

    masks = _pair_masks()
    k_sum = None
    work = []
    for d in range(2):
        z = w0_ref[d:d + 1, :] + proj[:, d * w:(d + 1) * w]
        lw = -math.exp(-0.5) * jax.nn.sigmoid(z)
        a = jax.nn.sigmoid(a0_ref[d:d + 1, :] + proj[:, (2 + d) * w:(3 + d) * w])
        kd = k * (1.0 + (a - 1.0) * ka_ref[...])
        k_sum = kd if k_sum is None else k_sum + kd
        cum = exact_dot(tri[d], lw)
        e_neg = jnp.exp(-cum)
        full = {"at": -kk * jnp.exp(cum - lw), "rt": r * jnp.exp(cum), "bt": kk * a * e_neg,
                "kt": kd * e_neg, "v": v}
        out_refs["rt"][d, 0] = full["rt"].astype(BF16)
        out_refs["bt"][d, 0] = full["bt"].astype(BF16)
        for j in range(tile // CHUNK):
            edge = (j + 1) * CHUNK - 1 if d == 0 else j * CHUNK
            wc_ref[0, j, d:d + 1, :] = jnp.exp(cum[edge:edge + 1, :])
        work += [(d, slice(j * CHUNK, (j + 1) * CHUNK), slice(p * LANES, (p + 1) * LANES), full)
                 for j in range(tile // CHUNK) for p in range(w // LANES)]
    for g0 in range(0, len(work), PREP_GROUP):
        group = work[g0:g0 + PREP_GROUP]
        inst = [dict({name: val[rs, ls] for name, val in full.items()}, rev=d) for d, rs, ls, full in group]
        for (d, rs, ls, _), res in zip(group, _chunk_local(inst, masks)):
            for name, val in res.items():
                out_refs[name][d, 0, rs, ls] = val.astype(BF16)
    if latent:
        bonus_ref[0] = _head_sum(r * k_sum * rk_ref[...], ones_bd) * v
        g_ref[0] = proj[:, 4 * w:5 * w].astype(BF16)


def _wkv_prep(rkv, lora, prm, *, latent, tile):
    b, l, w3 = rkv.shape
    w = w3 // 3
    nt = l // tile
    n8 = l // 8
    tok = lambda wd: pl.BlockSpec((1, tile, wd), lambda i, t: (i, t, 0))
    prev = lambda wd: pl.BlockSpec((1, 8, wd), lambda i, t: (i, jnp.maximum(t * (tile // 8) - 1, 0), 0))
    nxt = lambda wd: pl.BlockSpec((1, 8, wd), lambda i, t: (i, jnp.minimum((t + 1) * (tile // 8), n8 - 1), 0))
    wl = lora.shape[2]
    in_specs = [tok(w3), prev(w3), nxt(w3), tok(wl), prev(wl), nxt(wl)]
    consts = [prm["conv_rkv"], prm["conv_lora"], prm["k_k"], prm["k_a"], prm["decay_w0"],
              prm["iclr_a0"], prm["lora_w"], prm["r_k"]]
    in_specs += [_const_spec(c.shape) for c in consts]
    dirtok = pl.BlockSpec((2, 1, tile, w), lambda i, t: (0, i, t, 0))
    out_specs = [dirtok] * 7 + [pl.BlockSpec((1, tile // CHUNK, 2, w), lambda i, t: (i, t, 0, 0))]
    out_shape = [jax.ShapeDtypeStruct((2, b, l, w), BF16)] * 7 + [
        jax.ShapeDtypeStruct((b, l // CHUNK, 2, w), F32)]
    if latent:
        out_specs += [tok(w), tok(w)]
        out_shape += [jax.ShapeDtypeStruct((b, l, w), F32), jax.ShapeDtypeStruct((b, l, w), BF16)]
    return pl.pallas_call(
        functools.partial(_prep_kernel, latent=latent, width=w),
        grid=(b, nt),
        in_specs=in_specs,
        out_specs=out_specs,
        out_shape=out_shape,
        compiler_params=_params("parallel", "parallel"),
        name="wkv_prep_latent" if latent else "wkv_prep_context",
    )(rkv, rkv, rkv, lora, lora, lora, *consts)


def _wkv_kernel(*refs, emit_y, n_pairs):
    names = ("pm", "qm", "rt", "bt", "arb", "hm", "kv", "wc")
    dir_refs = (dict(zip(names, refs[0:8])), dict(zip(names, refs[8:16])))
    z0_ref = refs[16]
    if emit_y:
        y_refs = refs[17:19]
        z_scr = refs[19]
    else:
        zfin_ref = refs[17]
        z_scr = refs[18]
    c = pl.program_id(0)

    @pl.when(c == 0)
    def _():
        z_scr[...] = z0_ref[...]

    m = _pair_masks()
    bd, left_h = m["bd"], m["left_h"]

    def unfold(x):
        x = x.astype(F32)
        return _stack(jnp.where(left_h, x, 0.0), jnp.where(left_h, 0.0, x))

    tiles = [(i, d, p, slice(p * LANES, (p + 1) * LANES))
             for i in range(z_scr.shape[0]) for d in range(2) for p in range(n_pairs)]
    ld = lambda name: [dir_refs[d][name][0, i, :, sl] for i, d, _, sl in tiles]
    s_old = [z_scr[i, d, p] for i, d, p, _ in tiles]
    sb = [s.astype(BF16) for s in s_old]
    if emit_y:
        ur = [_dot_nt(_stack(pm, rt), s) for pm, rt, s in zip(ld("pm"), ld("rt"), sb)]
    else:
        ur = [_dot_nt(pm, s) for pm, s in zip(ld("pm"), sb)]
    u = [x[:CHUNK] + q.astype(F32) for x, q in zip(ur, ld("qm"))]
    inc = [_dot_tn(ui.astype(BF16), bt) for ui, bt in zip(u, ld("bt"))]
    if emit_y:
        yc = [_dot(a, unfold(ui).astype(BF16)) for a, ui in zip(ld("arb"), u)]
        for (i, d, _, sl), x, ys, h in zip(tiles, ur, yc, ld("hm")):
            y_refs[d][i, :, sl] = x[CHUNK:] + ys + h.astype(F32)
    for (i, d, p, sl), s, dz, kv in zip(tiles, s_old, inc, ld("kv")):
        z_scr[i, d, p] = (s + jnp.where(bd, dz, 0.0) + unfold(kv)) * dir_refs[d]["wc"][i, 0, d:d + 1, sl]

    if not emit_y:
        @pl.when(c == pl.num_programs(0) - 1)
        def _():
            zfin_ref[...] = z_scr[...]


def _wkv_scan(prep, z0, *, emit_y):
    wc = prep[7]
    _, b, l, w = prep[0].shape
    nc = l // CHUNK
    n_pairs = w // LANES
    fwd = lambda c: c
    rev = lambda c: nc - 1 - c
    in_specs, args = [], []
    for d, cm in enumerate((fwd, rev)):
        for arr in prep[:7]:
            in_specs.append(pl.BlockSpec((1, b, CHUNK, w), lambda c, d=d, cm=cm: (d, 0, cm(c), 0)))
            args.append(arr)
        in_specs.append(pl.BlockSpec((b, 1, 2, w), lambda c, cm=cm: (0, cm(c), 0, 0)))
        args.append(wc)
    zshape = (b, 2, n_pairs, LANES, LANES)
    zspec = pl.BlockSpec(zshape, lambda c: (0, 0, 0, 0, 0))
    in_specs.append(zspec)
    args.append(z0)
    if emit_y:
        out_specs = [pl.BlockSpec((b, CHUNK, w), lambda c: (0, c, 0)),
                     pl.BlockSpec((b, CHUNK, w), lambda c: (0, nc - 1 - c, 0))]
        out_shape = [jax.ShapeDtypeStruct((b, l, w), F32)] * 2
    else:
        out_specs = zspec
        out_shape = jax.ShapeDtypeStruct(zshape, F32)
    return pl.pallas_call(
        functools.partial(_wkv_kernel, emit_y=emit_y, n_pairs=n_pairs),
        grid=(nc,),
        in_specs=in_specs,
        out_specs=out_specs,
        out_shape=out_shape,
        scratch_shapes=[pltpu.VMEM(zshape, F32)],
        compiler_params=_params("arbitrary"),
        name="wkv_scan_latent" if emit_y else "wkv_scan_context",
    )(*args)


def _merge_kernel(x_ref, mod_ref, ya_ref, yf_ref, yr_ref, bonus_ref, g_ref, gate_ref,
                  lnw_ref, lnb_ref, wba_ref, wbr_ref, wo_ref, n2_ref, wu_ref, wd_ref, nf_ref,
                  o_ref, *, ff_chunk):
    x = x_ref[0]
    d = x.shape[1]
    mod = lambda j: mod_ref[0, j:j + 1, :]
    y = yf_ref[0] + yr_ref[0]
    ones_bd = _head_ones(y.shape[1])
    inv = 1.0 / HEAD_DIM
    mu = _head_sum(y, ones_bd) * inv
    yc = y - mu
    var = _head_sum(yc * yc, ones_bd) * inv
    yn = yc * lax.rsqrt(var + LNX_EPS)
    yr = (yn * lnw_ref[...] + lnb_ref[...] + bonus_ref[0]) * g_ref[0].astype(F32)
    gate = gate_ref[0].astype(F32)
    merged = gate[:, :d] * _dot(ya_ref[0], wba_ref[...]) + gate[:, d:] * _dot(yr.astype(BF16), wbr_ref[...])
    x1 = x + mod(2) * _dot(merged.astype(BF16), wo_ref[...])
    h2 = (_rmsnorm(x1, n2_ref[...]) * (1.0 + mod(4)) + mod(3)).astype(BF16)
    acc = jnp.zeros_like(x1)
    for j in range(wu_ref.shape[1] // ff_chunk):
        cs = slice(j * ff_chunk, (j + 1) * ff_chunk)
        up = jnp.maximum(_dot(h2, wu_ref[:, cs]), 0.0)
        acc = acc + _dot((up * up).astype(BF16), wd_ref[cs, :])
    x2 = x1 + mod(5) * acc
    o_ref[0] = _rmsnorm(x2, nf_ref[...])


def _merge_mlp(x, mod, ya, yf, yr, bonus, g, gate, prm, *, tile):
    b, l, d = x.shape
    tok = lambda arr: pl.BlockSpec((1, tile, arr.shape[2]), lambda i, t: (i, t, 0))
    consts = [prm["lnx_w"], prm["lnx_b"], prm["w_branch_attn"], prm["w_branch_rwkv"], prm["w_out"],
              prm["norm2_g"], prm["w_mlp_up"], prm["w_mlp_down"], prm["norm_f_g"]]
    toks = [ya, yf, yr, bonus, g, gate]
    return pl.pallas_call(
        functools.partial(_merge_kernel, ff_chunk=min(1024, prm["w_mlp_up"].shape[1])),
        grid=(b, l // tile),
        in_specs=[tok(x), pl.BlockSpec((1,) + mod.shape[1:], lambda i, t: (i, 0, 0))]
        + [tok(a) for a in toks] + [_const_spec(c.shape) for c in consts],
        out_specs=tok(x),
        out_shape=jax.ShapeDtypeStruct(x.shape, x.dtype),
        compiler_params=_params("parallel", "parallel"),
        name="merge_mlp",
    )(x, mod, *toks, *consts)


def _rope_tables(l):
    n_freq = HEAD_DIM // 4
    inv_freq = jnp.power(ROPE_BASE, -jnp.arange(n_freq, dtype=F32) / n_freq)
    rows = l // GRID_W
    row = jnp.repeat(jnp.arange(rows, dtype=F32), GRID_W)
    col = jnp.tile(jnp.arange(GRID_W, dtype=F32), rows)
    ang = jnp.concatenate([row[:, None] * inv_freq, col[:, None] * inv_freq], axis=-1)
    cos, sin = jnp.cos(ang), jnp.sin(ang)
    reps = LANES // HEAD_DIM
    return (jnp.tile(jnp.concatenate([cos, cos], axis=1), (1, reps)),
            jnp.tile(jnp.concatenate([-sin, sin], axis=1), (1, reps)))


def _dup_heads(w, n_heads):
    cols = [w[:, h * HEAD_DIM:(h + 1) * HEAD_DIM] for h in range(n_heads)]
    return jnp.concatenate([c for c in cols for _ in range(PAIR)], axis=1)


def _pad_cols(w, width):
    return jnp.pad(w, ((0, 0), (0, width - w.shape[1])))


def kernel(x, c, ctx, c_ctx, w_ada, b_ada, norm1_g, w_in, sink, conv_w, decay_w0, decay_w2, iclr_a0, iclr_a2, gate_g2, k_k, k_a, r_k, lnx_w, lnx_b, w_branch_attn, w_branch_rwkv, w_out, norm2_g, w_mlp_up, w_mlp_down, norm_f_g):
    assert w_in.shape[0] == 1, "single-layer block: context tokens are read, never updated"
    b, l, d = x.shape
    attn_w = w_branch_attn.shape[1]
    rw = w_branch_rwkv.shape[1]
    n_q = attn_w // HEAD_DIM
    n_kv = n_q // Q_PER_KV
    kv_w = n_kv * HEAD_DIM
    assert n_kv * PAIR * HEAD_DIM == n_kv * LANES and rw % LANES == 0 and l % 256 == 0 and ctx.shape[1] % 256 == 0

    w = w_in[0]
    o_k, o_v, o_r = attn_w, attn_w + kv_w, attn_w + 2 * kv_w
    o_l = o_r + 3 * rw
    o_g = o_l + DECAY_LORA + ICLR_LORA + GATE_LORA
    seg_k = _dup_heads(w[:, o_k:o_v], n_kv)
    seg_v = _dup_heads(w[:, o_v:o_r], n_kv)
    seg_rkv = w[:, o_r:o_l]
    seg_lora = _pad_cols(w[:, o_l:o_g], LORA_PAD)
    w_ctx = jnp.concatenate([seg_k, seg_v, seg_rkv, seg_lora], axis=1).astype(BF16)
    w_lat = jnp.concatenate([w[:, :o_k].astype(BF16), w_ctx, w[:, o_g:].astype(BF16)], axis=1)
    widths_ctx = {"k": seg_k.shape[1], "v": seg_v.shape[1], "rkv": 3 * rw, "lora": LORA_PAD}
    widths_lat = {"q": attn_w, **widths_ctx, "gate": 2 * d}

    cw = conv_w[0]
    lora_w = jnp.zeros((LORA_PAD, 5 * rw), F32)
    lora_w = lora_w.at[:DECAY_LORA, :rw].set(decay_w2[0, 0]).at[:DECAY_LORA, rw:2 * rw].set(decay_w2[0, 1])
    r1 = DECAY_LORA + ICLR_LORA
    lora_w = lora_w.at[DECAY_LORA:r1, 2 * rw:3 * rw].set(iclr_a2[0, 0]).at[DECAY_LORA:r1, 3 * rw:4 * rw].set(iclr_a2[0, 1])
    lora_w = lora_w.at[r1:r1 + GATE_LORA, 4 * rw:].set(gate_g2[0])
    prm = {
        "conv_rkv": cw[:, :3 * rw], "conv_lora": _pad_cols(cw[:, 3 * rw:], LORA_PAD),
        "k_k": k_k[0].reshape(1, rw), "k_a": k_a[0].reshape(1, rw),
        "decay_w0": decay_w0[0], "iclr_a0": iclr_a0[0], "lora_w": lora_w.astype(BF16),
        "r_k": r_k[0].reshape(1, rw),
        "lnx_w": lnx_w[0].reshape(1, rw), "lnx_b": lnx_b[0].reshape(1, rw),
        "w_branch_attn": w_branch_attn[0].astype(BF16), "w_branch_rwkv": w_branch_rwkv[0].astype(BF16),
        "w_out": w_out[0].astype(BF16), "norm2_g": norm2_g[0].reshape(1, d),
        "w_mlp_up": w_mlp_up[0].astype(BF16), "w_mlp_down": w_mlp_down[0].astype(BF16),
        "norm_f_g": norm_f_g.reshape(1, d),
    }

    rows = -(-(b + 1) // 8) * 8
    cc = jnp.zeros((rows, d), F32).at[:b].set(c).at[b].set(c_ctx)
    mod = _ada_mod(cc, w_ada[0], b_ada[0]).reshape(rows, -1, d)

    q, kd, vd, rkv, lora, gate = _in_proj(x, mod, b, norm1_g[0], w_lat, widths_lat, _rope_tables(l),
                                          latent=True, tile=512)
    kxd, vxd, rkv_c, lora_c = _in_proj(ctx, mod, b, norm1_g[0], w_ctx, widths_ctx, None,
                                       latent=False, tile=256)
    ya = _attention(sink[0], q, kd, vd, kxd, vxd)

    prep_c = _wkv_prep(rkv_c, lora_c, prm, latent=False, tile=256)
    z_ctx = _wkv_scan(prep_c, jnp.zeros((b, 2, rw // LANES, LANES, LANES), F32), emit_y=False)
    prep = _wkv_prep(rkv, lora, prm, latent=True, tile=256)
    yf, yr = _wkv_scan(prep, z_ctx, emit_y=True)
    bonus, g = prep[8], prep[9]

    return _merge_mlp(x, mod, ya, yf, yr, bonus, g, gate, prm, tile=512)
```

```python
import functools
import math

import jax
import jax.numpy as jnp
from jax import lax
from jax.experimental import pallas as pl
from jax.experimental.pallas import tpu as pltpu

F32 = jnp.float32
BF16 = jnp.bfloat16

GRID_W = 64
HEAD_DIM = 64
Q_PER_KV = 4
ATTN_BLOCK = 128
ATTN_QB = 4
ROPE_BASE = 10000.0
NORM_EPS = 1e-6
LNX_EPS = 1e-5 * HEAD_DIM
DECAY_LORA, ICLR_LORA, GATE_LORA = 32, 32, 96
LORA_PAD = 256
CHUNK = 64
PREP_GROUP = 32
LANES = 128
MXU_DIM = 256
PAIR = LANES // HEAD_DIM
NEG = -1e30
VMEM_LIMIT = 56 * 1024 * 1024


def _dot(a, b):
    return jnp.dot(a, b, preferred_element_type=F32)


def _dot_nt(a, b):
    return lax.dot_general(a, b, (((1,), (1,)), ((), ())), preferred_element_type=F32)


def _dot_tn(a, b):
    return lax.dot_general(a, b, (((0,), (0,)), ((), ())), preferred_element_type=F32)


def _iota(shape, dim):
    return lax.broadcasted_iota(jnp.int32, shape, dim)


def _head_ones(width):
    r = _iota((width, width), 0) // HEAD_DIM
    c = _iota((width, width), 1) // HEAD_DIM
    return jnp.where(r == c, 1.0, 0.0).astype(BF16)


def _head_sum(x, ones_bd):
    gw = ones_bd.shape[0]
    hi = x.astype(BF16)
    lo = (x - hi.astype(F32)).astype(BF16)
    parts = [_dot(hi[:, j:j + gw], ones_bd) + _dot(lo[:, j:j + gw], ones_bd) for j in range(0, x.shape[1], gw)]
    return jnp.concatenate(parts, axis=1)


def _rmsnorm(x, g):
    ms = jnp.mean(x * x, axis=-1, keepdims=True)
    return x * lax.rsqrt(ms + NORM_EPS) * g


def _params(*sem):
    return pltpu.CompilerParams(dimension_semantics=sem, vmem_limit_bytes=VMEM_LIMIT)


def _const_spec(shape):
    nd = len(shape)
    return pl.BlockSpec(shape, lambda *_: (0,) * nd, pipeline_mode=pl.Buffered(1))


def _ada_kernel(c_ref, w_ref, b_ref, o_ref):
    c = c_ref[...]
    s = c * jax.nn.sigmoid(c)
    o_ref[...] = _dot(s.astype(BF16), w_ref[...].astype(BF16)) + b_ref[...]


def _ada_mod(cc, w_ada, b_ada):
    rows, d = cc.shape
    n = w_ada.shape[1]
    return pl.pallas_call(
        _ada_kernel,
        grid=(n // d,),
        in_specs=[pl.BlockSpec((rows, d), lambda j: (0, 0)),
                  pl.BlockSpec((d, d), lambda j: (0, j)),
                  pl.BlockSpec((1, d), lambda j: (0, j))],
        out_specs=pl.BlockSpec((rows, d), lambda j: (0, j)),
        out_shape=jax.ShapeDtypeStruct((rows, n), F32),
        compiler_params=_params("arbitrary"),
        name="ada_mod",
    )(cc, w_ada, b_ada.reshape(1, n))


def _rope(x, cos_t, sin_t):
    w = x.shape[1]
    half = HEAD_DIM // 2
    first = (_iota(x.shape, 1) % HEAD_DIM) < half
    swapped = jnp.where(first, pltpu.roll(x, w - half, 1), pltpu.roll(x, half, 1))
    reps = w // LANES
    c = jnp.concatenate([cos_t] * reps, axis=1)
    s = jnp.concatenate([sin_t] * reps, axis=1)
    return x * c + swapped * s


def _inproj_kernel(*refs, latent, widths):
    if latent:
        (x_ref, mod_ref, g_ref, wq_ref, w_ref, wg_ref, cos_ref, sin_ref,
         q_ref, k_ref, v_ref, rkv_ref, lora_ref, gate_ref) = refs
    else:
        x_ref, mod_ref, g_ref, w_ref, k_ref, v_ref, rkv_ref, lora_ref = refs
    x = x_ref[0]
    h = _rmsnorm(x, g_ref[...]) * (1.0 + mod_ref[0, 1:2, :]) + mod_ref[0, 0:1, :]
    hb = h.astype(BF16)
    off = 0

    def seg(name):
        nonlocal off
        lo = off
        off += widths[name]
        return _dot(hb, w_ref[:, lo:off])

    if latent:
        cos_t, sin_t = cos_ref[...], sin_ref[...]
        q_ref[0] = (_rope(_dot(hb, wq_ref[...]), cos_t, sin_t) * (HEAD_DIM ** -0.5)).astype(BF16)
        k_ref[0] = _rope(seg("k"), cos_t, sin_t).astype(BF16)
    else:
        k_ref[0] = seg("k").astype(BF16)
    v_ref[0] = seg("v").astype(BF16)
    rkv_ref[0] = seg("rkv")
    lora_ref[0] = seg("lora")
    if latent:
        gate_ref[0] = jax.nn.sigmoid(_dot(hb, wg_ref[...])).astype(BF16)


def _in_proj(x, mod, mod_row, norm_g, weights, widths, tables, *, latent, tile):
    b, l, d = x.shape
    nt = l // tile
    if latent:
        mod_map = lambda i, t: (i, 0, 0)
    else:
        mod_map = lambda i, t: (mod_row, 0, 0)
    tok = lambda w: pl.BlockSpec((1, tile, w), lambda i, t: (i, t, 0))
    in_specs = [tok(d),
                pl.BlockSpec((1,) + mod.shape[1:], mod_map),
                _const_spec((1, d))] + [_const_spec(w.shape) for w in weights]
    args = [x, mod, norm_g.reshape(1, d), *weights]
    out_specs, out_shape = [], []

    def out(w, dt):
        out_specs.append(tok(w))
        out_shape.append(jax.ShapeDtypeStruct((b, l, w), dt))

    if latent:
        in_specs += [pl.BlockSpec((tile, LANES), lambda i, t: (t, 0))] * 2
        args += list(tables)
        out(widths["q"], BF16)
    out(widths["k"], BF16)
    out(widths["v"], BF16)
    out(widths["rkv"], F32)
    out(widths["lora"], F32)
    if latent:
        out(widths["gate"], BF16)
    return pl.pallas_call(
        functools.partial(_inproj_kernel, latent=latent, widths=widths),
        grid=(b, nt),
        in_specs=in_specs,
        out_specs=out_specs,
        out_shape=out_shape,
        compiler_params=_params("parallel", "parallel"),
        name="in_proj_latent" if latent else "in_proj_context",
    )(*args)


def _attn_kernel(sink_ref, q_ref, kp_ref, kc_ref, kn_ref, vp_ref, vc_ref, vn_ref,
                 kx_ref, vx_ref, o_ref, *, n_kv):
    i = pl.program_id(1)
    last = pl.num_programs(1) - 1
    blk = ATTN_BLOCK
    qi = _iota((blk, blk), 0)
    kj = _iota((blk, blk), 1)
    left = _iota((blk, LANES), 1) < HEAD_DIM

    def key_block(refs, j, gs):
        ref_p, ref_c, ref_n = refs[:3]
        if j < 0:
            return ref_p[0, :, gs]
        if j >= ATTN_QB:
            return ref_n[0, :, gs]
        return ref_c[0, j * blk:(j + 1) * blk, gs]

    items = []
    for qb in range(ATTN_QB):
        lo_ok = kj >= qi
        hi_ok = kj <= qi
        if qb == 0:
            lo_ok = lo_ok & (i > 0)
        if qb == ATTN_QB - 1:
            hi_ok = hi_ok & (i < last)
        bias_lo = jnp.concatenate([jnp.where(lo_ok, 0.0, NEG)] * Q_PER_KV, axis=0)
        bias_hi = jnp.concatenate([jnp.where(hi_ok, 0.0, NEG)] * Q_PER_KV, axis=0)
        rows = slice(qb * blk, (qb + 1) * blk)
        q = q_ref[0, rows, :].astype(F32)
        for g in range(n_kv):
            gs = slice(g * LANES, (g + 1) * LANES)
            heads = range(g * Q_PER_KV, (g + 1) * Q_PER_KV)
            qs, sinks = [], []
            for hd in heads:
                qp = q[:, (hd // PAIR) * LANES:(hd // PAIR + 1) * LANES]
                keep = left if hd % PAIR == 0 else jnp.logical_not(left)
                qs.append(jnp.where(keep, qp, 0.0).astype(BF16))
                sinks.append(jnp.full((blk, 1), sink_ref[hd], F32))
            items.append({
                "qb": qb, "gs": gs, "rows": rows, "heads": heads, "bias": (bias_lo, bias_hi),
                "qs": jnp.concatenate(qs, axis=0), "sink": jnp.concatenate(sinks, axis=0)})

    def cat(refs, it):
        return jnp.concatenate([key_block(refs, it["qb"] + j, it["gs"]) for j in (-1, 0, 1)]
                               + [refs[3][0, :, it["gs"]]], axis=0)

    s = [_dot_nt(it["qs"], cat((kp_ref, kc_ref, kn_ref, kx_ref), it)) for it in items]
    s = [jnp.concatenate([x[:, :blk] + it["bias"][0], x[:, blk:2 * blk], x[:, 2 * blk:3 * blk] + it["bias"][1],
                          x[:, 3 * blk:]], axis=1) for x, it in zip(s, items)]
    m = [jnp.maximum(jnp.max(x, axis=1, keepdims=True), it["sink"]) for x, it in zip(s, items)]
    p = [jnp.exp(x - mx) for x, mx in zip(s, m)]
    den = [jnp.sum(x, axis=1, keepdims=True) + jnp.exp(it["sink"] - mx) for x, mx, it in zip(p, m, items)]
    o = [_dot(x.astype(BF16), cat((vp_ref, vc_ref, vn_ref, vx_ref), it)) / dn for x, dn, it in zip(p, den, items)]
    for x, it in zip(o, items):
        for hd in it["heads"][::PAIR]:
            j = hd - it["heads"][0]
            pair = jnp.where(left, x[j * blk:(j + 1) * blk], x[(j + 1) * blk:(j + 2) * blk])
            col = (hd // PAIR) * LANES
            o_ref[0, it["rows"], col:col + LANES] = pair.astype(BF16)


def _attention(sink, q, kd, vd, kxd, vxd):
    b, l, wq = q.shape
    wk = kd.shape[2]
    lc = kxd.shape[1]
    nb = l // ATTN_BLOCK
    span = ATTN_QB * ATTN_BLOCK
    blk = lambda w, f: pl.BlockSpec((1, ATTN_BLOCK, w), f)
    prev = lambda bi, i: (bi, jnp.maximum(i * ATTN_QB - 1, 0), 0)
    nxt = lambda bi, i: (bi, jnp.minimum((i + 1) * ATTN_QB, nb - 1), 0)
    cur = lambda w: pl.BlockSpec((1, span, w), lambda bi, i: (bi, i, 0))
    ctx = pl.BlockSpec((1, lc, wk), lambda bi, i: (bi, 0, 0))
    return pl.pallas_call(
        functools.partial(_attn_kernel, n_kv=wk // LANES),
        grid=(b, l // span),
        in_specs=[pl.BlockSpec(memory_space=pltpu.SMEM),
                  cur(wq),
                  blk(wk, prev), cur(wk), blk(wk, nxt),
                  blk(wk, prev), cur(wk), blk(wk, nxt),
                  ctx, ctx],
        out_specs=cur(wq),
        out_shape=jax.ShapeDtypeStruct((b, l, wq), BF16),
        compiler_params=_params("parallel", "parallel"),
        name="attention",
    )(sink, q, kd, kd, kd, vd, vd, vd, kxd, vxd)


def _conv3(x, prev_row, next_row, w):
    n = x.shape[0]
    row = _iota(x.shape, 0)
    xm = jnp.where(row == 0, prev_row, pltpu.roll(x, 1, 0))
    xp = jnp.where(row == n - 1, next_row, pltpu.roll(x, n - 1, 0))
    return xm * w[0:1] + x * w[1:2] + xp * w[2:3]


def _pair_masks():
    n = 2 * CHUNK
    row = _iota((n, n), 0)
    lane = _iota((n, n), 1)
    top, left = row < CHUNK, lane < CHUNK
    return {"row": row % CHUNK, "lane": lane % CHUNK, "top": top, "left": left, "bd": top == left,
            "left_h": _iota((CHUNK, n), 1) < CHUNK}


def _stack(a, b):
    return jnp.concatenate([a, b], axis=0)


def _fold(x, m):
    return jnp.where(m["left_h"], x[:CHUNK], x[CHUNK:])


def _chunk_local(inst, m):
    bf = lambda x: x.astype(BF16)
    diag = jnp.logical_not(m["top"]) & (m["lane"] == m["row"])
    masks = ((m["lane"] < m["row"]) | diag, (m["lane"] > m["row"]) | diag)
    mask_a = [masks[i["rev"]] for i in inst]
    lh = m["left_h"]
    half = CHUNK // 2
    keep_l = jnp.where(m["left"], 1.0, 0.0).astype(BF16)
    keep_r = jnp.where(m["left"], 0.0, 1.0).astype(BF16)
    keep_lh = jnp.where(lh, 1.0, 0.0).astype(BF16)
    keep_rh = jnp.where(lh, 0.0, 1.0).astype(BF16)

    def unfold(xb, anti=False):
        a, b = xb * keep_lh, xb * keep_rh
        return _stack(b, a) if anti else _stack(a, b)

    lhs = [bf(_stack(i["at"], i["rt"])) for i in inst]
    a01 = [_dot_nt(l, _stack(bf(_stack(i["bt"], i["kt"])) * keep_l, bf(_stack(i["kt"], i["bt"])) * keep_r))
           for l, i in zip(lhs, inst)]
    a0 = [jnp.where(ma, a[:, :LANES], 0.0) for ma, a in zip(mask_a, a01)]
    a1 = [jnp.where(ma, a[:, LANES:], 0.0) for ma, a in zip(mask_a, a01)]
    nc = [jnp.where(lh, x[:CHUNK], y[:CHUNK]) for x, y in zip(a0, a1)]
    arb = [jnp.where(lh, x[CHUNK:], y[CHUNK:]) for x, y in zip(a0, a1)]
    ak_ark_sw = [bf(jnp.where(m["left"], y, x)) for x, y in zip(a0, a1)]
    vh = [_dot(a, unfold(bf(i["v"]), anti=True)) for a, i in zip(ak_ark_sw, inst)]
    eye = jnp.where(_iota((CHUNK, LANES), 1) % CHUNK == _iota((CHUNK, LANES), 0), 1.0, 0.0)
    tc = [eye + n for n in nc]
    ncb = [bf(n) for n in nc]
    nc = [_dot(n, unfold(n)) for n in ncb]
    steps = CHUNK.bit_length() - 1
    for _ in range(steps - 2):
        ncb = [bf(n) for n in nc]
        both = [_dot(n, jnp.concatenate([unfold(bf(t)), unfold(n)], axis=1)) for n, t in zip(ncb, tc)]
        tc = [t + r[:, :LANES] for t, r in zip(tc, both)]
        nc = [r[:, LANES:] for r in both]
    inc = [_dot(bf(n[half:]), unfold(bf(t))) for n, t in zip(nc, tc)]
    tc = [jnp.concatenate([t[:half], t[half:] + d], axis=0) for t, d in zip(tc, inc)]
    pq = [_dot(bf(t), jnp.concatenate([unfold(l[:CHUNK]), unfold(bf(x[:CHUNK]))], axis=1))
          for t, l, x in zip(tc, lhs, vh)]
    kvf = [_dot_tn(bf(i["v"]), bf(i["kt"])) for i in inst]
    return [{"pm": r[:, :LANES], "qm": r[:, LANES:], "arb": b, "hm": x[CHUNK:], "kv": _fold(k, m)}
            for r, b, x, k in zip(pq, arb, vh, kvf)]


def _prep_kernel(*refs, latent, width):
    (rkv_ref, rkv_p, rkv_n, lora_ref, lora_p, lora_n, cw_ref, cwl_ref, kk_ref, ka_ref,
     w0_ref, a0_ref, wl_ref, rk_ref) = refs[:14]
    outs = refs[14:]
    out_refs = dict(zip(("pm", "qm", "rt", "bt", "arb", "hm", "kv"), outs[:7]))
    wc_ref = outs[7]
    if latent:
        bonus_ref, g_ref = outs[8:]
    t = pl.program_id(1)
    nt = pl.num_programs(1)
    tile = rkv_ref.shape[1]
    w = width
    has_prev = (t > 0).astype(F32)
    has_next = (t < nt - 1).astype(F32)
    u = _conv3(rkv_ref[0], rkv_p[0, 7:8, :] * has_prev, rkv_n[0, 0:1, :] * has_next, cw_ref[...])
    ul = _conv3(lora_ref[0], lora_p[0, 7:8, :] * has_prev, lora_n[0, 0:1, :] * has_next, cwl_ref[...])
    r, k, v = u[:, :w], u[:, w:2 * w], u[:, 2 * w:]
    ones_bd = _head_ones(min(MXU_DIM, w))

    kk = k * kk_ref[...]
    kk = kk * lax.rsqrt(jnp.maximum(_head_sum(kk * kk, ones_bd), 1e-24))

    lane = _iota(ul.shape, 1)
    lin = jnp.where(lane < DECAY_LORA, jnp.tanh(ul),
                    jnp.where(lane < DECAY_LORA + ICLR_LORA, ul, jax.nn.sigmoid(ul)))
    proj = _dot(lin.astype(BF16), wl_ref[...])

    tr = _iota((tile, tile), 0)
    tc = _iota((tile, tile), 1)
    same = (tr // CHUNK) == (tc // CHUNK)
    tri = (jnp.where(same & (tc <= tr), 1.0, 0.0).astype(BF16),
           jnp.where(same & (tc >= tr), 1.0, 0.0).astype(BF16))

    def exact_dot(m, x):
        h1 = x.astype(BF16)
        r1 = x - h1.astype(F32)
        h2 = r1.astype(BF16)
        h3 = (r1 - h2.astype(F32)).astype(BF16)
        return _dot(m, h1) + _dot(m, h2) + _dot(m, h3)

    masks = _pair_masks()
    k_sum = None
    work = []
    for d in range(2):
        z = w0_ref[d:d + 1, :] + proj[:, d * w:(d + 1) * w]
        lw = -math.exp(-0.5) * jax.nn.sigmoid(z)
        a = jax.nn.sigmoid(a0_ref[d:d + 1, :] + proj[:, (2 + d) * w:(3 + d) * w])
        kd = k * (1.0 + (a - 1.0) * ka_ref[...])
        k_sum = kd if k_sum is None else k_sum + kd
        cum = exact_dot(tri[d], lw)
        e_neg = jnp.exp(-cum)
        full = {"at": -kk * jnp.exp(cum - lw), "rt": r * jnp.exp(cum), "bt": kk * a * e_neg,
                "kt": kd * e_neg, "v": v}
        out_refs["rt"][d, 0] = full["rt"].astype(BF16)
        out_refs["bt"][d, 0] = full["bt"].astype(BF16)
        for j in range(tile // CHUNK):
            edge = (j + 1) * CHUNK - 1 if d == 0 else j * CHUNK
            wc_ref[0, j, d:d + 1, :] = jnp.exp(cum[edge:edge + 1, :])
        work += [(d, slice(j * CHUNK, (j + 1) * CHUNK), slice(p * LANES, (p + 1) * LANES), full)
                 for j in range(tile // CHUNK) for p in range(w // LANES)]
    for g0 in range(0, len(work), PREP_GROUP):
        group = work[g0:g0 + PREP_GROUP]
        inst = [dict({name: val[rs, ls] for name, val in full.items()}, rev=d) for d, rs, ls, full in group]
        for (d, rs, ls, _), res in zip(group, _chunk_local(inst, masks)):
            for name, val in res.items():
                out_refs[name][d, 0, rs, ls] = val.astype(BF16)
    if latent:
        bonus_ref[0] = _head_sum(r * k_sum * rk_ref[...], ones_bd) * v
        g_ref[0] = proj[:, 4 * w:5 * w].astype(BF16)


def _wkv_prep(rkv, lora, prm, *, latent, tile):
    b, l, w3 = rkv.shape
    w = w3 // 3
    nt = l // tile
    n8 = l // 8
    tok = lambda wd: pl.BlockSpec((1, tile, wd), lambda i, t: (i, t, 0))
    prev = lambda wd: pl.BlockSpec((1, 8, wd), lambda i, t: (i, jnp.maximum(t * (tile // 8) - 1, 0), 0))
    nxt = lambda wd: pl.BlockSpec((1, 8, wd), lambda i, t: (i, jnp.minimum((t + 1) * (tile // 8), n8 - 1), 0))
    wl = lora.shape[2]
    in_specs = [tok(w3), prev(w3), nxt(w3), tok(wl), prev(wl), nxt(wl)]
    consts = [prm["conv_rkv"], prm["conv_lora"], prm["k_k"], prm["k_a"], prm["decay_w0"],
              prm["iclr_a0"], prm["lora_w"], prm["r_k"]]
    in_specs += [_const_spec(c.shape) for c in consts]
    dirtok = pl.BlockSpec((2, 1, tile, w), lambda i, t: (0, i, t, 0))
    out_specs = [dirtok] * 7 + [pl.BlockSpec((1, tile // CHUNK, 2, w), lambda i, t: (i, t, 0, 0))]
    out_shape = [jax.ShapeDtypeStruct((2, b, l, w), BF16)] * 7 + [
        jax.ShapeDtypeStruct((b, l // CHUNK, 2, w), F32)]
    if latent:
        out_specs += [tok(w), tok(w)]
        out_shape += [jax.ShapeDtypeStruct((b, l, w), F32), jax.ShapeDtypeStruct((b, l, w), BF16)]
    return pl.pallas_call(
        functools.partial(_prep_kernel, latent=latent, width=w),
        grid=(b, nt),
        in_specs=in_specs,
        out_specs=out_specs,
        out_shape=out_shape,
        compiler_params=_params("parallel", "parallel"),
        name="wkv_prep_latent" if latent else "wkv_prep_context",
    )(rkv, rkv, rkv, lora, lora, lora, *consts)


def _wkv_kernel(*refs, emit_y, n_pairs):
    names = ("pm", "qm", "rt", "bt", "arb", "hm", "kv", "wc")
    dir_refs = (dict(zip(names, refs[0:8])), dict(zip(names, refs[8:16])))
    z0_ref = refs[16]
    if emit_y:
        y_refs = refs[17:19]
        z_scr = refs[19]
    else:
        zfin_ref = refs[17]
        z_scr = refs[18]
    c = pl.program_id(0)

    @pl.when(c == 0)
    def _():
        z_scr[...] = z0_ref[...]

    m = _pair_masks()
    bd, left_h = m["bd"], m["left_h"]

    def unfold(x):
        x = x.astype(F32)
        return _stack(jnp.where(left_h, x, 0.0), jnp.where(left_h, 0.0, x))

    tiles = [(i, d, p, slice(p * LANES, (p + 1) * LANES))
             for i in range(z_scr.shape[0]) for d in range(2) for p in range(n_pairs)]
    ld = lambda name: [dir_refs[d][name][0, i, :, sl] for i, d, _, sl in tiles]
    s_old = [z_scr[i, d, p] for i, d, p, _ in tiles]
    sb = [s.astype(BF16) for s in s_old]
    if emit_y:
        ur = [_dot_nt(_stack(pm, rt), s) for pm, rt, s in zip(ld("pm"), ld("rt"), sb)]
    else:
        ur = [_dot_nt(pm, s) for pm, s in zip(ld("pm"), sb)]
    u = [x[:CHUNK] + q.astype(F32) for x, q in zip(ur, ld("qm"))]
    inc = [_dot_tn(ui.astype(BF16), bt) for ui, bt in zip(u, ld("bt"))]
    if emit_y:
        yc = [_dot(a, unfold(ui).astype(BF16)) for a, ui in zip(ld("arb"), u)]
        for (i, d, _, sl), x, ys, h in zip(tiles, ur, yc, ld("hm")):
            y_refs[d][i, :, sl] = x[CHUNK:] + ys + h.astype(F32)
    for (i, d, p, sl), s, dz, kv in zip(tiles, s_old, inc, ld("kv")):
        z_scr[i, d, p] = (s + jnp.where(bd, dz, 0.0) + unfold(kv)) * dir_refs[d]["wc"][i, 0, d:d + 1, sl]

    if not emit_y:
        @pl.when(c == pl.num_programs(0) - 1)
        def _():
            zfin_ref[...] = z_scr[...]


def _wkv_scan(prep, z0, *, emit_y):
    wc = prep[7]
    _, b, l, w = prep[0].shape
    nc = l // CHUNK
    n_pairs = w // LANES
    fwd = lambda c: c
    rev = lambda c: nc - 1 - c
    in_specs, args = [], []
    for d, cm in enumerate((fwd, rev)):
        for arr in prep[:7]:
            in_specs.append(pl.BlockSpec((1, b, CHUNK, w), lambda c, d=d, cm=cm: (d, 0, cm(c), 0)))
            args.append(arr)
        in_specs.append(pl.BlockSpec((b, 1, 2, w), lambda c, cm=cm: (0, cm(c), 0, 0)))
        args.append(wc)
    zshape = (b, 2, n_pairs, LANES, LANES)
    zspec = pl.BlockSpec(zshape, lambda c: (0, 0, 0, 0, 0))
    in_specs.append(zspec)
    args.append(z0)
    if emit_y:
        out_specs = [pl.BlockSpec((b, CHUNK, w), lambda c: (0, c, 0)),
                     pl.BlockSpec((b, CHUNK, w), lambda c: (0, nc - 1 - c, 0))]
        out_shape = [jax.ShapeDtypeStruct((b, l, w), F32)] * 2
    else:
        out_specs = zspec
        out_shape = jax.ShapeDtypeStruct(zshape, F32)
    return pl.pallas_call(
        functools.partial(_wkv_kernel, emit_y=emit_y, n_pairs=n_pairs),
        grid=(nc,),
        in_specs=in_specs,
        out_specs=out_specs,
        out_shape=out_shape,
        scratch_shapes=[pltpu.VMEM(zshape, F32)],
        compiler_params=_params("arbitrary"),
        name="wkv_scan_latent" if emit_y else "wkv_scan_context",
    )(*args)


def _merge_kernel(x_ref, mod_ref, ya_ref, yf_ref, yr_ref, bonus_ref, g_ref, gate_ref,
                  lnw_ref, lnb_ref, wba_ref, wbr_ref, wo_ref, n2_ref, wu_ref, wd_ref, nf_ref,
                  o_ref, *, ff_chunk):
    x = x_ref[0]
    d = x.shape[1]
    mod = lambda j: mod_ref[0, j:j + 1, :]
    y = yf_ref[0] + yr_ref[0]
    gw = min(MXU_DIM, y.shape[1])
    ones_bd = _head_ones(gw)

    def head_mean(t):
        tb = t.astype(BF16)
        parts = [_dot(tb[:, j:j + gw], ones_bd) for j in range(0, t.shape[1], gw)]
        return jnp.concatenate(parts, axis=1) * (1.0 / HEAD_DIM)

    mu = head_mean(y)
    yc = y - mu
    var = head_mean(yc * yc)
    yn = yc * lax.rsqrt(var + LNX_EPS)
    yr = (yn * lnw_ref[...] + lnb_ref[...] + bonus_ref[0]) * g_ref[0].astype(F32)
    gate = gate_ref[0].astype(F32)
    merged = gate[:, :d] * _dot(ya_ref[0], wba_ref[...]) + gate[:, d:] * _dot(yr.astype(BF16), wbr_ref[...])
    x1 = x + mod(2) * _dot(merged.astype(BF16), wo_ref[...])
    h2 = (_rmsnorm(x1, n2_ref[...]) * (1.0 + mod(4)) + mod(3)).astype(BF16)
    acc = jnp.zeros_like(x1)
    for j in range(wu_ref.shape[1] // ff_chunk):
        cs = slice(j * ff_chunk, (j + 1) * ff_chunk)
        up = jnp.maximum(_dot(h2, wu_ref[:, cs]), 0.0)
        acc = acc + _dot((up * up).astype(BF16), wd_ref[cs, :])
    x2 = x1 + mod(5) * acc
    o_ref[0] = _rmsnorm(x2, nf_ref[...])


def _merge_mlp(x, mod, ya, yf, yr, bonus, g, gate, prm, *, tile):
    b, l, d = x.shape
    tok = lambda arr: pl.BlockSpec((1, tile, arr.shape[2]), lambda i, t: (i, t, 0))
    consts = [prm["lnx_w"], prm["lnx_b"], prm["w_branch_attn"], prm["w_branch_rwkv"], prm["w_out"],
              prm["norm2_g"], prm["w_mlp_up"], prm["w_mlp_down"], prm["norm_f_g"]]
    toks = [ya, yf, yr, bonus, g, gate]
    return pl.pallas_call(
        functools.partial(_merge_kernel, ff_chunk=min(1024, prm["w_mlp_up"].shape[1])),
        grid=(b, l // tile),
        in_specs=[tok(x), pl.BlockSpec((1,) + mod.shape[1:], lambda i, t: (i, 0, 0))]
        + [tok(a) for a in toks] + [_const_spec(c.shape) for c in consts],
        out_specs=tok(x),
        out_shape=jax.ShapeDtypeStruct(x.shape, x.dtype),
        compiler_params=_params("parallel", "parallel"),
        name="merge_mlp",
    )(x, mod, *toks, *consts)


def _rope_tables(l):
    n_freq = HEAD_DIM // 4
    inv_freq = jnp.power(ROPE_BASE, -jnp.arange(n_freq, dtype=F32) / n_freq)
    rows = l // GRID_W
    row = jnp.repeat(jnp.arange(rows, dtype=F32), GRID_W)
    col = jnp.tile(jnp.arange(GRID_W, dtype=F32), rows)
    ang = jnp.concatenate([row[:, None] * inv_freq, col[:, None] * inv_freq], axis=-1)
    cos, sin = jnp.cos(ang), jnp.sin(ang)
    reps = LANES // HEAD_DIM
    return (jnp.tile(jnp.concatenate([cos, cos], axis=1), (1, reps)),
            jnp.tile(jnp.concatenate([-sin, sin], axis=1), (1, reps)))


def _dup_heads(w, n_heads):
    cols = [w[:, h * HEAD_DIM:(h + 1) * HEAD_DIM] for h in range(n_heads)]
    return jnp.concatenate([c for c in cols for _ in range(PAIR)], axis=1)


def _pad_cols(w, width):
    return jnp.pad(w, ((0, 0), (0, width - w.shape[1])))


def kernel(x, c, ctx, c_ctx, w_ada, b_ada, norm1_g, w_in, sink, conv_w, decay_w0, decay_w2, iclr_a0, iclr_a2, gate_g2, k_k, k_a, r_k, lnx_w, lnx_b, w_branch_attn, w_branch_rwkv, w_out, norm2_g, w_mlp_up, w_mlp_down, norm_f_g):
    assert w_in.shape[0] == 1, "single-layer block: context tokens are read, never updated"
    b, l, d = x.shape
    attn_w = w_branch_attn.shape[1]
    rw = w_branch_rwkv.shape[1]
    n_q = attn_w // HEAD_DIM
    n_kv = n_q // Q_PER_KV
    kv_w = n_kv * HEAD_DIM
    assert n_kv * PAIR * HEAD_DIM == n_kv * LANES and rw % LANES == 0 and l % 256 == 0 and ctx.shape[1] % 256 == 0

    w = w_in[0]
    o_k, o_v, o_r = attn_w, attn_w + kv_w, attn_w + 2 * kv_w
    o_l = o_r + 3 * rw
    o_g = o_l + DECAY_LORA + ICLR_LORA + GATE_LORA
    seg_k = _dup_heads(w[:, o_k:o_v], n_kv)
    seg_v = _dup_heads(w[:, o_v:o_r], n_kv)
    seg_rkv = w[:, o_r:o_l]
    seg_lora = _pad_cols(w[:, o_l:o_g], LORA_PAD)
    w_ctx = jnp.concatenate([seg_k, seg_v, seg_rkv, seg_lora], axis=1).astype(BF16)
    w_lat = [w[:, :o_k].astype(BF16), w_ctx, w[:, o_g:].astype(BF16)]
    widths_ctx = {"k": seg_k.shape[1], "v": seg_v.shape[1], "rkv": 3 * rw, "lora": LORA_PAD}
    widths_lat = {"q": attn_w, **widths_ctx, "gate": 2 * d}

    cw = conv_w[0]
    lora_w = jnp.zeros((LORA_PAD, 5 * rw), F32)
    lora_w = lora_w.at[:DECAY_LORA, :rw].set(decay_w2[0, 0]).at[:DECAY_LORA, rw:2 * rw].set(decay_w2[0, 1])
    r1 = DECAY_LORA + ICLR_LORA
    lora_w = lora_w.at[DECAY_LORA:r1, 2 * rw:3 * rw].set(iclr_a2[0, 0]).at[DECAY_LORA:r1, 3 * rw:4 * rw].set(iclr_a2[0, 1])
    lora_w = lora_w.at[r1:r1 + GATE_LORA, 4 * rw:].set(gate_g2[0])
    prm = {
        "conv_rkv": cw[:, :3 * rw], "conv_lora": _pad_cols(cw[:, 3 * rw:], LORA_PAD),
        "k_k": k_k[0].reshape(1, rw), "k_a": k_a[0].reshape(1, rw),
        "decay_w0": decay_w0[0], "iclr_a0": iclr_a0[0], "lora_w": lora_w.astype(BF16),
        "r_k": r_k[0].reshape(1, rw),
        "lnx_w": lnx_w[0].reshape(1, rw), "lnx_b": lnx_b[0].reshape(1, rw),
        "w_branch_attn": w_branch_attn[0].astype(BF16), "w_branch_rwkv": w_branch_rwkv[0].astype(BF16),
        "w_out": w_out[0].astype(BF16), "norm2_g": norm2_g[0].reshape(1, d),
        "w_mlp_up": w_mlp_up[0].astype(BF16), "w_mlp_down": w_mlp_down[0].astype(BF16),
        "norm_f_g": norm_f_g.reshape(1, d),
    }

    rows = -(-(b + 1) // 8) * 8
    cc = jnp.zeros((rows, d), F32).at[:b].set(c).at[b].set(c_ctx)
    mod = _ada_mod(cc, w_ada[0], b_ada[0]).reshape(rows, -1, d)

    q, kd, vd, rkv, lora, gate = _in_proj(x, mod, b, norm1_g[0], w_lat, widths_lat, _rope_tables(l),
                                          latent=True, tile=512)
    kxd, vxd, rkv_c, lora_c = _in_proj(ctx, mod, b, norm1_g[0], [w_ctx], widths_ctx, None,
                                       latent=False, tile=256)
    ya = _attention(sink[0], q, kd, vd, kxd, vxd)

    prep_c = _wkv_prep(rkv_c, lora_c, prm, latent=False, tile=256)
    z_ctx = _wkv_scan(prep_c, jnp.zeros((b, 2, rw // LANES, LANES, LANES), F32), emit_y=False)
    prep = _wkv_prep(rkv, lora, prm, latent=True, tile=256)
    yf, yr = _wkv_scan(prep, z_ctx, emit_y=True)
    bonus, g = prep[8], prep[9]

    return _merge_mlp(x, mod, ya, yf, yr, bonus, g, gate, prm, tile=512)
```

```python
import functools
import math

import jax
import jax.numpy as jnp
from jax import lax
from jax.experimental import pallas as pl
from jax.experimental.pallas import tpu as pltpu

F32 = jnp.float32
BF16 = jnp.bfloat16

GRID_W = 64
HEAD_DIM = 64
Q_PER_KV = 4
ATTN_BLOCK = 128
ATTN_QB = 4
ROPE_BASE = 10000.0
NORM_EPS = 1e-6
LNX_EPS = 1e-5 * HEAD_DIM
DECAY_LORA, ICLR_LORA, GATE_LORA = 32, 32, 96
LORA_PAD = 256
CHUNK = 64
PREP_GROUP = 32
LANES = 128
MXU_DIM = 256
PAIR = LANES // HEAD_DIM
NEG = -1e30
VMEM_LIMIT = 56 * 1024 * 1024


def _dot(a, b):
    return jnp.dot(a, b, preferred_element_type=F32)


def _dot_nt(a, b):
    return lax.dot_general(a, b, (((1,), (1,)), ((), ())), preferred_element_type=F32)


def _dot_tn(a, b):
    return lax.dot_general(a, b, (((0,), (0,)), ((), ())), preferred_element_type=F32)


def _iota(shape, dim):
    return lax.broadcasted_iota(jnp.int32, shape, dim)


def _head_ones(width):
    r = _iota((width, width), 0) // HEAD_DIM
    c = _iota((width, width), 1) // HEAD_DIM
    return jnp.where(r == c, 1.0, 0.0).astype(BF16)


def _head_sum(x, ones_bd):
    gw = ones_bd.shape[0]
    hi = x.astype(BF16)
    lo = (x - hi.astype(F32)).astype(BF16)
    parts = [_dot(hi[:, j:j + gw], ones_bd) + _dot(lo[:, j:j + gw], ones_bd) for j in range(0, x.shape[1], gw)]
    return jnp.concatenate(parts, axis=1)


def _rmsnorm(x, g):
    ms = jnp.mean(x * x, axis=-1, keepdims=True)
    return x * lax.rsqrt(ms + NORM_EPS) * g


def _params(*sem):
    return pltpu.CompilerParams(dimension_semantics=sem, vmem_limit_bytes=VMEM_LIMIT)


def _const_spec(shape):
    nd = len(shape)
    return pl.BlockSpec(shape, lambda *_: (0,) * nd, pipeline_mode=pl.Buffered(1))


def _ada_kernel(c_ref, w_ref, b_ref, o_ref):
    c = c_ref[...]
    s = c * jax.nn.sigmoid(c)
    o_ref[...] = _dot(s.astype(BF16), w_ref[...].astype(BF16)) + b_ref[...]


def _ada_mod(cc, w_ada, b_ada):
    rows, d = cc.shape
    n = w_ada.shape[1]
    return pl.pallas_call(
        _ada_kernel,
        grid=(n // d,),
        in_specs=[pl.BlockSpec((rows, d), lambda j: (0, 0)),
                  pl.BlockSpec((d, d), lambda j: (0, j)),
                  pl.BlockSpec((1, d), lambda j: (0, j))],
        out_specs=pl.BlockSpec((rows, d), lambda j: (0, j)),
        out_shape=jax.ShapeDtypeStruct((rows, n), F32),
        compiler_params=_params("arbitrary"),
        name="ada_mod",
    )(cc, w_ada, b_ada.reshape(1, n))


def _rope(x, cos_t, sin_t):
    w = x.shape[1]
    half = HEAD_DIM // 2
    first = (_iota(x.shape, 1) % HEAD_DIM) < half
    swapped = jnp.where(first, pltpu.roll(x, w - half, 1), pltpu.roll(x, half, 1))
    reps = w // LANES
    c = jnp.concatenate([cos_t] * reps, axis=1)
    s = jnp.concatenate([sin_t] * reps, axis=1)
    return x * c + swapped * s


def _inproj_kernel(*refs, latent, widths):
    if latent:
        (x_ref, mod_ref, g_ref, wq_ref, w_ref, wg_ref, cos_ref, sin_ref,
         q_ref, k_ref, v_ref, rkv_ref, lora_ref, gate_ref) = refs
    else:
        x_ref, mod_ref, g_ref, w_ref, k_ref, v_ref, rkv_ref, lora_ref = refs
    x = x_ref[0]
    h = _rmsnorm(x, g_ref[...]) * (1.0 + mod_ref[0, 1:2, :]) + mod_ref[0, 0:1, :]
    hb = h.astype(BF16)
    off = 0

    def seg(name):
        nonlocal off
        lo = off
        off += widths[name]
        return _dot(hb, w_ref[:, lo:off])

    def dup_heads(t):
        first = _iota(t.shape, 1) < HEAD_DIM
        other = pltpu.roll(t, HEAD_DIM, 1)
        return jnp.concatenate([jnp.where(first, t, other), jnp.where(first, other, t)], axis=1)

    kv = seg("kv")
    k, v = dup_heads(kv[:, :LANES]), dup_heads(kv[:, LANES:])
    if latent:
        cos_t, sin_t = cos_ref[...], sin_ref[...]
        q_ref[0] = (_rope(_dot(hb, wq_ref[...]), cos_t, sin_t) * (HEAD_DIM ** -0.5)).astype(BF16)
        k = _rope(k, cos_t, sin_t)
    k_ref[0] = k.astype(BF16)
    v_ref[0] = v.astype(BF16)
    rkv_ref[0] = seg("rkv")
    lora_ref[0] = seg("lora")
    if latent:
        gate_ref[0] = jax.nn.sigmoid(_dot(hb, wg_ref[...])).astype(BF16)


def _in_proj(x, mod, mod_row, norm_g, weights, widths, tables, *, latent, tile):
    b, l, d = x.shape
    nt = l // tile
    if latent:
        mod_map = lambda i, t: (i, 0, 0)
    else:
        mod_map = lambda i, t: (mod_row, 0, 0)
    tok = lambda w: pl.BlockSpec((1, tile, w), lambda i, t: (i, t, 0))
    in_specs = [tok(d),
                pl.BlockSpec((1,) + mod.shape[1:], mod_map),
                _const_spec((1, d))] + [_const_spec(w.shape) for w in weights]
    args = [x, mod, norm_g.reshape(1, d), *weights]
    out_specs, out_shape = [], []

    def out(w, dt):
        out_specs.append(tok(w))
        out_shape.append(jax.ShapeDtypeStruct((b, l, w), dt))

    if latent:
        in_specs += [pl.BlockSpec((tile, LANES), lambda i, t: (t, 0))] * 2
        args += list(tables)
        out(widths["q"], BF16)
    out(widths["k"], BF16)
    out(widths["v"], BF16)
    out(widths["rkv"], F32)
    out(widths["lora"], F32)
    if latent:
        out(widths["gate"], BF16)
    return pl.pallas_call(
        functools.partial(_inproj_kernel, latent=latent, widths=widths),
        grid=(b, nt),
        in_specs=in_specs,
        out_specs=out_specs,
        out_shape=out_shape,
        compiler_params=_params("parallel", "parallel"),
        name="in_proj_latent" if latent else "in_proj_context",
    )(*args)


def _attn_kernel(sink_ref, q_ref, kp_ref, kc_ref, kn_ref, vp_ref, vc_ref, vn_ref,
                 kx_ref, vx_ref, o_ref, *, n_kv):
    i = pl.program_id(1)
    last = pl.num_programs(1) - 1
    blk = ATTN_BLOCK
    qi = _iota((blk, blk), 0)
    kj = _iota((blk, blk), 1)
    left = _iota((blk, LANES), 1) < HEAD_DIM

    def key_block(refs, j, gs):
        ref_p, ref_c, ref_n = refs[:3]
        if j < 0:
            return ref_p[0, :, gs]
        if j >= ATTN_QB:
            return ref_n[0, :, gs]
        return ref_c[0, j * blk:(j + 1) * blk, gs]

    items = []
    for qb in range(ATTN_QB):
        lo_ok = kj >= qi
        hi_ok = kj <= qi
        if qb == 0:
            lo_ok = lo_ok & (i > 0)
        if qb == ATTN_QB - 1:
            hi_ok = hi_ok & (i < last)
        bias_lo = jnp.concatenate([jnp.where(lo_ok, 0.0, NEG)] * Q_PER_KV, axis=0)
        bias_hi = jnp.concatenate([jnp.where(hi_ok, 0.0, NEG)] * Q_PER_KV, axis=0)
        rows = slice(qb * blk, (qb + 1) * blk)
        q = q_ref[0, rows, :].astype(F32)
        for g in range(n_kv):
            gs = slice(g * LANES, (g + 1) * LANES)
            heads = range(g * Q_PER_KV, (g + 1) * Q_PER_KV)
            qs, sinks = [], []
            for hd in heads:
                qp = q[:, (hd // PAIR) * LANES:(hd // PAIR + 1) * LANES]
                keep = left if hd % PAIR == 0 else jnp.logical_not(left)
                qs.append(jnp.where(keep, qp, 0.0).astype(BF16))
                sinks.append(jnp.full((blk, 1), sink_ref[hd], F32))
            items.append({
                "qb": qb, "gs": gs, "rows": rows, "heads": heads, "bias": (bias_lo, bias_hi),
                "qs": jnp.concatenate(qs, axis=0), "sink": jnp.concatenate(sinks, axis=0)})

    def cat(refs, it):
        return jnp.concatenate([key_block(refs, it["qb"] + j, it["gs"]) for j in (-1, 0, 1)]
                               + [refs[3][0, :, it["gs"]]], axis=0)

    s = [_dot_nt(it["qs"], cat((kp_ref, kc_ref, kn_ref, kx_ref), it)) for it in items]
    s = [jnp.concatenate([x[:, :blk] + it["bias"][0], x[:, blk:2 * blk], x[:, 2 * blk:3 * blk] + it["bias"][1],
                          x[:, 3 * blk:]], axis=1) for x, it in zip(s, items)]
    m = [jnp.maximum(jnp.max(x, axis=1, keepdims=True), it["sink"]) for x, it in zip(s, items)]
    p = [jnp.exp(x - mx) for x, mx in zip(s, m)]
    den = [jnp.sum(x, axis=1, keepdims=True) + jnp.exp(it["sink"] - mx) for x, mx, it in zip(p, m, items)]
    o = [_dot(x.astype(BF16), cat((vp_ref, vc_ref, vn_ref, vx_ref), it)) / dn for x, dn, it in zip(p, den, items)]
    for x, it in zip(o, items):
        for hd in it["heads"][::PAIR]:
            j = hd - it["heads"][0]
            pair = jnp.where(left, x[j * blk:(j + 1) * blk], x[(j + 1) * blk:(j + 2) * blk])
            col = (hd // PAIR) * LANES
            o_ref[0, it["rows"], col:col + LANES] = pair.astype(BF16)


def _attention(sink, q, kd, vd, kxd, vxd):
    b, l, wq = q.shape
    wk = kd.shape[2]
    lc = kxd.shape[1]
    nb = l // ATTN_BLOCK
    span = ATTN_QB * ATTN_BLOCK
    blk = lambda w, f: pl.BlockSpec((1, ATTN_BLOCK, w), f)
    prev = lambda bi, i: (bi, jnp.maximum(i * ATTN_QB - 1, 0), 0)
    nxt = lambda bi, i: (bi, jnp.minimum((i + 1) * ATTN_QB, nb - 1), 0)
    cur = lambda w: pl.BlockSpec((1, span, w), lambda bi, i: (bi, i, 0))
    ctx = pl.BlockSpec((1, lc, wk), lambda bi, i: (bi, 0, 0))
    return pl.pallas_call(
        functools.partial(_attn_kernel, n_kv=wk // LANES),
        grid=(b, l // span),
        in_specs=[pl.BlockSpec(memory_space=pltpu.SMEM),
                  cur(wq),
                  blk(wk, prev), cur(wk), blk(wk, nxt),
                  blk(wk, prev), cur(wk), blk(wk, nxt),
                  ctx, ctx],
        out_specs=cur(wq),
        out_shape=jax.ShapeDtypeStruct((b, l, wq), BF16),
        compiler_params=_params("parallel", "parallel"),
        name="attention",
    )(sink, q, kd, kd, kd, vd, vd, vd, kxd, vxd)


def _conv3(x, prev_row, next_row, w):
    n = x.shape[0]
    row = _iota(x.shape, 0)
    xm = jnp.where(row == 0, prev_row, pltpu.roll(x, 1, 0))
    xp = jnp.where(row == n - 1, next_row, pltpu.roll(x, n - 1, 0))
    return xm * w[0:1] + x * w[1:2] + xp * w[2:3]


def _pair_masks():
    n = 2 * CHUNK
    row = _iota((n, n), 0)
    lane = _iota((n, n), 1)
    top, left = row < CHUNK, lane < CHUNK
    return {"row": row % CHUNK, "lane": lane % CHUNK, "top": top, "left": left, "bd": top == left,
            "left_h": _iota((CHUNK, n), 1) < CHUNK}


def _stack(a, b):
    return jnp.concatenate([a, b], axis=0)


def _fold(x, m):
    return jnp.where(m["left_h"], x[:CHUNK], x[CHUNK:])


def _chunk_local(inst, m):
    bf = lambda x: x.astype(BF16)
    diag = jnp.logical_not(m["top"]) & (m["lane"] == m["row"])
    masks = ((m["lane"] < m["row"]) | diag, (m["lane"] > m["row"]) | diag)
    mask_a = [masks[i["rev"]] for i in inst]
    lh = m["left_h"]
    half = CHUNK // 2
    keep_l = jnp.where(m["left"], 1.0, 0.0).astype(BF16)
    keep_r = jnp.where(m["left"], 0.0, 1.0).astype(BF16)
    keep_lh = jnp.where(lh, 1.0, 0.0).astype(BF16)
    keep_rh = jnp.where(lh, 0.0, 1.0).astype(BF16)

    def unfold(xb, anti=False):
        a, b = xb * keep_lh, xb * keep_rh
        return _stack(b, a) if anti else _stack(a, b)

    lhs = [bf(_stack(i["at"], i["rt"])) for i in inst]
    a01 = [_dot_nt(l, _stack(bf(_stack(i["bt"], i["kt"])) * keep_l, bf(_stack(i["kt"], i["bt"])) * keep_r))
           for l, i in zip(lhs, inst)]
    a0 = [jnp.where(ma, a[:, :LANES], 0.0) for ma, a in zip(mask_a, a01)]
    a1 = [jnp.where(ma, a[:, LANES:], 0.0) for ma, a in zip(mask_a, a01)]
    nc = [jnp.where(lh, x[:CHUNK], y[:CHUNK]) for x, y in zip(a0, a1)]
    arb = [jnp.where(lh, x[CHUNK:], y[CHUNK:]) for x, y in zip(a0, a1)]
    ak_ark_sw = [bf(jnp.where(m["left"], y, x)) for x, y in zip(a0, a1)]
    vh = [_dot(a, unfold(bf(i["v"]), anti=True)) for a, i in zip(ak_ark_sw, inst)]
    eye = jnp.where(_iota((CHUNK, LANES), 1) % CHUNK == _iota((CHUNK, LANES), 0), 1.0, 0.0)
    tc = [eye + n for n in nc]
    ncb = [bf(n) for n in nc]
    nc = [_dot(n, unfold(n)) for n in ncb]
    steps = CHUNK.bit_length() - 1
    for _ in range(steps - 2):
        ncb = [bf(n) for n in nc]
        both = [_dot(n, jnp.concatenate([unfold(bf(t)), unfold(n)], axis=1)) for n, t in zip(ncb, tc)]
        tc = [t + r[:, :LANES] for t, r in zip(tc, both)]
        nc = [r[:, LANES:] for r in both]
    inc = [_dot(bf(n[half:]), unfold(bf(t))) for n, t in zip(nc, tc)]
    tc = [jnp.concatenate([t[:half], t[half:] + d], axis=0) for t, d in zip(tc, inc)]
    pq = [_dot(bf(t), jnp.concatenate([unfold(l[:CHUNK]), unfold(bf(x[:CHUNK]))], axis=1))
          for t, l, x in zip(tc, lhs, vh)]
    kvf = [_dot_tn(bf(i["v"]), bf(i["kt"])) for i in inst]
    return [{"pm": r[:, :LANES], "qm": r[:, LANES:], "arb": b, "hm": x[CHUNK:], "kv": _fold(k, m)}
            for r, b, x, k in zip(pq, arb, vh, kvf)]


def _prep_kernel(*refs, latent, width):
    (rkv_ref, rkv_p, rkv_n, lora_ref, lora_p, lora_n, cw_ref, cwl_ref, kk_ref, ka_ref,
     w0_ref, a0_ref, wl_ref, rk_ref) = refs[:14]
    outs = refs[14:]
    out_refs = dict(zip(("pm", "qm", "rt", "bt", "arb", "kv"), outs[:6]))
    wc_ref = outs[6]
    if latent:
        bonus_ref, g_ref, hsum_ref = outs[7:]
    t = pl.program_id(1)
    nt = pl.num_programs(1)
    tile = rkv_ref.shape[1]
    w = width
    has_prev = (t > 0).astype(F32)
    has_next = (t < nt - 1).astype(F32)
    u = _conv3(rkv_ref[0], rkv_p[0, 7:8, :] * has_prev, rkv_n[0, 0:1, :] * has_next, cw_ref[...])
    ul = _conv3(lora_ref[0], lora_p[0, 7:8, :] * has_prev, lora_n[0, 0:1, :] * has_next, cwl_ref[...])
    r, k, v = u[:, :w], u[:, w:2 * w], u[:, 2 * w:]
    ones_bd = _head_ones(min(MXU_DIM, w))

    kk = k * kk_ref[...]
    kk = kk * lax.rsqrt(jnp.maximum(_head_sum(kk * kk, ones_bd), 1e-24))

    lane = _iota(ul.shape, 1)
    lin = jnp.where(lane < DECAY_LORA, jnp.tanh(ul),
                    jnp.where(lane < DECAY_LORA + ICLR_LORA, ul, jax.nn.sigmoid(ul)))
    proj = _dot(lin.astype(BF16), wl_ref[...])

    tr = _iota((tile, tile), 0)
    tc = _iota((tile, tile), 1)
    same = (tr // CHUNK) == (tc // CHUNK)
    tri = (jnp.where(same & (tc <= tr), 1.0, 0.0).astype(BF16),
           jnp.where(same & (tc >= tr), 1.0, 0.0).astype(BF16))

    def exact_dot(m, x):
        h1 = x.astype(BF16)
        r1 = x - h1.astype(F32)
        h2 = r1.astype(BF16)
        h3 = (r1 - h2.astype(F32)).astype(BF16)
        return _dot(m, h1) + _dot(m, h2) + _dot(m, h3)

    masks = _pair_masks()
    k_sum = None
    work = []
    for d in range(2):
        z = w0_ref[d:d + 1, :] + proj[:, d * w:(d + 1) * w]
        lw = -math.exp(-0.5) * jax.nn.sigmoid(z)
        a = jax.nn.sigmoid(a0_ref[d:d + 1, :] + proj[:, (2 + d) * w:(3 + d) * w])
        kd = k * (1.0 + (a - 1.0) * ka_ref[...])
        k_sum = kd if k_sum is None else k_sum + kd
        cum = exact_dot(tri[d], lw)
        e_neg = jnp.exp(-cum)
        full = {"at": -kk * jnp.exp(cum - lw), "rt": r * jnp.exp(cum), "bt": kk * a * e_neg,
                "kt": kd * e_neg, "v": v}
        out_refs["rt"][d, 0] = full["rt"].astype(BF16)
        out_refs["bt"][d, 0] = full["bt"].astype(BF16)
        for j in range(tile // CHUNK):
            edge = (j + 1) * CHUNK - 1 if d == 0 else j * CHUNK
            wc_ref[0, j, d:d + 1, :] = jnp.exp(cum[edge:edge + 1, :])
        work += [(d, slice(j * CHUNK, (j + 1) * CHUNK), slice(p * LANES, (p + 1) * LANES), full)
                 for j in range(tile // CHUNK) for p in range(w // LANES)]
    hsum = {}
    for g0 in range(0, len(work), PREP_GROUP):
        group = work[g0:g0 + PREP_GROUP]
        inst = [dict({name: val[rs, ls] for name, val in full.items()}, rev=d) for d, rs, ls, full in group]
        for (d, rs, ls, _), res in zip(group, _chunk_local(inst, masks)):
            seen = hsum.get((rs.start, ls.start))
            hsum[(rs.start, ls.start)] = (rs, ls, res["hm"] if seen is None else seen[2] + res["hm"])
            for name in res:
                if name in out_refs:
                    out_refs[name][d, 0, rs, ls] = res[name].astype(BF16)
    if latent:
        for rs, ls, val in hsum.values():
            hsum_ref[0, rs, ls] = val
        bonus_ref[0] = _head_sum(r * k_sum * rk_ref[...], ones_bd) * v
        g_ref[0] = proj[:, 4 * w:5 * w].astype(BF16)


def _wkv_prep(rkv, lora, prm, *, latent, tile):
    b, l, w3 = rkv.shape
    w = w3 // 3
    nt = l // tile
    n8 = l // 8
    tok = lambda wd: pl.BlockSpec((1, tile, wd), lambda i, t: (i, t, 0))
    prev = lambda wd: pl.BlockSpec((1, 8, wd), lambda i, t: (i, jnp.maximum(t * (tile // 8) - 1, 0), 0))
    nxt = lambda wd: pl.BlockSpec((1, 8, wd), lambda i, t: (i, jnp.minimum((t + 1) * (tile // 8), n8 - 1), 0))
    wl = lora.shape[2]
    in_specs = [tok(w3), prev(w3), nxt(w3), tok(wl), prev(wl), nxt(wl)]
    consts = [prm["conv_rkv"], prm["conv_lora"], prm["k_k"], prm["k_a"], prm["decay_w0"],
              prm["iclr_a0"], prm["lora_w"], prm["r_k"]]
    in_specs += [_const_spec(c.shape) for c in consts]
    dirtok = pl.BlockSpec((2, 1, tile, w), lambda i, t: (0, i, t, 0))
    out_specs = [dirtok] * 6 + [pl.BlockSpec((1, tile // CHUNK, 2, w), lambda i, t: (i, t, 0, 0))]
    out_shape = [jax.ShapeDtypeStruct((2, b, l, w), BF16)] * 6 + [
        jax.ShapeDtypeStruct((b, l // CHUNK, 2, w), F32)]
    if latent:
        out_specs += [tok(w), tok(w), tok(w)]
        out_shape += [jax.ShapeDtypeStruct((b, l, w), F32), jax.ShapeDtypeStruct((b, l, w), BF16),
                      jax.ShapeDtypeStruct((b, l, w), F32)]
    return pl.pallas_call(
        functools.partial(_prep_kernel, latent=latent, width=w),
        grid=(b, nt),
        in_specs=in_specs,
        out_specs=out_specs,
        out_shape=out_shape,
        compiler_params=_params("parallel", "parallel"),
        name="wkv_prep_latent" if latent else "wkv_prep_context",
    )(rkv, rkv, rkv, lora, lora, lora, *consts)


def _wkv_kernel(*refs, emit_y, n_pairs):
    names = ("pm", "qm", "rt", "bt", "arb", "kv", "wc")
    n = len(names)
    dir_refs = (dict(zip(names, refs[0:n])), dict(zip(names, refs[n:2 * n])))
    z0_ref = refs[2 * n]
    if emit_y:
        y_refs = refs[2 * n + 1:2 * n + 3]
        z_scr = refs[2 * n + 3]
    else:
        zfin_ref = refs[2 * n + 1]
        z_scr = refs[2 * n + 2]
    c = pl.program_id(0)

    @pl.when(c == 0)
    def _():
        z_scr[...] = z0_ref[...]

    m = _pair_masks()
    bd, left_h = m["bd"], m["left_h"]

    def unfold(x):
        x = x.astype(F32)
        return _stack(jnp.where(left_h, x, 0.0), jnp.where(left_h, 0.0, x))

    tiles = [(i, d, p, slice(p * LANES, (p + 1) * LANES))
             for i in range(z_scr.shape[0]) for d in range(2) for p in range(n_pairs)]
    ld = lambda name: [dir_refs[d][name][0, i, :, sl] for i, d, _, sl in tiles]
    s_old = [z_scr[i, d, p] for i, d, p, _ in tiles]
    sb = [s.astype(BF16) for s in s_old]
    if emit_y:
        ur = [_dot_nt(_stack(pm, rt), s) for pm, rt, s in zip(ld("pm"), ld("rt"), sb)]
    else:
        ur = [_dot_nt(pm, s) for pm, s in zip(ld("pm"), sb)]
    u = [x[:CHUNK] + q.astype(F32) for x, q in zip(ur, ld("qm"))]
    ub = [ui.astype(BF16) for ui in u]
    inc = [_dot_tn(ui, bt) for ui, bt in zip(ub, ld("bt"))]
    if emit_y:
        keep_l = jnp.where(left_h, 1.0, 0.0).astype(BF16)
        keep_r = jnp.where(left_h, 0.0, 1.0).astype(BF16)
        yc = [_dot(a, _stack(ui * keep_l, ui * keep_r)) for a, ui in zip(ld("arb"), ub)]
        for (i, d, _, sl), x, ys in zip(tiles, ur, yc):
            y_refs[d][i, :, sl] = x[CHUNK:] + ys
    for (i, d, p, sl), s, dz, kv in zip(tiles, s_old, inc, ld("kv")):
        z_scr[i, d, p] = (s + jnp.where(bd, dz, 0.0) + unfold(kv)) * dir_refs[d]["wc"][i, 0, d:d + 1, sl]

    if not emit_y:
        @pl.when(c == pl.num_programs(0) - 1)
        def _():
            zfin_ref[...] = z_scr[...]


def _wkv_scan(prep, z0, *, emit_y):
    wc = prep[6]
    _, b, l, w = prep[0].shape
    nc = l // CHUNK
    n_pairs = w // LANES
    fwd = lambda c: c
    rev = lambda c: nc - 1 - c
    in_specs, args = [], []
    for d, cm in enumerate((fwd, rev)):
        for arr in prep[:6]:
            in_specs.append(pl.BlockSpec((1, b, CHUNK, w), lambda c, d=d, cm=cm: (d, 0, cm(c), 0)))
            args.append(arr)
        in_specs.append(pl.BlockSpec((b, 1, 2, w), lambda c, cm=cm: (0, cm(c), 0, 0)))
        args.append(wc)
    zshape = (b, 2, n_pairs, LANES, LANES)
    zspec = pl.BlockSpec(zshape, lambda c: (0, 0, 0, 0, 0))
    in_specs.append(zspec)
    args.append(z0)
    if emit_y:
        out_specs = [pl.BlockSpec((b, CHUNK, w), lambda c: (0, c, 0)),
                     pl.BlockSpec((b, CHUNK, w), lambda c: (0, nc - 1 - c, 0))]
        out_shape = [jax.ShapeDtypeStruct((b, l, w), F32)] * 2
    else:
        out_specs = zspec
        out_shape = jax.ShapeDtypeStruct(zshape, F32)
    return pl.pallas_call(
        functools.partial(_wkv_kernel, emit_y=emit_y, n_pairs=n_pairs),
        grid=(nc,),
        in_specs=in_specs,
        out_specs=out_specs,
        out_shape=out_shape,
        scratch_shapes=[pltpu.VMEM(zshape, F32)],
        compiler_params=_params("arbitrary"),
        name="wkv_scan_latent" if emit_y else "wkv_scan_context",
    )(*args)


def _merge_kernel(x_ref, mod_ref, ya_ref, yf_ref, yr_ref, yh_ref, bonus_ref, g_ref, gate_ref,
                  lnw_ref, lnb_ref, wba_ref, wbr_ref, wo_ref, n2_ref, wu_ref, wd_ref, nf_ref,
                  o_ref, *, ff_chunk):
    x = x_ref[0]
    d = x.shape[1]
    mod = lambda j: mod_ref[0, j:j + 1, :]
    y = yf_ref[0] + yr_ref[0] + yh_ref[0]
    gw = min(MXU_DIM, y.shape[1])
    ones_bd = _head_ones(gw)

    def head_mean(t):
        tb = t.astype(BF16)
        parts = [_dot(tb[:, j:j + gw], ones_bd) for j in range(0, t.shape[1], gw)]
        return jnp.concatenate(parts, axis=1) * (1.0 / HEAD_DIM)

    mu = head_mean(y)
    yc = y - mu
    var = head_mean(yc * yc)
    yn = yc * lax.rsqrt(var + LNX_EPS)
    yr = (yn * lnw_ref[...] + lnb_ref[...] + bonus_ref[0]) * g_ref[0].astype(F32)
    gate = gate_ref[0].astype(F32)
    merged = gate[:, :d] * _dot(ya_ref[0], wba_ref[...]) + gate[:, d:] * _dot(yr.astype(BF16), wbr_ref[...])
    x1 = x + mod(2) * _dot(merged.astype(BF16), wo_ref[...])
    h2 = (_rmsnorm(x1, n2_ref[...]) * (1.0 + mod(4)) + mod(3)).astype(BF16)
    acc = jnp.zeros_like(x1)
    for j in range(wu_ref.shape[1] // ff_chunk):
        cs = slice(j * ff_chunk, (j + 1) * ff_chunk)
        up = jnp.maximum(_dot(h2, wu_ref[:, cs]), 0.0)
        acc = acc + _dot((up * up).astype(BF16), wd_ref[cs, :])
    x2 = x1 + mod(5) * acc
    o_ref[0] = _rmsnorm(x2, nf_ref[...])


def _merge_mlp(x, mod, ya, yf, yr, yh, bonus, g, gate, prm, *, tile):
    b, l, d = x.shape
    tok = lambda arr: pl.BlockSpec((1, tile, arr.shape[2]), lambda i, t: (i, t, 0))
    consts = [prm["lnx_w"], prm["lnx_b"], prm["w_branch_attn"], prm["w_branch_rwkv"], prm["w_out"],
              prm["norm2_g"], prm["w_mlp_up"], prm["w_mlp_down"], prm["norm_f_g"]]
    toks = [ya, yf, yr, yh, bonus, g, gate]
    return pl.pallas_call(
        functools.partial(_merge_kernel, ff_chunk=min(1024, prm["w_mlp_up"].shape[1])),
        grid=(b, l // tile),
        in_specs=[tok(x), pl.BlockSpec((1,) + mod.shape[1:], lambda i, t: (i, 0, 0))]
        + [tok(a) for a in toks] + [_const_spec(c.shape) for c in consts],
        out_specs=tok(x),
        out_shape=jax.ShapeDtypeStruct(x.shape, x.dtype),
        compiler_params=_params("parallel", "parallel"),
        name="merge_mlp",
    )(x, mod, *toks, *consts)


def _rope_tables(l):
    n_freq = HEAD_DIM // 4
    inv_freq = jnp.power(ROPE_BASE, -jnp.arange(n_freq, dtype=F32) / n_freq)
    rows = l // GRID_W
    row = jnp.repeat(jnp.arange(rows, dtype=F32), GRID_W)
    col = jnp.tile(jnp.arange(GRID_W, dtype=F32), rows)
    ang = jnp.concatenate([row[:, None] * inv_freq, col[:, None] * inv_freq], axis=-1)
    cos, sin = jnp.cos(ang), jnp.sin(ang)
    reps = LANES // HEAD_DIM
    return (jnp.tile(jnp.concatenate([cos, cos], axis=1), (1, reps)),
            jnp.tile(jnp.concatenate([-sin, sin], axis=1), (1, reps)))


def _pad_cols(w, width):
    return jnp.pad(w, ((0, 0), (0, width - w.shape[1])))


def kernel(x, c, ctx, c_ctx, w_ada, b_ada, norm1_g, w_in, sink, conv_w, decay_w0, decay_w2, iclr_a0, iclr_a2, gate_g2, k_k, k_a, r_k, lnx_w, lnx_b, w_branch_attn, w_branch_rwkv, w_out, norm2_g, w_mlp_up, w_mlp_down, norm_f_g):
    assert w_in.shape[0] == 1, "single-layer block: context tokens are read, never updated"
    b, l, d = x.shape
    attn_w = w_branch_attn.shape[1]
    rw = w_branch_rwkv.shape[1]
    n_q = attn_w // HEAD_DIM
    n_kv = n_q // Q_PER_KV
    kv_w = n_kv * HEAD_DIM
    assert kv_w == LANES and rw % LANES == 0 and l % 256 == 0 and ctx.shape[1] % 256 == 0

    w = w_in[0]
    o_k, o_r = attn_w, attn_w + 2 * kv_w
    o_l = o_r + 3 * rw
    o_g = o_l + DECAY_LORA + ICLR_LORA + GATE_LORA
    w_ctx = _pad_cols(w[:, o_k:o_g], o_l - o_k + LORA_PAD).astype(BF16)
    w_lat = [w[:, :o_k].astype(BF16), w_ctx, w[:, o_g:].astype(BF16)]
    widths_ctx = {"kv": 2 * kv_w, "k": PAIR * kv_w, "v": PAIR * kv_w, "rkv": 3 * rw, "lora": LORA_PAD}
    widths_lat = {"q": attn_w, **widths_ctx, "gate": 2 * d}

    cw = conv_w[0]
    lora_w = jnp.zeros((LORA_PAD, 5 * rw), F32)
    lora_w = lora_w.at[:DECAY_LORA, :rw].set(decay_w2[0, 0]).at[:DECAY_LORA, rw:2 * rw].set(decay_w2[0, 1])
    r1 = DECAY_LORA + ICLR_LORA
    lora_w = lora_w.at[DECAY_LORA:r1, 2 * rw:3 * rw].set(iclr_a2[0, 0]).at[DECAY_LORA:r1, 3 * rw:4 * rw].set(iclr_a2[0, 1])
    lora_w = lora_w.at[r1:r1 + GATE_LORA, 4 * rw:].set(gate_g2[0])
    prm = {
        "conv_rkv": cw[:, :3 * rw], "conv_lora": _pad_cols(cw[:, 3 * rw:], LORA_PAD),
        "k_k": k_k[0].reshape(1, rw), "k_a": k_a[0].reshape(1, rw),
        "decay_w0": decay_w0[0], "iclr_a0": iclr_a0[0], "lora_w": lora_w.astype(BF16),
        "r_k": r_k[0].reshape(1, rw),
        "lnx_w": lnx_w[0].reshape(1, rw), "lnx_b": lnx_b[0].reshape(1, rw),
        "w_branch_attn": w_branch_attn[0].astype(BF16), "w_branch_rwkv": w_branch_rwkv[0].astype(BF16),
        "w_out": w_out[0].astype(BF16), "norm2_g": norm2_g[0].reshape(1, d),
        "w_mlp_up": w_mlp_up[0].astype(BF16), "w_mlp_down": w_mlp_down[0].astype(BF16),
        "norm_f_g": norm_f_g.reshape(1, d),
    }

    rows = -(-(b + 1) // 8) * 8
    cc = jnp.zeros((rows, d), F32).at[:b].set(c).at[b].set(c_ctx)
    mod = _ada_mod(cc, w_ada[0], b_ada[0]).reshape(rows, -1, d)

    q, kd, vd, rkv, lora, gate = _in_proj(x, mod, b, norm1_g[0], w_lat, widths_lat, _rope_tables(l),
                                          latent=True, tile=512)
    kxd, vxd, rkv_c, lora_c = _in_proj(ctx, mod, b, norm1_g[0], [w_ctx], widths_ctx, None,
                                       latent=False, tile=256)
    ya = _attention(sink[0], q, kd, vd, kxd, vxd)

    prep_c = _wkv_prep(rkv_c, lora_c, prm, latent=False, tile=256)
    z_ctx = _wkv_scan(prep_c, jnp.zeros((b, 2, rw // LANES, LANES, LANES), F32), emit_y=False)
    prep = _wkv_prep(rkv, lora, prm, latent=True, tile=256)
    yf, yr = _wkv_scan(prep, z_ctx, emit_y=True)
    bonus, g, yh = prep[7], prep[8], prep[9]

    return _merge_mlp(x, mod, ya, yf, yr, yh, bonus, g, gate, prm, tile=512)
```

```python
import functools
import math

import jax
import jax.numpy as jnp
from jax import lax
from jax.experimental import pallas as pl
from jax.experimental.pallas import tpu as pltpu

F32 = jnp.float32
BF16 = jnp.bfloat16

GRID_W = 64
HEAD_DIM = 64
Q_PER_KV = 4
ATTN_BLOCK = 128
ATTN_QB = 4
ROPE_BASE = 10000.0
NORM_EPS = 1e-6
LNX_EPS = 1e-5 * HEAD_DIM
DECAY_LORA, ICLR_LORA, GATE_LORA = 32, 32, 96
LORA_PAD = 256
CHUNK = 64
PREP_GROUP = 32
LANES = 128
MXU_DIM = 256
PAIR = LANES // HEAD_DIM
NEG = -1e30
LOG2_E = math.log2(math.e)
VMEM_LIMIT = 56 * 1024 * 1024


def _dot(a, b):
    return jnp.dot(a, b, preferred_element_type=F32)


def _dot_nt(a, b):
    return lax.dot_general(a, b, (((1,), (1,)), ((), ())), preferred_element_type=F32)


def _dot_tn(a, b):
    return lax.dot_general(a, b, (((0,), (0,)), ((), ())), preferred_element_type=F32)


def _iota(shape, dim):
    return lax.broadcasted_iota(jnp.int32, shape, dim)


def _head_ones(width):
    r = _iota((width, width), 0) // HEAD_DIM
    c = _iota((width, width), 1) // HEAD_DIM
    return jnp.where(r == c, 1.0, 0.0).astype(BF16)


def _head_sum(x, ones_bd):
    gw = ones_bd.shape[0]
    hi = x.astype(BF16)
    lo = (x - hi.astype(F32)).astype(BF16)
    parts = [_dot(hi[:, j:j + gw], ones_bd) + _dot(lo[:, j:j + gw], ones_bd) for j in range(0, x.shape[1], gw)]
    return jnp.concatenate(parts, axis=1)


def _rmsnorm(x, g):
    ms = jnp.mean(x * x, axis=-1, keepdims=True)
    return x * lax.rsqrt(ms + NORM_EPS) * g


def _params(*sem):
    return pltpu.CompilerParams(dimension_semantics=sem, vmem_limit_bytes=VMEM_LIMIT)


def _const_spec(shape):
    nd = len(shape)
    return pl.BlockSpec(shape, lambda *_: (0,) * nd, pipeline_mode=pl.Buffered(1))


def _ada_kernel(c_ref, w_ref, b_ref, o_ref):
    c = c_ref[...]
    s = c * jax.nn.sigmoid(c)
    o_ref[...] = _dot(s.astype(BF16), w_ref[...].astype(BF16)) + b_ref[...]


def _ada_mod(cc, w_ada, b_ada):
    rows, d = cc.shape
    n = w_ada.shape[1]
    return pl.pallas_call(
        _ada_kernel,
        grid=(n // d,),
        in_specs=[pl.BlockSpec((rows, d), lambda j: (0, 0)),
                  pl.BlockSpec((d, d), lambda j: (0, j)),
                  pl.BlockSpec((1, d), lambda j: (0, j))],
        out_specs=pl.BlockSpec((rows, d), lambda j: (0, j)),
        out_shape=jax.ShapeDtypeStruct((rows, n), F32),
        compiler_params=_params("arbitrary"),
        name="ada_mod",
    )(cc, w_ada, b_ada.reshape(1, n))


def _rope(x, cos_t, sin_t):
    w = x.shape[1]
    half = HEAD_DIM // 2
    first = (_iota(x.shape, 1) % HEAD_DIM) < half
    swapped = jnp.where(first, pltpu.roll(x, w - half, 1), pltpu.roll(x, half, 1))
    reps = w // LANES
    c = jnp.concatenate([cos_t] * reps, axis=1)
    s = jnp.concatenate([sin_t] * reps, axis=1)
    return x * c + swapped * s


def _inproj_kernel(*refs, latent, widths):
    if latent:
        (x_ref, mod_ref, g_ref, wq_ref, w_ref, wg_ref, cos_ref, sin_ref,
         q_ref, k_ref, v_ref, rkv_ref, lora_ref, gate_ref) = refs
    else:
        x_ref, mod_ref, g_ref, w_ref, k_ref, v_ref, rkv_ref, lora_ref = refs
    x = x_ref[0]
    h = _rmsnorm(x, g_ref[...]) * (1.0 + mod_ref[0, 1:2, :]) + mod_ref[0, 0:1, :]
    hb = h.astype(BF16)
    off = 0

    def seg(name):
        nonlocal off
        lo = off
        off += widths[name]
        return _dot(hb, w_ref[:, lo:off])

    def dup_heads(t):
        first = _iota(t.shape, 1) < HEAD_DIM
        other = pltpu.roll(t, HEAD_DIM, 1)
        return jnp.concatenate([jnp.where(first, t, other), jnp.where(first, other, t)], axis=1)

    kv = seg("kv")
    k, v = dup_heads(kv[:, :LANES]), dup_heads(kv[:, LANES:])
    if latent:
        cos_t, sin_t = cos_ref[...], sin_ref[...]
        q_ref[0] = (_rope(_dot(hb, wq_ref[...]), cos_t, sin_t) * (LOG2_E * HEAD_DIM ** -0.5)).astype(BF16)
        k = _rope(k, cos_t, sin_t)
    k_ref[0] = k.astype(BF16)
    v_ref[0] = v.astype(BF16)
    rkv_ref[0] = seg("rkv")
    lora_ref[0] = seg("lora")
    if latent:
        gate_ref[0] = jax.nn.sigmoid(_dot(hb, wg_ref[...])).astype(BF16)


def _in_proj(x, mod, mod_row, norm_g, weights, widths, tables, *, latent, tile):
    b, l, d = x.shape
    nt = l // tile
    if latent:
        mod_map = lambda i, t: (i, 0, 0)
    else:
        mod_map = lambda i, t: (mod_row, 0, 0)
    tok = lambda w: pl.BlockSpec((1, tile, w), lambda i, t: (i, t, 0))
    in_specs = [tok(d),
                pl.BlockSpec((1,) + mod.shape[1:], mod_map),
                _const_spec((1, d))] + [_const_spec(w.shape) for w in weights]
    args = [x, mod, norm_g.reshape(1, d), *weights]
    out_specs, out_shape = [], []

    def out(w, dt):
        out_specs.append(tok(w))
        out_shape.append(jax.ShapeDtypeStruct((b, l, w), dt))

    if latent:
        in_specs += [pl.BlockSpec((tile, LANES), lambda i, t: (t, 0))] * 2
        args += list(tables)
        out(widths["q"], BF16)
    out(widths["k"], BF16)
    out(widths["v"], BF16)
    out(widths["rkv"], F32)
    out(widths["lora"], F32)
    if latent:
        out(widths["gate"], BF16)
    return pl.pallas_call(
        functools.partial(_inproj_kernel, latent=latent, widths=widths),
        grid=(b, nt),
        in_specs=in_specs,
        out_specs=out_specs,
        out_shape=out_shape,
        compiler_params=_params("parallel", "parallel"),
        name="in_proj_latent" if latent else "in_proj_context",
    )(*args)


def _attn_kernel(sink_ref, q_ref, kp_ref, kc_ref, kn_ref, vp_ref, vc_ref, vn_ref,
                 kx_ref, vx_ref, o_ref, *, n_kv):
    i = pl.program_id(1)
    last = pl.num_programs(1) - 1
    blk = ATTN_BLOCK
    qi = _iota((blk, blk), 0)
    kj = _iota((blk, blk), 1)
    left = _iota((blk, LANES), 1) < HEAD_DIM

    def key_block(refs, j, gs):
        ref_p, ref_c, ref_n = refs[:3]
        if j < 0:
            return ref_p[0, :, gs]
        if j >= ATTN_QB:
            return ref_n[0, :, gs]
        return ref_c[0, j * blk:(j + 1) * blk, gs]

    items = []
    for qb in range(ATTN_QB):
        lo_ok = kj >= qi
        hi_ok = kj <= qi
        if qb == 0:
            lo_ok = lo_ok & (i > 0)
        if qb == ATTN_QB - 1:
            hi_ok = hi_ok & (i < last)
        bias_lo = jnp.concatenate([jnp.where(lo_ok, 0.0, NEG)] * Q_PER_KV, axis=0)
        bias_hi = jnp.concatenate([jnp.where(hi_ok, 0.0, NEG)] * Q_PER_KV, axis=0)
        rows = slice(qb * blk, (qb + 1) * blk)
        q = q_ref[0, rows, :].astype(F32)
        for g in range(n_kv):
            gs = slice(g * LANES, (g + 1) * LANES)
            heads = range(g * Q_PER_KV, (g + 1) * Q_PER_KV)
            qs, sinks = [], []
            for hd in heads:
                qp = q[:, (hd // PAIR) * LANES:(hd // PAIR + 1) * LANES]
                keep = left if hd % PAIR == 0 else jnp.logical_not(left)
                qs.append(jnp.where(keep, qp, 0.0).astype(BF16))
                sinks.append(jnp.full((blk, 1), sink_ref[hd] * LOG2_E, F32))
            items.append({
                "qb": qb, "gs": gs, "rows": rows, "heads": heads, "bias": (bias_lo, bias_hi),
                "qs": jnp.concatenate(qs, axis=0), "sink": jnp.concatenate(sinks, axis=0)})

    def cat(refs, it):
        return jnp.concatenate([key_block(refs, it["qb"] + j, it["gs"]) for j in (-1, 0, 1)]
                               + [refs[3][0, :, it["gs"]]], axis=0)

    s = [_dot_nt(it["qs"], cat((kp_ref, kc_ref, kn_ref, kx_ref), it)) for it in items]
    s = [jnp.concatenate([x[:, :blk] + it["bias"][0], x[:, blk:2 * blk], x[:, 2 * blk:3 * blk] + it["bias"][1],
                          x[:, 3 * blk:]], axis=1) for x, it in zip(s, items)]
    m = [jnp.maximum(jnp.max(x, axis=1, keepdims=True), it["sink"]) for x, it in zip(s, items)]
    p = [jnp.exp2(x - mx) for x, mx in zip(s, m)]
    den = [jnp.sum(x, axis=1, keepdims=True) + jnp.exp2(it["sink"] - mx) for x, mx, it in zip(p, m, items)]
    o = [_dot(x.astype(BF16), cat((vp_ref, vc_ref, vn_ref, vx_ref), it)) / dn for x, dn, it in zip(p, den, items)]
    for x, it in zip(o, items):
        for hd in it["heads"][::PAIR]:
            j = hd - it["heads"][0]
            pair = jnp.where(left, x[j * blk:(j + 1) * blk], x[(j + 1) * blk:(j + 2) * blk])
            col = (hd // PAIR) * LANES
            o_ref[0, it["rows"], col:col + LANES] = pair.astype(BF16)


def _attention(sink, q, kd, vd, kxd, vxd):
    b, l, wq = q.shape
    wk = kd.shape[2]
    lc = kxd.shape[1]
    nb = l // ATTN_BLOCK
    span = ATTN_QB * ATTN_BLOCK
    blk = lambda w, f: pl.BlockSpec((1, ATTN_BLOCK, w), f)
    prev = lambda bi, i: (bi, jnp.maximum(i * ATTN_QB - 1, 0), 0)
    nxt = lambda bi, i: (bi, jnp.minimum((i + 1) * ATTN_QB, nb - 1), 0)
    cur = lambda w: pl.BlockSpec((1, span, w), lambda bi, i: (bi, i, 0))
    ctx = pl.BlockSpec((1, lc, wk), lambda bi, i: (bi, 0, 0))
    return pl.pallas_call(
        functools.partial(_attn_kernel, n_kv=wk // LANES),
        grid=(b, l // span),
        in_specs=[pl.BlockSpec(memory_space=pltpu.SMEM),
                  cur(wq),
                  blk(wk, prev), cur(wk), blk(wk, nxt),
                  blk(wk, prev), cur(wk), blk(wk, nxt),
                  ctx, ctx],
        out_specs=cur(wq),
        out_shape=jax.ShapeDtypeStruct((b, l, wq), BF16),
        compiler_params=_params("parallel", "parallel"),
        name="attention",
    )(sink, q, kd, kd, kd, vd, vd, vd, kxd, vxd)


def _conv3(x, prev_row, next_row, w):
    n = x.shape[0]
    row = _iota(x.shape, 0)
    xm = jnp.where(row == 0, prev_row, pltpu.roll(x, 1, 0))
    xp = jnp.where(row == n - 1, next_row, pltpu.roll(x, n - 1, 0))
    return xm * w[0:1] + x * w[1:2] + xp * w[2:3]


def _pair_masks():
    n = 2 * CHUNK
    row = _iota((n, n), 0)
    lane = _iota((n, n), 1)
    top, left = row < CHUNK, lane < CHUNK
    return {"row": row % CHUNK, "lane": lane % CHUNK, "top": top, "left": left, "bd": top == left,
            "left_h": _iota((CHUNK, n), 1) < CHUNK}


def _stack(a, b):
    return jnp.concatenate([a, b], axis=0)


def _fold(x, m):
    return jnp.where(m["left_h"], x[:CHUNK], x[CHUNK:])


def _chunk_local(inst, m):
    bf = lambda x: x.astype(BF16)
    diag = jnp.logical_not(m["top"]) & (m["lane"] == m["row"])
    masks = ((m["lane"] < m["row"]) | diag, (m["lane"] > m["row"]) | diag)
    mask_a = [masks[i["rev"]] for i in inst]
    lh = m["left_h"]
    half = CHUNK // 2
    keep_l = jnp.where(m["left"], 1.0, 0.0).astype(BF16)
    keep_r = jnp.where(m["left"], 0.0, 1.0).astype(BF16)
    keep_lh = jnp.where(lh, 1.0, 0.0).astype(BF16)
    keep_rh = jnp.where(lh, 0.0, 1.0).astype(BF16)

    def unfold(xb, anti=False):
        a, b = xb * keep_lh, xb * keep_rh
        return _stack(b, a) if anti else _stack(a, b)

    lhs = [bf(_stack(i["at"], i["rt"])) for i in inst]
    a01 = [_dot_nt(l, _stack(bf(_stack(i["bt"], i["kt"])) * keep_l, bf(_stack(i["kt"], i["bt"])) * keep_r))
           for l, i in zip(lhs, inst)]
    a0 = [jnp.where(ma, a[:, :LANES], 0.0) for ma, a in zip(mask_a, a01)]
    a1 = [jnp.where(ma, a[:, LANES:], 0.0) for ma, a in zip(mask_a, a01)]
    nc = [jnp.where(lh, x[:CHUNK], y[:CHUNK]) for x, y in zip(a0, a1)]
    arb = [jnp.where(lh, x[CHUNK:], y[CHUNK:]) for x, y in zip(a0, a1)]
    ak_ark_sw = [bf(jnp.where(m["left"], y, x)) for x, y in zip(a0, a1)]
    vh = [_dot(a, unfold(bf(i["v"]), anti=True)) for a, i in zip(ak_ark_sw, inst)]
    eye = jnp.where(_iota((CHUNK, LANES), 1) % CHUNK == _iota((CHUNK, LANES), 0), 1.0, 0.0)
    tc = [eye + n for n in nc]
    ncb = [bf(n) for n in nc]
    nc = [_dot(n, unfold(n)) for n in ncb]
    steps = CHUNK.bit_length() - 1
    for _ in range(steps - 2):
        ncb = [bf(n) for n in nc]
        both = [_dot(n, jnp.concatenate([unfold(bf(t)), unfold(n)], axis=1)) for n, t in zip(ncb, tc)]
        tc = [t + r[:, :LANES] for t, r in zip(tc, both)]
        nc = [r[:, LANES:] for r in both]
    inc = [_dot(bf(n[half:]), unfold(bf(t))) for n, t in zip(nc, tc)]
    tc = [jnp.concatenate([t[:half], t[half:] + d], axis=0) for t, d in zip(tc, inc)]
    pq = [_dot(bf(t), jnp.concatenate([unfold(l[:CHUNK]), unfold(bf(x[:CHUNK]))], axis=1))
          for t, l, x in zip(tc, lhs, vh)]
    kvf = [_dot_tn(bf(i["v"]), bf(i["kt"])) for i in inst]
    return [{"pm": r[:, :LANES], "qm": r[:, LANES:], "arb": b, "hm": x[CHUNK:], "kv": _fold(k, m)}
            for r, b, x, k in zip(pq, arb, vh, kvf)]


def _prep_kernel(*refs, latent, width):
    (rkv_ref, rkv_p, rkv_n, lora_ref, lora_p, lora_n, cw_ref, cwl_ref, kk_ref, ka_ref,
     w0_ref, a0_ref, wl_ref, rk_ref) = refs[:14]
    outs = refs[14:]
    out_refs = dict(zip(("pm", "qm", "rt", "bt", "arb", "kv"), outs[:6]))
    wc_ref = outs[6]
    if latent:
        bonus_ref, g_ref, hsum_ref = outs[7:]
    t = pl.program_id(1)
    nt = pl.num_programs(1)
    tile = rkv_ref.shape[1]
    w = width
    has_prev = (t > 0).astype(F32)
    has_next = (t < nt - 1).astype(F32)
    u = _conv3(rkv_ref[0], rkv_p[0, 7:8, :] * has_prev, rkv_n[0, 0:1, :] * has_next, cw_ref[...])
    ul = _conv3(lora_ref[0], lora_p[0, 7:8, :] * has_prev, lora_n[0, 0:1, :] * has_next, cwl_ref[...])
    r, k, v = u[:, :w], u[:, w:2 * w], u[:, 2 * w:]
    ones_bd = _head_ones(min(MXU_DIM, w))

    kk = k * kk_ref[...]
    kk = kk * lax.rsqrt(jnp.maximum(_head_sum(kk * kk, ones_bd), 1e-24))

    lane = _iota(ul.shape, 1)
    lin = jnp.where(lane < DECAY_LORA, jnp.tanh(ul),
                    jnp.where(lane < DECAY_LORA + ICLR_LORA, ul, jax.nn.sigmoid(ul)))
    proj = _dot(lin.astype(BF16), wl_ref[...])

    tr = _iota((tile, tile), 0)
    tc = _iota((tile, tile), 1)
    same = (tr // CHUNK) == (tc // CHUNK)
    tri = (jnp.where(same & (tc <= tr), 1.0, 0.0).astype(BF16),
           jnp.where(same & (tc >= tr), 1.0, 0.0).astype(BF16))

    def exact_dot(m, x):
        h1 = x.astype(BF16)
        r1 = x - h1.astype(F32)
        h2 = r1.astype(BF16)
        h3 = (r1 - h2.astype(F32)).astype(BF16)
        return _dot(m, h1) + _dot(m, h2) + _dot(m, h3)

    masks = _pair_masks()
    k_sum = None
    work = []
    for d in range(2):
        z = w0_ref[d:d + 1, :] + proj[:, d * w:(d + 1) * w]
        lw = -math.exp(-0.5) * jax.nn.sigmoid(z)
        a = jax.nn.sigmoid(a0_ref[d:d + 1, :] + proj[:, (2 + d) * w:(3 + d) * w])
        kd = k * (1.0 + (a - 1.0) * ka_ref[...])
        k_sum = kd if k_sum is None else k_sum + kd
        cum = exact_dot(tri[d], lw)
        e_neg = jnp.exp(-cum)
        full = {"at": -kk * jnp.exp(cum - lw), "rt": r * jnp.exp(cum), "bt": kk * a * e_neg,
                "kt": kd * e_neg, "v": v}
        out_refs["rt"][d, 0] = full["rt"].astype(BF16)
        out_refs["bt"][d, 0] = full["bt"].astype(BF16)
        for j in range(tile // CHUNK):
            edge = (j + 1) * CHUNK - 1 if d == 0 else j * CHUNK
            wc_ref[0, j, d:d + 1, :] = jnp.exp(cum[edge:edge + 1, :])
        work += [(d, slice(j * CHUNK, (j + 1) * CHUNK), slice(p * LANES, (p + 1) * LANES), full)
                 for j in range(tile // CHUNK) for p in range(w // LANES)]
    hsum = {}
    for g0 in range(0, len(work), PREP_GROUP):
        group = work[g0:g0 + PREP_GROUP]
        inst = [dict({name: val[rs, ls] for name, val in full.items()}, rev=d) for d, rs, ls, full in group]
        for (d, rs, ls, _), res in zip(group, _chunk_local(inst, masks)):
            seen = hsum.get((rs.start, ls.start))
            hsum[(rs.start, ls.start)] = (rs, ls, res["hm"] if seen is None else seen[2] + res["hm"])
            for name in res:
                if name in out_refs:
                    out_refs[name][d, 0, rs, ls] = res[name].astype(BF16)
    if latent:
        for rs, ls, val in hsum.values():
            hsum_ref[0, rs, ls] = val
        bonus_ref[0] = _head_sum(r * k_sum * rk_ref[...], ones_bd) * v
        g_ref[0] = proj[:, 4 * w:5 * w].astype(BF16)


def _wkv_prep(rkv, lora, prm, *, latent, tile):
    b, l, w3 = rkv.shape
    w = w3 // 3
    nt = l // tile
    n8 = l // 8
    tok = lambda wd: pl.BlockSpec((1, tile, wd), lambda i, t: (i, t, 0))
    prev = lambda wd: pl.BlockSpec((1, 8, wd), lambda i, t: (i, jnp.maximum(t * (tile // 8) - 1, 0), 0))
    nxt = lambda wd: pl.BlockSpec((1, 8, wd), lambda i, t: (i, jnp.minimum((t + 1) * (tile // 8), n8 - 1), 0))
    wl = lora.shape[2]
    in_specs = [tok(w3), prev(w3), nxt(w3), tok(wl), prev(wl), nxt(wl)]
    consts = [prm["conv_rkv"], prm["conv_lora"], prm["k_k"], prm["k_a"], prm["decay_w0"],
              prm["iclr_a0"], prm["lora_w"], prm["r_k"]]
    in_specs += [_const_spec(c.shape) for c in consts]
    dirtok = pl.BlockSpec((2, 1, tile, w), lambda i, t: (0, i, t, 0))
    out_specs = [dirtok] * 6 + [pl.BlockSpec((1, tile // CHUNK, 2, w), lambda i, t: (i, t, 0, 0))]
    out_shape = [jax.ShapeDtypeStruct((2, b, l, w), BF16)] * 6 + [
        jax.ShapeDtypeStruct((b, l // CHUNK, 2, w), F32)]
    if latent:
        out_specs += [tok(w), tok(w), tok(w)]
        out_shape += [jax.ShapeDtypeStruct((b, l, w), F32), jax.ShapeDtypeStruct((b, l, w), BF16),
                      jax.ShapeDtypeStruct((b, l, w), F32)]
    return pl.pallas_call(
        functools.partial(_prep_kernel, latent=latent, width=w),
        grid=(b, nt),
        in_specs=in_specs,
        out_specs=out_specs,
        out_shape=out_shape,
        compiler_params=_params("parallel", "parallel"),
        name="wkv_prep_latent" if latent else "wkv_prep_context",
    )(rkv, rkv, rkv, lora, lora, lora, *consts)


def _wkv_kernel(*refs, emit_y, n_pairs):
    names = ("pm", "qm", "rt", "bt", "arb", "kv", "wc")
    n = len(names)
    dir_refs = (dict(zip(names, refs[0:n])), dict(zip(names, refs[n:2 * n])))
    z0_ref = refs[2 * n]
    if emit_y:
        y_refs = refs[2 * n + 1:2 * n + 3]
        z_scr = refs[2 * n + 3]
    else:
        zfin_ref = refs[2 * n + 1]
        z_scr = refs[2 * n + 2]
    c = pl.program_id(0)

    @pl.when(c == 0)
    def _():
        z_scr[...] = z0_ref[...]

    left_h = _iota((CHUNK, LANES), 1) < CHUNK
    keep_l = jnp.where(left_h, 1.0, 0.0).astype(BF16)
    keep_r = jnp.where(left_h, 0.0, 1.0).astype(BF16)

    def unfold(xb):
        return _stack(xb * keep_l, xb * keep_r)

    tiles = [(i, d, p, slice(p * LANES, (p + 1) * LANES))
             for i in range(z_scr.shape[0]) for d in range(2) for p in range(n_pairs)]
    ld = lambda name: [dir_refs[d][name][0, i, :, sl] for i, d, _, sl in tiles]
    s_old = [z_scr[i, d, p] for i, d, p, _ in tiles]
    sbd = [unfold(s.astype(BF16)) for s in s_old]
    if emit_y:
        ur = [_dot_nt(_stack(pm, rt), s) for pm, rt, s in zip(ld("pm"), ld("rt"), sbd)]
    else:
        ur = [_dot_nt(pm, s) for pm, s in zip(ld("pm"), sbd)]
    u = [x[:CHUNK] + q.astype(F32) for x, q in zip(ur, ld("qm"))]
    ub = [ui.astype(BF16) for ui in u]
    inc = [_dot_tn(ui, bt) for ui, bt in zip(ub, ld("bt"))]
    inc = [jnp.where(left_h, x[:CHUNK], x[CHUNK:]) for x in inc]
    if emit_y:
        yc = [_dot(a, unfold(ui)) for a, ui in zip(ld("arb"), ub)]
        for (i, d, _, sl), x, ys in zip(tiles, ur, yc):
            y_refs[d][i, :, sl] = x[CHUNK:] + ys
    for (i, d, p, sl), s, dz, kv in zip(tiles, s_old, inc, ld("kv")):
        z_scr[i, d, p] = (s + dz + kv.astype(F32)) * dir_refs[d]["wc"][i, 0, d:d + 1, sl]

    if not emit_y:
        @pl.when(c == pl.num_programs(0) - 1)
        def _():
            zfin_ref[...] = z_scr[...]


def _wkv_scan(prep, z0, *, emit_y):
    wc = prep[6]
    _, b, l, w = prep[0].shape
    nc = l // CHUNK
    n_pairs = w // LANES
    fwd = lambda c: c
    rev = lambda c: nc - 1 - c
    in_specs, args = [], []
    for d, cm in enumerate((fwd, rev)):
        for arr in prep[:6]:
            in_specs.append(pl.BlockSpec((1, b, CHUNK, w), lambda c, d=d, cm=cm: (d, 0, cm(c), 0)))
            args.append(arr)
        in_specs.append(pl.BlockSpec((b, 1, 2, w), lambda c, cm=cm: (0, cm(c), 0, 0)))
        args.append(wc)
    zshape = (b, 2, n_pairs, CHUNK, LANES)
    zspec = pl.BlockSpec(zshape, lambda c: (0, 0, 0, 0, 0))
    in_specs.append(zspec)
    args.append(z0)
    if emit_y:
        out_specs = [pl.BlockSpec((b, CHUNK, w), lambda c: (0, c, 0)),
                     pl.BlockSpec((b, CHUNK, w), lambda c: (0, nc - 1 - c, 0))]
        out_shape = [jax.ShapeDtypeStruct((b, l, w), F32)] * 2
    else:
        out_specs = zspec
        out_shape = jax.ShapeDtypeStruct(zshape, F32)
    return pl.pallas_call(
        functools.partial(_wkv_kernel, emit_y=emit_y, n_pairs=n_pairs),
        grid=(nc,),
        in_specs=in_specs,
        out_specs=out_specs,
        out_shape=out_shape,
        scratch_shapes=[pltpu.VMEM(zshape, F32)],
        compiler_params=_params("arbitrary"),
        name="wkv_scan_latent" if emit_y else "wkv_scan_context",
    )(*args)


def _merge_kernel(x_ref, mod_ref, ya_ref, yf_ref, yr_ref, yh_ref, bonus_ref, g_ref, gate_ref,
                  lnw_ref, lnb_ref, wba_ref, wbr_ref, wo_ref, n2_ref, wu_ref, wd_ref, nf_ref,
                  o_ref, *, ff_chunk):
    x = x_ref[0]
    d = x.shape[1]
    mod = lambda j: mod_ref[0, j:j + 1, :]
    y = yf_ref[0] + yr_ref[0] + yh_ref[0]
    gw = min(MXU_DIM, y.shape[1])
    ones_bd = _head_ones(gw)

    def head_mean(t):
        tb = t.astype(BF16)
        parts = [_dot(tb[:, j:j + gw], ones_bd) for j in range(0, t.shape[1], gw)]
        return jnp.concatenate(parts, axis=1) * (1.0 / HEAD_DIM)

    mu = head_mean(y)
    yc = y - mu
    var = head_mean(yc * yc)
    yn = yc * lax.rsqrt(var + LNX_EPS)
    yr = (yn * lnw_ref[...] + lnb_ref[...] + bonus_ref[0]) * g_ref[0].astype(F32)
    gate = gate_ref[0].astype(F32)
    merged = gate[:, :d] * _dot(ya_ref[0], wba_ref[...]) + gate[:, d:] * _dot(yr.astype(BF16), wbr_ref[...])
    x1 = x + mod(2) * _dot(merged.astype(BF16), wo_ref[...])
    h2 = (_rmsnorm(x1, n2_ref[...]) * (1.0 + mod(4)) + mod(3)).astype(BF16)
    acc = jnp.zeros_like(x1)
    for j in range(wu_ref.shape[1] // ff_chunk):
        cs = slice(j * ff_chunk, (j + 1) * ff_chunk)
        up = jnp.maximum(_dot(h2, wu_ref[:, cs]), 0.0)
        acc = acc + _dot((up * up).astype(BF16), wd_ref[cs, :])
    x2 = x1 + mod(5) * acc
    o_ref[0] = _rmsnorm(x2, nf_ref[...])


def _merge_mlp(x, mod, ya, yf, yr, yh, bonus, g, gate, prm, *, tile):
    b, l, d = x.shape
    tok = lambda arr: pl.BlockSpec((1, tile, arr.shape[2]), lambda i, t: (i, t, 0))
    consts = [prm["lnx_w"], prm["lnx_b"], prm["w_branch_attn"], prm["w_branch_rwkv"], prm["w_out"],
              prm["norm2_g"], prm["w_mlp_up"], prm["w_mlp_down"], prm["norm_f_g"]]
    toks = [ya, yf, yr, yh, bonus, g, gate]
    return pl.pallas_call(
        functools.partial(_merge_kernel, ff_chunk=min(1024, prm["w_mlp_up"].shape[1])),
        grid=(b, l // tile),
        in_specs=[tok(x), pl.BlockSpec((1,) + mod.shape[1:], lambda i, t: (i, 0, 0))]
        + [tok(a) for a in toks] + [_const_spec(c.shape) for c in consts],
        out_specs=tok(x),
        out_shape=jax.ShapeDtypeStruct(x.shape, x.dtype),
        compiler_params=_params("parallel", "parallel"),
        name="merge_mlp",
    )(x, mod, *toks, *consts)


def _rope_tables(l):
    n_freq = HEAD_DIM // 4
    inv_freq = jnp.power(ROPE_BASE, -jnp.arange(n_freq, dtype=F32) / n_freq)
    rows = l // GRID_W
    row = jnp.repeat(jnp.arange(rows, dtype=F32), GRID_W)
    col = jnp.tile(jnp.arange(GRID_W, dtype=F32), rows)
    ang = jnp.concatenate([row[:, None] * inv_freq, col[:, None] * inv_freq], axis=-1)
    cos, sin = jnp.cos(ang), jnp.sin(ang)
    reps = LANES // HEAD_DIM
    return (jnp.tile(jnp.concatenate([cos, cos], axis=1), (1, reps)),
            jnp.tile(jnp.concatenate([-sin, sin], axis=1), (1, reps)))


def _pad_cols(w, width):
    return jnp.pad(w, ((0, 0), (0, width - w.shape[1])))


def kernel(x, c, ctx, c_ctx, w_ada, b_ada, norm1_g, w_in, sink, conv_w, decay_w0, decay_w2, iclr_a0, iclr_a2, gate_g2, k_k, k_a, r_k, lnx_w, lnx_b, w_branch_attn, w_branch_rwkv, w_out, norm2_g, w_mlp_up, w_mlp_down, norm_f_g):
    assert w_in.shape[0] == 1, "single-layer block: context tokens are read, never updated"
    b, l, d = x.shape
    attn_w = w_branch_attn.shape[1]
    rw = w_branch_rwkv.shape[1]
    n_q = attn_w // HEAD_DIM
    n_kv = n_q // Q_PER_KV
    kv_w = n_kv * HEAD_DIM
    assert kv_w == LANES and rw % LANES == 0 and l % 256 == 0 and ctx.shape[1] % 256 == 0

    w = w_in[0]
    o_k, o_r = attn_w, attn_w + 2 * kv_w
    o_l = o_r + 3 * rw
    o_g = o_l + DECAY_LORA + ICLR_LORA + GATE_LORA
    w_ctx = _pad_cols(w[:, o_k:o_g], o_l - o_k + LORA_PAD).astype(BF16)
    w_lat = [w[:, :o_k].astype(BF16), w_ctx, w[:, o_g:].astype(BF16)]
    widths_ctx = {"kv": 2 * kv_w, "k": PAIR * kv_w, "v": PAIR * kv_w, "rkv": 3 * rw, "lora": LORA_PAD}
    widths_lat = {"q": attn_w, **widths_ctx, "gate": 2 * d}

    cw = conv_w[0]
    lora_w = jnp.zeros((LORA_PAD, 5 * rw), F32)
    lora_w = lora_w.at[:DECAY_LORA, :rw].set(decay_w2[0, 0]).at[:DECAY_LORA, rw:2 * rw].set(decay_w2[0, 1])
    r1 = DECAY_LORA + ICLR_LORA
    lora_w = lora_w.at[DECAY_LORA:r1, 2 * rw:3 * rw].set(iclr_a2[0, 0]).at[DECAY_LORA:r1, 3 * rw:4 * rw].set(iclr_a2[0, 1])
    lora_w = lora_w.at[r1:r1 + GATE_LORA, 4 * rw:].set(gate_g2[0])
    prm = {
        "conv_rkv": cw[:, :3 * rw], "conv_lora": _pad_cols(cw[:, 3 * rw:], LORA_PAD),
        "k_k": k_k[0].reshape(1, rw), "k_a": k_a[0].reshape(1, rw),
        "decay_w0": decay_w0[0], "iclr_a0": iclr_a0[0], "lora_w": lora_w.astype(BF16),
        "r_k": r_k[0].reshape(1, rw),
        "lnx_w": lnx_w[0].reshape(1, rw), "lnx_b": lnx_b[0].reshape(1, rw),
        "w_branch_attn": w_branch_attn[0].astype(BF16), "w_branch_rwkv": w_branch_rwkv[0].astype(BF16),
        "w_out": w_out[0].astype(BF16), "norm2_g": norm2_g[0].reshape(1, d),
        "w_mlp_up": w_mlp_up[0].astype(BF16), "w_mlp_down": w_mlp_down[0].astype(BF16),
        "norm_f_g": norm_f_g.reshape(1, d),
    }

    rows = -(-(b + 1) // 8) * 8
    cc = jnp.zeros((rows, d), F32).at[:b].set(c).at[b].set(c_ctx)
    mod = _ada_mod(cc, w_ada[0], b_ada[0]).reshape(rows, -1, d)

    q, kd, vd, rkv, lora, gate = _in_proj(x, mod, b, norm1_g[0], w_lat, widths_lat, _rope_tables(l),
                                          latent=True, tile=512)
    kxd, vxd, rkv_c, lora_c = _in_proj(ctx, mod, b, norm1_g[0], [w_ctx], widths_ctx, None,
                                       latent=False, tile=256)
    ya = _attention(sink[0], q, kd, vd, kxd, vxd)

    prep_c = _wkv_prep(rkv_c, lora_c, prm, latent=False, tile=256)
    z_ctx = _wkv_scan(prep_c, jnp.zeros((b, 2, rw // LANES, CHUNK, LANES), F32), emit_y=False)
    prep = _wkv_prep(rkv, lora, prm, latent=True, tile=256)
    yf, yr = _wkv_scan(prep, z_ctx, emit_y=True)
    bonus, g, yh = prep[7], prep[8], prep[9]

    return _merge_mlp(x, mod, ya, yf, yr, yh, bonus, g, gate, prm, tile=512)
```

```python
import functools
import math

import jax
import jax.numpy as jnp
from jax import lax
from jax.experimental import pallas as pl
from jax.experimental.pallas import tpu as pltpu

F32 = jnp.float32
BF16 = jnp.bfloat16

GRID_W = 64
HEAD_DIM = 64
Q_PER_KV = 4
ATTN_BLOCK = 128
ATTN_QB = 4
ROPE_BASE = 10000.0
NORM_EPS = 1e-6
LNX_EPS = 1e-5 * HEAD_DIM
DECAY_LORA, ICLR_LORA, GATE_LORA = 32, 32, 96
LORA_PAD = 256
CHUNK = 64
SCAN_CHUNKS = 4
PREP_GROUP = 32
LANES = 128
MXU_DIM = 256
PAIR = LANES // HEAD_DIM
NEG = -1e30
LOG2_E = math.log2(math.e)
VMEM_LIMIT = 56 * 1024 * 1024


def _dot(a, b):
    return jnp.dot(a, b, preferred_element_type=F32)


def _dot_nt(a, b):
    return lax.dot_general(a, b, (((1,), (1,)), ((), ())), preferred_element_type=F32)


def _dot_tn(a, b):
    return lax.dot_general(a, b, (((0,), (0,)), ((), ())), preferred_element_type=F32)


def _iota(shape, dim):
    return lax.broadcasted_iota(jnp.int32, shape, dim)


def _head_ones(width):
    r = _iota((width, width), 0) // HEAD_DIM
    c = _iota((width, width), 1) // HEAD_DIM
    return jnp.where(r == c, 1.0, 0.0).astype(BF16)


def _head_sum(x, ones_bd):
    gw = ones_bd.shape[0]
    hi = x.astype(BF16)
    lo = (x - hi.astype(F32)).astype(BF16)
    parts = [_dot(hi[:, j:j + gw], ones_bd) + _dot(lo[:, j:j + gw], ones_bd) for j in range(0, x.shape[1], gw)]
    return jnp.concatenate(parts, axis=1)


def _rmsnorm(x, g):
    ms = jnp.mean(x * x, axis=-1, keepdims=True)
    return x * lax.rsqrt(ms + NORM_EPS) * g


def _params(*sem):
    return pltpu.CompilerParams(dimension_semantics=sem, vmem_limit_bytes=VMEM_LIMIT)


def _const_spec(shape):
    nd = len(shape)
    return pl.BlockSpec(shape, lambda *_: (0,) * nd, pipeline_mode=pl.Buffered(1))


def _ada_kernel(c_ref, w_ref, b_ref, o_ref):
    c = c_ref[...]
    s = c * jax.nn.sigmoid(c)
    o_ref[...] = _dot(s.astype(BF16), w_ref[...].astype(BF16)) + b_ref[...]


def _ada_mod(cc, w_ada, b_ada):
    rows, d = cc.shape
    n = w_ada.shape[1]
    return pl.pallas_call(
        _ada_kernel,
        grid=(n // d,),
        in_specs=[pl.BlockSpec((rows, d), lambda j: (0, 0)),
                  pl.BlockSpec((d, d), lambda j: (0, j)),
                  pl.BlockSpec((1, d), lambda j: (0, j))],
        out_specs=pl.BlockSpec((rows, d), lambda j: (0, j)),
        out_shape=jax.ShapeDtypeStruct((rows, n), F32),
        compiler_params=_params("arbitrary"),
        name="ada_mod",
    )(cc, w_ada, b_ada.reshape(1, n))


def _rope(x, cos_t, sin_t):
    w = x.shape[1]
    half = HEAD_DIM // 2
    first = (_iota(x.shape, 1) % HEAD_DIM) < half
    swapped = jnp.where(first, pltpu.roll(x, w - half, 1), pltpu.roll(x, half, 1))
    reps = w // LANES
    c = jnp.concatenate([cos_t] * reps, axis=1)
    s = jnp.concatenate([sin_t] * reps, axis=1)
    return x * c + swapped * s


def _inproj_kernel(*refs, latent, widths):
    if latent:
        (x_ref, mod_ref, g_ref, wq_ref, w_ref, wg_ref, cos_ref, sin_ref,
         q_ref, k_ref, v_ref, rkv_ref, lora_ref, gate_ref) = refs
    else:
        x_ref, mod_ref, g_ref, w_ref, k_ref, v_ref, rkv_ref, lora_ref = refs
    x = x_ref[0]
    h = _rmsnorm(x, g_ref[...]) * (1.0 + mod_ref[0, 1:2, :]) + mod_ref[0, 0:1, :]
    hb = h.astype(BF16)
    off = 0

    def seg(name):
        nonlocal off
        lo = off
        off += widths[name]
        return _dot(hb, w_ref[:, lo:off])

    def dup_heads(t):
        first = _iota(t.shape, 1) < HEAD_DIM
        other = pltpu.roll(t, HEAD_DIM, 1)
        return jnp.concatenate([jnp.where(first, t, other), jnp.where(first, other, t)], axis=1)

    kv = seg("kv")
    k, v = dup_heads(kv[:, :LANES]), dup_heads(kv[:, LANES:])
    if latent:
        cos_t, sin_t = cos_ref[...], sin_ref[...]
        q_ref[0] = (_rope(_dot(hb, wq_ref[...]), cos_t, sin_t) * (LOG2_E * HEAD_DIM ** -0.5)).astype(BF16)
        k = _rope(k, cos_t, sin_t)
    k_ref[0] = k.astype(BF16)
    v_ref[0] = v.astype(BF16)
    rkv_ref[0] = seg("rkv")
    lora_ref[0] = seg("lora")
    if latent:
        gate_ref[0] = jax.nn.sigmoid(_dot(hb, wg_ref[...])).astype(BF16)


def _in_proj(x, mod, mod_row, norm_g, weights, widths, tables, *, latent, tile):
    b, l, d = x.shape
    nt = l // tile
    if latent:
        mod_map = lambda i, t: (i, 0, 0)
    else:
        mod_map = lambda i, t: (mod_row, 0, 0)
    tok = lambda w: pl.BlockSpec((1, tile, w), lambda i, t: (i, t, 0))
    in_specs = [tok(d),
                pl.BlockSpec((1,) + mod.shape[1:], mod_map),
                _const_spec((1, d))] + [_const_spec(w.shape) for w in weights]
    args = [x, mod, norm_g.reshape(1, d), *weights]
    out_specs, out_shape = [], []

    def out(w, dt):
        out_specs.append(tok(w))
        out_shape.append(jax.ShapeDtypeStruct((b, l, w), dt))

    if latent:
        in_specs += [pl.BlockSpec((tile, LANES), lambda i, t: (t, 0))] * 2
        args += list(tables)
        out(widths["q"], BF16)
    out(widths["k"], BF16)
    out(widths["v"], BF16)
    out(widths["rkv"], F32)
    out(widths["lora"], F32)
    if latent:
        out(widths["gate"], BF16)
    return pl.pallas_call(
        functools.partial(_inproj_kernel, latent=latent, widths=widths),
        grid=(b, nt),
        in_specs=in_specs,
        out_specs=out_specs,
        out_shape=out_shape,
        compiler_params=_params("parallel", "parallel"),
        name="in_proj_latent" if latent else "in_proj_context",
    )(*args)


def _attn_kernel(sink_ref, q_ref, kp_ref, kc_ref, kn_ref, vp_ref, vc_ref, vn_ref,
                 kx_ref, vx_ref, o_ref, *, n_kv):
    i = pl.program_id(1)
    last = pl.num_programs(1) - 1
    blk = ATTN_BLOCK
    qi = _iota((blk, blk), 0)
    kj = _iota((blk, blk), 1)
    left = _iota((blk, LANES), 1) < HEAD_DIM

    def key_block(refs, j, gs):
        ref_p, ref_c, ref_n = refs[:3]
        if j < 0:
            return ref_p[0, :, gs]
        if j >= ATTN_QB:
            return ref_n[0, :, gs]
        return ref_c[0, j * blk:(j + 1) * blk, gs]

    items = []
    for qb in range(ATTN_QB):
        lo_ok = kj >= qi
        hi_ok = kj <= qi
        if qb == 0:
            lo_ok = lo_ok & (i > 0)
        if qb == ATTN_QB - 1:
            hi_ok = hi_ok & (i < last)
        bias_lo = jnp.concatenate([jnp.where(lo_ok, 0.0, NEG)] * Q_PER_KV, axis=0)
        bias_hi = jnp.concatenate([jnp.where(hi_ok, 0.0, NEG)] * Q_PER_KV, axis=0)
        rows = slice(qb * blk, (qb + 1) * blk)
        q = q_ref[0, rows, :].astype(F32)
        for g in range(n_kv):
            gs = slice(g * LANES, (g + 1) * LANES)
            heads = range(g * Q_PER_KV, (g + 1) * Q_PER_KV)
            qs, sinks = [], []
            for hd in heads:
                qp = q[:, (hd // PAIR) * LANES:(hd // PAIR + 1) * LANES]
                keep = left if hd % PAIR == 0 else jnp.logical_not(left)
                qs.append(jnp.where(keep, qp, 0.0).astype(BF16))
                sinks.append(jnp.full((blk, 1), sink_ref[hd] * LOG2_E, F32))
            items.append({
                "qb": qb, "gs": gs, "rows": rows, "heads": heads, "bias": (bias_lo, bias_hi),
                "qs": jnp.concatenate(qs, axis=0), "sink": jnp.concatenate(sinks, axis=0)})

    def cat(refs, it):
        return jnp.concatenate([key_block(refs, it["qb"] + j, it["gs"]) for j in (-1, 0, 1)]
                               + [refs[3][0, :, it["gs"]]], axis=0)

    s = [_dot_nt(it["qs"], cat((kp_ref, kc_ref, kn_ref, kx_ref), it)) for it in items]
    s = [jnp.concatenate([x[:, :blk] + it["bias"][0], x[:, blk:2 * blk], x[:, 2 * blk:3 * blk] + it["bias"][1],
                          x[:, 3 * blk:]], axis=1) for x, it in zip(s, items)]
    m = [jnp.maximum(jnp.max(x, axis=1, keepdims=True), it["sink"]) for x, it in zip(s, items)]
    p = [jnp.exp2(x - mx) for x, mx in zip(s, m)]
    den = [jnp.sum(x, axis=1, keepdims=True) + jnp.exp2(it["sink"] - mx) for x, mx, it in zip(p, m, items)]
    o = [_dot(x.astype(BF16), cat((vp_ref, vc_ref, vn_ref, vx_ref), it)) / dn for x, dn, it in zip(p, den, items)]
    for x, it in zip(o, items):
        for hd in it["heads"][::PAIR]:
            j = hd - it["heads"][0]
            pair = jnp.where(left, x[j * blk:(j + 1) * blk], x[(j + 1) * blk:(j + 2) * blk])
            col = (hd // PAIR) * LANES
            o_ref[0, it["rows"], col:col + LANES] = pair.astype(BF16)


def _attention(sink, q, kd, vd, kxd, vxd):
    b, l, wq = q.shape
    wk = kd.shape[2]
    lc = kxd.shape[1]
    nb = l // ATTN_BLOCK
    span = ATTN_QB * ATTN_BLOCK
    blk = lambda w, f: pl.BlockSpec((1, ATTN_BLOCK, w), f)
    prev = lambda bi, i: (bi, jnp.maximum(i * ATTN_QB - 1, 0), 0)
    nxt = lambda bi, i: (bi, jnp.minimum((i + 1) * ATTN_QB, nb - 1), 0)
    cur = lambda w: pl.BlockSpec((1, span, w), lambda bi, i: (bi, i, 0))
    ctx = pl.BlockSpec((1, lc, wk), lambda bi, i: (bi, 0, 0))
    return pl.pallas_call(
        functools.partial(_attn_kernel, n_kv=wk // LANES),
        grid=(b, l // span),
        in_specs=[pl.BlockSpec(memory_space=pltpu.SMEM),
                  cur(wq),
                  blk(wk, prev), cur(wk), blk(wk, nxt),
                  blk(wk, prev), cur(wk), blk(wk, nxt),
                  ctx, ctx],
        out_specs=cur(wq),
        out_shape=jax.ShapeDtypeStruct((b, l, wq), BF16),
        compiler_params=_params("parallel", "parallel"),
        name="attention",
    )(sink, q, kd, kd, kd, vd, vd, vd, kxd, vxd)


def _conv3(x, prev_row, next_row, w):
    n = x.shape[0]
    row = _iota(x.shape, 0)
    xm = jnp.where(row == 0, prev_row, pltpu.roll(x, 1, 0))
    xp = jnp.where(row == n - 1, next_row, pltpu.roll(x, n - 1, 0))
    return xm * w[0:1] + x * w[1:2] + xp * w[2:3]


def _pair_masks():
    n = 2 * CHUNK
    row = _iota((n, n), 0)
    lane = _iota((n, n), 1)
    top, left = row < CHUNK, lane < CHUNK
    return {"row": row % CHUNK, "lane": lane % CHUNK, "top": top, "left": left, "bd": top == left,
            "left_h": _iota((CHUNK, n), 1) < CHUNK}


def _stack(a, b):
    return jnp.concatenate([a, b], axis=0)


def _fold(x, m):
    return jnp.where(m["left_h"], x[:CHUNK], x[CHUNK:])


def _chunk_local(inst, m):
    bf = lambda x: x.astype(BF16)
    diag = jnp.logical_not(m["top"]) & (m["lane"] == m["row"])
    masks = ((m["lane"] < m["row"]) | diag, (m["lane"] > m["row"]) | diag)
    mask_a = [masks[i["rev"]] for i in inst]
    lh = m["left_h"]
    half = CHUNK // 2
    keep_l = jnp.where(m["left"], 1.0, 0.0).astype(BF16)
    keep_r = jnp.where(m["left"], 0.0, 1.0).astype(BF16)
    keep_lh = jnp.where(lh, 1.0, 0.0).astype(BF16)
    keep_rh = jnp.where(lh, 0.0, 1.0).astype(BF16)

    def unfold(xb, anti=False):
        a, b = xb * keep_lh, xb * keep_rh
        return _stack(b, a) if anti else _stack(a, b)

    lhs = [bf(_stack(i["at"], i["rt"])) for i in inst]
    a01 = [_dot_nt(l, _stack(bf(_stack(i["bt"], i["kt"])) * keep_l, bf(_stack(i["kt"], i["bt"])) * keep_r))
           for l, i in zip(lhs, inst)]
    a0 = [jnp.where(ma, a[:, :LANES], 0.0) for ma, a in zip(mask_a, a01)]
    a1 = [jnp.where(ma, a[:, LANES:], 0.0) for ma, a in zip(mask_a, a01)]
    nc = [jnp.where(lh, x[:CHUNK], y[:CHUNK]) for x, y in zip(a0, a1)]
    arb = [jnp.where(lh, x[CHUNK:], y[CHUNK:]) for x, y in zip(a0, a1)]
    ak_ark_sw = [bf(jnp.where(m["left"], y, x)) for x, y in zip(a0, a1)]
    vh = [_dot(a, unfold(bf(i["v"]), anti=True)) for a, i in zip(ak_ark_sw, inst)]
    eye = jnp.where(_iota((CHUNK, LANES), 1) % CHUNK == _iota((CHUNK, LANES), 0), 1.0, 0.0)
    tc = [eye + n for n in nc]
    ncb = [bf(n) for n in nc]
    nc = [_dot(n, unfold(n)) for n in ncb]
    steps = CHUNK.bit_length() - 1
    for _ in range(steps - 2):
        ncb = [bf(n) for n in nc]
        both = [_dot(n, jnp.concatenate([unfold(bf(t)), unfold(n)], axis=1)) for n, t in zip(ncb, tc)]
        tc = [t + r[:, :LANES] for t, r in zip(tc, both)]
        nc = [r[:, LANES:] for r in both]
    inc = [_dot(bf(n[half:]), unfold(bf(t))) for n, t in zip(nc, tc)]
    tc = [jnp.concatenate([t[:half], t[half:] + d], axis=0) for t, d in zip(tc, inc)]
    pq = [_dot(bf(t), jnp.concatenate([unfold(l[:CHUNK]), unfold(bf(x[:CHUNK]))], axis=1))
          for t, l, x in zip(tc, lhs, vh)]
    kvf = [_dot_tn(bf(i["v"]), bf(i["kt"])) for i in inst]
    return [{"pm": r[:, :LANES], "qm": r[:, LANES:], "arb": b, "hm": x[CHUNK:], "kv": _fold(k, m)}
            for r, b, x, k in zip(pq, arb, vh, kvf)]


def _prep_kernel(*refs, latent, width):
    (rkv_ref, rkv_p, rkv_n, lora_ref, lora_p, lora_n, cw_ref, cwl_ref, kk_ref, ka_ref,
     w0_ref, a0_ref, wl_ref, rk_ref) = refs[:14]
    outs = refs[14:]
    out_refs = dict(zip(("pm", "qm", "rt", "bt", "arb", "kv"), outs[:6]))
    wc_ref = outs[6]
    if latent:
        bonus_ref, g_ref, hsum_ref = outs[7:]
    t = pl.program_id(1)
    nt = pl.num_programs(1)
    tile = rkv_ref.shape[1]
    w = width
    has_prev = (t > 0).astype(F32)
    has_next = (t < nt - 1).astype(F32)
    u = _conv3(rkv_ref[0], rkv_p[0, 7:8, :] * has_prev, rkv_n[0, 0:1, :] * has_next, cw_ref[...])
    ul = _conv3(lora_ref[0], lora_p[0, 7:8, :] * has_prev, lora_n[0, 0:1, :] * has_next, cwl_ref[...])
    r, k, v = u[:, :w], u[:, w:2 * w], u[:, 2 * w:]
    ones_bd = _head_ones(min(MXU_DIM, w))

    kk = k * kk_ref[...]
    kk = kk * lax.rsqrt(jnp.maximum(_head_sum(kk * kk, ones_bd), 1e-24))

    lane = _iota(ul.shape, 1)
    lin = jnp.where(lane < DECAY_LORA, jnp.tanh(ul),
                    jnp.where(lane < DECAY_LORA + ICLR_LORA, ul, jax.nn.sigmoid(ul)))
    proj = _dot(lin.astype(BF16), wl_ref[...])

    tr = _iota((tile, tile), 0)
    tc = _iota((tile, tile), 1)
    same = (tr // CHUNK) == (tc // CHUNK)
    tri = (jnp.where(same & (tc <= tr), 1.0, 0.0).astype(BF16),
           jnp.where(same & (tc >= tr), 1.0, 0.0).astype(BF16))

    def exact_dot(m, x):
        h1 = x.astype(BF16)
        r1 = x - h1.astype(F32)
        h2 = r1.astype(BF16)
        h3 = (r1 - h2.astype(F32)).astype(BF16)
        return _dot(m, h1) + _dot(m, h2) + _dot(m, h3)

    masks = _pair_masks()
    k_sum = None
    work = []
    for d in range(2):
        z = w0_ref[d:d + 1, :] + proj[:, d * w:(d + 1) * w]
        lw = -math.exp(-0.5) * jax.nn.sigmoid(z)
        a = jax.nn.sigmoid(a0_ref[d:d + 1, :] + proj[:, (2 + d) * w:(3 + d) * w])
        kd = k * (1.0 + (a - 1.0) * ka_ref[...])
        k_sum = kd if k_sum is None else k_sum + kd
        cum = exact_dot(tri[d], lw)
        e_neg = jnp.exp(-cum)
        full = {"at": -kk * jnp.exp(cum - lw), "rt": r * jnp.exp(cum), "bt": kk * a * e_neg,
                "kt": kd * e_neg, "v": v}
        out_refs["rt"][d, 0] = full["rt"].astype(BF16)
        out_refs["bt"][d, 0] = full["bt"].astype(BF16)
        for j in range(tile // CHUNK):
            edge = (j + 1) * CHUNK - 1 if d == 0 else j * CHUNK
            wc_ref[0, j, d:d + 1, :] = jnp.exp(cum[edge:edge + 1, :])
        work += [(d, slice(j * CHUNK, (j + 1) * CHUNK), slice(p * LANES, (p + 1) * LANES), full)
                 for j in range(tile // CHUNK) for p in range(w // LANES)]
    hsum = {}
    for g0 in range(0, len(work), PREP_GROUP):
        group = work[g0:g0 + PREP_GROUP]
        inst = [dict({name: val[rs, ls] for name, val in full.items()}, rev=d) for d, rs, ls, full in group]
        for (d, rs, ls, _), res in zip(group, _chunk_local(inst, masks)):
            seen = hsum.get((rs.start, ls.start))
            hsum[(rs.start, ls.start)] = (rs, ls, res["hm"] if seen is None else seen[2] + res["hm"])
            for name in res:
                if name in out_refs:
                    out_refs[name][d, 0, rs, ls] = res[name].astype(BF16)
    if latent:
        for rs, ls, val in hsum.values():
            hsum_ref[0, rs, ls] = val
        bonus_ref[0] = _head_sum(r * k_sum * rk_ref[...], ones_bd) * v
        g_ref[0] = proj[:, 4 * w:5 * w].astype(BF16)


def _wkv_prep(rkv, lora, prm, *, latent, tile):
    b, l, w3 = rkv.shape
    w = w3 // 3
    nt = l // tile
    n8 = l // 8
    tok = lambda wd: pl.BlockSpec((1, tile, wd), lambda i, t: (i, t, 0))
    prev = lambda wd: pl.BlockSpec((1, 8, wd), lambda i, t: (i, jnp.maximum(t * (tile // 8) - 1, 0), 0))
    nxt = lambda wd: pl.BlockSpec((1, 8, wd), lambda i, t: (i, jnp.minimum((t + 1) * (tile // 8), n8 - 1), 0))
    wl = lora.shape[2]
    in_specs = [tok(w3), prev(w3), nxt(w3), tok(wl), prev(wl), nxt(wl)]
    consts = [prm["conv_rkv"], prm["conv_lora"], prm["k_k"], prm["k_a"], prm["decay_w0"],
              prm["iclr_a0"], prm["lora_w"], prm["r_k"]]
    in_specs += [_const_spec(c.shape) for c in consts]
    dirtok = pl.BlockSpec((2, 1, tile, w), lambda i, t: (0, i, t, 0))
    out_specs = [dirtok] * 6 + [pl.BlockSpec((1, tile // CHUNK, 2, w), lambda i, t: (i, t, 0, 0))]
    out_shape = [jax.ShapeDtypeStruct((2, b, l, w), BF16)] * 6 + [
        jax.ShapeDtypeStruct((b, l // CHUNK, 2, w), F32)]
    if latent:
        out_specs += [tok(w), tok(w), tok(w)]
        out_shape += [jax.ShapeDtypeStruct((b, l, w), F32), jax.ShapeDtypeStruct((b, l, w), BF16),
                      jax.ShapeDtypeStruct((b, l, w), F32)]
    return pl.pallas_call(
        functools.partial(_prep_kernel, latent=latent, width=w),
        grid=(b, nt),
        in_specs=in_specs,
        out_specs=out_specs,
        out_shape=out_shape,
        compiler_params=_params("parallel", "parallel"),
        name="wkv_prep_latent" if latent else "wkv_prep_context",
    )(rkv, rkv, rkv, lora, lora, lora, *consts)


def _wkv_kernel(*refs, emit_y, n_pairs):
    names = ("pm", "qm", "rt", "bt", "arb", "kv", "wc")
    n = len(names)
    dir_refs = (dict(zip(names, refs[0:n])), dict(zip(names, refs[n:2 * n])))
    z0_ref = refs[2 * n]
    if emit_y:
        y_refs = refs[2 * n + 1:2 * n + 3]
        z_scr = refs[2 * n + 3]
    else:
        zfin_ref = refs[2 * n + 1]
        z_scr = refs[2 * n + 2]
    c = pl.program_id(0)

    @pl.when(c == 0)
    def _():
        z_scr[...] = z0_ref[...]

    left_h = _iota((CHUNK, LANES), 1) < CHUNK
    keep_l = jnp.where(left_h, 1.0, 0.0).astype(BF16)
    keep_r = jnp.where(left_h, 0.0, 1.0).astype(BF16)

    def unfold(xb):
        return _stack(xb * keep_l, xb * keep_r)

    tiles = [(i, d, p, slice(p * LANES, (p + 1) * LANES))
             for i in range(z_scr.shape[0]) for d in range(2) for p in range(n_pairs)]
    state = [z_scr[i, d, p] for i, d, p, _ in tiles]
    n_sub = dir_refs[0]["wc"].shape[1]
    for step in range(n_sub):
        sub = (step, n_sub - 1 - step)
        rows = [slice(sub[d] * CHUNK, (sub[d] + 1) * CHUNK) for d in range(2)]
        ld = lambda name: [dir_refs[d][name][0, i, rows[d], sl] for i, d, _, sl in tiles]
        sbd = [unfold(s.astype(BF16)) for s in state]
        if emit_y:
            ur = [_dot_nt(_stack(pm, rt), s) for pm, rt, s in zip(ld("pm"), ld("rt"), sbd)]
        else:
            ur = [_dot_nt(pm, s) for pm, s in zip(ld("pm"), sbd)]
        u = [x[:CHUNK] + q.astype(F32) for x, q in zip(ur, ld("qm"))]
        ub = [ui.astype(BF16) for ui in u]
        inc = [_dot_tn(ui, bt) for ui, bt in zip(ub, ld("bt"))]
        inc = [jnp.where(left_h, x[:CHUNK], x[CHUNK:]) for x in inc]
        if emit_y:
            yc = [_dot(a, unfold(ui)) for a, ui in zip(ld("arb"), ub)]
            for (i, d, _, sl), x, ys in zip(tiles, ur, yc):
                y_refs[d][i, rows[d], sl] = x[CHUNK:] + ys
        state = [(s + dz + kv.astype(F32)) * dir_refs[d]["wc"][i, sub[d], d:d + 1, sl]
                 for (i, d, _, sl), s, dz, kv in zip(tiles, state, inc, ld("kv"))]
    for (i, d, p, _), s in zip(tiles, state):
        z_scr[i, d, p] = s

    if not emit_y:
        @pl.when(c == pl.num_programs(0) - 1)
        def _():
            zfin_ref[...] = z_scr[...]


def _wkv_scan(prep, z0, *, emit_y):
    wc = prep[6]
    _, b, l, w = prep[0].shape
    n_sub = min(SCAN_CHUNKS, l // CHUNK)
    blk = n_sub * CHUNK
    nc = l // blk
    n_pairs = w // LANES
    fwd = lambda c: c
    rev = lambda c: nc - 1 - c
    in_specs, args = [], []
    for d, cm in enumerate((fwd, rev)):
        for arr in prep[:6]:
            in_specs.append(pl.BlockSpec((1, b, blk, w), lambda c, d=d, cm=cm: (d, 0, cm(c), 0)))
            args.append(arr)
        in_specs.append(pl.BlockSpec((b, n_sub, 2, w), lambda c, cm=cm: (0, cm(c), 0, 0)))
        args.append(wc)
    zshape = (b, 2, n_pairs, CHUNK, LANES)
    zspec = pl.BlockSpec(zshape, lambda c: (0, 0, 0, 0, 0))
    in_specs.append(zspec)
    args.append(z0)
    if emit_y:
        out_specs = [pl.BlockSpec((b, blk, w), lambda c: (0, c, 0)),
                     pl.BlockSpec((b, blk, w), lambda c: (0, nc - 1 - c, 0))]
        out_shape = [jax.ShapeDtypeStruct((b, l, w), F32)] * 2
    else:
        out_specs = zspec
        out_shape = jax.ShapeDtypeStruct(zshape, F32)
    return pl.pallas_call(
        functools.partial(_wkv_kernel, emit_y=emit_y, n_pairs=n_pairs),
        grid=(nc,),
        in_specs=in_specs,
        out_specs=out_specs,
        out_shape=out_shape,
        scratch_shapes=[pltpu.VMEM(zshape, F32)],
        compiler_params=_params("arbitrary"),
        name="wkv_scan_latent" if emit_y else "wkv_scan_context",
    )(*args)


def _merge_kernel(x_ref, mod_ref, ya_ref, yf_ref, yr_ref, yh_ref, bonus_ref, g_ref, gate_ref,
                  lnw_ref, lnb_ref, wba_ref, wbr_ref, wo_ref, n2_ref, wu_ref, wd_ref, nf_ref,
                  o_ref, *, ff_chunk):
    x = x_ref[0]
    d = x.shape[1]
    mod = lambda j: mod_ref[0, j:j + 1, :]
    y = yf_ref[0] + yr_ref[0] + yh_ref[0]
    gw = min(MXU_DIM, y.shape[1])
    ones_bd = _head_ones(gw)

    def head_mean(t):
        tb = t.astype(BF16)
        parts = [_dot(tb[:, j:j + gw], ones_bd) for j in range(0, t.shape[1], gw)]
        return jnp.concatenate(parts, axis=1) * (1.0 / HEAD_DIM)

    mu = head_mean(y)
    yc = y - mu
    var = head_mean(yc * yc)
    yn = yc * lax.rsqrt(var + LNX_EPS)
    yr = (yn * lnw_ref[...] + lnb_ref[...] + bonus_ref[0]) * g_ref[0].astype(F32)
    gate = gate_ref[0].astype(F32)
    merged = gate[:, :d] * _dot(ya_ref[0], wba_ref[...]) + gate[:, d:] * _dot(yr.astype(BF16), wbr_ref[...])
    x1 = x + mod(2) * _dot(merged.astype(BF16), wo_ref[...])
    h2 = (_rmsnorm(x1, n2_ref[...]) * (1.0 + mod(4)) + mod(3)).astype(BF16)
    acc = jnp.zeros_like(x1)
    for j in range(wu_ref.shape[1] // ff_chunk):
        cs = slice(j * ff_chunk, (j + 1) * ff_chunk)
        up = jnp.maximum(_dot(h2, wu_ref[:, cs]), 0.0)
        acc = acc + _dot((up * up).astype(BF16), wd_ref[cs, :])
    x2 = x1 + mod(5) * acc
    o_ref[0] = _rmsnorm(x2, nf_ref[...])


def _merge_mlp(x, mod, ya, yf, yr, yh, bonus, g, gate, prm, *, tile):
    b, l, d = x.shape
    tok = lambda arr: pl.BlockSpec((1, tile, arr.shape[2]), lambda i, t: (i, t, 0))
    consts = [prm["lnx_w"], prm["lnx_b"], prm["w_branch_attn"], prm["w_branch_rwkv"], prm["w_out"],
              prm["norm2_g"], prm["w_mlp_up"], prm["w_mlp_down"], prm["norm_f_g"]]
    toks = [ya, yf, yr, yh, bonus, g, gate]
    return pl.pallas_call(
        functools.partial(_merge_kernel, ff_chunk=min(1024, prm["w_mlp_up"].shape[1])),
        grid=(b, l // tile),
        in_specs=[tok(x), pl.BlockSpec((1,) + mod.shape[1:], lambda i, t: (i, 0, 0))]
        + [tok(a) for a in toks] + [_const_spec(c.shape) for c in consts],
        out_specs=tok(x),
        out_shape=jax.ShapeDtypeStruct(x.shape, x.dtype),
        compiler_params=_params("parallel", "parallel"),
        name="merge_mlp",
    )(x, mod, *toks, *consts)


def _rope_tables(l):
    n_freq = HEAD_DIM // 4
    inv_freq = jnp.power(ROPE_BASE, -jnp.arange(n_freq, dtype=F32) / n_freq)
    rows = l // GRID_W
    row = jnp.repeat(jnp.arange(rows, dtype=F32), GRID_W)
    col = jnp.tile(jnp.arange(GRID_W, dtype=F32), rows)
    ang = jnp.concatenate([row[:, None] * inv_freq, col[:, None] * inv_freq], axis=-1)
    cos, sin = jnp.cos(ang), jnp.sin(ang)
    reps = LANES // HEAD_DIM
    return (jnp.tile(jnp.concatenate([cos, cos], axis=1), (1, reps)),
            jnp.tile(jnp.concatenate([-sin, sin], axis=1), (1, reps)))


def _pad_cols(w, width):
    return jnp.pad(w, ((0, 0), (0, width - w.shape[1])))


def kernel(x, c, ctx, c_ctx, w_ada, b_ada, norm1_g, w_in, sink, conv_w, decay_w0, decay_w2, iclr_a0, iclr_a2, gate_g2, k_k, k_a, r_k, lnx_w, lnx_b, w_branch_attn, w_branch_rwkv, w_out, norm2_g, w_mlp_up, w_mlp_down, norm_f_g):
    assert w_in.shape[0] == 1, "single-layer block: context tokens are read, never updated"
    b, l, d = x.shape
    attn_w = w_branch_attn.shape[1]
    rw = w_branch_rwkv.shape[1]
    n_q = attn_w // HEAD_DIM
    n_kv = n_q // Q_PER_KV
    kv_w = n_kv * HEAD_DIM
    assert kv_w == LANES and rw % LANES == 0 and l % 256 == 0 and ctx.shape[1] % 256 == 0

    w = w_in[0]
    o_k, o_r = attn_w, attn_w + 2 * kv_w
    o_l = o_r + 3 * rw
    o_g = o_l + DECAY_LORA + ICLR_LORA + GATE_LORA
    w_ctx = _pad_cols(w[:, o_k:o_g], o_l - o_k + LORA_PAD).astype(BF16)
    w_lat = [w[:, :o_k].astype(BF16), w_ctx, w[:, o_g:].astype(BF16)]
    widths_ctx = {"kv": 2 * kv_w, "k": PAIR * kv_w, "v": PAIR * kv_w, "rkv": 3 * rw, "lora": LORA_PAD}
    widths_lat = {"q": attn_w, **widths_ctx, "gate": 2 * d}

    cw = conv_w[0]
    lora_w = jnp.zeros((LORA_PAD, 5 * rw), F32)
    lora_w = lora_w.at[:DECAY_LORA, :rw].set(decay_w2[0, 0]).at[:DECAY_LORA, rw:2 * rw].set(decay_w2[0, 1])
    r1 = DECAY_LORA + ICLR_LORA
    lora_w = lora_w.at[DECAY_LORA:r1, 2 * rw:3 * rw].set(iclr_a2[0, 0]).at[DECAY_LORA:r1, 3 * rw:4 * rw].set(iclr_a2[0, 1])
    lora_w = lora_w.at[r1:r1 + GATE_LORA, 4 * rw:].set(gate_g2[0])
    prm = {
        "conv_rkv": cw[:, :3 * rw], "conv_lora": _pad_cols(cw[:, 3 * rw:], LORA_PAD),
        "k_k": k_k[0].reshape(1, rw), "k_a": k_a[0].reshape(1, rw),
        "decay_w0": decay_w0[0], "iclr_a0": iclr_a0[0], "lora_w": lora_w.astype(BF16),
        "r_k": r_k[0].reshape(1, rw),
        "lnx_w": lnx_w[0].reshape(1, rw), "lnx_b": lnx_b[0].reshape(1, rw),
        "w_branch_attn": w_branch_attn[0].astype(BF16), "w_branch_rwkv": w_branch_rwkv[0].astype(BF16),
        "w_out": w_out[0].astype(BF16), "norm2_g": norm2_g[0].reshape(1, d),
        "w_mlp_up": w_mlp_up[0].astype(BF16), "w_mlp_down": w_mlp_down[0].astype(BF16),
        "norm_f_g": norm_f_g.reshape(1, d),
    }

    rows = -(-(b + 1) // 8) * 8
    cc = jnp.zeros((rows, d), F32).at[:b].set(c).at[b].set(c_ctx)
    mod = _ada_mod(cc, w_ada[0], b_ada[0]).reshape(rows, -1, d)

    q, kd, vd, rkv, lora, gate = _in_proj(x, mod, b, norm1_g[0], w_lat, widths_lat, _rope_tables(l),
                                          latent=True, tile=512)
    kxd, vxd, rkv_c, lora_c = _in_proj(ctx, mod, b, norm1_g[0], [w_ctx], widths_ctx, None,
                                       latent=False, tile=256)
    ya = _attention(sink[0], q, kd, vd, kxd, vxd)

    prep_c = _wkv_prep(rkv_c, lora_c, prm, latent=False, tile=256)
    z_ctx = _wkv_scan(prep_c, jnp.zeros((b, 2, rw // LANES, CHUNK, LANES), F32), emit_y=False)
    prep = _wkv_prep(rkv, lora, prm, latent=True, tile=256)
    yf, yr = _wkv_scan(prep, z_ctx, emit_y=True)
    bonus, g, yh = prep[7], prep[8], prep[9]

    return _merge_mlp(x, mod, ya, yf, yr, yh, bonus, g, gate, prm, tile=512)
```

```python
import functools
import math

import jax
import jax.numpy as jnp
from jax import lax
from jax.experimental import pallas as pl
from jax.experimental.pallas import tpu as pltpu

F32 = jnp.float32
BF16 = jnp.bfloat16

GRID_W = 64
HEAD_DIM = 64
Q_PER_KV = 4
ATTN_BLOCK = 128
ATTN_QB = 4
ROPE_BASE = 10000.0
NORM_EPS = 1e-6
LNX_EPS = 1e-5 * HEAD_DIM
DECAY_LORA, ICLR_LORA, GATE_LORA = 32, 32, 96
LORA_PAD = 256
CHUNK = 64
SCAN_CHUNKS = 4
PREP_GROUP = 16
LANES = 128
MXU_DIM = 256
PAIR = LANES // HEAD_DIM
NEG = -1e30
LOG2_E = math.log2(math.e)
VMEM_LIMIT = 56 * 1024 * 1024


def _dot(a, b):
    return jnp.dot(a, b, preferred_element_type=F32)


def _dot_nt(a, b):
    return lax.dot_general(a, b, (((1,), (1,)), ((), ())), preferred_element_type=F32)


def _dot_tn(a, b):
    return lax.dot_general(a, b, (((0,), (0,)), ((), ())), preferred_element_type=F32)


def _iota(shape, dim):
    return lax.broadcasted_iota(jnp.int32, shape, dim)


def _head_ones(width):
    r = _iota((width, width), 0) // HEAD_DIM
    c = _iota((width, width), 1) // HEAD_DIM
    return jnp.where(r == c, 1.0, 0.0).astype(BF16)


def _head_sum(x, ones_bd):
    gw = ones_bd.shape[0]
    hi = x.astype(BF16)
    lo = (x - hi.astype(F32)).astype(BF16)
    parts = [_dot(hi[:, j:j + gw], ones_bd) + _dot(lo[:, j:j + gw], ones_bd) for j in range(0, x.shape[1], gw)]
    return jnp.concatenate(parts, axis=1)


def _rmsnorm(x, g):
    ms = jnp.mean(x * x, axis=-1, keepdims=True)
    return x * lax.rsqrt(ms + NORM_EPS) * g


def _params(*sem):
    return pltpu.CompilerParams(dimension_semantics=sem, vmem_limit_bytes=VMEM_LIMIT)


def _const_spec(shape):
    nd = len(shape)
    return pl.BlockSpec(shape, lambda *_: (0,) * nd, pipeline_mode=pl.Buffered(1))


def _ada_kernel(c_ref, w_ref, b_ref, o_ref):
    c = c_ref[...]
    s = c * jax.nn.sigmoid(c)
    o_ref[...] = _dot(s.astype(BF16), w_ref[...].astype(BF16)) + b_ref[...]


def _ada_mod(cc, w_ada, b_ada):
    rows, d = cc.shape
    n = w_ada.shape[1]
    return pl.pallas_call(
        _ada_kernel,
        grid=(n // d,),
        in_specs=[pl.BlockSpec((rows, d), lambda j: (0, 0)),
                  pl.BlockSpec((d, d), lambda j: (0, j)),
                  pl.BlockSpec((1, d), lambda j: (0, j))],
        out_specs=pl.BlockSpec((rows, d), lambda j: (0, j)),
        out_shape=jax.ShapeDtypeStruct((rows, n), F32),
        compiler_params=_params("arbitrary"),
        name="ada_mod",
    )(cc, w_ada, b_ada.reshape(1, n))


def _rope(x, cos_t, sin_t):
    w = x.shape[1]
    half = HEAD_DIM // 2
    first = (_iota(x.shape, 1) % HEAD_DIM) < half
    swapped = jnp.where(first, pltpu.roll(x, w - half, 1), pltpu.roll(x, half, 1))
    reps = w // LANES
    c = jnp.concatenate([cos_t] * reps, axis=1)
    s = jnp.concatenate([sin_t] * reps, axis=1)
    return x * c + swapped * s


def _inproj_kernel(*refs, latent, widths):
    if latent:
        (x_ref, mod_ref, g_ref, wq_ref, w_ref, wg_ref, cos_ref, sin_ref,
         q_ref, k_ref, v_ref, rkv_ref, lora_ref, gate_ref) = refs
    else:
        x_ref, mod_ref, g_ref, w_ref, k_ref, v_ref, rkv_ref, lora_ref = refs
    x = x_ref[0]
    h = _rmsnorm(x, g_ref[...]) * (1.0 + mod_ref[0, 1:2, :]) + mod_ref[0, 0:1, :]
    hb = h.astype(BF16)
    off = 0

    def seg(name):
        nonlocal off
        lo = off
        off += widths[name]
        return _dot(hb, w_ref[:, lo:off])

    def dup_heads(t):
        first = _iota(t.shape, 1) < HEAD_DIM
        other = pltpu.roll(t, HEAD_DIM, 1)
        return jnp.concatenate([jnp.where(first, t, other), jnp.where(first, other, t)], axis=1)

    kv = seg("kv")
    k, v = dup_heads(kv[:, :LANES]), dup_heads(kv[:, LANES:])
    if latent:
        cos_t, sin_t = cos_ref[...], sin_ref[...]
        q_ref[0] = (_rope(_dot(hb, wq_ref[...]), cos_t, sin_t) * (LOG2_E * HEAD_DIM ** -0.5)).astype(BF16)
        k = _rope(k, cos_t, sin_t)
    k_ref[0] = k.astype(BF16)
    v_ref[0] = v.astype(BF16)
    rkv_ref[0] = seg("rkv")
    lora_ref[0] = seg("lora")
    if latent:
        gate_ref[0] = jax.nn.sigmoid(_dot(hb, wg_ref[...])).astype(BF16)


def _in_proj(x, mod, mod_row, norm_g, weights, widths, tables, *, latent, tile):
    b, l, d = x.shape
    nt = l // tile
    if latent:
        mod_map = lambda i, t: (i, 0, 0)
    else:
        mod_map = lambda i, t: (mod_row, 0, 0)
    tok = lambda w: pl.BlockSpec((1, tile, w), lambda i, t: (i, t, 0))
    in_specs = [tok(d),
                pl.BlockSpec((1,) + mod.shape[1:], mod_map),
                _const_spec((1, d))] + [_const_spec(w.shape) for w in weights]
    args = [x, mod, norm_g.reshape(1, d), *weights]
    out_specs, out_shape = [], []

    def out(w, dt):
        out_specs.append(tok(w))
        out_shape.append(jax.ShapeDtypeStruct((b, l, w), dt))

    if latent:
        in_specs += [pl.BlockSpec((tile, LANES), lambda i, t: (t, 0))] * 2
        args += list(tables)
        out(widths["q"], BF16)
    out(widths["k"], BF16)
    out(widths["v"], BF16)
    out(widths["rkv"], F32)
    out(widths["lora"], F32)
    if latent:
        out(widths["gate"], BF16)
    return pl.pallas_call(
        functools.partial(_inproj_kernel, latent=latent, widths=widths),
        grid=(b, nt),
        in_specs=in_specs,
        out_specs=out_specs,
        out_shape=out_shape,
        compiler_params=_params("parallel", "parallel"),
        name="in_proj_latent" if latent else "in_proj_context",
    )(*args)


def _attn_kernel(sink_ref, q_ref, kp_ref, kc_ref, kn_ref, vp_ref, vc_ref, vn_ref,
                 kx_ref, vx_ref, o_ref, *, n_kv):
    i = pl.program_id(1)
    last = pl.num_programs(1) - 1
    blk = ATTN_BLOCK
    qi = _iota((blk, blk), 0)
    kj = _iota((blk, blk), 1)
    left = _iota((blk, LANES), 1) < HEAD_DIM

    def key_block(refs, j, gs):
        ref_p, ref_c, ref_n = refs[:3]
        if j < 0:
            return ref_p[0, :, gs]
        if j >= ATTN_QB:
            return ref_n[0, :, gs]
        return ref_c[0, j * blk:(j + 1) * blk, gs]

    items = []
    for qb in range(ATTN_QB):
        lo_ok = kj >= qi
        hi_ok = kj <= qi
        if qb == 0:
            lo_ok = lo_ok & (i > 0)
        if qb == ATTN_QB - 1:
            hi_ok = hi_ok & (i < last)
        bias_lo = jnp.concatenate([jnp.where(lo_ok, 0.0, NEG)] * Q_PER_KV, axis=0)
        bias_hi = jnp.concatenate([jnp.where(hi_ok, 0.0, NEG)] * Q_PER_KV, axis=0)
        rows = slice(qb * blk, (qb + 1) * blk)
        q = q_ref[0, rows, :].astype(F32)
        for g in range(n_kv):
            gs = slice(g * LANES, (g + 1) * LANES)
            heads = range(g * Q_PER_KV, (g + 1) * Q_PER_KV)
            qs, sinks = [], []
            for hd in heads:
                qp = q[:, (hd // PAIR) * LANES:(hd // PAIR + 1) * LANES]
                keep = left if hd % PAIR == 0 else jnp.logical_not(left)
                qs.append(jnp.where(keep, qp, 0.0).astype(BF16))
                sinks.append(jnp.full((blk, 1), sink_ref[hd] * LOG2_E, F32))
            items.append({
                "qb": qb, "gs": gs, "rows": rows, "heads": heads, "bias": (bias_lo, bias_hi),
                "qs": jnp.concatenate(qs, axis=0), "sink": jnp.concatenate(sinks, axis=0)})

    def cat(refs, it):
        return jnp.concatenate([key_block(refs, it["qb"] + j, it["gs"]) for j in (-1, 0, 1)]
                               + [refs[3][0, :, it["gs"]]], axis=0)

    s = [_dot_nt(it["qs"], cat((kp_ref, kc_ref, kn_ref, kx_ref), it)) for it in items]
    s = [jnp.concatenate([x[:, :blk] + it["bias"][0], x[:, blk:2 * blk], x[:, 2 * blk:3 * blk] + it["bias"][1],
                          x[:, 3 * blk:]], axis=1) for x, it in zip(s, items)]
    m = [jnp.maximum(jnp.max(x, axis=1, keepdims=True), it["sink"]) for x, it in zip(s, items)]
    p = [jnp.exp2(x - mx) for x, mx in zip(s, m)]
    den = [jnp.sum(x, axis=1, keepdims=True) + jnp.exp2(it["sink"] - mx) for x, mx, it in zip(p, m, items)]
    o = [_dot(x.astype(BF16), cat((vp_ref, vc_ref, vn_ref, vx_ref), it)) / dn for x, dn, it in zip(p, den, items)]
    for x, it in zip(o, items):
        for hd in it["heads"][::PAIR]:
            j = hd - it["heads"][0]
            pair = jnp.where(left, x[j * blk:(j + 1) * blk], x[(j + 1) * blk:(j + 2) * blk])
            col = (hd // PAIR) * LANES
            o_ref[0, it["rows"], col:col + LANES] = pair.astype(BF16)


def _attention(sink, q, kd, vd, kxd, vxd):
    b, l, wq = q.shape
    wk = kd.shape[2]
    lc = kxd.shape[1]
    nb = l // ATTN_BLOCK
    span = ATTN_QB * ATTN_BLOCK
    blk = lambda w, f: pl.BlockSpec((1, ATTN_BLOCK, w), f)
    prev = lambda bi, i: (bi, jnp.maximum(i * ATTN_QB - 1, 0), 0)
    nxt = lambda bi, i: (bi, jnp.minimum((i + 1) * ATTN_QB, nb - 1), 0)
    cur = lambda w: pl.BlockSpec((1, span, w), lambda bi, i: (bi, i, 0))
    ctx = pl.BlockSpec((1, lc, wk), lambda bi, i: (bi, 0, 0))
    return pl.pallas_call(
        functools.partial(_attn_kernel, n_kv=wk // LANES),
        grid=(b, l // span),
        in_specs=[pl.BlockSpec(memory_space=pltpu.SMEM),
                  cur(wq),
                  blk(wk, prev), cur(wk), blk(wk, nxt),
                  blk(wk, prev), cur(wk), blk(wk, nxt),
                  ctx, ctx],
        out_specs=cur(wq),
        out_shape=jax.ShapeDtypeStruct((b, l, wq), BF16),
        compiler_params=_params("parallel", "parallel"),
        name="attention",
    )(sink, q, kd, kd, kd, vd, vd, vd, kxd, vxd)


def _conv3(x, prev_row, next_row, w):
    n = x.shape[0]
    row = _iota((8, x.shape[1]), 0)
    xm = pltpu.roll(x, 1, 0)
    xm = jnp.concatenate([jnp.where(row == 0, prev_row, xm[:8]), xm[8:]], axis=0)
    xp = pltpu.roll(x, n - 1, 0)
    xp = jnp.concatenate([xp[:n - 8], jnp.where(row == 7, next_row, xp[n - 8:])], axis=0)
    return xm * w[0:1] + x * w[1:2] + xp * w[2:3]


def _pair_masks():
    n = 2 * CHUNK
    row = _iota((n, n), 0)
    lane = _iota((n, n), 1)
    top, left = row < CHUNK, lane < CHUNK
    return {"row": row % CHUNK, "lane": lane % CHUNK, "top": top, "left": left, "bd": top == left,
            "left_h": _iota((CHUNK, n), 1) < CHUNK}


def _stack(a, b):
    return jnp.concatenate([a, b], axis=0)


def _fold(x, m):
    return jnp.where(m["left_h"], x[:CHUNK], x[CHUNK:])


def _chunk_local(inst, m):
    bf = lambda x: x.astype(BF16)
    diag = jnp.logical_not(m["top"]) & (m["lane"] == m["row"])
    masks = ((m["lane"] < m["row"]) | diag, (m["lane"] > m["row"]) | diag)
    mask_a = [masks[i["rev"]] for i in inst]
    lh = m["left_h"]
    half = CHUNK // 2
    keep_l = jnp.where(m["left"], 1.0, 0.0).astype(BF16)
    keep_r = jnp.where(m["left"], 0.0, 1.0).astype(BF16)
    keep_lh = jnp.where(lh, 1.0, 0.0).astype(BF16)
    keep_rh = jnp.where(lh, 0.0, 1.0).astype(BF16)

    def unfold(xb, anti=False):
        a, b = xb * keep_lh, xb * keep_rh
        return _stack(b, a) if anti else _stack(a, b)

    lhs = [bf(_stack(i["at"], i["rt"])) for i in inst]
    a01 = [_dot_nt(l, _stack(bf(_stack(i["bt"], i["kt"])) * keep_l, bf(_stack(i["kt"], i["bt"])) * keep_r))
           for l, i in zip(lhs, inst)]
    a0 = [jnp.where(ma, a[:, :LANES], 0.0) for ma, a in zip(mask_a, a01)]
    a1 = [jnp.where(ma, a[:, LANES:], 0.0) for ma, a in zip(mask_a, a01)]
    nc = [jnp.where(lh, x[:CHUNK], y[:CHUNK]) for x, y in zip(a0, a1)]
    arb = [jnp.where(lh, x[CHUNK:], y[CHUNK:]) for x, y in zip(a0, a1)]
    ak_ark_sw = [bf(jnp.where(m["left"], y, x)) for x, y in zip(a0, a1)]
    vh = [_dot(a, unfold(bf(i["v"]), anti=True)) for a, i in zip(ak_ark_sw, inst)]
    eye = jnp.where(_iota((CHUNK, LANES), 1) % CHUNK == _iota((CHUNK, LANES), 0), 1.0, 0.0)
    tc = [eye + n for n in nc]
    ncb = [bf(n) for n in nc]
    nc = [_dot(n, unfold(n)) for n in ncb]
    steps = CHUNK.bit_length() - 1
    for _ in range(steps - 2):
        ncb = [bf(n) for n in nc]
        both = [_dot(n, jnp.concatenate([unfold(bf(t)), unfold(n)], axis=1)) for n, t in zip(ncb, tc)]
        tc = [t + r[:, :LANES] for t, r in zip(tc, both)]
        nc = [r[:, LANES:] for r in both]
    inc = [_dot(bf(n[half:]), unfold(bf(t))) for n, t in zip(nc, tc)]
    tc = [jnp.concatenate([t[:half], t[half:] + d], axis=0) for t, d in zip(tc, inc)]
    pq = [_dot(bf(t), jnp.concatenate([unfold(l[:CHUNK]), unfold(bf(x[:CHUNK]))], axis=1))
          for t, l, x in zip(tc, lhs, vh)]
    kvf = [_dot_tn(bf(i["v"]), bf(i["kt"])) for i in inst]
    return [{"pm": r[:, :LANES], "qm": r[:, LANES:], "arb": b, "hm": x[CHUNK:], "kv": _fold(k, m)}
            for r, b, x, k in zip(pq, arb, vh, kvf)]


def _prep_kernel(*refs, latent, width):
    (rkv_ref, rkv_p, rkv_n, lora_ref, lora_p, lora_n, cw_ref, cwl_ref, kk_ref, ka_ref,
     w0_ref, a0_ref, wl_ref, rk_ref) = refs[:14]
    outs = refs[14:]
    out_refs = dict(zip(("pm", "qm", "rt", "bt", "arb", "kv"), outs[:6]))
    wc_ref = outs[6]
    if latent:
        bonus_ref, g_ref, hsum_ref = outs[7:]
    t = pl.program_id(1)
    nt = pl.num_programs(1)
    tile = rkv_ref.shape[1]
    w = width
    has_prev = (t > 0).astype(F32)
    has_next = (t < nt - 1).astype(F32)
    u = _conv3(rkv_ref[0], rkv_p[0, 7:8, :] * has_prev, rkv_n[0, 0:1, :] * has_next, cw_ref[...])
    ul = _conv3(lora_ref[0], lora_p[0, 7:8, :] * has_prev, lora_n[0, 0:1, :] * has_next, cwl_ref[...])
    r, k, v = u[:, :w], u[:, w:2 * w], u[:, 2 * w:]
    ones_bd = _head_ones(min(MXU_DIM, w))

    kk = k * kk_ref[...]
    kk = kk * lax.rsqrt(jnp.maximum(_head_sum(kk * kk, ones_bd), 1e-24))

    lane = _iota(ul.shape, 1)
    lin = jnp.where(lane < DECAY_LORA, jnp.tanh(ul),
                    jnp.where(lane < DECAY_LORA + ICLR_LORA, ul, jax.nn.sigmoid(ul)))
    proj = _dot(lin.astype(BF16), wl_ref[...])

    tr = _iota((tile, tile), 0)
    tc = _iota((tile, tile), 1)
    same = (tr // CHUNK) == (tc // CHUNK)
    tri = (jnp.where(same & (tc <= tr), 1.0, 0.0).astype(BF16),
           jnp.where(same & (tc >= tr), 1.0, 0.0).astype(BF16))

    def exact_dot(m, x):
        h1 = x.astype(BF16)
        h2 = (x - h1.astype(F32)).astype(BF16)
        return _dot(m, h1) + _dot(m, h2)

    masks = _pair_masks()
    k_sum = None
    work = []
    for d in range(2):
        z = w0_ref[d:d + 1, :] + proj[:, d * w:(d + 1) * w]
        lw = -math.exp(-0.5) * jax.nn.sigmoid(z)
        a = jax.nn.sigmoid(a0_ref[d:d + 1, :] + proj[:, (2 + d) * w:(3 + d) * w])
        kd = k * (1.0 + (a - 1.0) * ka_ref[...])
        k_sum = kd if k_sum is None else k_sum + kd
        cum = exact_dot(tri[d], lw)
        e_neg = jnp.exp(-cum)
        full = {"at": -kk * jnp.exp(cum - lw), "rt": r * jnp.exp(cum), "bt": kk * a * e_neg,
                "kt": kd * e_neg, "v": v}
        out_refs["rt"][d, 0] = full["rt"].astype(BF16)
        out_refs["bt"][d, 0] = full["bt"].astype(BF16)
        for j in range(tile // CHUNK):
            edge = (j + 1) * CHUNK - 1 if d == 0 else j * CHUNK
            wc_ref[0, j, d:d + 1, :] = jnp.exp(cum[edge:edge + 1, :])
        work += [(d, slice(j * CHUNK, (j + 1) * CHUNK), slice(p * LANES, (p + 1) * LANES), full)
                 for j in range(tile // CHUNK) for p in range(w // LANES)]
    hsum = {}
    for g0 in range(0, len(work), PREP_GROUP):
        group = work[g0:g0 + PREP_GROUP]
        inst = [dict({name: val[rs, ls] for name, val in full.items()}, rev=d) for d, rs, ls, full in group]
        for (d, rs, ls, _), res in zip(group, _chunk_local(inst, masks)):
            seen = hsum.get((rs.start, ls.start))
            hsum[(rs.start, ls.start)] = (rs, ls, res["hm"] if seen is None else seen[2] + res["hm"])
            for name in res:
                if name in out_refs:
                    out_refs[name][d, 0, rs, ls] = res[name].astype(BF16)
    if latent:
        for rs, ls, val in hsum.values():
            hsum_ref[0, rs, ls] = val
        bonus_ref[0] = _head_sum(r * k_sum * rk_ref[...], ones_bd) * v
        g_ref[0] = proj[:, 4 * w:5 * w].astype(BF16)


def _wkv_prep(rkv, lora, prm, *, latent, tile):
    b, l, w3 = rkv.shape
    w = w3 // 3
    nt = l // tile
    n8 = l // 8
    tok = lambda wd: pl.BlockSpec((1, tile, wd), lambda i, t: (i, t, 0))
    prev = lambda wd: pl.BlockSpec((1, 8, wd), lambda i, t: (i, jnp.maximum(t * (tile // 8) - 1, 0), 0))
    nxt = lambda wd: pl.BlockSpec((1, 8, wd), lambda i, t: (i, jnp.minimum((t + 1) * (tile // 8), n8 - 1), 0))
    wl = lora.shape[2]
    in_specs = [tok(w3), prev(w3), nxt(w3), tok(wl), prev(wl), nxt(wl)]
    consts = [prm["conv_rkv"], prm["conv_lora"], prm["k_k"], prm["k_a"], prm["decay_w0"],
              prm["iclr_a0"], prm["lora_w"], prm["r_k"]]
    in_specs += [_const_spec(c.shape) for c in consts]
    dirtok = pl.BlockSpec((2, 1, tile, w), lambda i, t: (0, i, t, 0))
    out_specs = [dirtok] * 6 + [pl.BlockSpec((1, tile // CHUNK, 2, w), lambda i, t: (i, t, 0, 0))]
    out_shape = [jax.ShapeDtypeStruct((2, b, l, w), BF16)] * 6 + [
        jax.ShapeDtypeStruct((b, l // CHUNK, 2, w), F32)]
    if latent:
        out_specs += [tok(w), tok(w), tok(w)]
        out_shape += [jax.ShapeDtypeStruct((b, l, w), F32), jax.ShapeDtypeStruct((b, l, w), BF16),
                      jax.ShapeDtypeStruct((b, l, w), F32)]
    return pl.pallas_call(
        functools.partial(_prep_kernel, latent=latent, width=w),
        grid=(b, nt),
        in_specs=in_specs,
        out_specs=out_specs,
        out_shape=out_shape,
        compiler_params=_params("parallel", "parallel"),
        name="wkv_prep_latent" if latent else "wkv_prep_context",
    )(rkv, rkv, rkv, lora, lora, lora, *consts)


def _wkv_kernel(*refs, emit_y, n_pairs):
    names = ("pm", "qm", "rt", "bt", "arb", "kv", "wc")
    n = len(names)
    dir_refs = (dict(zip(names, refs[0:n])), dict(zip(names, refs[n:2 * n])))
    z0_ref = refs[2 * n]
    if emit_y:
        y_refs = refs[2 * n + 1:2 * n + 3]
        z_scr = refs[2 * n + 3]
    else:
        zfin_ref = refs[2 * n + 1]
        z_scr = refs[2 * n + 2]
    c = pl.program_id(0)

    @pl.when(c == 0)
    def _():
        z_scr[...] = z0_ref[...]

    left_h = _iota((CHUNK, LANES), 1) < CHUNK
    keep_l = jnp.where(left_h, 1.0, 0.0).astype(BF16)
    keep_r = jnp.where(left_h, 0.0, 1.0).astype(BF16)

    def unfold(xb):
        return _stack(xb * keep_l, xb * keep_r)

    tiles = [(i, d, p, slice(p * LANES, (p + 1) * LANES))
             for i in range(z_scr.shape[0]) for d in range(2) for p in range(n_pairs)]
    state = [z_scr[i, d, p] for i, d, p, _ in tiles]
    n_sub = dir_refs[0]["wc"].shape[1]
    for step in range(n_sub):
        sub = (step, n_sub - 1 - step)
        rows = [slice(sub[d] * CHUNK, (sub[d] + 1) * CHUNK) for d in range(2)]
        ld = lambda name: [dir_refs[d][name][0, i, rows[d], sl] for i, d, _, sl in tiles]
        sbd = [unfold(s.astype(BF16)) for s in state]
        if emit_y:
            ur = [_dot_nt(_stack(pm, rt), s) for pm, rt, s in zip(ld("pm"), ld("rt"), sbd)]
        else:
            ur = [_dot_nt(pm, s) for pm, s in zip(ld("pm"), sbd)]
        u = [x[:CHUNK] + q.astype(F32) for x, q in zip(ur, ld("qm"))]
        ub = [ui.astype(BF16) for ui in u]
        inc = [_dot_tn(ui, bt) for ui, bt in zip(ub, ld("bt"))]
        inc = [jnp.where(left_h, x[:CHUNK], x[CHUNK:]) for x in inc]
        if emit_y:
            yc = [_dot(a, unfold(ui)) for a, ui in zip(ld("arb"), ub)]
            for (i, d, _, sl), x, ys in zip(tiles, ur, yc):
                y_refs[d][i, rows[d], sl] = x[CHUNK:] + ys
        state = [(s + dz + kv.astype(F32)) * dir_refs[d]["wc"][i, sub[d], d:d + 1, sl]
                 for (i, d, _, sl), s, dz, kv in zip(tiles, state, inc, ld("kv"))]
    for (i, d, p, _), s in zip(tiles, state):
        z_scr[i, d, p] = s

    if not emit_y:
        @pl.when(c == pl.num_programs(0) - 1)
        def _():
            zfin_ref[...] = z_scr[...]


def _wkv_scan(prep, z0, *, emit_y):
    wc = prep[6]
    _, b, l, w = prep[0].shape
    n_sub = min(SCAN_CHUNKS, l // CHUNK)
    blk = n_sub * CHUNK
    nc = l // blk
    n_pairs = w // LANES
    fwd = lambda c: c
    rev = lambda c: nc - 1 - c
    in_specs, args = [], []
    for d, cm in enumerate((fwd, rev)):
        for arr in prep[:6]:
            in_specs.append(pl.BlockSpec((1, b, blk, w), lambda c, d=d, cm=cm: (d, 0, cm(c), 0)))
            args.append(arr)
        in_specs.append(pl.BlockSpec((b, n_sub, 2, w), lambda c, cm=cm: (0, cm(c), 0, 0)))
        args.append(wc)
    zshape = (b, 2, n_pairs, CHUNK, LANES)
    zspec = pl.BlockSpec(zshape, lambda c: (0, 0, 0, 0, 0))
    in_specs.append(zspec)
    args.append(z0)
    if emit_y:
        out_specs = [pl.BlockSpec((b, blk, w), lambda c: (0, c, 0)),
                     pl.BlockSpec((b, blk, w), lambda c: (0, nc - 1 - c, 0))]
        out_shape = [jax.ShapeDtypeStruct((b, l, w), F32)] * 2
    else:
        out_specs = zspec
        out_shape = jax.ShapeDtypeStruct(zshape, F32)
    return pl.pallas_call(
        functools.partial(_wkv_kernel, emit_y=emit_y, n_pairs=n_pairs),
        grid=(nc,),
        in_specs=in_specs,
        out_specs=out_specs,
        out_shape=out_shape,
        scratch_shapes=[pltpu.VMEM(zshape, F32)],
        compiler_params=_params("arbitrary"),
        name="wkv_scan_latent" if emit_y else "wkv_scan_context",
    )(*args)


def _merge_kernel(x_ref, mod_ref, ya_ref, yf_ref, yr_ref, yh_ref, bonus_ref, g_ref, gate_ref,
                  lnw_ref, lnb_ref, wba_ref, wbr_ref, wo_ref, n2_ref, wu_ref, wd_ref, nf_ref,
                  o_ref, *, ff_chunk):
    x = x_ref[0]
    d = x.shape[1]
    mod = lambda j: mod_ref[0, j:j + 1, :]
    y = yf_ref[0] + yr_ref[0] + yh_ref[0]
    gw = min(MXU_DIM, y.shape[1])
    ones_bd = _head_ones(gw)

    def head_mean(t):
        tb = t.astype(BF16)
        parts = [_dot(tb[:, j:j + gw], ones_bd) for j in range(0, t.shape[1], gw)]
        return jnp.concatenate(parts, axis=1) * (1.0 / HEAD_DIM)

    mu = head_mean(y)
    yc = y - mu
    var = head_mean(yc * yc)
    yn = yc * lax.rsqrt(var + LNX_EPS)
    yr = (yn * lnw_ref[...] + lnb_ref[...] + bonus_ref[0]) * g_ref[0].astype(F32)
    gate = gate_ref[0].astype(F32)
    merged = gate[:, :d] * _dot(ya_ref[0], wba_ref[...]) + gate[:, d:] * _dot(yr.astype(BF16), wbr_ref[...])
    x1 = x + mod(2) * _dot(merged.astype(BF16), wo_ref[...])
    h2 = (_rmsnorm(x1, n2_ref[...]) * (1.0 + mod(4)) + mod(3)).astype(BF16)
    acc = jnp.zeros_like(x1)
    for j in range(wu_ref.shape[1] // ff_chunk):
        cs = slice(j * ff_chunk, (j + 1) * ff_chunk)
        up = jnp.maximum(_dot(h2, wu_ref[:, cs]), 0.0)
        acc = acc + _dot((up * up).astype(BF16), wd_ref[cs, :])
    x2 = x1 + mod(5) * acc
    o_ref[0] = _rmsnorm(x2, nf_ref[...])


def _merge_mlp(x, mod, ya, yf, yr, yh, bonus, g, gate, prm, *, tile):
    b, l, d = x.shape
    tok = lambda arr: pl.BlockSpec((1, tile, arr.shape[2]), lambda i, t: (i, t, 0))
    consts = [prm["lnx_w"], prm["lnx_b"], prm["w_branch_attn"], prm["w_branch_rwkv"], prm["w_out"],
              prm["norm2_g"], prm["w_mlp_up"], prm["w_mlp_down"], prm["norm_f_g"]]
    toks = [ya, yf, yr, yh, bonus, g, gate]
    return pl.pallas_call(
        functools.partial(_merge_kernel, ff_chunk=min(1024, prm["w_mlp_up"].shape[1])),
        grid=(b, l // tile),
        in_specs=[tok(x), pl.BlockSpec((1,) + mod.shape[1:], lambda i, t: (i, 0, 0))]
        + [tok(a) for a in toks] + [_const_spec(c.shape) for c in consts],
        out_specs=tok(x),
        out_shape=jax.ShapeDtypeStruct(x.shape, x.dtype),
        compiler_params=_params("parallel", "parallel"),
        name="merge_mlp",
    )(x, mod, *toks, *consts)


def _rope_tables(l):
    n_freq = HEAD_DIM // 4
    inv_freq = jnp.power(ROPE_BASE, -jnp.arange(n_freq, dtype=F32) / n_freq)
    rows = l // GRID_W
    row = jnp.repeat(jnp.arange(rows, dtype=F32), GRID_W)
    col = jnp.tile(jnp.arange(GRID_W, dtype=F32), rows)
    ang = jnp.concatenate([row[:, None] * inv_freq, col[:, None] * inv_freq], axis=-1)
    cos, sin = jnp.cos(ang), jnp.sin(ang)
    reps = LANES // HEAD_DIM
    return (jnp.tile(jnp.concatenate([cos, cos], axis=1), (1, reps)),
            jnp.tile(jnp.concatenate([-sin, sin], axis=1), (1, reps)))


def _pad_cols(w, width):
    return jnp.pad(w, ((0, 0), (0, width - w.shape[1])))


def kernel(x, c, ctx, c_ctx, w_ada, b_ada, norm1_g, w_in, sink, conv_w, decay_w0, decay_w2, iclr_a0, iclr_a2, gate_g2, k_k, k_a, r_k, lnx_w, lnx_b, w_branch_attn, w_branch_rwkv, w_out, norm2_g, w_mlp_up, w_mlp_down, norm_f_g):
    assert w_in.shape[0] == 1, "single-layer block: context tokens are read, never updated"
    b, l, d = x.shape
    attn_w = w_branch_attn.shape[1]
    rw = w_branch_rwkv.shape[1]
    n_q = attn_w // HEAD_DIM
    n_kv = n_q // Q_PER_KV
    kv_w = n_kv * HEAD_DIM
    assert kv_w == LANES and rw % LANES == 0 and l % 256 == 0 and ctx.shape[1] % 256 == 0

    w = w_in[0]
    o_k, o_r = attn_w, attn_w + 2 * kv_w
    o_l = o_r + 3 * rw
    o_g = o_l + DECAY_LORA + ICLR_LORA + GATE_LORA
    w_ctx = _pad_cols(w[:, o_k:o_g], o_l - o_k + LORA_PAD).astype(BF16)
    w_lat = [w[:, :o_k].astype(BF16), w_ctx, w[:, o_g:].astype(BF16)]
    widths_ctx = {"kv": 2 * kv_w, "k": PAIR * kv_w, "v": PAIR * kv_w, "rkv": 3 * rw, "lora": LORA_PAD}
    widths_lat = {"q": attn_w, **widths_ctx, "gate": 2 * d}

    cw = conv_w[0]
    lora_w = jnp.zeros((LORA_PAD, 5 * rw), F32)
    lora_w = lora_w.at[:DECAY_LORA, :rw].set(decay_w2[0, 0]).at[:DECAY_LORA, rw:2 * rw].set(decay_w2[0, 1])
    r1 = DECAY_LORA + ICLR_LORA
    lora_w = lora_w.at[DECAY_LORA:r1, 2 * rw:3 * rw].set(iclr_a2[0, 0]).at[DECAY_LORA:r1, 3 * rw:4 * rw].set(iclr_a2[0, 1])
    lora_w = lora_w.at[r1:r1 + GATE_LORA, 4 * rw:].set(gate_g2[0])
    prm = {
        "conv_rkv": cw[:, :3 * rw], "conv_lora": _pad_cols(cw[:, 3 * rw:], LORA_PAD),
        "k_k": k_k[0].reshape(1, rw), "k_a": k_a[0].reshape(1, rw),
        "decay_w0": decay_w0[0], "iclr_a0": iclr_a0[0], "lora_w": lora_w.astype(BF16),
        "r_k": r_k[0].reshape(1, rw),
        "lnx_w": lnx_w[0].reshape(1, rw), "lnx_b": lnx_b[0].reshape(1, rw),
        "w_branch_attn": w_branch_attn[0].astype(BF16), "w_branch_rwkv": w_branch_rwkv[0].astype(BF16),
        "w_out": w_out[0].astype(BF16), "norm2_g": norm2_g[0].reshape(1, d),
        "w_mlp_up": w_mlp_up[0].astype(BF16), "w_mlp_down": w_mlp_down[0].astype(BF16),
        "norm_f_g": norm_f_g.reshape(1, d),
    }

    rows = -(-(b + 1) // 8) * 8
    cc = jnp.zeros((rows, d), F32).at[:b].set(c).at[b].set(c_ctx)
    mod = _ada_mod(cc, w_ada[0], b_ada[0]).reshape(rows, -1, d)

    q, kd, vd, rkv, lora, gate = _in_proj(x, mod, b, norm1_g[0], w_lat, widths_lat, _rope_tables(l),
                                          latent=True, tile=512)
    kxd, vxd, rkv_c, lora_c = _in_proj(ctx, mod, b, norm1_g[0], [w_ctx], widths_ctx, None,
                                       latent=False, tile=256)
    ya = _attention(sink[0], q, kd, vd, kxd, vxd)

    prep_c = _wkv_prep(rkv_c, lora_c, prm, latent=False, tile=256)
    z_ctx = _wkv_scan(prep_c, jnp.zeros((b, 2, rw // LANES, CHUNK, LANES), F32), emit_y=False)
    prep = _wkv_prep(rkv, lora, prm, latent=True, tile=512)
    yf, yr = _wkv_scan(prep, z_ctx, emit_y=True)
    bonus, g, yh = prep[7], prep[8], prep[9]

    return _merge_mlp(x, mod, ya, yf, yr, yh, bonus, g, gate, prm, tile=512)
```

```python
import functools
import math

import jax
import jax.numpy as jnp
from jax import lax
from jax.experimental import pallas as pl
from jax.experimental.pallas import tpu as pltpu

F32 = jnp.float32
BF16 = jnp.bfloat16

GRID_W = 64
HEAD_DIM = 64
Q_PER_KV = 4
ATTN_BLOCK = 128
ATTN_QB = 4
ROPE_BASE = 10000.0
NORM_EPS = 1e-6
LNX_EPS = 1e-5 * HEAD_DIM
DECAY_LORA, ICLR_LORA, GATE_LORA = 32, 32, 96
LORA_PAD = 256
CHUNK = 64
SCAN_CHUNKS = 4
PREP_GROUP = 16
LANES = 128
MXU_DIM = 256
PAIR = LANES // HEAD_DIM
NEG = -1e30
LOG2_E = math.log2(math.e)
VMEM_LIMIT = 56 * 1024 * 1024


def _dot(a, b):
    return jnp.dot(a, b, preferred_element_type=F32)


def _dot_nt(a, b):
    return lax.dot_general(a, b, (((1,), (1,)), ((), ())), preferred_element_type=F32)


def _dot_tn(a, b):
    return lax.dot_general(a, b, (((0,), (0,)), ((), ())), preferred_element_type=F32)


def _iota(shape, dim):
    return lax.broadcasted_iota(jnp.int32, shape, dim)


def _head_ones(width):
    r = _iota((width, width), 0) // HEAD_DIM
    c = _iota((width, width), 1) // HEAD_DIM
    return jnp.where(r == c, 1.0, 0.0).astype(BF16)


def _head_sum(x, ones_bd):
    gw = ones_bd.shape[0]
    hi = x.astype(BF16)
    lo = (x - hi.astype(F32)).astype(BF16)
    parts = [_dot(hi[:, j:j + gw], ones_bd) + _dot(lo[:, j:j + gw], ones_bd) for j in range(0, x.shape[1], gw)]
    return jnp.concatenate(parts, axis=1)


def _rmsnorm(x, g):
    ms = jnp.mean(x * x, axis=-1, keepdims=True)
    return x * lax.rsqrt(ms + NORM_EPS) * g


def _params(*sem):
    return pltpu.CompilerParams(dimension_semantics=sem, vmem_limit_bytes=VMEM_LIMIT)


def _const_spec(shape):
    nd = len(shape)
    return pl.BlockSpec(shape, lambda *_: (0,) * nd, pipeline_mode=pl.Buffered(1))


def _ada_kernel(c_ref, w_ref, b_ref, o_ref):
    c = c_ref[...]
    s = c * jax.nn.sigmoid(c)
    o_ref[...] = _dot(s.astype(BF16), w_ref[...].astype(BF16)) + b_ref[...]


def _ada_mod(cc, w_ada, b_ada):
    rows, d = cc.shape
    n = w_ada.shape[1]
    return pl.pallas_call(
        _ada_kernel,
        grid=(n // d,),
        in_specs=[pl.BlockSpec((rows, d), lambda j: (0, 0)),
                  pl.BlockSpec((d, d), lambda j: (0, j)),
                  pl.BlockSpec((1, d), lambda j: (0, j))],
        out_specs=pl.BlockSpec((rows, d), lambda j: (0, j)),
        out_shape=jax.ShapeDtypeStruct((rows, n), F32),
        compiler_params=_params("arbitrary"),
        name="ada_mod",
    )(cc, w_ada, b_ada.reshape(1, n))


def _rope(x, cos_t, sin_t):
    w = x.shape[1]
    half = HEAD_DIM // 2
    first = (_iota(x.shape, 1) % HEAD_DIM) < half
    swapped = jnp.where(first, pltpu.roll(x, w - half, 1), pltpu.roll(x, half, 1))
    reps = w // LANES
    c = jnp.concatenate([cos_t] * reps, axis=1)
    s = jnp.concatenate([sin_t] * reps, axis=1)
    return x * c + swapped * s


def _inproj_kernel(*refs, latent, widths):
    if latent:
        (x_ref, mod_ref, g_ref, w_ref, wg_ref, cos_ref, sin_ref,
         q_ref, k_ref, v_ref, rkv_ref, lora_ref, gate_ref) = refs
    else:
        x_ref, mod_ref, g_ref, w_ref, k_ref, v_ref, rkv_ref, lora_ref = refs
    x = x_ref[0]
    h = _rmsnorm(x, g_ref[...]) * (1.0 + mod_ref[0, 1:2, :]) + mod_ref[0, 0:1, :]
    hb = h.astype(BF16)
    off = widths["q"]

    def seg(name):
        nonlocal off
        lo = off
        off += widths[name]
        return _dot(hb, w_ref[:, lo:off].astype(BF16))

    def dup_heads(t):
        first = _iota(t.shape, 1) < HEAD_DIM
        other = pltpu.roll(t, HEAD_DIM, 1)
        return jnp.concatenate([jnp.where(first, t, other), jnp.where(first, other, t)], axis=1)

    kv = seg("kv")
    k, v = dup_heads(kv[:, :LANES]), dup_heads(kv[:, LANES:])
    if latent:
        cos_t, sin_t = cos_ref[...], sin_ref[...]
        q = _dot(hb, w_ref[:, :widths["q"]].astype(BF16))
        q_ref[0] = (_rope(q, cos_t, sin_t) * (LOG2_E * HEAD_DIM ** -0.5)).astype(BF16)
        k = _rope(k, cos_t, sin_t)
    k_ref[0] = k.astype(BF16)
    v_ref[0] = v.astype(BF16)
    rkv_ref[0] = seg("rkv")
    lora_ref[0] = seg("lora")
    if latent:
        gate_ref[0] = jax.nn.sigmoid(_dot(hb, wg_ref[...])).astype(BF16)


def _in_proj(x, mod, mod_row, norm_g, weights, widths, tables, *, latent, tile):
    b, l, d = x.shape
    nt = l // tile
    if latent:
        mod_map = lambda i, t: (i, 0, 0)
    else:
        mod_map = lambda i, t: (mod_row, 0, 0)
    tok = lambda w: pl.BlockSpec((1, tile, w), lambda i, t: (i, t, 0))
    in_specs = [tok(d),
                pl.BlockSpec((1,) + mod.shape[1:], mod_map),
                _const_spec((1, d))] + [_const_spec(w.shape) for w in weights]
    args = [x, mod, norm_g.reshape(1, d), *weights]
    out_specs, out_shape = [], []

    def out(w, dt):
        out_specs.append(tok(w))
        out_shape.append(jax.ShapeDtypeStruct((b, l, w), dt))

    if latent:
        in_specs += [pl.BlockSpec((tile, LANES), lambda i, t: (t, 0))] * 2
        args += list(tables)
        out(widths["q"], BF16)
    out(widths["k"], BF16)
    out(widths["v"], BF16)
    out(widths["rkv"], F32)
    out(widths["lora"], F32)
    if latent:
        out(widths["gate"], BF16)
    return pl.pallas_call(
        functools.partial(_inproj_kernel, latent=latent, widths=widths),
        grid=(b, nt),
        in_specs=in_specs,
        out_specs=out_specs,
        out_shape=out_shape,
        compiler_params=_params("parallel", "parallel"),
        name="in_proj_latent" if latent else "in_proj_context",
    )(*args)


def _attn_kernel(sink_ref, q_ref, kp_ref, kc_ref, kn_ref, vp_ref, vc_ref, vn_ref,
                 kx_ref, vx_ref, o_ref, *, n_kv):
    i = pl.program_id(1)
    last = pl.num_programs(1) - 1
    blk = ATTN_BLOCK
    qi = _iota((blk, blk), 0)
    kj = _iota((blk, blk), 1)
    left = _iota((blk, LANES), 1) < HEAD_DIM

    def key_block(refs, j, gs):
        ref_p, ref_c, ref_n = refs[:3]
        if j < 0:
            return ref_p[0, :, gs]
        if j >= ATTN_QB:
            return ref_n[0, :, gs]
        return ref_c[0, j * blk:(j + 1) * blk, gs]

    items = []
    for qb in range(ATTN_QB):
        lo_ok = kj >= qi
        hi_ok = kj <= qi
        if qb == 0:
            lo_ok = lo_ok & (i > 0)
        if qb == ATTN_QB - 1:
            hi_ok = hi_ok & (i < last)
        bias_lo = jnp.concatenate([jnp.where(lo_ok, 0.0, NEG)] * Q_PER_KV, axis=0)
        bias_hi = jnp.concatenate([jnp.where(hi_ok, 0.0, NEG)] * Q_PER_KV, axis=0)
        rows = slice(qb * blk, (qb + 1) * blk)
        q = q_ref[0, rows, :].astype(F32)
        for g in range(n_kv):
            gs = slice(g * LANES, (g + 1) * LANES)
            heads = range(g * Q_PER_KV, (g + 1) * Q_PER_KV)
            qs, sinks = [], []
            for hd in heads:
                qp = q[:, (hd // PAIR) * LANES:(hd // PAIR + 1) * LANES]
                keep = left if hd % PAIR == 0 else jnp.logical_not(left)
                qs.append(jnp.where(keep, qp, 0.0).astype(BF16))
                sinks.append(jnp.full((blk, 1), sink_ref[hd] * LOG2_E, F32))
            items.append({
                "qb": qb, "gs": gs, "rows": rows, "heads": heads, "bias": (bias_lo, bias_hi),
                "qs": jnp.concatenate(qs, axis=0), "sink": jnp.concatenate(sinks, axis=0)})

    def cat(refs, it):
        return jnp.concatenate([key_block(refs, it["qb"] + j, it["gs"]) for j in (-1, 0, 1)]
                               + [refs[3][0, :, it["gs"]]], axis=0)

    s = [_dot_nt(it["qs"], cat((kp_ref, kc_ref, kn_ref, kx_ref), it)) for it in items]
    s = [jnp.concatenate([x[:, :blk] + it["bias"][0], x[:, blk:2 * blk], x[:, 2 * blk:3 * blk] + it["bias"][1],
                          x[:, 3 * blk:]], axis=1) for x, it in zip(s, items)]
    m = [jnp.maximum(jnp.max(x, axis=1, keepdims=True), it["sink"]) for x, it in zip(s, items)]
    p = [jnp.exp2(x - mx) for x, mx in zip(s, m)]
    den = [jnp.sum(x, axis=1, keepdims=True) + jnp.exp2(it["sink"] - mx) for x, mx, it in zip(p, m, items)]
    o = [_dot(x.astype(BF16), cat((vp_ref, vc_ref, vn_ref, vx_ref), it)) / dn for x, dn, it in zip(p, den, items)]
    for x, it in zip(o, items):
        for hd in it["heads"][::PAIR]:
            j = hd - it["heads"][0]
            pair = jnp.where(left, x[j * blk:(j + 1) * blk], x[(j + 1) * blk:(j + 2) * blk])
            col = (hd // PAIR) * LANES
            o_ref[0, it["rows"], col:col + LANES] = pair.astype(BF16)


def _attention(sink, q, kd, vd, kxd, vxd):
    b, l, wq = q.shape
    wk = kd.shape[2]
    lc = kxd.shape[1]
    nb = l // ATTN_BLOCK
    span = ATTN_QB * ATTN_BLOCK
    blk = lambda w, f: pl.BlockSpec((1, ATTN_BLOCK, w), f)
    prev = lambda bi, i: (bi, jnp.maximum(i * ATTN_QB - 1, 0), 0)
    nxt = lambda bi, i: (bi, jnp.minimum((i + 1) * ATTN_QB, nb - 1), 0)
    cur = lambda w: pl.BlockSpec((1, span, w), lambda bi, i: (bi, i, 0))
    ctx = pl.BlockSpec((1, lc, wk), lambda bi, i: (bi, 0, 0))
    return pl.pallas_call(
        functools.partial(_attn_kernel, n_kv=wk // LANES),
        grid=(b, l // span),
        in_specs=[pl.BlockSpec(memory_space=pltpu.SMEM),
                  cur(wq),
                  blk(wk, prev), cur(wk), blk(wk, nxt),
                  blk(wk, prev), cur(wk), blk(wk, nxt),
                  ctx, ctx],
        out_specs=cur(wq),
        out_shape=jax.ShapeDtypeStruct((b, l, wq), BF16),
        compiler_params=_params("parallel", "parallel"),
        name="attention",
    )(sink, q, kd, kd, kd, vd, vd, vd, kxd, vxd)


def _conv3(x, prev_row, next_row, w):
    n = x.shape[0]
    row = _iota((8, x.shape[1]), 0)
    xm = pltpu.roll(x, 1, 0)
    xm = jnp.concatenate([jnp.where(row == 0, prev_row, xm[:8]), xm[8:]], axis=0)
    xp = pltpu.roll(x, n - 1, 0)
    xp = jnp.concatenate([xp[:n - 8], jnp.where(row == 7, next_row, xp[n - 8:])], axis=0)
    return xm * w[0:1] + x * w[1:2] + xp * w[2:3]


def _pair_masks():
    n = 2 * CHUNK
    row = _iota((n, n), 0)
    lane = _iota((n, n), 1)
    top, left = row < CHUNK, lane < CHUNK
    return {"row": row % CHUNK, "lane": lane % CHUNK, "top": top, "left": left, "bd": top == left,
            "left_h": _iota((CHUNK, n), 1) < CHUNK}


def _stack(a, b):
    return jnp.concatenate([a, b], axis=0)


def _fold(x, m):
    return jnp.where(m["left_h"], x[:CHUNK], x[CHUNK:])


def _chunk_local(inst, m):
    bf = lambda x: x.astype(BF16)
    diag = jnp.logical_not(m["top"]) & (m["lane"] == m["row"])
    masks = ((m["lane"] < m["row"]) | diag, (m["lane"] > m["row"]) | diag)
    mask_a = [masks[i["rev"]] for i in inst]
    lh = m["left_h"]
    half = CHUNK // 2
    keep_l = jnp.where(m["left"], 1.0, 0.0).astype(BF16)
    keep_r = jnp.where(m["left"], 0.0, 1.0).astype(BF16)
    keep_lh = jnp.where(lh, 1.0, 0.0).astype(BF16)
    keep_rh = jnp.where(lh, 0.0, 1.0).astype(BF16)

    def unfold(xb, anti=False):
        a, b = xb * keep_lh, xb * keep_rh
        return _stack(b, a) if anti else _stack(a, b)

    lhs = [bf(_stack(i["at"], i["rt"])) for i in inst]
    a01 = [_dot_nt(l, _stack(bf(_stack(i["bt"], i["kt"])) * keep_l, bf(_stack(i["kt"], i["bt"])) * keep_r))
           for l, i in zip(lhs, inst)]
    a0 = [jnp.where(ma, a[:, :LANES], 0.0) for ma, a in zip(mask_a, a01)]
    a1 = [jnp.where(ma, a[:, LANES:], 0.0) for ma, a in zip(mask_a, a01)]
    nc = [jnp.where(lh, x[:CHUNK], y[:CHUNK]) for x, y in zip(a0, a1)]
    arb = [jnp.where(lh, x[CHUNK:], y[CHUNK:]) for x, y in zip(a0, a1)]
    ak_ark_sw = [bf(jnp.where(m["left"], y, x)) for x, y in zip(a0, a1)]
    vh = [_dot(a, unfold(bf(i["v"]), anti=True)) for a, i in zip(ak_ark_sw, inst)]
    eye = jnp.where(_iota((CHUNK, LANES), 1) % CHUNK == _iota((CHUNK, LANES), 0), 1.0, 0.0)
    tc = [eye + n for n in nc]
    ncb = [bf(n) for n in nc]
    nc = [_dot(n, unfold(n)) for n in ncb]
    steps = CHUNK.bit_length() - 1
    for _ in range(steps - 2):
        ncb = [bf(n) for n in nc]
        both = [_dot(n, jnp.concatenate([unfold(bf(t)), unfold(n)], axis=1)) for n, t in zip(ncb, tc)]
        tc = [t + r[:, :LANES] for t, r in zip(tc, both)]
        nc = [r[:, LANES:] for r in both]
    inc = [_dot(bf(n[half:]), unfold(bf(t))) for n, t in zip(nc, tc)]
    tc = [jnp.concatenate([t[:half], t[half:] + d], axis=0) for t, d in zip(tc, inc)]
    pq = [_dot(bf(t), jnp.concatenate([unfold(l[:CHUNK]), unfold(bf(x[:CHUNK]))], axis=1))
          for t, l, x in zip(tc, lhs, vh)]
    kvf = [_dot_tn(bf(i["v"]), bf(i["kt"])) for i in inst]
    return [{"pm": r[:, :LANES], "qm": r[:, LANES:], "arb": b, "hm": x[CHUNK:], "kv": _fold(k, m)}
            for r, b, x, k in zip(pq, arb, vh, kvf)]


def _prep_kernel(*refs, latent, width):
    (rkv_ref, rkv_p, rkv_n, lora_ref, lora_p, lora_n, cw_ref, cwl_ref, kk_ref, ka_ref,
     w0_ref, a0_ref, wl_ref, rk_ref) = refs[:14]
    outs = refs[14:]
    out_refs = dict(zip(("pm", "qm", "rt", "bt", "arb", "kv"), outs[:6]))
    wc_ref = outs[6]
    if latent:
        bonus_ref, g_ref, hsum_ref = outs[7:]
    t = pl.program_id(1)
    nt = pl.num_programs(1)
    tile = rkv_ref.shape[1]
    w = width
    has_prev = (t > 0).astype(F32)
    has_next = (t < nt - 1).astype(F32)
    u = _conv3(rkv_ref[0], rkv_p[0, 7:8, :] * has_prev, rkv_n[0, 0:1, :] * has_next, cw_ref[...])
    ul = _conv3(lora_ref[0], lora_p[0, 7:8, :] * has_prev, lora_n[0, 0:1, :] * has_next, cwl_ref[...])
    r, k, v = u[:, :w], u[:, w:2 * w], u[:, 2 * w:]
    ones_bd = _head_ones(min(MXU_DIM, w))

    kk = k * kk_ref[...]
    kk = kk * lax.rsqrt(jnp.maximum(_head_sum(kk * kk, ones_bd), 1e-24))

    lane = _iota(ul.shape, 1)
    lin = jnp.where(lane < DECAY_LORA, jnp.tanh(ul),
                    jnp.where(lane < DECAY_LORA + ICLR_LORA, ul, jax.nn.sigmoid(ul)))
    proj = _dot(lin.astype(BF16), wl_ref[...])

    tr = _iota((tile, tile), 0)
    tc = _iota((tile, tile), 1)
    same = (tr // CHUNK) == (tc // CHUNK)
    tri = (jnp.where(same & (tc <= tr), 1.0, 0.0).astype(BF16),
           jnp.where(same & (tc >= tr), 1.0, 0.0).astype(BF16))

    def exact_dot(m, x):
        h1 = x.astype(BF16)
        h2 = (x - h1.astype(F32)).astype(BF16)
        return _dot(m, h1) + _dot(m, h2)

    masks = _pair_masks()
    k_sum = None
    work = []
    for d in range(2):
        z = w0_ref[d:d + 1, :] + proj[:, d * w:(d + 1) * w]
        lw = -math.exp(-0.5) * jax.nn.sigmoid(z)
        a = jax.nn.sigmoid(a0_ref[d:d + 1, :] + proj[:, (2 + d) * w:(3 + d) * w])
        kd = k * (1.0 + (a - 1.0) * ka_ref[...])
        k_sum = kd if k_sum is None else k_sum + kd
        cum = exact_dot(tri[d], lw)
        e_neg = jnp.exp(-cum)
        full = {"at": -kk * jnp.exp(cum - lw), "rt": r * jnp.exp(cum), "bt": kk * a * e_neg,
                "kt": kd * e_neg, "v": v}
        out_refs["rt"][d, 0] = full["rt"].astype(BF16)
        out_refs["bt"][d, 0] = full["bt"].astype(BF16)
        for j in range(tile // CHUNK):
            edge = (j + 1) * CHUNK - 1 if d == 0 else j * CHUNK
            wc_ref[0, j, d:d + 1, :] = jnp.exp(cum[edge:edge + 1, :])
        work += [(d, slice(j * CHUNK, (j + 1) * CHUNK), slice(p * LANES, (p + 1) * LANES), full)
                 for j in range(tile // CHUNK) for p in range(w // LANES)]
    hsum = {}
    for g0 in range(0, len(work), PREP_GROUP):
        group = work[g0:g0 + PREP_GROUP]
        inst = [dict({name: val[rs, ls] for name, val in full.items()}, rev=d) for d, rs, ls, full in group]
        for (d, rs, ls, _), res in zip(group, _chunk_local(inst, masks)):
            seen = hsum.get((rs.start, ls.start))
            hsum[(rs.start, ls.start)] = (rs, ls, res["hm"] if seen is None else seen[2] + res["hm"])
            for name in res:
                if name in out_refs:
                    out_refs[name][d, 0, rs, ls] = res[name].astype(BF16)
    if latent:
        for rs, ls, val in hsum.values():
            hsum_ref[0, rs, ls] = val
        bonus_ref[0] = _head_sum(r * k_sum * rk_ref[...], ones_bd) * v
        g_ref[0] = proj[:, 4 * w:5 * w].astype(BF16)


def _wkv_prep(rkv, lora, prm, *, latent, tile):
    b, l, w3 = rkv.shape
    w = w3 // 3
    nt = l // tile
    n8 = l // 8
    tok = lambda wd: pl.BlockSpec((1, tile, wd), lambda i, t: (i, t, 0))
    prev = lambda wd: pl.BlockSpec((1, 8, wd), lambda i, t: (i, jnp.maximum(t * (tile // 8) - 1, 0), 0))
    nxt = lambda wd: pl.BlockSpec((1, 8, wd), lambda i, t: (i, jnp.minimum((t + 1) * (tile // 8), n8 - 1), 0))
    wl = lora.shape[2]
    in_specs = [tok(w3), prev(w3), nxt(w3), tok(wl), prev(wl), nxt(wl)]
    consts = [prm["conv_rkv"], prm["conv_lora"], prm["k_k"], prm["k_a"], prm["decay_w0"],
              prm["iclr_a0"], prm["lora_w"], prm["r_k"]]
    in_specs += [_const_spec(c.shape) for c in consts]
    dirtok = pl.BlockSpec((2, 1, tile, w), lambda i, t: (0, i, t, 0))
    out_specs = [dirtok] * 6 + [pl.BlockSpec((1, tile // CHUNK, 2, w), lambda i, t: (i, t, 0, 0))]
    out_shape = [jax.ShapeDtypeStruct((2, b, l, w), BF16)] * 6 + [
        jax.ShapeDtypeStruct((b, l // CHUNK, 2, w), F32)]
    if latent:
        out_specs += [tok(w), tok(w), tok(w)]
        out_shape += [jax.ShapeDtypeStruct((b, l, w), F32), jax.ShapeDtypeStruct((b, l, w), BF16),
                      jax.ShapeDtypeStruct((b, l, w), F32)]
    return pl.pallas_call(
        functools.partial(_prep_kernel, latent=latent, width=w),
        grid=(b, nt),
        in_specs=in_specs,
        out_specs=out_specs,
        out_shape=out_shape,
        compiler_params=_params("parallel", "parallel"),
        name="wkv_prep_latent" if latent else "wkv_prep_context",
    )(rkv, rkv, rkv, lora, lora, lora, *consts)


def _wkv_kernel(*refs, emit_y, n_pairs):
    names = ("pm", "qm", "rt", "bt", "arb", "kv", "wc")
    n = len(names)
    dir_refs = (dict(zip(names, refs[0:n])), dict(zip(names, refs[n:2 * n])))
    z0_ref = refs[2 * n]
    if emit_y:
        y_refs = refs[2 * n + 1:2 * n + 3]
        z_scr = refs[2 * n + 3]
    else:
        zfin_ref = refs[2 * n + 1]
        z_scr = refs[2 * n + 2]
    c = pl.program_id(0)

    @pl.when(c == 0)
    def _():
        z_scr[...] = z0_ref[...]

    left_h = _iota((CHUNK, LANES), 1) < CHUNK
    keep_l = jnp.where(left_h, 1.0, 0.0).astype(BF16)
    keep_r = jnp.where(left_h, 0.0, 1.0).astype(BF16)

    def unfold(xb):
        return _stack(xb * keep_l, xb * keep_r)

    tiles = [(i, d, p, slice(p * LANES, (p + 1) * LANES))
             for i in range(z_scr.shape[0]) for d in range(2) for p in range(n_pairs)]
    state = [z_scr[i, d, p] for i, d, p, _ in tiles]
    n_sub = dir_refs[0]["wc"].shape[1]
    for step in range(n_sub):
        sub = (step, n_sub - 1 - step)
        rows = [slice(sub[d] * CHUNK, (sub[d] + 1) * CHUNK) for d in range(2)]
        ld = lambda name: [dir_refs[d][name][0, i, rows[d], sl] for i, d, _, sl in tiles]
        sbd = [unfold(s.astype(BF16)) for s in state]
        if emit_y:
            ur = [_dot_nt(_stack(pm, rt), s) for pm, rt, s in zip(ld("pm"), ld("rt"), sbd)]
        else:
            ur = [_dot_nt(pm, s) for pm, s in zip(ld("pm"), sbd)]
        u = [x[:CHUNK] + q.astype(F32) for x, q in zip(ur, ld("qm"))]
        ub = [ui.astype(BF16) for ui in u]
        inc = [_dot_tn(ui, bt) for ui, bt in zip(ub, ld("bt"))]
        inc = [jnp.where(left_h, x[:CHUNK], x[CHUNK:]) for x in inc]
        if emit_y:
            yc = [_dot(a, unfold(ui)) for a, ui in zip(ld("arb"), ub)]
            for (i, d, _, sl), x, ys in zip(tiles, ur, yc):
                y_refs[d][i, rows[d], sl] = x[CHUNK:] + ys
        state = [(s + dz + kv.astype(F32)) * dir_refs[d]["wc"][i, sub[d], d:d + 1, sl]
                 for (i, d, _, sl), s, dz, kv in zip(tiles, state, inc, ld("kv"))]
    for (i, d, p, _), s in zip(tiles, state):
        z_scr[i, d, p] = s

    if not emit_y:
        @pl.when(c == pl.num_programs(0) - 1)
        def _():
            zfin_ref[...] = z_scr[...]


def _wkv_scan(prep, z0, *, emit_y):
    wc = prep[6]
    _, b, l, w = prep[0].shape
    n_sub = min(SCAN_CHUNKS, l // CHUNK)
    blk = n_sub * CHUNK
    nc = l // blk
    n_pairs = w // LANES
    fwd = lambda c: c
    rev = lambda c: nc - 1 - c
    in_specs, args = [], []
    for d, cm in enumerate((fwd, rev)):
        for arr in prep[:6]:
            in_specs.append(pl.BlockSpec((1, b, blk, w), lambda c, d=d, cm=cm: (d, 0, cm(c), 0)))
            args.append(arr)
        in_specs.append(pl.BlockSpec((b, n_sub, 2, w), lambda c, cm=cm: (0, cm(c), 0, 0)))
        args.append(wc)
    zshape = (b, 2, n_pairs, CHUNK, LANES)
    zspec = pl.BlockSpec(zshape, lambda c: (0, 0, 0, 0, 0))
    in_specs.append(zspec)
    args.append(z0)
    if emit_y:
        out_specs = [pl.BlockSpec((b, blk, w), lambda c: (0, c, 0)),
                     pl.BlockSpec((b, blk, w), lambda c: (0, nc - 1 - c, 0))]
        out_shape = [jax.ShapeDtypeStruct((b, l, w), F32)] * 2
    else:
        out_specs = zspec
        out_shape = jax.ShapeDtypeStruct(zshape, F32)
    return pl.pallas_call(
        functools.partial(_wkv_kernel, emit_y=emit_y, n_pairs=n_pairs),
        grid=(nc,),
        in_specs=in_specs,
        out_specs=out_specs,
        out_shape=out_shape,
        scratch_shapes=[pltpu.VMEM(zshape, F32)],
        compiler_params=_params("arbitrary"),
        name="wkv_scan_latent" if emit_y else "wkv_scan_context",
    )(*args)


def _merge_kernel(x_ref, mod_ref, ya_ref, yf_ref, yr_ref, yh_ref, bonus_ref, g_ref, gate_ref,
                  lnw_ref, lnb_ref, wba_ref, wbr_ref, wo_ref, n2_ref, wu_ref, wd_ref, nf_ref,
                  o_ref, *, ff_chunk):
    x = x_ref[0]
    d = x.shape[1]
    mod = lambda j: mod_ref[0, j:j + 1, :]
    y = yf_ref[0] + yr_ref[0] + yh_ref[0]
    gw = min(MXU_DIM, y.shape[1])
    ones_bd = _head_ones(gw)

    def head_mean(t):
        tb = t.astype(BF16)
        parts = [_dot(tb[:, j:j + gw], ones_bd) for j in range(0, t.shape[1], gw)]
        return jnp.concatenate(parts, axis=1) * (1.0 / HEAD_DIM)

    mu = head_mean(y)
    yc = y - mu
    var = head_mean(yc * yc)
    yn = yc * lax.rsqrt(var + LNX_EPS)
    yr = (yn * lnw_ref[...] + lnb_ref[...] + bonus_ref[0]) * g_ref[0].astype(F32)
    gate = gate_ref[0].astype(F32)
    merged = (gate[:, :d] * _dot(ya_ref[0], wba_ref[...].astype(BF16))
              + gate[:, d:] * _dot(yr.astype(BF16), wbr_ref[...].astype(BF16)))
    x1 = x + mod(2) * _dot(merged.astype(BF16), wo_ref[...].astype(BF16))
    h2 = (_rmsnorm(x1, n2_ref[...]) * (1.0 + mod(4)) + mod(3)).astype(BF16)
    acc = jnp.zeros_like(x1)
    for j in range(wu_ref.shape[1] // ff_chunk):
        cs = slice(j * ff_chunk, (j + 1) * ff_chunk)
        up = jnp.maximum(_dot(h2, wu_ref[:, cs]), 0.0)
        acc = acc + _dot((up * up).astype(BF16), wd_ref[cs, :])
    x2 = x1 + mod(5) * acc
    o_ref[0] = _rmsnorm(x2, nf_ref[...])


def _merge_mlp(x, mod, ya, yf, yr, yh, bonus, g, gate, prm, *, tile):
    b, l, d = x.shape
    tok = lambda arr: pl.BlockSpec((1, tile, arr.shape[2]), lambda i, t: (i, t, 0))
    consts = [prm["lnx_w"], prm["lnx_b"], prm["w_branch_attn"], prm["w_branch_rwkv"], prm["w_out"],
              prm["norm2_g"], prm["w_mlp_up"], prm["w_mlp_down"], prm["norm_f_g"]]
    toks = [ya, yf, yr, yh, bonus, g, gate]
    return pl.pallas_call(
        functools.partial(_merge_kernel, ff_chunk=min(1024, prm["w_mlp_up"].shape[1])),
        grid=(b, l // tile),
        in_specs=[tok(x), pl.BlockSpec((1,) + mod.shape[1:], lambda i, t: (i, 0, 0))]
        + [tok(a) for a in toks] + [_const_spec(c.shape) for c in consts],
        out_specs=tok(x),
        out_shape=jax.ShapeDtypeStruct(x.shape, x.dtype),
        compiler_params=_params("parallel", "parallel"),
        name="merge_mlp",
    )(x, mod, *toks, *consts)


def _rope_tables(l):
    n_freq = HEAD_DIM // 4
    inv_freq = jnp.power(ROPE_BASE, -jnp.arange(n_freq, dtype=F32) / n_freq)
    rows = l // GRID_W
    row = jnp.repeat(jnp.arange(rows, dtype=F32), GRID_W)
    col = jnp.tile(jnp.arange(GRID_W, dtype=F32), rows)
    ang = jnp.concatenate([row[:, None] * inv_freq, col[:, None] * inv_freq], axis=-1)
    cos, sin = jnp.cos(ang), jnp.sin(ang)
    reps = LANES // HEAD_DIM
    return (jnp.tile(jnp.concatenate([cos, cos], axis=1), (1, reps)),
            jnp.tile(jnp.concatenate([-sin, sin], axis=1), (1, reps)))


def _pad_cols(w, width):
    return jnp.pad(w, ((0, 0), (0, width - w.shape[1])))


def kernel(x, c, ctx, c_ctx, w_ada, b_ada, norm1_g, w_in, sink, conv_w, decay_w0, decay_w2, iclr_a0, iclr_a2, gate_g2, k_k, k_a, r_k, lnx_w, lnx_b, w_branch_attn, w_branch_rwkv, w_out, norm2_g, w_mlp_up, w_mlp_down, norm_f_g):
    assert w_in.shape[0] == 1, "single-layer block: context tokens are read, never updated"
    b, l, d = x.shape
    attn_w = w_branch_attn.shape[1]
    rw = w_branch_rwkv.shape[1]
    n_q = attn_w // HEAD_DIM
    n_kv = n_q // Q_PER_KV
    kv_w = n_kv * HEAD_DIM
    assert kv_w == LANES and rw % LANES == 0 and l % 256 == 0 and ctx.shape[1] % 256 == 0

    w = w_in[0]
    o_g = attn_w + 2 * kv_w + 3 * rw + DECAY_LORA + ICLR_LORA + GATE_LORA
    assert o_g - (DECAY_LORA + ICLR_LORA + GATE_LORA) + LORA_PAD <= w.shape[1]
    widths = {"q": attn_w, "kv": 2 * kv_w, "k": PAIR * kv_w, "v": PAIR * kv_w, "rkv": 3 * rw,
              "lora": LORA_PAD, "gate": 2 * d}

    cw = conv_w[0]
    lora_w = jnp.zeros((LORA_PAD, 5 * rw), F32)
    lora_w = lora_w.at[:DECAY_LORA, :rw].set(decay_w2[0, 0]).at[:DECAY_LORA, rw:2 * rw].set(decay_w2[0, 1])
    r1 = DECAY_LORA + ICLR_LORA
    lora_w = lora_w.at[DECAY_LORA:r1, 2 * rw:3 * rw].set(iclr_a2[0, 0]).at[DECAY_LORA:r1, 3 * rw:4 * rw].set(iclr_a2[0, 1])
    lora_w = lora_w.at[r1:r1 + GATE_LORA, 4 * rw:].set(gate_g2[0])
    prm = {
        "conv_rkv": cw[:, :3 * rw], "conv_lora": _pad_cols(cw[:, 3 * rw:], LORA_PAD),
        "k_k": k_k[0].reshape(1, rw), "k_a": k_a[0].reshape(1, rw),
        "decay_w0": decay_w0[0], "iclr_a0": iclr_a0[0], "lora_w": lora_w.astype(BF16),
        "r_k": r_k[0].reshape(1, rw),
        "lnx_w": lnx_w[0].reshape(1, rw), "lnx_b": lnx_b[0].reshape(1, rw),
        "w_branch_attn": w_branch_attn[0], "w_branch_rwkv": w_branch_rwkv[0],
        "w_out": w_out[0], "norm2_g": norm2_g[0].reshape(1, d),
        "w_mlp_up": w_mlp_up[0].astype(BF16), "w_mlp_down": w_mlp_down[0].astype(BF16),
        "norm_f_g": norm_f_g.reshape(1, d),
    }

    rows = -(-(b + 1) // 8) * 8
    cc = jnp.zeros((rows, d), F32).at[:b].set(c).at[b].set(c_ctx)
    mod = _ada_mod(cc, w_ada[0], b_ada[0]).reshape(rows, -1, d)

    q, kd, vd, rkv, lora, gate = _in_proj(x, mod, b, norm1_g[0], [w, w[:, o_g:].astype(BF16)], widths,
                                          _rope_tables(l), latent=True, tile=512)
    kxd, vxd, rkv_c, lora_c = _in_proj(ctx, mod, b, norm1_g[0], [w], widths, None, latent=False, tile=256)
    ya = _attention(sink[0], q, kd, vd, kxd, vxd)

    prep_c = _wkv_prep(rkv_c, lora_c, prm, latent=False, tile=256)
    z_ctx = _wkv_scan(prep_c, jnp.zeros((b, 2, rw // LANES, CHUNK, LANES), F32), emit_y=False)
    prep = _wkv_prep(rkv, lora, prm, latent=True, tile=512)
    yf, yr = _wkv_scan(prep, z_ctx, emit_y=True)
    bonus, g, yh = prep[7], prep[8], prep[9]

    return _merge_mlp(x, mod, ya, yf, yr, yh, bonus, g, gate, prm, tile=512)
```

```python
import functools
import math

import jax
import jax.numpy as jnp
from jax import lax
from jax.experimental import pallas as pl
from jax.experimental.pallas import tpu as pltpu

F32 = jnp.float32
BF16 = jnp.bfloat16

GRID_W = 64
HEAD_DIM = 64
Q_PER_KV = 4
ATTN_BLOCK = 128
ATTN_QB = 8
ROPE_BASE = 10000.0
NORM_EPS = 1e-6
LNX_EPS = 1e-5 * HEAD_DIM
DECAY_LORA, ICLR_LORA, GATE_LORA = 32, 32, 96
LORA_PAD = 256
CHUNK = 64
SCAN_CHUNKS = 4
PREP_GROUP = 16
LANES = 128
MXU_DIM = 256
PAIR = LANES // HEAD_DIM
NEG = -1e30
LOG2_E = math.log2(math.e)
VMEM_LIMIT = 56 * 1024 * 1024


def _dot(a, b):
    return jnp.dot(a, b, preferred_element_type=F32)


def _dot_nt(a, b):
    return lax.dot_general(a, b, (((1,), (1,)), ((), ())), preferred_element_type=F32)


def _dot_tn(a, b):
    return lax.dot_general(a, b, (((0,), (0,)), ((), ())), preferred_element_type=F32)


def _iota(shape, dim):
    return lax.broadcasted_iota(jnp.int32, shape, dim)


def _head_ones(width):
    r = _iota((width, width), 0) // HEAD_DIM
    c = _iota((width, width), 1) // HEAD_DIM
    return jnp.where(r == c, 1.0, 0.0).astype(BF16)


def _head_sum(x, ones_bd):
    gw = ones_bd.shape[0]
    hi = x.astype(BF16)
    lo = (x - hi.astype(F32)).astype(BF16)
    parts = [_dot(hi[:, j:j + gw], ones_bd) + _dot(lo[:, j:j + gw], ones_bd) for j in range(0, x.shape[1], gw)]
    return jnp.concatenate(parts, axis=1)


def _rmsnorm(x, g):
    ms = jnp.mean(x * x, axis=-1, keepdims=True)
    return x * lax.rsqrt(ms + NORM_EPS) * g


def _params(*sem):
    return pltpu.CompilerParams(dimension_semantics=sem, vmem_limit_bytes=VMEM_LIMIT)


def _const_spec(shape):
    nd = len(shape)
    return pl.BlockSpec(shape, lambda *_: (0,) * nd, pipeline_mode=pl.Buffered(1))


def _ada_kernel(c_ref, w_ref, b_ref, o_ref):
    c = c_ref[...]
    s = c * jax.nn.sigmoid(c)
    o_ref[...] = _dot(s.astype(BF16), w_ref[...].astype(BF16)) + b_ref[...]


def _ada_mod(cc, w_ada, b_ada):
    rows, d = cc.shape
    n = w_ada.shape[1]
    return pl.pallas_call(
        _ada_kernel,
        grid=(n // d,),
        in_specs=[pl.BlockSpec((rows, d), lambda j: (0, 0)),
                  pl.BlockSpec((d, d), lambda j: (0, j)),
                  pl.BlockSpec((1, d), lambda j: (0, j))],
        out_specs=pl.BlockSpec((rows, d), lambda j: (0, j)),
        out_shape=jax.ShapeDtypeStruct((rows, n), F32),
        compiler_params=_params("arbitrary"),
        name="ada_mod",
    )(cc, w_ada, b_ada.reshape(1, n))


def _rope(x, cos_t, sin_t):
    w = x.shape[1]
    half = HEAD_DIM // 2
    first = (_iota(x.shape, 1) % HEAD_DIM) < half
    swapped = jnp.where(first, pltpu.roll(x, w - half, 1), pltpu.roll(x, half, 1))
    reps = w // LANES
    c = jnp.concatenate([cos_t] * reps, axis=1)
    s = jnp.concatenate([sin_t] * reps, axis=1)
    return x * c + swapped * s


def _inproj_kernel(*refs, latent, widths):
    if latent:
        (x_ref, mod_ref, g_ref, wq_ref, w_ref, wg_ref, cos_ref, sin_ref,
         q_ref, k_ref, v_ref, rkv_ref, lora_ref, gate_ref) = refs
    else:
        x_ref, mod_ref, g_ref, w_ref, k_ref, v_ref, rkv_ref, lora_ref = refs
    x = x_ref[0]
    h = _rmsnorm(x, g_ref[...]) * (1.0 + mod_ref[0, 1:2, :]) + mod_ref[0, 0:1, :]
    hb = h.astype(BF16)
    off = 0

    def seg(name):
        nonlocal off
        lo = off
        off += widths[name]
        return _dot(hb, w_ref[:, lo:off])

    def dup_heads(t):
        first = _iota(t.shape, 1) < HEAD_DIM
        other = pltpu.roll(t, HEAD_DIM, 1)
        return jnp.concatenate([jnp.where(first, t, other), jnp.where(first, other, t)], axis=1)

    kv = seg("kv")
    k, v = dup_heads(kv[:, :LANES]), dup_heads(kv[:, LANES:])
    if latent:
        cos_t, sin_t = cos_ref[...], sin_ref[...]
        q_ref[0] = (_rope(_dot(hb, wq_ref[...]), cos_t, sin_t) * (LOG2_E * HEAD_DIM ** -0.5)).astype(BF16)
        k = _rope(k, cos_t, sin_t)
    k_ref[0] = k.astype(BF16)
    v_ref[0] = v.astype(BF16)
    rkv_ref[0] = seg("rkv")
    lora_ref[0] = seg("lora")
    if latent:
        gate_ref[0] = jax.nn.sigmoid(_dot(hb, wg_ref[...])).astype(BF16)


def _in_proj(x, mod, mod_row, norm_g, weights, widths, tables, *, latent, tile):
    b, l, d = x.shape
    nt = l // tile
    if latent:
        mod_map = lambda i, t: (i, 0, 0)
    else:
        mod_map = lambda i, t: (mod_row, 0, 0)
    tok = lambda w: pl.BlockSpec((1, tile, w), lambda i, t: (i, t, 0))
    in_specs = [tok(d),
                pl.BlockSpec((1,) + mod.shape[1:], mod_map),
                _const_spec((1, d))] + [_const_spec(w.shape) for w in weights]
    args = [x, mod, norm_g.reshape(1, d), *weights]
    out_specs, out_shape = [], []

    def out(w, dt):
        out_specs.append(tok(w))
        out_shape.append(jax.ShapeDtypeStruct((b, l, w), dt))

    if latent:
        in_specs += [pl.BlockSpec((tile, LANES), lambda i, t: (t, 0))] * 2
        args += list(tables)
        out(widths["q"], BF16)
    out(widths["k"], BF16)
    out(widths["v"], BF16)
    out(widths["rkv"], F32)
    out(widths["lora"], F32)
    if latent:
        out(widths["gate"], BF16)
    return pl.pallas_call(
        functools.partial(_inproj_kernel, latent=latent, widths=widths),
        grid=(b, nt),
        in_specs=in_specs,
        out_specs=out_specs,
        out_shape=out_shape,
        compiler_params=_params("parallel", "parallel"),
        name="in_proj_latent" if latent else "in_proj_context",
    )(*args)


def _attn_kernel(sink_ref, q_ref, kp_ref, kc_ref, kn_ref, vp_ref, vc_ref, vn_ref,
                 kx_ref, vx_ref, o_ref, *, n_kv):
    i = pl.program_id(1)
    last = pl.num_programs(1) - 1
    blk = ATTN_BLOCK
    qi = _iota((blk, blk), 0)
    kj = _iota((blk, blk), 1)
    left = _iota((blk, LANES), 1) < HEAD_DIM

    def key_block(refs, j, gs):
        ref_p, ref_c, ref_n = refs[:3]
        if j < 0:
            return ref_p[0, :, gs]
        if j >= ATTN_QB:
            return ref_n[0, :, gs]
        return ref_c[0, j * blk:(j + 1) * blk, gs]

    items = []
    for qb in range(ATTN_QB):
        lo_ok = kj >= qi
        hi_ok = kj <= qi
        if qb == 0:
            lo_ok = lo_ok & (i > 0)
        if qb == ATTN_QB - 1:
            hi_ok = hi_ok & (i < last)
        bias_lo = jnp.concatenate([jnp.where(lo_ok, 0.0, NEG)] * Q_PER_KV, axis=0)
        bias_hi = jnp.concatenate([jnp.where(hi_ok, 0.0, NEG)] * Q_PER_KV, axis=0)
        rows = slice(qb * blk, (qb + 1) * blk)
        q = q_ref[0, rows, :].astype(F32)
        for g in range(n_kv):
            gs = slice(g * LANES, (g + 1) * LANES)
            heads = range(g * Q_PER_KV, (g + 1) * Q_PER_KV)
            qs, sinks = [], []
            for hd in heads:
                qp = q[:, (hd // PAIR) * LANES:(hd // PAIR + 1) * LANES]
                keep = left if hd % PAIR == 0 else jnp.logical_not(left)
                qs.append(jnp.where(keep, qp, 0.0).astype(BF16))
                sinks.append(jnp.full((blk, 1), sink_ref[hd] * LOG2_E, F32))
            items.append({
                "qb": qb, "gs": gs, "rows": rows, "heads": heads, "bias": (bias_lo, bias_hi),
                "qs": jnp.concatenate(qs, axis=0), "sink": jnp.concatenate(sinks, axis=0)})

    def cat(refs, it):
        return jnp.concatenate([key_block(refs, it["qb"] + j, it["gs"]) for j in (-1, 0, 1)]
                               + [refs[3][0, :, it["gs"]]], axis=0)

    s = [_dot_nt(it["qs"], cat((kp_ref, kc_ref, kn_ref, kx_ref), it)) for it in items]
    s = [jnp.concatenate([x[:, :blk] + it["bias"][0], x[:, blk:2 * blk], x[:, 2 * blk:3 * blk] + it["bias"][1],
                          x[:, 3 * blk:]], axis=1) for x, it in zip(s, items)]
    m = [jnp.maximum(jnp.max(x, axis=1, keepdims=True), it["sink"]) for x, it in zip(s, items)]
    p = [jnp.exp2(x - mx) for x, mx in zip(s, m)]
    den = [jnp.sum(x, axis=1, keepdims=True) + jnp.exp2(it["sink"] - mx) for x, mx, it in zip(p, m, items)]
    o = [_dot(x.astype(BF16), cat((vp_ref, vc_ref, vn_ref, vx_ref), it)) / dn for x, dn, it in zip(p, den, items)]
    for x, it in zip(o, items):
        for hd in it["heads"][::PAIR]:
            j = hd - it["heads"][0]
            pair = jnp.where(left, x[j * blk:(j + 1) * blk], x[(j + 1) * blk:(j + 2) * blk])
            col = (hd // PAIR) * LANES
            o_ref[0, it["rows"], col:col + LANES] = pair.astype(BF16)


def _attention(sink, q, kd, vd, kxd, vxd):
    b, l, wq = q.shape
    wk = kd.shape[2]
    lc = kxd.shape[1]
    nb = l // ATTN_BLOCK
    span = ATTN_QB * ATTN_BLOCK
    blk = lambda w, f: pl.BlockSpec((1, ATTN_BLOCK, w), f)
    prev = lambda bi, i: (bi, jnp.maximum(i * ATTN_QB - 1, 0), 0)
    nxt = lambda bi, i: (bi, jnp.minimum((i + 1) * ATTN_QB, nb - 1), 0)
    cur = lambda w: pl.BlockSpec((1, span, w), lambda bi, i: (bi, i, 0))
    ctx = pl.BlockSpec((1, lc, wk), lambda bi, i: (bi, 0, 0))
    return pl.pallas_call(
        functools.partial(_attn_kernel, n_kv=wk // LANES),
        grid=(b, l // span),
        in_specs=[pl.BlockSpec(memory_space=pltpu.SMEM),
                  cur(wq),
                  blk(wk, prev), cur(wk), blk(wk, nxt),
                  blk(wk, prev), cur(wk), blk(wk, nxt),
                  ctx, ctx],
        out_specs=cur(wq),
        out_shape=jax.ShapeDtypeStruct((b, l, wq), BF16),
        compiler_params=_params("parallel", "parallel"),
        name="attention",
    )(sink, q, kd, kd, kd, vd, vd, vd, kxd, vxd)


def _conv3(x, prev_row, next_row, w):
    n = x.shape[0]
    row = _iota((8, x.shape[1]), 0)
    xm = pltpu.roll(x, 1, 0)
    xm = jnp.concatenate([jnp.where(row == 0, prev_row, xm[:8]), xm[8:]], axis=0)
    xp = pltpu.roll(x, n - 1, 0)
    xp = jnp.concatenate([xp[:n - 8], jnp.where(row == 7, next_row, xp[n - 8:])], axis=0)
    return xm * w[0:1] + x * w[1:2] + xp * w[2:3]


def _pair_masks():
    n = 2 * CHUNK
    row = _iota((n, n), 0)
    lane = _iota((n, n), 1)
    top, left = row < CHUNK, lane < CHUNK
    return {"row": row % CHUNK, "lane": lane % CHUNK, "top": top, "left": left, "bd": top == left,
            "left_h": _iota((CHUNK, n), 1) < CHUNK}


def _stack(a, b):
    return jnp.concatenate([a, b], axis=0)


def _fold(x, m):
    return jnp.where(m["left_h"], x[:CHUNK], x[CHUNK:])


def _chunk_local(inst, m):
    bf = lambda x: x.astype(BF16)
    diag = jnp.logical_not(m["top"]) & (m["lane"] == m["row"])
    masks = ((m["lane"] < m["row"]) | diag, (m["lane"] > m["row"]) | diag)
    mask_a = [masks[i["rev"]] for i in inst]
    lh = m["left_h"]
    half = CHUNK // 2
    keep_l = jnp.where(m["left"], 1.0, 0.0).astype(BF16)
    keep_r = jnp.where(m["left"], 0.0, 1.0).astype(BF16)
    keep_lh = jnp.where(lh, 1.0, 0.0).astype(BF16)
    keep_rh = jnp.where(lh, 0.0, 1.0).astype(BF16)

    def unfold(xb, anti=False):
        a, b = xb * keep_lh, xb * keep_rh
        return _stack(b, a) if anti else _stack(a, b)

    lhs = [bf(_stack(i["at"], i["rt"])) for i in inst]
    a01 = [_dot_nt(l, _stack(bf(_stack(i["bt"], i["kt"])) * keep_l, bf(_stack(i["kt"], i["bt"])) * keep_r))
           for l, i in zip(lhs, inst)]
    a0 = [jnp.where(ma, a[:, :LANES], 0.0) for ma, a in zip(mask_a, a01)]
    a1 = [jnp.where(ma, a[:, LANES:], 0.0) for ma, a in zip(mask_a, a01)]
    nc = [jnp.where(lh, x[:CHUNK], y[:CHUNK]) for x, y in zip(a0, a1)]
    arb = [jnp.where(lh, x[CHUNK:], y[CHUNK:]) for x, y in zip(a0, a1)]
    ak_ark_sw = [bf(jnp.where(m["left"], y, x)) for x, y in zip(a0, a1)]
    vh = [_dot(a, unfold(bf(i["v"]), anti=True)) for a, i in zip(ak_ark_sw, inst)]
    eye = jnp.where(_iota((CHUNK, LANES), 1) % CHUNK == _iota((CHUNK, LANES), 0), 1.0, 0.0)
    tc = [eye + n for n in nc]
    ncb = [bf(n) for n in nc]
    nc = [_dot(n, unfold(n)) for n in ncb]
    steps = CHUNK.bit_length() - 1
    for _ in range(steps - 2):
        ncb = [bf(n) for n in nc]
        both = [_dot(n, jnp.concatenate([unfold(bf(t)), unfold(n)], axis=1)) for n, t in zip(ncb, tc)]
        tc = [t + r[:, :LANES] for t, r in zip(tc, both)]
        nc = [r[:, LANES:] for r in both]
    inc = [_dot(bf(n[half:]), unfold(bf(t))) for n, t in zip(nc, tc)]
    tc = [jnp.concatenate([t[:half], t[half:] + d], axis=0) for t, d in zip(tc, inc)]
    pq = [_dot(bf(t), jnp.concatenate([unfold(l[:CHUNK]), unfold(bf(x[:CHUNK]))], axis=1))
          for t, l, x in zip(tc, lhs, vh)]
    kvf = [_dot_tn(bf(i["v"]), bf(i["kt"])) for i in inst]
    return [{"pm": r[:, :LANES], "qm": r[:, LANES:], "arb": b, "hm": x[CHUNK:], "kv": _fold(k, m)}
            for r, b, x, k in zip(pq, arb, vh, kvf)]


def _prep_kernel(*refs, latent, width):
    (rkv_ref, rkv_p, rkv_n, lora_ref, lora_p, lora_n, cw_ref, cwl_ref, kk_ref, ka_ref,
     w0_ref, a0_ref, wl_ref, rk_ref) = refs[:14]
    outs = refs[14:]
    out_refs = dict(zip(("pm", "qm", "rt", "bt", "arb", "kv"), outs[:6]))
    wc_ref = outs[6]
    if latent:
        bonus_ref, g_ref, hsum_ref = outs[7:]
    t = pl.program_id(1)
    nt = pl.num_programs(1)
    tile = rkv_ref.shape[1]
    w = width
    has_prev = (t > 0).astype(F32)
    has_next = (t < nt - 1).astype(F32)
    u = _conv3(rkv_ref[0], rkv_p[0, 7:8, :] * has_prev, rkv_n[0, 0:1, :] * has_next, cw_ref[...])
    ul = _conv3(lora_ref[0], lora_p[0, 7:8, :] * has_prev, lora_n[0, 0:1, :] * has_next, cwl_ref[...])
    r, k, v = u[:, :w], u[:, w:2 * w], u[:, 2 * w:]
    ones_bd = _head_ones(min(MXU_DIM, w))

    kk = k * kk_ref[...]
    kk = kk * lax.rsqrt(jnp.maximum(_head_sum(kk * kk, ones_bd), 1e-24))

    lane = _iota(ul.shape, 1)
    lin = jnp.where(lane < DECAY_LORA, jnp.tanh(ul),
                    jnp.where(lane < DECAY_LORA + ICLR_LORA, ul, jax.nn.sigmoid(ul)))
    proj = _dot(lin.astype(BF16), wl_ref[...])

    tr = _iota((tile, tile), 0)
    tc = _iota((tile, tile), 1)
    same = (tr // CHUNK) == (tc // CHUNK)
    tri = (jnp.where(same & (tc <= tr), 1.0, 0.0).astype(BF16),
           jnp.where(same & (tc >= tr), 1.0, 0.0).astype(BF16))

    def exact_dot(m, x):
        h1 = x.astype(BF16)
        h2 = (x - h1.astype(F32)).astype(BF16)
        return _dot(m, h1) + _dot(m, h2)

    masks = _pair_masks()
    k_sum = None
    work = []
    for d in range(2):
        z = w0_ref[d:d + 1, :] + proj[:, d * w:(d + 1) * w]
        lw = -math.exp(-0.5) * jax.nn.sigmoid(z)
        a = jax.nn.sigmoid(a0_ref[d:d + 1, :] + proj[:, (2 + d) * w:(3 + d) * w])
        kd = k * (1.0 + (a - 1.0) * ka_ref[...])
        k_sum = kd if k_sum is None else k_sum + kd
        cum = exact_dot(tri[d], lw)
        e_neg = jnp.exp(-cum)
        full = {"at": -kk * jnp.exp(cum - lw), "rt": r * jnp.exp(cum), "bt": kk * a * e_neg,
                "kt": kd * e_neg, "v": v}
        out_refs["rt"][d, 0] = full["rt"].astype(BF16)
        out_refs["bt"][d, 0] = full["bt"].astype(BF16)
        for j in range(tile // CHUNK):
            edge = (j + 1) * CHUNK - 1 if d == 0 else j * CHUNK
            wc_ref[0, j, d:d + 1, :] = jnp.exp(cum[edge:edge + 1, :])
        work += [(d, slice(j * CHUNK, (j + 1) * CHUNK), slice(p * LANES, (p + 1) * LANES), full)
                 for j in range(tile // CHUNK) for p in range(w // LANES)]
    hsum = {}
    for g0 in range(0, len(work), PREP_GROUP):
        group = work[g0:g0 + PREP_GROUP]
        inst = [dict({name: val[rs, ls] for name, val in full.items()}, rev=d) for d, rs, ls, full in group]
        for (d, rs, ls, _), res in zip(group, _chunk_local(inst, masks)):
            seen = hsum.get((rs.start, ls.start))
            hsum[(rs.start, ls.start)] = (rs, ls, res["hm"] if seen is None else seen[2] + res["hm"])
            for name in res:
                if name in out_refs:
                    out_refs[name][d, 0, rs, ls] = res[name].astype(BF16)
    if latent:
        for rs, ls, val in hsum.values():
            hsum_ref[0, rs, ls] = val
        bonus_ref[0] = _head_sum(r * k_sum * rk_ref[...], ones_bd) * v
        g_ref[0] = proj[:, 4 * w:5 * w].astype(BF16)


def _wkv_prep(rkv, lora, prm, *, latent, tile):
    b, l, w3 = rkv.shape
    w = w3 // 3
    nt = l // tile
    n8 = l // 8
    tok = lambda wd: pl.BlockSpec((1, tile, wd), lambda i, t: (i, t, 0))
    prev = lambda wd: pl.BlockSpec((1, 8, wd), lambda i, t: (i, jnp.maximum(t * (tile // 8) - 1, 0), 0))
    nxt = lambda wd: pl.BlockSpec((1, 8, wd), lambda i, t: (i, jnp.minimum((t + 1) * (tile // 8), n8 - 1), 0))
    wl = lora.shape[2]
    in_specs = [tok(w3), prev(w3), nxt(w3), tok(wl), prev(wl), nxt(wl)]
    consts = [prm["conv_rkv"], prm["conv_lora"], prm["k_k"], prm["k_a"], prm["decay_w0"],
              prm["iclr_a0"], prm["lora_w"], prm["r_k"]]
    in_specs += [_const_spec(c.shape) for c in consts]
    dirtok = pl.BlockSpec((2, 1, tile, w), lambda i, t: (0, i, t, 0))
    out_specs = [dirtok] * 6 + [pl.BlockSpec((1, tile // CHUNK, 2, w), lambda i, t: (i, t, 0, 0))]
    out_shape = [jax.ShapeDtypeStruct((2, b, l, w), BF16)] * 6 + [
        jax.ShapeDtypeStruct((b, l // CHUNK, 2, w), F32)]
    if latent:
        out_specs += [tok(w), tok(w), tok(w)]
        out_shape += [jax.ShapeDtypeStruct((b, l, w), F32), jax.ShapeDtypeStruct((b, l, w), BF16),
                      jax.ShapeDtypeStruct((b, l, w), F32)]
    return pl.pallas_call(
        functools.partial(_prep_kernel, latent=latent, width=w),
        grid=(b, nt),
        in_specs=in_specs,
        out_specs=out_specs,
        out_shape=out_shape,
        compiler_params=_params("parallel", "parallel"),
        name="wkv_prep_latent" if latent else "wkv_prep_context",
    )(rkv, rkv, rkv, lora, lora, lora, *consts)


def _wkv_kernel(*refs, emit_y, n_pairs):
    names = ("pm", "qm", "rt", "bt", "arb", "kv", "wc")
    n = len(names)
    dir_refs = (dict(zip(names, refs[0:n])), dict(zip(names, refs[n:2 * n])))
    z0_ref = refs[2 * n]
    if emit_y:
        y_refs = refs[2 * n + 1:2 * n + 3]
        z_scr = refs[2 * n + 3]
    else:
        zfin_ref = refs[2 * n + 1]
        z_scr = refs[2 * n + 2]
    c = pl.program_id(0)

    @pl.when(c == 0)
    def _():
        z_scr[...] = z0_ref[...]

    left_h = _iota((CHUNK, LANES), 1) < CHUNK
    keep_l = jnp.where(left_h, 1.0, 0.0).astype(BF16)
    keep_r = jnp.where(left_h, 0.0, 1.0).astype(BF16)

    def unfold(xb):
        return _stack(xb * keep_l, xb * keep_r)

    tiles = [(i, d, p, slice(p * LANES, (p + 1) * LANES))
             for i in range(z_scr.shape[0]) for d in range(2) for p in range(n_pairs)]
    state = [z_scr[i, d, p] for i, d, p, _ in tiles]
    n_sub = dir_refs[0]["wc"].shape[1]
    for step in range(n_sub):
        sub = (step, n_sub - 1 - step)
        rows = [slice(sub[d] * CHUNK, (sub[d] + 1) * CHUNK) for d in range(2)]
        ld = lambda name: [dir_refs[d][name][0, i, rows[d], sl] for i, d, _, sl in tiles]
        sbd = [unfold(s.astype(BF16)) for s in state]
        if emit_y:
            ur = [_dot_nt(_stack(pm, rt), s) for pm, rt, s in zip(ld("pm"), ld("rt"), sbd)]
        else:
            ur = [_dot_nt(pm, s) for pm, s in zip(ld("pm"), sbd)]
        u = [x[:CHUNK] + q.astype(F32) for x, q in zip(ur, ld("qm"))]
        ub = [ui.astype(BF16) for ui in u]
        inc = [_dot_tn(ui, bt) for ui, bt in zip(ub, ld("bt"))]
        inc = [jnp.where(left_h, x[:CHUNK], x[CHUNK:]) for x in inc]
        if emit_y:
            yc = [_dot(a, unfold(ui)) for a, ui in zip(ld("arb"), ub)]
            for (i, d, _, sl), x, ys in zip(tiles, ur, yc):
                y_refs[d][i, rows[d], sl] = (x[CHUNK:] + ys).astype(BF16)
        state = [(s + dz + kv.astype(F32)) * dir_refs[d]["wc"][i, sub[d], d:d + 1, sl]
                 for (i, d, _, sl), s, dz, kv in zip(tiles, state, inc, ld("kv"))]
    for (i, d, p, _), s in zip(tiles, state):
        z_scr[i, d, p] = s

    if not emit_y:
        @pl.when(c == pl.num_programs(0) - 1)
        def _():
            zfin_ref[...] = z_scr[...]


def _wkv_scan(prep, z0, *, emit_y):
    wc = prep[6]
    _, b, l, w = prep[0].shape
    n_sub = min(SCAN_CHUNKS, l // CHUNK)
    blk = n_sub * CHUNK
    nc = l // blk
    n_pairs = w // LANES
    fwd = lambda c: c
    rev = lambda c: nc - 1 - c
    in_specs, args = [], []
    for d, cm in enumerate((fwd, rev)):
        for arr in prep[:6]:
            in_specs.append(pl.BlockSpec((1, b, blk, w), lambda c, d=d, cm=cm: (d, 0, cm(c), 0)))
            args.append(arr)
        in_specs.append(pl.BlockSpec((b, n_sub, 2, w), lambda c, cm=cm: (0, cm(c), 0, 0)))
        args.append(wc)
    zshape = (b, 2, n_pairs, CHUNK, LANES)
    zspec = pl.BlockSpec(zshape, lambda c: (0, 0, 0, 0, 0))
    in_specs.append(zspec)
    args.append(z0)
    if emit_y:
        out_specs = [pl.BlockSpec((b, blk, w), lambda c: (0, c, 0)),
                     pl.BlockSpec((b, blk, w), lambda c: (0, nc - 1 - c, 0))]
        out_shape = [jax.ShapeDtypeStruct((b, l, w), BF16)] * 2
    else:
        out_specs = zspec
        out_shape = jax.ShapeDtypeStruct(zshape, F32)
    return pl.pallas_call(
        functools.partial(_wkv_kernel, emit_y=emit_y, n_pairs=n_pairs),
        grid=(nc,),
        in_specs=in_specs,
        out_specs=out_specs,
        out_shape=out_shape,
        scratch_shapes=[pltpu.VMEM(zshape, F32)],
        compiler_params=_params("arbitrary"),
        name="wkv_scan_latent" if emit_y else "wkv_scan_context",
    )(*args)


def _merge_kernel(x_ref, mod_ref, ya_ref, yf_ref, yr_ref, yh_ref, bonus_ref, g_ref, gate_ref,
                  lnw_ref, lnb_ref, wba_ref, wbr_ref, wo_ref, n2_ref, wu_ref, wd_ref, nf_ref,
                  o_ref, *, ff_chunk):
    x = x_ref[0]
    d = x.shape[1]
    mod = lambda j: mod_ref[0, j:j + 1, :]
    y = yf_ref[0].astype(F32) + yr_ref[0].astype(F32) + yh_ref[0]
    gw = min(MXU_DIM, y.shape[1])
    ones_bd = _head_ones(gw)

    def head_mean(t):
        tb = t.astype(BF16)
        parts = [_dot(tb[:, j:j + gw], ones_bd) for j in range(0, t.shape[1], gw)]
        return jnp.concatenate(parts, axis=1) * (1.0 / HEAD_DIM)

    mu = head_mean(y)
    yc = y - mu
    var = head_mean(yc * yc)
    yn = yc * lax.rsqrt(var + LNX_EPS)
    yr = (yn * lnw_ref[...] + lnb_ref[...] + bonus_ref[0]) * g_ref[0].astype(F32)
    gate = gate_ref[0].astype(F32)
    merged = gate[:, :d] * _dot(ya_ref[0], wba_ref[...]) + gate[:, d:] * _dot(yr.astype(BF16), wbr_ref[...])
    x1 = x + mod(2) * _dot(merged.astype(BF16), wo_ref[...])
    h2 = (_rmsnorm(x1, n2_ref[...]) * (1.0 + mod(4)) + mod(3)).astype(BF16)
    acc = jnp.zeros_like(x1)
    for j in range(wu_ref.shape[1] // ff_chunk):
        cs = slice(j * ff_chunk, (j + 1) * ff_chunk)
        up = jnp.maximum(_dot(h2, wu_ref[:, cs]), 0.0)
        acc = acc + _dot((up * up).astype(BF16), wd_ref[cs, :])
    x2 = x1 + mod(5) * acc
    o_ref[0] = _rmsnorm(x2, nf_ref[...])


def _merge_mlp(x, mod, ya, yf, yr, yh, bonus, g, gate, prm, *, tile):
    b, l, d = x.shape
    tok = lambda arr: pl.BlockSpec((1, tile, arr.shape[2]), lambda i, t: (i, t, 0))
    consts = [prm["lnx_w"], prm["lnx_b"], prm["w_branch_attn"], prm["w_branch_rwkv"], prm["w_out"],
              prm["norm2_g"], prm["w_mlp_up"], prm["w_mlp_down"], prm["norm_f_g"]]
    toks = [ya, yf, yr, yh, bonus, g, gate]
    return pl.pallas_call(
        functools.partial(_merge_kernel, ff_chunk=min(1024, prm["w_mlp_up"].shape[1])),
        grid=(b, l // tile),
        in_specs=[tok(x), pl.BlockSpec((1,) + mod.shape[1:], lambda i, t: (i, 0, 0))]
        + [tok(a) for a in toks] + [_const_spec(c.shape) for c in consts],
        out_specs=tok(x),
        out_shape=jax.ShapeDtypeStruct(x.shape, x.dtype),
        compiler_params=_params("parallel", "parallel"),
        name="merge_mlp",
    )(x, mod, *toks, *consts)


def _rope_tables(l):
    n_freq = HEAD_DIM // 4
    inv_freq = jnp.power(ROPE_BASE, -jnp.arange(n_freq, dtype=F32) / n_freq)
    rows = l // GRID_W
    row = jnp.repeat(jnp.arange(rows, dtype=F32), GRID_W)
    col = jnp.tile(jnp.arange(GRID_W, dtype=F32), rows)
    ang = jnp.concatenate([row[:, None] * inv_freq, col[:, None] * inv_freq], axis=-1)
    cos, sin = jnp.cos(ang), jnp.sin(ang)
    reps = LANES // HEAD_DIM
    return (jnp.tile(jnp.concatenate([cos, cos], axis=1), (1, reps)),
            jnp.tile(jnp.concatenate([-sin, sin], axis=1), (1, reps)))


def _pad_cols(w, width):
    return jnp.pad(w, ((0, 0), (0, width - w.shape[1])))


def kernel(x, c, ctx, c_ctx, w_ada, b_ada, norm1_g, w_in, sink, conv_w, decay_w0, decay_w2, iclr_a0, iclr_a2, gate_g2, k_k, k_a, r_k, lnx_w, lnx_b, w_branch_attn, w_branch_rwkv, w_out, norm2_g, w_mlp_up, w_mlp_down, norm_f_g):
    assert w_in.shape[0] == 1, "single-layer block: context tokens are read, never updated"
    b, l, d = x.shape
    attn_w = w_branch_attn.shape[1]
    rw = w_branch_rwkv.shape[1]
    n_q = attn_w // HEAD_DIM
    n_kv = n_q // Q_PER_KV
    kv_w = n_kv * HEAD_DIM
    assert kv_w == LANES and rw % LANES == 0 and ctx.shape[1] % 256 == 0
    assert l % (ATTN_QB * ATTN_BLOCK) == 0 and l % 512 == 0

    w = w_in[0]
    o_k, o_r = attn_w, attn_w + 2 * kv_w
    o_l = o_r + 3 * rw
    o_g = o_l + DECAY_LORA + ICLR_LORA + GATE_LORA
    w_ctx = _pad_cols(w[:, o_k:o_g], o_l - o_k + LORA_PAD).astype(BF16)
    w_lat = [w[:, :o_k].astype(BF16), w_ctx, w[:, o_g:].astype(BF16)]
    widths_ctx = {"kv": 2 * kv_w, "k": PAIR * kv_w, "v": PAIR * kv_w, "rkv": 3 * rw, "lora": LORA_PAD}
    widths_lat = {"q": attn_w, **widths_ctx, "gate": 2 * d}

    cw = conv_w[0]
    lora_w = jnp.zeros((LORA_PAD, 5 * rw), F32)
    lora_w = lora_w.at[:DECAY_LORA, :rw].set(decay_w2[0, 0]).at[:DECAY_LORA, rw:2 * rw].set(decay_w2[0, 1])
    r1 = DECAY_LORA + ICLR_LORA
    lora_w = lora_w.at[DECAY_LORA:r1, 2 * rw:3 * rw].set(iclr_a2[0, 0]).at[DECAY_LORA:r1, 3 * rw:4 * rw].set(iclr_a2[0, 1])
    lora_w = lora_w.at[r1:r1 + GATE_LORA, 4 * rw:].set(gate_g2[0])
    prm = {
        "conv_rkv": cw[:, :3 * rw], "conv_lora": _pad_cols(cw[:, 3 * rw:], LORA_PAD),
        "k_k": k_k[0].reshape(1, rw), "k_a": k_a[0].reshape(1, rw),
        "decay_w0": decay_w0[0], "iclr_a0": iclr_a0[0], "lora_w": lora_w.astype(BF16),
        "r_k": r_k[0].reshape(1, rw),
        "lnx_w": lnx_w[0].reshape(1, rw), "lnx_b": lnx_b[0].reshape(1, rw),
        "w_branch_attn": w_branch_attn[0].astype(BF16), "w_branch_rwkv": w_branch_rwkv[0].astype(BF16),
        "w_out": w_out[0].astype(BF16), "norm2_g": norm2_g[0].reshape(1, d),
        "w_mlp_up": w_mlp_up[0].astype(BF16), "w_mlp_down": w_mlp_down[0].astype(BF16),
        "norm_f_g": norm_f_g.reshape(1, d),
    }

    rows = -(-(b + 1) // 8) * 8
    cc = jnp.zeros((rows, d), F32).at[:b].set(c).at[b].set(c_ctx)
    mod = _ada_mod(cc, w_ada[0], b_ada[0]).reshape(rows, -1, d)

    q, kd, vd, rkv, lora, gate = _in_proj(x, mod, b, norm1_g[0], w_lat, widths_lat, _rope_tables(l),
                                          latent=True, tile=512)
    kxd, vxd, rkv_c, lora_c = _in_proj(ctx, mod, b, norm1_g[0], [w_ctx], widths_ctx, None,
                                       latent=False, tile=256)
    ya = _attention(sink[0], q, kd, vd, kxd, vxd)

    prep_c = _wkv_prep(rkv_c, lora_c, prm, latent=False, tile=256)
    z_ctx = _wkv_scan(prep_c, jnp.zeros((b, 2, rw // LANES, CHUNK, LANES), F32), emit_y=False)
    prep = _wkv_prep(rkv, lora, prm, latent=True, tile=512)
    yf, yr = _wkv_scan(prep, z_ctx, emit_y=True)
    bonus, g, yh = prep[7], prep[8], prep[9]

    return _merge_mlp(x, mod, ya, yf, yr, yh, bonus, g, gate, prm, tile=512)
```

```python
import functools
import math

import jax
import jax.numpy as jnp
from jax import lax
from jax.experimental import pallas as pl
from jax.experimental.pallas import tpu as pltpu

F32 = jnp.float32
BF16 = jnp.bfloat16

GRID_W = 64
HEAD_DIM = 64
Q_PER_KV = 4
ATTN_BLOCK = 128
ATTN_QB = 8
ROPE_BASE = 10000.0
NORM_EPS = 1e-6
LNX_EPS = 1e-5 * HEAD_DIM
DECAY_LORA, ICLR_LORA, GATE_LORA = 32, 32, 96
LORA_PAD = 256
CHUNK = 64
SCAN_CHUNKS = 4
PREP_GROUP = 16
LANES = 128
MXU_DIM = 256
PAIR = LANES // HEAD_DIM
NEG = -1e30
LOG2_E = math.log2(math.e)
VMEM_LIMIT = 56 * 1024 * 1024


def _dot(a, b):
    return jnp.dot(a, b, preferred_element_type=F32)


def _dot_nt(a, b):
    return lax.dot_general(a, b, (((1,), (1,)), ((), ())), preferred_element_type=F32)


def _dot_tn(a, b):
    return lax.dot_general(a, b, (((0,), (0,)), ((), ())), preferred_element_type=F32)


def _iota(shape, dim):
    return lax.broadcasted_iota(jnp.int32, shape, dim)


def _head_ones(width):
    r = _iota((width, width), 0) // HEAD_DIM
    c = _iota((width, width), 1) // HEAD_DIM
    return jnp.where(r == c, 1.0, 0.0).astype(BF16)


def _head_sum(x, ones_bd):
    gw = ones_bd.shape[0]
    hi = x.astype(BF16)
    lo = (x - hi.astype(F32)).astype(BF16)
    parts = [_dot(hi[:, j:j + gw], ones_bd) + _dot(lo[:, j:j + gw], ones_bd) for j in range(0, x.shape[1], gw)]
    return jnp.concatenate(parts, axis=1)


def _rmsnorm(x, g):
    ms = jnp.mean(x * x, axis=-1, keepdims=True)
    return x * lax.rsqrt(ms + NORM_EPS) * g


def _params(*sem):
    return pltpu.CompilerParams(dimension_semantics=sem, vmem_limit_bytes=VMEM_LIMIT)


def _const_spec(shape):
    nd = len(shape)
    return pl.BlockSpec(shape, lambda *_: (0,) * nd, pipeline_mode=pl.Buffered(1))


def _ada_kernel(c_ref, w_ref, b_ref, o_ref):
    c = c_ref[...]
    s = c * jax.nn.sigmoid(c)
    o_ref[...] = _dot(s.astype(BF16), w_ref[...].astype(BF16)) + b_ref[...]


def _ada_mod(cc, w_ada, b_ada):
    rows, d = cc.shape
    n = w_ada.shape[1]
    return pl.pallas_call(
        _ada_kernel,
        grid=(n // d,),
        in_specs=[pl.BlockSpec((rows, d), lambda j: (0, 0)),
                  pl.BlockSpec((d, d), lambda j: (0, j)),
                  pl.BlockSpec((1, d), lambda j: (0, j))],
        out_specs=pl.BlockSpec((rows, d), lambda j: (0, j)),
        out_shape=jax.ShapeDtypeStruct((rows, n), F32),
        compiler_params=_params("arbitrary"),
        name="ada_mod",
    )(cc, w_ada, b_ada.reshape(1, n))


def _rope(x, cos_t, sin_t):
    w = x.shape[1]
    half = HEAD_DIM // 2
    first = (_iota(x.shape, 1) % HEAD_DIM) < half
    swapped = jnp.where(first, pltpu.roll(x, w - half, 1), pltpu.roll(x, half, 1))
    reps = w // LANES
    c = jnp.concatenate([cos_t] * reps, axis=1)
    s = jnp.concatenate([sin_t] * reps, axis=1)
    return x * c + swapped * s


def _inproj_kernel(*refs, latent, widths):
    if latent:
        (x_ref, mod_ref, g_ref, wq_ref, w_ref, wg_ref, cos_ref, sin_ref,
         q_ref, k_ref, v_ref, rkv_ref, lora_ref, gate_ref) = refs
    else:
        x_ref, mod_ref, g_ref, w_ref, k_ref, v_ref, rkv_ref, lora_ref = refs
    x = x_ref[0]
    h = _rmsnorm(x, g_ref[...]) * (1.0 + mod_ref[0, 1:2, :]) + mod_ref[0, 0:1, :]
    hb = h.astype(BF16)
    off = 0

    def seg(name):
        nonlocal off
        lo = off
        off += widths[name]
        return _dot(hb, w_ref[:, lo:off])

    def dup_heads(t):
        first = _iota(t.shape, 1) < HEAD_DIM
        other = pltpu.roll(t, HEAD_DIM, 1)
        return jnp.concatenate([jnp.where(first, t, other), jnp.where(first, other, t)], axis=1)

    kv = seg("kv")
    k, v = dup_heads(kv[:, :LANES]), dup_heads(kv[:, LANES:])
    if latent:
        cos_t, sin_t = cos_ref[...], sin_ref[...]
        q_ref[0] = (_rope(_dot(hb, wq_ref[...]), cos_t, sin_t) * (LOG2_E * HEAD_DIM ** -0.5)).astype(BF16)
        k = _rope(k, cos_t, sin_t)
    k_ref[0] = k.astype(BF16)
    v_ref[0] = v.astype(BF16)
    rkv_ref[0] = seg("rkv")
    lora_ref[0] = seg("lora")
    if latent:
        gate_ref[0] = jax.nn.sigmoid(_dot(hb, wg_ref[...])).astype(BF16)


def _in_proj(x, mod, mod_row, norm_g, weights, widths, tables, *, latent, tile):
    b, l, d = x.shape
    nt = l // tile
    if latent:
        mod_map = lambda i, t: (i, 0, 0)
    else:
        mod_map = lambda i, t: (mod_row, 0, 0)
    tok = lambda w: pl.BlockSpec((1, tile, w), lambda i, t: (i, t, 0))
    in_specs = [tok(d),
                pl.BlockSpec((1,) + mod.shape[1:], mod_map),
                _const_spec((1, d))] + [_const_spec(w.shape) for w in weights]
    args = [x, mod, norm_g.reshape(1, d), *weights]
    out_specs, out_shape = [], []

    def out(w, dt):
        out_specs.append(tok(w))
        out_shape.append(jax.ShapeDtypeStruct((b, l, w), dt))

    if latent:
        in_specs += [pl.BlockSpec((tile, LANES), lambda i, t: (t, 0))] * 2
        args += list(tables)
        out(widths["q"], BF16)
    out(widths["k"], BF16)
    out(widths["v"], BF16)
    out(widths["rkv"], F32)
    out(widths["lora"], F32)
    if latent:
        out(widths["gate"], BF16)
    return pl.pallas_call(
        functools.partial(_inproj_kernel, latent=latent, widths=widths),
        grid=(b, nt),
        in_specs=in_specs,
        out_specs=out_specs,
        out_shape=out_shape,
        compiler_params=_params("parallel", "parallel"),
        name="in_proj_latent" if latent else "in_proj_context",
    )(*args)


def _attn_kernel(sink_ref, q_ref, kp_ref, kc_ref, kn_ref, vp_ref, vc_ref, vn_ref,
                 kx_ref, vx_ref, o_ref, *, n_kv):
    i = pl.program_id(1)
    last = pl.num_programs(1) - 1
    blk = ATTN_BLOCK
    qi = _iota((blk, blk), 0)
    kj = _iota((blk, blk), 1)
    left = _iota((blk, LANES), 1) < HEAD_DIM

    def key_block(refs, j, gs):
        ref_p, ref_c, ref_n = refs[:3]
        if j < 0:
            return ref_p[0, :, gs]
        if j >= ATTN_QB:
            return ref_n[0, :, gs]
        return ref_c[0, j * blk:(j + 1) * blk, gs]

    items = []
    for qb in range(ATTN_QB):
        lo_ok = kj >= qi
        hi_ok = kj <= qi
        if qb == 0:
            lo_ok = lo_ok & (i > 0)
        if qb == ATTN_QB - 1:
            hi_ok = hi_ok & (i < last)
        bias_lo = jnp.concatenate([jnp.where(lo_ok, 0.0, NEG)] * Q_PER_KV, axis=0)
        bias_hi = jnp.concatenate([jnp.where(hi_ok, 0.0, NEG)] * Q_PER_KV, axis=0)
        rows = slice(qb * blk, (qb + 1) * blk)
        q = q_ref[0, rows, :].astype(F32)
        for g in range(n_kv):
            gs = slice(g * LANES, (g + 1) * LANES)
            heads = range(g * Q_PER_KV, (g + 1) * Q_PER_KV)
            qs, sinks = [], []
            for hd in heads:
                qp = q[:, (hd // PAIR) * LANES:(hd // PAIR + 1) * LANES]
                keep = left if hd % PAIR == 0 else jnp.logical_not(left)
                qs.append(jnp.where(keep, qp, 0.0).astype(BF16))
                sinks.append(jnp.full((blk, 1), sink_ref[hd] * LOG2_E, F32))
            items.append({
                "qb": qb, "gs": gs, "rows": rows, "heads": heads, "bias": (bias_lo, bias_hi),
                "qs": jnp.concatenate(qs, axis=0), "sink": jnp.concatenate(sinks, axis=0)})

    def cat(refs, it):
        return jnp.concatenate([key_block(refs, it["qb"] + j, it["gs"]) for j in (-1, 0, 1)]
                               + [refs[3][0, :, it["gs"]]], axis=0)

    s = [_dot_nt(it["qs"], cat((kp_ref, kc_ref, kn_ref, kx_ref), it)) for it in items]
    s = [jnp.concatenate([x[:, :blk] + it["bias"][0], x[:, blk:2 * blk], x[:, 2 * blk:3 * blk] + it["bias"][1],
                          x[:, 3 * blk:]], axis=1) for x, it in zip(s, items)]
    m = [jnp.maximum(jnp.max(x, axis=1, keepdims=True), it["sink"]) for x, it in zip(s, items)]
    p = [jnp.exp2(x - mx) for x, mx in zip(s, m)]
    den = [jnp.sum(x, axis=1, keepdims=True) + jnp.exp2(it["sink"] - mx) for x, mx, it in zip(p, m, items)]
    o = [_dot(x.astype(BF16), cat((vp_ref, vc_ref, vn_ref, vx_ref), it)) / dn for x, dn, it in zip(p, den, items)]
    for x, it in zip(o, items):
        for hd in it["heads"][::PAIR]:
            j = hd - it["heads"][0]
            pair = jnp.where(left, x[j * blk:(j + 1) * blk], x[(j + 1) * blk:(j + 2) * blk])
            col = (hd // PAIR) * LANES
            o_ref[0, it["rows"], col:col + LANES] = pair.astype(BF16)


def _attention(sink, q, kd, vd, kxd, vxd):
    b, l, wq = q.shape
    wk = kd.shape[2]
    lc = kxd.shape[1]
    nb = l // ATTN_BLOCK
    span = ATTN_QB * ATTN_BLOCK
    blk = lambda w, f: pl.BlockSpec((1, ATTN_BLOCK, w), f)
    prev = lambda bi, i: (bi, jnp.maximum(i * ATTN_QB - 1, 0), 0)
    nxt = lambda bi, i: (bi, jnp.minimum((i + 1) * ATTN_QB, nb - 1), 0)
    cur = lambda w: pl.BlockSpec((1, span, w), lambda bi, i: (bi, i, 0))
    ctx = pl.BlockSpec((1, lc, wk), lambda bi, i: (bi, 0, 0))
    return pl.pallas_call(
        functools.partial(_attn_kernel, n_kv=wk // LANES),
        grid=(b, l // span),
        in_specs=[pl.BlockSpec(memory_space=pltpu.SMEM),
                  cur(wq),
                  blk(wk, prev), cur(wk), blk(wk, nxt),
                  blk(wk, prev), cur(wk), blk(wk, nxt),
                  ctx, ctx],
        out_specs=cur(wq),
        out_shape=jax.ShapeDtypeStruct((b, l, wq), BF16),
        compiler_params=_params("parallel", "parallel"),
        name="attention",
    )(sink, q, kd, kd, kd, vd, vd, vd, kxd, vxd)


def _conv3(x, prev_row, next_row, w):
    n = x.shape[0]
    row = _iota((8, x.shape[1]), 0)
    xm = pltpu.roll(x, 1, 0)
    xm = jnp.concatenate([jnp.where(row == 0, prev_row, xm[:8]), xm[8:]], axis=0)
    xp = pltpu.roll(x, n - 1, 0)
    xp = jnp.concatenate([xp[:n - 8], jnp.where(row == 7, next_row, xp[n - 8:])], axis=0)
    return xm * w[0:1] + x * w[1:2] + xp * w[2:3]


def _pair_masks():
    n = 2 * CHUNK
    row = _iota((n, n), 0)
    lane = _iota((n, n), 1)
    top, left = row < CHUNK, lane < CHUNK
    return {"row": row % CHUNK, "lane": lane % CHUNK, "top": top, "left": left, "bd": top == left,
            "left_h": _iota((CHUNK, n), 1) < CHUNK}


def _stack(a, b):
    return jnp.concatenate([a, b], axis=0)


def _fold(x, m):
    return jnp.where(m["left_h"], x[:CHUNK], x[CHUNK:])


def _chunk_local(inst, m):
    bf = lambda x: x.astype(BF16)
    diag = jnp.logical_not(m["top"]) & (m["lane"] == m["row"])
    masks = ((m["lane"] < m["row"]) | diag, (m["lane"] > m["row"]) | diag)
    mask_a = [masks[i["rev"]] for i in inst]
    lh = m["left_h"]
    half = CHUNK // 2
    keep_l = jnp.where(m["left"], 1.0, 0.0).astype(BF16)
    keep_r = jnp.where(m["left"], 0.0, 1.0).astype(BF16)
    keep_lh = jnp.where(lh, 1.0, 0.0).astype(BF16)
    keep_rh = jnp.where(lh, 0.0, 1.0).astype(BF16)

    def unfold(xb, anti=False):
        a, b = xb * keep_lh, xb * keep_rh
        return _stack(b, a) if anti else _stack(a, b)

    lhs = [bf(_stack(i["at"], i["rt"])) for i in inst]
    a01 = [_dot_nt(l, _stack(bf(_stack(i["bt"], i["kt"])) * keep_l, bf(_stack(i["kt"], i["bt"])) * keep_r))
           for l, i in zip(lhs, inst)]
    a0 = [jnp.where(ma, a[:, :LANES], 0.0) for ma, a in zip(mask_a, a01)]
    a1 = [jnp.where(ma, a[:, LANES:], 0.0) for ma, a in zip(mask_a, a01)]
    nc = [jnp.where(lh, x[:CHUNK], y[:CHUNK]) for x, y in zip(a0, a1)]
    arb = [jnp.where(lh, x[CHUNK:], y[CHUNK:]) for x, y in zip(a0, a1)]
    ak_ark_sw = [bf(jnp.where(m["left"], y, x)) for x, y in zip(a0, a1)]
    vh = [_dot(a, unfold(bf(i["v"]), anti=True)) for a, i in zip(ak_ark_sw, inst)]
    eye = jnp.where(_iota((CHUNK, LANES), 1) % CHUNK == _iota((CHUNK, LANES), 0), 1.0, 0.0)
    tc = [eye + n for n in nc]
    ncb = [bf(n) for n in nc]
    nc = [_dot(n, unfold(n)) for n in ncb]
    steps = CHUNK.bit_length() - 1
    for _ in range(steps - 2):
        ncb = [bf(n) for n in nc]
        both = [_dot(n, jnp.concatenate([unfold(bf(t)), unfold(n)], axis=1)) for n, t in zip(ncb, tc)]
        tc = [t + r[:, :LANES] for t, r in zip(tc, both)]
        nc = [r[:, LANES:] for r in both]
    inc = [_dot(bf(n[half:]), unfold(bf(t))) for n, t in zip(nc, tc)]
    tc = [jnp.concatenate([t[:half], t[half:] + d], axis=0) for t, d in zip(tc, inc)]
    pq = [_dot(bf(t), jnp.concatenate([unfold(l[:CHUNK]), unfold(bf(x[:CHUNK]))], axis=1))
          for t, l, x in zip(tc, lhs, vh)]
    return [{"pm": r[:, :LANES], "qm": r[:, LANES:], "arb": b, "hm": x[CHUNK:]} for r, b, x in zip(pq, arb, vh)]


def _prep_kernel(*refs, latent, width):
    (rkv_ref, rkv_p, rkv_n, lora_ref, lora_p, lora_n, cw_ref, cwl_ref, kk_ref, ka_ref,
     w0_ref, a0_ref, wl_ref, rk_ref) = refs[:14]
    outs = refs[14:]
    out_refs = dict(zip(("pm", "qm", "rt", "bt", "arb", "kt"), outs[:6]))
    wc_ref, v_ref = outs[6:8]
    if latent:
        bonus_ref, g_ref, hsum_ref = outs[8:]
    t = pl.program_id(1)
    nt = pl.num_programs(1)
    tile = rkv_ref.shape[1]
    w = width
    has_prev = (t > 0).astype(F32)
    has_next = (t < nt - 1).astype(F32)
    u = _conv3(rkv_ref[0], rkv_p[0, 7:8, :] * has_prev, rkv_n[0, 0:1, :] * has_next, cw_ref[...])
    ul = _conv3(lora_ref[0], lora_p[0, 7:8, :] * has_prev, lora_n[0, 0:1, :] * has_next, cwl_ref[...])
    r, k, v = u[:, :w], u[:, w:2 * w], u[:, 2 * w:]
    ones_bd = _head_ones(min(MXU_DIM, w))

    kk = k * kk_ref[...]
    kk = kk * lax.rsqrt(jnp.maximum(_head_sum(kk * kk, ones_bd), 1e-24))

    lane = _iota(ul.shape, 1)
    lin = jnp.where(lane < DECAY_LORA, jnp.tanh(ul),
                    jnp.where(lane < DECAY_LORA + ICLR_LORA, ul, jax.nn.sigmoid(ul)))
    proj = _dot(lin.astype(BF16), wl_ref[...])

    tr = _iota((tile, tile), 0)
    tc = _iota((tile, tile), 1)
    same = (tr // CHUNK) == (tc // CHUNK)
    tri = (jnp.where(same & (tc <= tr), 1.0, 0.0).astype(BF16),
           jnp.where(same & (tc >= tr), 1.0, 0.0).astype(BF16))

    def exact_dot(m, x):
        h1 = x.astype(BF16)
        h2 = (x - h1.astype(F32)).astype(BF16)
        return _dot(m, h1) + _dot(m, h2)

    masks = _pair_masks()
    k_sum = None
    work = []
    for d in range(2):
        z = w0_ref[d:d + 1, :] + proj[:, d * w:(d + 1) * w]
        lw = -math.exp(-0.5) * jax.nn.sigmoid(z)
        a = jax.nn.sigmoid(a0_ref[d:d + 1, :] + proj[:, (2 + d) * w:(3 + d) * w])
        kd = k * (1.0 + (a - 1.0) * ka_ref[...])
        k_sum = kd if k_sum is None else k_sum + kd
        cum = exact_dot(tri[d], lw)
        e_neg = jnp.exp(-cum)
        full = {"at": -kk * jnp.exp(cum - lw), "rt": r * jnp.exp(cum), "bt": kk * a * e_neg,
                "kt": kd * e_neg, "v": v}
        for name in ("rt", "bt", "kt"):
            out_refs[name][d, 0] = full[name].astype(BF16)
        for j in range(tile // CHUNK):
            edge = (j + 1) * CHUNK - 1 if d == 0 else j * CHUNK
            wc_ref[0, j, d:d + 1, :] = jnp.exp(cum[edge:edge + 1, :])
        work += [(d, slice(j * CHUNK, (j + 1) * CHUNK), slice(p * LANES, (p + 1) * LANES), full)
                 for j in range(tile // CHUNK) for p in range(w // LANES)]
    hsum = {}
    for g0 in range(0, len(work), PREP_GROUP):
        group = work[g0:g0 + PREP_GROUP]
        inst = [dict({name: val[rs, ls] for name, val in full.items()}, rev=d) for d, rs, ls, full in group]
        for (d, rs, ls, _), res in zip(group, _chunk_local(inst, masks)):
            seen = hsum.get((rs.start, ls.start))
            hsum[(rs.start, ls.start)] = (rs, ls, res["hm"] if seen is None else seen[2] + res["hm"])
            for name in res:
                if name in out_refs:
                    out_refs[name][d, 0, rs, ls] = res[name].astype(BF16)
    v_ref[0] = v.astype(BF16)
    if latent:
        for rs, ls, val in hsum.values():
            hsum_ref[0, rs, ls] = val
        bonus_ref[0] = _head_sum(r * k_sum * rk_ref[...], ones_bd) * v
        g_ref[0] = proj[:, 4 * w:5 * w].astype(BF16)


def _wkv_prep(rkv, lora, prm, *, latent, tile):
    b, l, w3 = rkv.shape
    w = w3 // 3
    nt = l // tile
    n8 = l // 8
    tok = lambda wd: pl.BlockSpec((1, tile, wd), lambda i, t: (i, t, 0))
    prev = lambda wd: pl.BlockSpec((1, 8, wd), lambda i, t: (i, jnp.maximum(t * (tile // 8) - 1, 0), 0))
    nxt = lambda wd: pl.BlockSpec((1, 8, wd), lambda i, t: (i, jnp.minimum((t + 1) * (tile // 8), n8 - 1), 0))
    wl = lora.shape[2]
    in_specs = [tok(w3), prev(w3), nxt(w3), tok(wl), prev(wl), nxt(wl)]
    consts = [prm["conv_rkv"], prm["conv_lora"], prm["k_k"], prm["k_a"], prm["decay_w0"],
              prm["iclr_a0"], prm["lora_w"], prm["r_k"]]
    in_specs += [_const_spec(c.shape) for c in consts]
    dirtok = pl.BlockSpec((2, 1, tile, w), lambda i, t: (0, i, t, 0))
    out_specs = [dirtok] * 6 + [pl.BlockSpec((1, tile // CHUNK, 2, w), lambda i, t: (i, t, 0, 0)), tok(w)]
    out_shape = [jax.ShapeDtypeStruct((2, b, l, w), BF16)] * 6 + [
        jax.ShapeDtypeStruct((b, l // CHUNK, 2, w), F32), jax.ShapeDtypeStruct((b, l, w), BF16)]
    if latent:
        out_specs += [tok(w), tok(w), tok(w)]
        out_shape += [jax.ShapeDtypeStruct((b, l, w), F32), jax.ShapeDtypeStruct((b, l, w), BF16),
                      jax.ShapeDtypeStruct((b, l, w), F32)]
    return pl.pallas_call(
        functools.partial(_prep_kernel, latent=latent, width=w),
        grid=(b, nt),
        in_specs=in_specs,
        out_specs=out_specs,
        out_shape=out_shape,
        compiler_params=_params("parallel", "parallel"),
        name="wkv_prep_latent" if latent else "wkv_prep_context",
    )(rkv, rkv, rkv, lora, lora, lora, *consts)


def _wkv_kernel(*refs, emit_y, n_pairs):
    names = ("pm", "qm", "rt", "bt", "arb", "kt", "wc", "v")
    n = len(names)
    dir_refs = (dict(zip(names, refs[0:n])), dict(zip(names, refs[n:2 * n])))
    z0_ref = refs[2 * n]
    if emit_y:
        y_refs = refs[2 * n + 1:2 * n + 3]
        z_scr = refs[2 * n + 3]
    else:
        zfin_ref = refs[2 * n + 1]
        z_scr = refs[2 * n + 2]
    c = pl.program_id(0)

    @pl.when(c == 0)
    def _():
        z_scr[...] = z0_ref[...]

    left_h = _iota((CHUNK, LANES), 1) < CHUNK
    keep_l = jnp.where(left_h, 1.0, 0.0).astype(BF16)
    keep_r = jnp.where(left_h, 0.0, 1.0).astype(BF16)

    def unfold(xb):
        return _stack(xb * keep_l, xb * keep_r)

    tiles = [(i, d, p, slice(p * LANES, (p + 1) * LANES))
             for i in range(z_scr.shape[0]) for d in range(2) for p in range(n_pairs)]
    state = [z_scr[i, d, p] for i, d, p, _ in tiles]
    n_sub = dir_refs[0]["wc"].shape[1]
    for step in range(n_sub):
        sub = (step, n_sub - 1 - step)
        rows = [slice(sub[d] * CHUNK, (sub[d] + 1) * CHUNK) for d in range(2)]
        ld = lambda name: [dir_refs[d][name][0, i, rows[d], sl] for i, d, _, sl in tiles]
        v = [dir_refs[d]["v"][i, rows[d], sl] for i, d, _, sl in tiles]
        sbd = [unfold(s.astype(BF16)) for s in state]
        if emit_y:
            ur = [_dot_nt(_stack(pm, rt), s) for pm, rt, s in zip(ld("pm"), ld("rt"), sbd)]
        else:
            ur = [_dot_nt(pm, s) for pm, s in zip(ld("pm"), sbd)]
        u = [x[:CHUNK] + q.astype(F32) for x, q in zip(ur, ld("qm"))]
        ub = [ui.astype(BF16) for ui in u]
        inc = [_dot_tn(_stack(ui, vi), _stack(bt, kt)) for ui, vi, bt, kt in zip(ub, v, ld("bt"), ld("kt"))]
        inc = [jnp.where(left_h, x[:CHUNK], x[CHUNK:]) for x in inc]
        if emit_y:
            yc = [_dot(a, unfold(ui)) for a, ui in zip(ld("arb"), ub)]
            for (i, d, _, sl), x, ys in zip(tiles, ur, yc):
                y_refs[d][i, rows[d], sl] = (x[CHUNK:] + ys).astype(BF16)
        state = [(s + dz) * dir_refs[d]["wc"][i, sub[d], d:d + 1, sl]
                 for (i, d, _, sl), s, dz in zip(tiles, state, inc)]
    for (i, d, p, _), s in zip(tiles, state):
        z_scr[i, d, p] = s

    if not emit_y:
        @pl.when(c == pl.num_programs(0) - 1)
        def _():
            zfin_ref[...] = z_scr[...]


def _wkv_scan(prep, z0, *, emit_y):
    wc = prep[6]
    _, b, l, w = prep[0].shape
    n_sub = min(SCAN_CHUNKS, l // CHUNK)
    blk = n_sub * CHUNK
    nc = l // blk
    n_pairs = w // LANES
    fwd = lambda c: c
    rev = lambda c: nc - 1 - c
    in_specs, args = [], []
    for d, cm in enumerate((fwd, rev)):
        for arr in prep[:6]:
            in_specs.append(pl.BlockSpec((1, b, blk, w), lambda c, d=d, cm=cm: (d, 0, cm(c), 0)))
            args.append(arr)
        in_specs.append(pl.BlockSpec((b, n_sub, 2, w), lambda c, cm=cm: (0, cm(c), 0, 0)))
        args.append(wc)
        in_specs.append(pl.BlockSpec((b, blk, w), lambda c, cm=cm: (0, cm(c), 0)))
        args.append(prep[7])
    zshape = (b, 2, n_pairs, CHUNK, LANES)
    zspec = pl.BlockSpec(zshape, lambda c: (0, 0, 0, 0, 0))
    in_specs.append(zspec)
    args.append(z0)
    if emit_y:
        out_specs = [pl.BlockSpec((b, blk, w), lambda c: (0, c, 0)),
                     pl.BlockSpec((b, blk, w), lambda c: (0, nc - 1 - c, 0))]
        out_shape = [jax.ShapeDtypeStruct((b, l, w), BF16)] * 2
    else:
        out_specs = zspec
        out_shape = jax.ShapeDtypeStruct(zshape, F32)
    return pl.pallas_call(
        functools.partial(_wkv_kernel, emit_y=emit_y, n_pairs=n_pairs),
        grid=(nc,),
        in_specs=in_specs,
        out_specs=out_specs,
        out_shape=out_shape,
        scratch_shapes=[pltpu.VMEM(zshape, F32)],
        compiler_params=_params("arbitrary"),
        name="wkv_scan_latent" if emit_y else "wkv_scan_context",
    )(*args)


def _merge_kernel(x_ref, mod_ref, ya_ref, yf_ref, yr_ref, yh_ref, bonus_ref, g_ref, gate_ref,
                  lnw_ref, lnb_ref, wba_ref, wbr_ref, wo_ref, n2_ref, wu_ref, wd_ref, nf_ref,
                  o_ref, *, ff_chunk):
    x = x_ref[0]
    d = x.shape[1]
    mod = lambda j: mod_ref[0, j:j + 1, :]
    y = yf_ref[0].astype(F32) + yr_ref[0].astype(F32) + yh_ref[0]
    gw = min(MXU_DIM, y.shape[1])
    ones_bd = _head_ones(gw)

    def head_mean(t):
        tb = t.astype(BF16)
        parts = [_dot(tb[:, j:j + gw], ones_bd) for j in range(0, t.shape[1], gw)]
        return jnp.concatenate(parts, axis=1) * (1.0 / HEAD_DIM)

    mu = head_mean(y)
    yc = y - mu
    var = head_mean(yc * yc)
    yn = yc * lax.rsqrt(var + LNX_EPS)
    yr = (yn * lnw_ref[...] + lnb_ref[...] + bonus_ref[0]) * g_ref[0].astype(F32)
    gate = gate_ref[0].astype(F32)
    merged = gate[:, :d] * _dot(ya_ref[0], wba_ref[...]) + gate[:, d:] * _dot(yr.astype(BF16), wbr_ref[...])
    x1 = x + mod(2) * _dot(merged.astype(BF16), wo_ref[...])
    h2 = (_rmsnorm(x1, n2_ref[...]) * (1.0 + mod(4)) + mod(3)).astype(BF16)
    acc = jnp.zeros_like(x1)
    for j in range(wu_ref.shape[1] // ff_chunk):
        cs = slice(j * ff_chunk, (j + 1) * ff_chunk)
        up = jnp.maximum(_dot(h2, wu_ref[:, cs]), 0.0)
        acc = acc + _dot((up * up).astype(BF16), wd_ref[cs, :])
    x2 = x1 + mod(5) * acc
    o_ref[0] = _rmsnorm(x2, nf_ref[...])


def _merge_mlp(x, mod, ya, yf, yr, yh, bonus, g, gate, prm, *, tile):
    b, l, d = x.shape
    tok = lambda arr: pl.BlockSpec((1, tile, arr.shape[2]), lambda i, t: (i, t, 0))
    consts = [prm["lnx_w"], prm["lnx_b"], prm["w_branch_attn"], prm["w_branch_rwkv"], prm["w_out"],
              prm["norm2_g"], prm["w_mlp_up"], prm["w_mlp_down"], prm["norm_f_g"]]
    toks = [ya, yf, yr, yh, bonus, g, gate]
    return pl.pallas_call(
        functools.partial(_merge_kernel, ff_chunk=min(1024, prm["w_mlp_up"].shape[1])),
        grid=(b, l // tile),
        in_specs=[tok(x), pl.BlockSpec((1,) + mod.shape[1:], lambda i, t: (i, 0, 0))]
        + [tok(a) for a in toks] + [_const_spec(c.shape) for c in consts],
        out_specs=tok(x),
        out_shape=jax.ShapeDtypeStruct(x.shape, x.dtype),
        compiler_params=_params("parallel", "parallel"),
        name="merge_mlp",
    )(x, mod, *toks, *consts)


def _rope_tables(l):
    n_freq = HEAD_DIM // 4
    inv_freq = jnp.power(ROPE_BASE, -jnp.arange(n_freq, dtype=F32) / n_freq)
    rows = l // GRID_W
    row = jnp.repeat(jnp.arange(rows, dtype=F32), GRID_W)
    col = jnp.tile(jnp.arange(GRID_W, dtype=F32), rows)
    ang = jnp.concatenate([row[:, None] * inv_freq, col[:, None] * inv_freq], axis=-1)
    cos, sin = jnp.cos(ang), jnp.sin(ang)
    reps = LANES // HEAD_DIM
    return (jnp.tile(jnp.concatenate([cos, cos], axis=1), (1, reps)),
            jnp.tile(jnp.concatenate([-sin, sin], axis=1), (1, reps)))


def _pad_cols(w, width):
    return jnp.pad(w, ((0, 0), (0, width - w.shape[1])))


def kernel(x, c, ctx, c_ctx, w_ada, b_ada, norm1_g, w_in, sink, conv_w, decay_w0, decay_w2, iclr_a0, iclr_a2, gate_g2, k_k, k_a, r_k, lnx_w, lnx_b, w_branch_attn, w_branch_rwkv, w_out, norm2_g, w_mlp_up, w_mlp_down, norm_f_g):
    assert w_in.shape[0] == 1, "single-layer block: context tokens are read, never updated"
    b, l, d = x.shape
    attn_w = w_branch_attn.shape[1]
    rw = w_branch_rwkv.shape[1]
    n_q = attn_w // HEAD_DIM
    n_kv = n_q // Q_PER_KV
    kv_w = n_kv * HEAD_DIM
    assert kv_w == LANES and rw % LANES == 0 and ctx.shape[1] % 256 == 0
    assert l % (ATTN_QB * ATTN_BLOCK) == 0 and l % 512 == 0

    w = w_in[0]
    o_k, o_r = attn_w, attn_w + 2 * kv_w
    o_l = o_r + 3 * rw
    o_g = o_l + DECAY_LORA + ICLR_LORA + GATE_LORA
    w_ctx = _pad_cols(w[:, o_k:o_g], o_l - o_k + LORA_PAD).astype(BF16)
    w_lat = [w[:, :o_k].astype(BF16), w_ctx, w[:, o_g:].astype(BF16)]
    widths_ctx = {"kv": 2 * kv_w, "k": PAIR * kv_w, "v": PAIR * kv_w, "rkv": 3 * rw, "lora": LORA_PAD}
    widths_lat = {"q": attn_w, **widths_ctx, "gate": 2 * d}

    cw = conv_w[0]
    lora_w = jnp.zeros((LORA_PAD, 5 * rw), F32)
    lora_w = lora_w.at[:DECAY_LORA, :rw].set(decay_w2[0, 0]).at[:DECAY_LORA, rw:2 * rw].set(decay_w2[0, 1])
    r1 = DECAY_LORA + ICLR_LORA
    lora_w = lora_w.at[DECAY_LORA:r1, 2 * rw:3 * rw].set(iclr_a2[0, 0]).at[DECAY_LORA:r1, 3 * rw:4 * rw].set(iclr_a2[0, 1])
    lora_w = lora_w.at[r1:r1 + GATE_LORA, 4 * rw:].set(gate_g2[0])
    prm = {
        "conv_rkv": cw[:, :3 * rw], "conv_lora": _pad_cols(cw[:, 3 * rw:], LORA_PAD),
        "k_k": k_k[0].reshape(1, rw), "k_a": k_a[0].reshape(1, rw),
        "decay_w0": decay_w0[0], "iclr_a0": iclr_a0[0], "lora_w": lora_w.astype(BF16),
        "r_k": r_k[0].reshape(1, rw),
        "lnx_w": lnx_w[0].reshape(1, rw), "lnx_b": lnx_b[0].reshape(1, rw),
        "w_branch_attn": w_branch_attn[0].astype(BF16), "w_branch_rwkv": w_branch_rwkv[0].astype(BF16),
        "w_out": w_out[0].astype(BF16), "norm2_g": norm2_g[0].reshape(1, d),
        "w_mlp_up": w_mlp_up[0].astype(BF16), "w_mlp_down": w_mlp_down[0].astype(BF16),
        "norm_f_g": norm_f_g.reshape(1, d),
    }

    rows = -(-(b + 1) // 8) * 8
    cc = jnp.zeros((rows, d), F32).at[:b].set(c).at[b].set(c_ctx)
    mod = _ada_mod(cc, w_ada[0], b_ada[0]).reshape(rows, -1, d)

    q, kd, vd, rkv, lora, gate = _in_proj(x, mod, b, norm1_g[0], w_lat, widths_lat, _rope_tables(l),
                                          latent=True, tile=512)
    kxd, vxd, rkv_c, lora_c = _in_proj(ctx, mod, b, norm1_g[0], [w_ctx], widths_ctx, None,
                                       latent=False, tile=256)
    ya = _attention(sink[0], q, kd, vd, kxd, vxd)

    prep_c = _wkv_prep(rkv_c, lora_c, prm, latent=False, tile=256)
    z_ctx = _wkv_scan(prep_c, jnp.zeros((b, 2, rw // LANES, CHUNK, LANES), F32), emit_y=False)
    prep = _wkv_prep(rkv, lora, prm, latent=True, tile=512)
    yf, yr = _wkv_scan(prep, z_ctx, emit_y=True)
    bonus, g, yh = prep[8], prep[9], prep[10]

    return _merge_mlp(x, mod, ya, yf, yr, yh, bonus, g, gate, prm, tile=512)
```

```python
import functools
import math

import jax
import jax.numpy as jnp
from jax import lax
from jax.experimental import pallas as pl
from jax.experimental.pallas import tpu as pltpu

F32 = jnp.float32
BF16 = jnp.bfloat16

GRID_W = 64
HEAD_DIM = 64
Q_PER_KV = 4
ATTN_BLOCK = 128
ATTN_QB = 8
ROPE_BASE = 10000.0
NORM_EPS = 1e-6
LNX_EPS = 1e-5 * HEAD_DIM
DECAY_LORA, ICLR_LORA, GATE_LORA = 32, 32, 96
LORA_PAD = 256
CHUNK = 64
SCAN_CHUNKS = 4
PREP_GROUP = 16
LANES = 128
MXU_DIM = 256
PAIR = LANES // HEAD_DIM
NEG = -1e30
LOG2_E = math.log2(math.e)
VMEM_LIMIT = 56 * 1024 * 1024


def _dot(a, b):
    return jnp.dot(a, b, preferred_element_type=F32)


def _dot_nt(a, b):
    return lax.dot_general(a, b, (((1,), (1,)), ((), ())), preferred_element_type=F32)


def _dot_tn(a, b):
    return lax.dot_general(a, b, (((0,), (0,)), ((), ())), preferred_element_type=F32)


def _iota(shape, dim):
    return lax.broadcasted_iota(jnp.int32, shape, dim)


def _head_ones(width):
    r = _iota((width, width), 0) // HEAD_DIM
    c = _iota((width, width), 1) // HEAD_DIM
    return jnp.where(r == c, 1.0, 0.0).astype(BF16)


def _head_sum(x, ones_bd):
    gw = ones_bd.shape[0]
    hi = x.astype(BF16)
    lo = (x - hi.astype(F32)).astype(BF16)
    parts = [_dot(hi[:, j:j + gw], ones_bd) + _dot(lo[:, j:j + gw], ones_bd) for j in range(0, x.shape[1], gw)]
    return jnp.concatenate(parts, axis=1)


def _rmsnorm(x, g):
    ms = jnp.mean(x * x, axis=-1, keepdims=True)
    return x * lax.rsqrt(ms + NORM_EPS) * g


def _params(*sem):
    return pltpu.CompilerParams(dimension_semantics=sem, vmem_limit_bytes=VMEM_LIMIT)


def _const_spec(shape):
    nd = len(shape)
    return pl.BlockSpec(shape, lambda *_: (0,) * nd, pipeline_mode=pl.Buffered(1))


def _ada_kernel(c_ref, w_ref, b_ref, o_ref):
    c = c_ref[...]
    s = c * jax.nn.sigmoid(c)
    o_ref[...] = _dot(s.astype(BF16), w_ref[...].astype(BF16)) + b_ref[...]


def _ada_mod(cc, w_ada, b_ada):
    rows, d = cc.shape
    n = w_ada.shape[1]
    return pl.pallas_call(
        _ada_kernel,
        grid=(n // d,),
        in_specs=[pl.BlockSpec((rows, d), lambda j: (0, 0)),
                  pl.BlockSpec((d, d), lambda j: (0, j)),
                  pl.BlockSpec((1, d), lambda j: (0, j))],
        out_specs=pl.BlockSpec((rows, d), lambda j: (0, j)),
        out_shape=jax.ShapeDtypeStruct((rows, n), F32),
        compiler_params=_params("arbitrary"),
        name="ada_mod",
    )(cc, w_ada, b_ada.reshape(1, n))


def _rope(x, cos_t, sin_t):
    w = x.shape[1]
    half = HEAD_DIM // 2
    first = (_iota(x.shape, 1) % HEAD_DIM) < half
    swapped = jnp.where(first, pltpu.roll(x, w - half, 1), pltpu.roll(x, half, 1))
    reps = w // LANES
    c = jnp.concatenate([cos_t] * reps, axis=1)
    s = jnp.concatenate([sin_t] * reps, axis=1)
    return x * c + swapped * s


def _inproj_kernel(*refs, latent, widths):
    if latent:
        (x_ref, mod_ref, g_ref, w_ref, wg_ref, cos_ref, sin_ref,
         q_ref, k_ref, v_ref, rkv_ref, lora_ref, gate_ref) = refs
    else:
        x_ref, mod_ref, g_ref, w_ref, k_ref, v_ref, rkv_ref, lora_ref = refs
    x = x_ref[0]
    h = _rmsnorm(x, g_ref[...]) * (1.0 + mod_ref[0, 1:2, :]) + mod_ref[0, 0:1, :]
    hb = h.astype(BF16)
    off = widths["q"]

    def seg(name):
        nonlocal off
        lo = off
        off += widths[name]
        return _dot(hb, w_ref[:, lo:off])

    def dup_heads(t):
        first = _iota(t.shape, 1) < HEAD_DIM
        other = pltpu.roll(t, HEAD_DIM, 1)
        return jnp.concatenate([jnp.where(first, t, other), jnp.where(first, other, t)], axis=1)

    kv = seg("kv")
    k, v = dup_heads(kv[:, :LANES]), dup_heads(kv[:, LANES:])
    if latent:
        cos_t, sin_t = cos_ref[...], sin_ref[...]
        q = _dot(hb, w_ref[:, :widths["q"]])
        q_ref[0] = (_rope(q, cos_t, sin_t) * (LOG2_E * HEAD_DIM ** -0.5)).astype(BF16)
        k = _rope(k, cos_t, sin_t)
    k_ref[0] = k.astype(BF16)
    v_ref[0] = v.astype(BF16)
    rkv_ref[0] = seg("rkv")
    lora_ref[0] = seg("lora")
    if latent:
        gate_ref[0] = jax.nn.sigmoid(_dot(hb, wg_ref[...])).astype(BF16)


def _in_proj(x, mod, mod_row, norm_g, weights, widths, tables, *, latent, tile):
    b, l, d = x.shape
    nt = l // tile
    if latent:
        mod_map = lambda i, t: (i, 0, 0)
    else:
        mod_map = lambda i, t: (mod_row, 0, 0)
    tok = lambda w: pl.BlockSpec((1, tile, w), lambda i, t: (i, t, 0))
    in_specs = [tok(d),
                pl.BlockSpec((1,) + mod.shape[1:], mod_map),
                _const_spec((1, d))] + [_const_spec(w.shape) for w in weights]
    args = [x, mod, norm_g.reshape(1, d), *weights]
    out_specs, out_shape = [], []

    def out(w, dt):
        out_specs.append(tok(w))
        out_shape.append(jax.ShapeDtypeStruct((b, l, w), dt))

    if latent:
        in_specs += [pl.BlockSpec((tile, LANES), lambda i, t: (t, 0))] * 2
        args += list(tables)
        out(widths["q"], BF16)
    out(widths["k"], BF16)
    out(widths["v"], BF16)
    out(widths["rkv"], F32)
    out(widths["lora"], F32)
    if latent:
        out(widths["gate"], BF16)
    return pl.pallas_call(
        functools.partial(_inproj_kernel, latent=latent, widths=widths),
        grid=(b, nt),
        in_specs=in_specs,
        out_specs=out_specs,
        out_shape=out_shape,
        compiler_params=_params("parallel", "parallel"),
        name="in_proj_latent" if latent else "in_proj_context",
    )(*args)


def _attn_kernel(sink_ref, q_ref, kp_ref, kc_ref, kn_ref, vp_ref, vc_ref, vn_ref,
                 kx_ref, vx_ref, o_ref, *, n_kv):
    i = pl.program_id(1)
    last = pl.num_programs(1) - 1
    blk = ATTN_BLOCK
    qi = _iota((blk, blk), 0)
    kj = _iota((blk, blk), 1)
    left = _iota((blk, LANES), 1) < HEAD_DIM

    def key_block(refs, j, gs):
        ref_p, ref_c, ref_n = refs[:3]
        if j < 0:
            return ref_p[0, :, gs]
        if j >= ATTN_QB:
            return ref_n[0, :, gs]
        return ref_c[0, j * blk:(j + 1) * blk, gs]

    items = []
    for qb in range(ATTN_QB):
        lo_ok = kj >= qi
        hi_ok = kj <= qi
        if qb == 0:
            lo_ok = lo_ok & (i > 0)
        if qb == ATTN_QB - 1:
            hi_ok = hi_ok & (i < last)
        bias_lo = jnp.concatenate([jnp.where(lo_ok, 0.0, NEG)] * Q_PER_KV, axis=0)
        bias_hi = jnp.concatenate([jnp.where(hi_ok, 0.0, NEG)] * Q_PER_KV, axis=0)
        rows = slice(qb * blk, (qb + 1) * blk)
        q = q_ref[0, rows, :].astype(F32)
        for g in range(n_kv):
            gs = slice(g * LANES, (g + 1) * LANES)
            heads = range(g * Q_PER_KV, (g + 1) * Q_PER_KV)
            qs, sinks = [], []
            for hd in heads:
                qp = q[:, (hd // PAIR) * LANES:(hd // PAIR + 1) * LANES]
                keep = left if hd % PAIR == 0 else jnp.logical_not(left)
                qs.append(jnp.where(keep, qp, 0.0).astype(BF16))
                sinks.append(jnp.full((blk, 1), sink_ref[hd] * LOG2_E, F32))
            items.append({
                "qb": qb, "gs": gs, "rows": rows, "heads": heads, "bias": (bias_lo, bias_hi),
                "qs": jnp.concatenate(qs, axis=0), "sink": jnp.concatenate(sinks, axis=0)})

    def cat(refs, it):
        return jnp.concatenate([key_block(refs, it["qb"] + j, it["gs"]) for j in (-1, 0, 1)]
                               + [refs[3][0, :, it["gs"]]], axis=0)

    s = [_dot_nt(it["qs"], cat((kp_ref, kc_ref, kn_ref, kx_ref), it)) for it in items]
    s = [jnp.concatenate([x[:, :blk] + it["bias"][0], x[:, blk:2 * blk], x[:, 2 * blk:3 * blk] + it["bias"][1],
                          x[:, 3 * blk:]], axis=1) for x, it in zip(s, items)]
    m = [jnp.maximum(jnp.max(x, axis=1, keepdims=True), it["sink"]) for x, it in zip(s, items)]
    p = [jnp.exp2(x - mx) for x, mx in zip(s, m)]
    den = [jnp.sum(x, axis=1, keepdims=True) + jnp.exp2(it["sink"] - mx) for x, mx, it in zip(p, m, items)]
    o = [_dot(x.astype(BF16), cat((vp_ref, vc_ref, vn_ref, vx_ref), it)) / dn for x, dn, it in zip(p, den, items)]
    for x, it in zip(o, items):
        for hd in it["heads"][::PAIR]:
            j = hd - it["heads"][0]
            pair = jnp.where(left, x[j * blk:(j + 1) * blk], x[(j + 1) * blk:(j + 2) * blk])
            col = (hd // PAIR) * LANES
            o_ref[0, it["rows"], col:col + LANES] = pair.astype(BF16)


def _attention(sink, q, kd, vd, kxd, vxd):
    b, l, wq = q.shape
    wk = kd.shape[2]
    lc = kxd.shape[1]
    nb = l // ATTN_BLOCK
    span = ATTN_QB * ATTN_BLOCK
    blk = lambda w, f: pl.BlockSpec((1, ATTN_BLOCK, w), f)
    prev = lambda bi, i: (bi, jnp.maximum(i * ATTN_QB - 1, 0), 0)
    nxt = lambda bi, i: (bi, jnp.minimum((i + 1) * ATTN_QB, nb - 1), 0)
    cur = lambda w: pl.BlockSpec((1, span, w), lambda bi, i: (bi, i, 0))
    ctx = pl.BlockSpec((1, lc, wk), lambda bi, i: (bi, 0, 0))
    return pl.pallas_call(
        functools.partial(_attn_kernel, n_kv=wk // LANES),
        grid=(b, l // span),
        in_specs=[pl.BlockSpec(memory_space=pltpu.SMEM),
                  cur(wq),
                  blk(wk, prev), cur(wk), blk(wk, nxt),
                  blk(wk, prev), cur(wk), blk(wk, nxt),
                  ctx, ctx],
        out_specs=cur(wq),
        out_shape=jax.ShapeDtypeStruct((b, l, wq), BF16),
        compiler_params=_params("parallel", "parallel"),
        name="attention",
    )(sink, q, kd, kd, kd, vd, vd, vd, kxd, vxd)


def _conv3(x, prev_row, next_row, w):
    n = x.shape[0]
    row = _iota((8, x.shape[1]), 0)
    xm = pltpu.roll(x, 1, 0)
    xm = jnp.concatenate([jnp.where(row == 0, prev_row, xm[:8]), xm[8:]], axis=0)
    xp = pltpu.roll(x, n - 1, 0)
    xp = jnp.concatenate([xp[:n - 8], jnp.where(row == 7, next_row, xp[n - 8:])], axis=0)
    return xm * w[0:1] + x * w[1:2] + xp * w[2:3]


def _pair_masks():
    n = 2 * CHUNK
    row = _iota((n, n), 0)
    lane = _iota((n, n), 1)
    top, left = row < CHUNK, lane < CHUNK
    return {"row": row % CHUNK, "lane": lane % CHUNK, "top": top, "left": left, "bd": top == left,
            "left_h": _iota((CHUNK, n), 1) < CHUNK}


def _stack(a, b):
    return jnp.concatenate([a, b], axis=0)


def _fold(x, m):
    return jnp.where(m["left_h"], x[:CHUNK], x[CHUNK:])


def _chunk_local(inst, m):
    bf = lambda x: x.astype(BF16)
    diag = jnp.logical_not(m["top"]) & (m["lane"] == m["row"])
    masks = ((m["lane"] < m["row"]) | diag, (m["lane"] > m["row"]) | diag)
    mask_a = [masks[i["rev"]] for i in inst]
    lh = m["left_h"]
    half = CHUNK // 2
    keep_l = jnp.where(m["left"], 1.0, 0.0).astype(BF16)
    keep_r = jnp.where(m["left"], 0.0, 1.0).astype(BF16)
    keep_lh = jnp.where(lh, 1.0, 0.0).astype(BF16)
    keep_rh = jnp.where(lh, 0.0, 1.0).astype(BF16)

    def unfold(xb, anti=False):
        a, b = xb * keep_lh, xb * keep_rh
        return _stack(b, a) if anti else _stack(a, b)

    lhs = [bf(_stack(i["at"], i["rt"])) for i in inst]
    a01 = [_dot_nt(l, _stack(bf(_stack(i["bt"], i["kt"])) * keep_l, bf(_stack(i["kt"], i["bt"])) * keep_r))
           for l, i in zip(lhs, inst)]
    a0 = [jnp.where(ma, a[:, :LANES], 0.0) for ma, a in zip(mask_a, a01)]
    a1 = [jnp.where(ma, a[:, LANES:], 0.0) for ma, a in zip(mask_a, a01)]
    nc = [jnp.where(lh, x[:CHUNK], y[:CHUNK]) for x, y in zip(a0, a1)]
    arb = [jnp.where(lh, x[CHUNK:], y[CHUNK:]) for x, y in zip(a0, a1)]
    ak_ark_sw = [bf(jnp.where(m["left"], y, x)) for x, y in zip(a0, a1)]
    vh = [_dot(a, unfold(bf(i["v"]), anti=True)) for a, i in zip(ak_ark_sw, inst)]
    eye = jnp.where(_iota((CHUNK, LANES), 1) % CHUNK == _iota((CHUNK, LANES), 0), 1.0, 0.0)
    tc = [eye + n for n in nc]
    ncb = [bf(n) for n in nc]
    nc = [_dot(n, unfold(n)) for n in ncb]
    steps = CHUNK.bit_length() - 1
    for _ in range(steps - 2):
        ncb = [bf(n) for n in nc]
        both = [_dot(n, jnp.concatenate([unfold(bf(t)), unfold(n)], axis=1)) for n, t in zip(ncb, tc)]
        tc = [t + r[:, :LANES] for t, r in zip(tc, both)]
        nc = [r[:, LANES:] for r in both]
    inc = [_dot(bf(n[half:]), unfold(bf(t))) for n, t in zip(nc, tc)]
    tc = [jnp.concatenate([t[:half], t[half:] + d], axis=0) for t, d in zip(tc, inc)]
    pq = [_dot(bf(t), jnp.concatenate([unfold(l[:CHUNK]), unfold(bf(x[:CHUNK]))], axis=1))
          for t, l, x in zip(tc, lhs, vh)]
    return [{"pm": r[:, :LANES], "qm": r[:, LANES:], "arb": b, "hm": x[CHUNK:]} for r, b, x in zip(pq, arb, vh)]


def _prep_kernel(*refs, latent, width):
    (rkv_ref, rkv_p, rkv_n, lora_ref, lora_p, lora_n, cw_ref, cwl_ref, kk_ref, ka_ref,
     w0_ref, a0_ref, wl_ref, rk_ref) = refs[:14]
    outs = refs[14:]
    out_refs = dict(zip(("pm", "qm", "rt", "bt", "arb", "kt"), outs[:6]))
    wc_ref, v_ref = outs[6:8]
    if latent:
        bonus_ref, g_ref, hsum_ref = outs[8:]
    t = pl.program_id(1)
    nt = pl.num_programs(1)
    tile = rkv_ref.shape[1]
    w = width
    has_prev = (t > 0).astype(F32)
    has_next = (t < nt - 1).astype(F32)
    u = _conv3(rkv_ref[0], rkv_p[0, 7:8, :] * has_prev, rkv_n[0, 0:1, :] * has_next, cw_ref[...])
    ul = _conv3(lora_ref[0], lora_p[0, 7:8, :] * has_prev, lora_n[0, 0:1, :] * has_next, cwl_ref[...])
    r, k, v = u[:, :w], u[:, w:2 * w], u[:, 2 * w:]
    ones_bd = _head_ones(min(MXU_DIM, w))

    kk = k * kk_ref[...]
    kk = kk * lax.rsqrt(jnp.maximum(_head_sum(kk * kk, ones_bd), 1e-24))

    lane = _iota(ul.shape, 1)
    lin = jnp.where(lane < DECAY_LORA, jnp.tanh(ul),
                    jnp.where(lane < DECAY_LORA + ICLR_LORA, ul, jax.nn.sigmoid(ul)))
    proj = _dot(lin.astype(BF16), wl_ref[...])

    tr = _iota((tile, tile), 0)
    tc = _iota((tile, tile), 1)
    same = (tr // CHUNK) == (tc // CHUNK)
    tri = (jnp.where(same & (tc <= tr), 1.0, 0.0).astype(BF16),
           jnp.where(same & (tc >= tr), 1.0, 0.0).astype(BF16))

    def exact_dot(m, x):
        h1 = x.astype(BF16)
        h2 = (x - h1.astype(F32)).astype(BF16)
        return _dot(m, h1) + _dot(m, h2)

    masks = _pair_masks()
    k_sum = None
    work = []
    for d in range(2):
        z = w0_ref[d:d + 1, :] + proj[:, d * w:(d + 1) * w]
        lw = -math.exp(-0.5) * jax.nn.sigmoid(z)
        a = jax.nn.sigmoid(a0_ref[d:d + 1, :] + proj[:, (2 + d) * w:(3 + d) * w])
        kd = k * (1.0 + (a - 1.0) * ka_ref[...])
        k_sum = kd if k_sum is None else k_sum + kd
        cum = exact_dot(tri[d], lw)
        e_neg = jnp.exp(-cum)
        full = {"at": -kk * jnp.exp(cum - lw), "rt": r * jnp.exp(cum), "bt": kk * a * e_neg,
                "kt": kd * e_neg, "v": v}
        for name in ("rt", "bt", "kt"):
            out_refs[name][d, 0] = full[name].astype(BF16)
        for j in range(tile // CHUNK):
            edge = (j + 1) * CHUNK - 1 if d == 0 else j * CHUNK
            wc_ref[0, j, d:d + 1, :] = jnp.exp(cum[edge:edge + 1, :])
        work += [(d, slice(j * CHUNK, (j + 1) * CHUNK), slice(p * LANES, (p + 1) * LANES), full)
                 for j in range(tile // CHUNK) for p in range(w // LANES)]
    hsum = {}
    for g0 in range(0, len(work), PREP_GROUP):
        group = work[g0:g0 + PREP_GROUP]
        inst = [dict({name: val[rs, ls] for name, val in full.items()}, rev=d) for d, rs, ls, full in group]
        for (d, rs, ls, _), res in zip(group, _chunk_local(inst, masks)):
            seen = hsum.get((rs.start, ls.start))
            hsum[(rs.start, ls.start)] = (rs, ls, res["hm"] if seen is None else seen[2] + res["hm"])
            for name in res:
                if name in out_refs:
                    out_refs[name][d, 0, rs, ls] = res[name].astype(BF16)
    v_ref[0] = v.astype(BF16)
    if latent:
        for rs, ls, val in hsum.values():
            hsum_ref[0, rs, ls] = val
        bonus_ref[0] = _head_sum(r * k_sum * rk_ref[...], ones_bd) * v
        g_ref[0] = proj[:, 4 * w:5 * w].astype(BF16)


def _wkv_prep(rkv, lora, prm, *, latent, tile):
    b, l, w3 = rkv.shape
    w = w3 // 3
    nt = l // tile
    n8 = l // 8
    tok = lambda wd: pl.BlockSpec((1, tile, wd), lambda i, t: (i, t, 0))
    prev = lambda wd: pl.BlockSpec((1, 8, wd), lambda i, t: (i, jnp.maximum(t * (tile // 8) - 1, 0), 0))
    nxt = lambda wd: pl.BlockSpec((1, 8, wd), lambda i, t: (i, jnp.minimum((t + 1) * (tile // 8), n8 - 1), 0))
    wl = lora.shape[2]
    in_specs = [tok(w3), prev(w3), nxt(w3), tok(wl), prev(wl), nxt(wl)]
    consts = [prm["conv_rkv"], prm["conv_lora"], prm["k_k"], prm["k_a"], prm["decay_w0"],
              prm["iclr_a0"], prm["lora_w"], prm["r_k"]]
    in_specs += [_const_spec(c.shape) for c in consts]
    dirtok = pl.BlockSpec((2, 1, tile, w), lambda i, t: (0, i, t, 0))
    out_specs = [dirtok] * 6 + [pl.BlockSpec((1, tile // CHUNK, 2, w), lambda i, t: (i, t, 0, 0)), tok(w)]
    out_shape = [jax.ShapeDtypeStruct((2, b, l, w), BF16)] * 6 + [
        jax.ShapeDtypeStruct((b, l // CHUNK, 2, w), F32), jax.ShapeDtypeStruct((b, l, w), BF16)]
    if latent:
        out_specs += [tok(w), tok(w), tok(w)]
        out_shape += [jax.ShapeDtypeStruct((b, l, w), F32), jax.ShapeDtypeStruct((b, l, w), BF16),
                      jax.ShapeDtypeStruct((b, l, w), F32)]
    return pl.pallas_call(
        functools.partial(_prep_kernel, latent=latent, width=w),
        grid=(b, nt),
        in_specs=in_specs,
        out_specs=out_specs,
        out_shape=out_shape,
        compiler_params=_params("parallel", "parallel"),
        name="wkv_prep_latent" if latent else "wkv_prep_context",
    )(rkv, rkv, rkv, lora, lora, lora, *consts)


def _wkv_kernel(*refs, emit_y, n_pairs):
    names = ("pm", "qm", "rt", "bt", "arb", "kt", "wc", "v")
    n = len(names)
    dir_refs = (dict(zip(names, refs[0:n])), dict(zip(names, refs[n:2 * n])))
    z0_ref = refs[2 * n]
    if emit_y:
        y_refs = refs[2 * n + 1:2 * n + 3]
        z_scr = refs[2 * n + 3]
    else:
        zfin_ref = refs[2 * n + 1]
        z_scr = refs[2 * n + 2]
    c = pl.program_id(0)

    @pl.when(c == 0)
    def _():
        z_scr[...] = z0_ref[...]

    left_h = _iota((CHUNK, LANES), 1) < CHUNK
    keep_l = jnp.where(left_h, 1.0, 0.0).astype(BF16)
    keep_r = jnp.where(left_h, 0.0, 1.0).astype(BF16)

    def unfold(xb):
        return _stack(xb * keep_l, xb * keep_r)

    tiles = [(i, d, p, slice(p * LANES, (p + 1) * LANES))
             for i in range(z_scr.shape[0]) for d in range(2) for p in range(n_pairs)]
    state = [z_scr[i, d, p] for i, d, p, _ in tiles]
    n_sub = dir_refs[0]["wc"].shape[1]
    for step in range(n_sub):
        sub = (step, n_sub - 1 - step)
        rows = [slice(sub[d] * CHUNK, (sub[d] + 1) * CHUNK) for d in range(2)]
        ld = lambda name: [dir_refs[d][name][0, i, rows[d], sl] for i, d, _, sl in tiles]
        v = [dir_refs[d]["v"][i, rows[d], sl] for i, d, _, sl in tiles]
        sbd = [unfold(s.astype(BF16)) for s in state]
        if emit_y:
            ur = [_dot_nt(_stack(pm, rt), s) for pm, rt, s in zip(ld("pm"), ld("rt"), sbd)]
        else:
            ur = [_dot_nt(pm, s) for pm, s in zip(ld("pm"), sbd)]
        u = [x[:CHUNK] + q.astype(F32) for x, q in zip(ur, ld("qm"))]
        ub = [ui.astype(BF16) for ui in u]
        inc = [_dot_tn(_stack(ui, vi), _stack(bt, kt)) for ui, vi, bt, kt in zip(ub, v, ld("bt"), ld("kt"))]
        inc = [jnp.where(left_h, x[:CHUNK], x[CHUNK:]) for x in inc]
        if emit_y:
            yc = [_dot(a, unfold(ui)) for a, ui in zip(ld("arb"), ub)]
            for (i, d, _, sl), x, ys in zip(tiles, ur, yc):
                y_refs[d][i, rows[d], sl] = (x[CHUNK:] + ys).astype(BF16)
        state = [(s + dz) * dir_refs[d]["wc"][i, sub[d], d:d + 1, sl]
                 for (i, d, _, sl), s, dz in zip(tiles, state, inc)]
    for (i, d, p, _), s in zip(tiles, state):
        z_scr[i, d, p] = s

    if not emit_y:
        @pl.when(c == pl.num_programs(0) - 1)
        def _():
            zfin_ref[...] = z_scr[...]


def _wkv_scan(prep, z0, *, emit_y):
    wc = prep[6]
    _, b, l, w = prep[0].shape
    n_sub = min(SCAN_CHUNKS, l // CHUNK)
    blk = n_sub * CHUNK
    nc = l // blk
    n_pairs = w // LANES
    fwd = lambda c: c
    rev = lambda c: nc - 1 - c
    in_specs, args = [], []
    for d, cm in enumerate((fwd, rev)):
        for arr in prep[:6]:
            in_specs.append(pl.BlockSpec((1, b, blk, w), lambda c, d=d, cm=cm: (d, 0, cm(c), 0)))
            args.append(arr)
        in_specs.append(pl.BlockSpec((b, n_sub, 2, w), lambda c, cm=cm: (0, cm(c), 0, 0)))
        args.append(wc)
        in_specs.append(pl.BlockSpec((b, blk, w), lambda c, cm=cm: (0, cm(c), 0)))
        args.append(prep[7])
    zshape = (b, 2, n_pairs, CHUNK, LANES)
    zspec = pl.BlockSpec(zshape, lambda c: (0, 0, 0, 0, 0))
    in_specs.append(zspec)
    args.append(z0)
    if emit_y:
        out_specs = [pl.BlockSpec((b, blk, w), lambda c: (0, c, 0)),
                     pl.BlockSpec((b, blk, w), lambda c: (0, nc - 1 - c, 0))]
        out_shape = [jax.ShapeDtypeStruct((b, l, w), BF16)] * 2
    else:
        out_specs = zspec
        out_shape = jax.ShapeDtypeStruct(zshape, F32)
    return pl.pallas_call(
        functools.partial(_wkv_kernel, emit_y=emit_y, n_pairs=n_pairs),
        grid=(nc,),
        in_specs=in_specs,
        out_specs=out_specs,
        out_shape=out_shape,
        scratch_shapes=[pltpu.VMEM(zshape, F32)],
        compiler_params=_params("arbitrary"),
        name="wkv_scan_latent" if emit_y else "wkv_scan_context",
    )(*args)


def _merge_kernel(x_ref, mod_ref, ya_ref, yf_ref, yr_ref, yh_ref, bonus_ref, g_ref, gate_ref,
                  lnw_ref, lnb_ref, wba_ref, wbr_ref, wo_ref, n2_ref, wu_ref, wd_ref, nf_ref,
                  o_ref, *, ff_chunk):
    x = x_ref[0]
    d = x.shape[1]
    mod = lambda j: mod_ref[0, j:j + 1, :]
    y = yf_ref[0].astype(F32) + yr_ref[0].astype(F32) + yh_ref[0]
    gw = min(MXU_DIM, y.shape[1])
    ones_bd = _head_ones(gw)

    def head_mean(t):
        tb = t.astype(BF16)
        parts = [_dot(tb[:, j:j + gw], ones_bd) for j in range(0, t.shape[1], gw)]
        return jnp.concatenate(parts, axis=1) * (1.0 / HEAD_DIM)

    mu = head_mean(y)
    yc = y - mu
    var = head_mean(yc * yc)
    yn = yc * lax.rsqrt(var + LNX_EPS)
    yr = (yn * lnw_ref[...] + lnb_ref[...] + bonus_ref[0]) * g_ref[0].astype(F32)
    gate = gate_ref[0].astype(F32)
    merged = gate[:, :d] * _dot(ya_ref[0], wba_ref[...]) + gate[:, d:] * _dot(yr.astype(BF16), wbr_ref[...])
    x1 = x + mod(2) * _dot(merged.astype(BF16), wo_ref[...])
    h2 = (_rmsnorm(x1, n2_ref[...]) * (1.0 + mod(4)) + mod(3)).astype(BF16)
    acc = jnp.zeros_like(x1)
    for j in range(wu_ref.shape[1] // ff_chunk):
        cs = slice(j * ff_chunk, (j + 1) * ff_chunk)
        up = jnp.maximum(_dot(h2, wu_ref[:, cs]), 0.0)
        acc = acc + _dot((up * up).astype(BF16), wd_ref[cs, :])
    x2 = x1 + mod(5) * acc
    o_ref[0] = _rmsnorm(x2, nf_ref[...])


def _merge_mlp(x, mod, ya, yf, yr, yh, bonus, g, gate, prm, *, tile):
    b, l, d = x.shape
    tok = lambda arr: pl.BlockSpec((1, tile, arr.shape[2]), lambda i, t: (i, t, 0))
    consts = [prm["lnx_w"], prm["lnx_b"], prm["w_branch_attn"], prm["w_branch_rwkv"], prm["w_out"],
              prm["norm2_g"], prm["w_mlp_up"], prm["w_mlp_down"], prm["norm_f_g"]]
    toks = [ya, yf, yr, yh, bonus, g, gate]
    return pl.pallas_call(
        functools.partial(_merge_kernel, ff_chunk=min(1024, prm["w_mlp_up"].shape[1])),
        grid=(b, l // tile),
        in_specs=[tok(x), pl.BlockSpec((1,) + mod.shape[1:], lambda i, t: (i, 0, 0))]
        + [tok(a) for a in toks] + [_const_spec(c.shape) for c in consts],
        out_specs=tok(x),
        out_shape=jax.ShapeDtypeStruct(x.shape, x.dtype),
        compiler_params=_params("parallel", "parallel"),
        name="merge_mlp",
    )(x, mod, *toks, *consts)


def _rope_tables(l):
    n_freq = HEAD_DIM // 4
    inv_freq = jnp.power(ROPE_BASE, -jnp.arange(n_freq, dtype=F32) / n_freq)
    rows = l // GRID_W
    row = jnp.repeat(jnp.arange(rows, dtype=F32), GRID_W)
    col = jnp.tile(jnp.arange(GRID_W, dtype=F32), rows)
    ang = jnp.concatenate([row[:, None] * inv_freq, col[:, None] * inv_freq], axis=-1)
    cos, sin = jnp.cos(ang), jnp.sin(ang)
    reps = LANES // HEAD_DIM
    return (jnp.tile(jnp.concatenate([cos, cos], axis=1), (1, reps)),
            jnp.tile(jnp.concatenate([-sin, sin], axis=1), (1, reps)))


def _pad_cols(w, width):
    return jnp.pad(w, ((0, 0), (0, width - w.shape[1])))


def kernel(x, c, ctx, c_ctx, w_ada, b_ada, norm1_g, w_in, sink, conv_w, decay_w0, decay_w2, iclr_a0, iclr_a2, gate_g2, k_k, k_a, r_k, lnx_w, lnx_b, w_branch_attn, w_branch_rwkv, w_out, norm2_g, w_mlp_up, w_mlp_down, norm_f_g):
    assert w_in.shape[0] == 1, "single-layer block: context tokens are read, never updated"
    b, l, d = x.shape
    attn_w = w_branch_attn.shape[1]
    rw = w_branch_rwkv.shape[1]
    n_q = attn_w // HEAD_DIM
    n_kv = n_q // Q_PER_KV
    kv_w = n_kv * HEAD_DIM
    assert kv_w == LANES and rw % LANES == 0 and ctx.shape[1] % 256 == 0
    assert l % (ATTN_QB * ATTN_BLOCK) == 0 and l % 512 == 0

    w = w_in[0]
    o_k, o_r = attn_w, attn_w + 2 * kv_w
    o_l = o_r + 3 * rw
    o_g = o_l + DECAY_LORA + ICLR_LORA + GATE_LORA
    assert o_l + LORA_PAD <= w.shape[1]
    w_main = w[:, :o_l + LORA_PAD].astype(BF16)
    w_gate = w[:, o_g:].astype(BF16)
    widths = {"q": attn_w, "kv": 2 * kv_w, "k": PAIR * kv_w, "v": PAIR * kv_w, "rkv": 3 * rw,
              "lora": LORA_PAD, "gate": 2 * d}

    cw = conv_w[0]
    lora_w = jnp.zeros((LORA_PAD, 5 * rw), F32)
    lora_w = lora_w.at[:DECAY_LORA, :rw].set(decay_w2[0, 0]).at[:DECAY_LORA, rw:2 * rw].set(decay_w2[0, 1])
    r1 = DECAY_LORA + ICLR_LORA
    lora_w = lora_w.at[DECAY_LORA:r1, 2 * rw:3 * rw].set(iclr_a2[0, 0]).at[DECAY_LORA:r1, 3 * rw:4 * rw].set(iclr_a2[0, 1])
    lora_w = lora_w.at[r1:r1 + GATE_LORA, 4 * rw:].set(gate_g2[0])
    prm = {
        "conv_rkv": cw[:, :3 * rw], "conv_lora": _pad_cols(cw[:, 3 * rw:], LORA_PAD),
        "k_k": k_k[0].reshape(1, rw), "k_a": k_a[0].reshape(1, rw),
        "decay_w0": decay_w0[0], "iclr_a0": iclr_a0[0], "lora_w": lora_w.astype(BF16),
        "r_k": r_k[0].reshape(1, rw),
        "lnx_w": lnx_w[0].reshape(1, rw), "lnx_b": lnx_b[0].reshape(1, rw),
        "w_branch_attn": w_branch_attn[0].astype(BF16), "w_branch_rwkv": w_branch_rwkv[0].astype(BF16),
        "w_out": w_out[0].astype(BF16), "norm2_g": norm2_g[0].reshape(1, d),
        "w_mlp_up": w_mlp_up[0].astype(BF16), "w_mlp_down": w_mlp_down[0].astype(BF16),
        "norm_f_g": norm_f_g.reshape(1, d),
    }

    rows = -(-(b + 1) // 8) * 8
    cc = jnp.zeros((rows, d), F32).at[:b].set(c).at[b].set(c_ctx)
    mod = _ada_mod(cc, w_ada[0], b_ada[0]).reshape(rows, -1, d)

    q, kd, vd, rkv, lora, gate = _in_proj(x, mod, b, norm1_g[0], [w_main, w_gate], widths, _rope_tables(l),
                                          latent=True, tile=512)
    kxd, vxd, rkv_c, lora_c = _in_proj(ctx, mod, b, norm1_g[0], [w_main], widths, None,
                                       latent=False, tile=256)
    ya = _attention(sink[0], q, kd, vd, kxd, vxd)

    prep_c = _wkv_prep(rkv_c, lora_c, prm, latent=False, tile=256)
    z_ctx = _wkv_scan(prep_c, jnp.zeros((b, 2, rw // LANES, CHUNK, LANES), F32), emit_y=False)
    prep = _wkv_prep(rkv, lora, prm, latent=True, tile=512)
    yf, yr = _wkv_scan(prep, z_ctx, emit_y=True)
    bonus, g, yh = prep[8], prep[9], prep[10]

    return _merge_mlp(x, mod, ya, yf, yr, yh, bonus, g, gate, prm, tile=512)
```

```python
import functools
import math

import jax
import jax.numpy as jnp
import numpy as np
from jax import lax
from jax.experimental import pallas as pl
from jax.experimental.pallas import tpu as pltpu

F32 = jnp.float32
BF16 = jnp.bfloat16

GRID_W = 64
HEAD_DIM = 64
Q_PER_KV = 4
ATTN_BLOCK = 128
ATTN_QB = 8
ROPE_BASE = 10000.0
NORM_EPS = 1e-6
LNX_EPS = 1e-5 * HEAD_DIM
DECAY_LORA, ICLR_LORA, GATE_LORA = 32, 32, 96
LORA_PAD = 256
CHUNK = 64
SCAN_CHUNKS = 4
PREP_GROUP = 16
LANES = 128
MXU_DIM = 256
PAIR = LANES // HEAD_DIM
NEG = -1e30
LOG2_E = math.log2(math.e)
VMEM_LIMIT = 56 * 1024 * 1024


def _dot(a, b):
    return jnp.dot(a, b, preferred_element_type=F32)


def _dot_nt(a, b):
    return lax.dot_general(a, b, (((1,), (1,)), ((), ())), preferred_element_type=F32)


def _dot_tn(a, b):
    return lax.dot_general(a, b, (((0,), (0,)), ((), ())), preferred_element_type=F32)


def _iota(shape, dim):
    return lax.broadcasted_iota(jnp.int32, shape, dim)


def _head_ones(width):
    r = _iota((width, width), 0) // HEAD_DIM
    c = _iota((width, width), 1) // HEAD_DIM
    return jnp.where(r == c, 1.0, 0.0).astype(BF16)


def _head_sum(x, ones_bd):
    gw = ones_bd.shape[0]
    hi = x.astype(BF16)
    lo = (x - hi.astype(F32)).astype(BF16)
    parts = [_dot(hi[:, j:j + gw], ones_bd) + _dot(lo[:, j:j + gw], ones_bd) for j in range(0, x.shape[1], gw)]
    return jnp.concatenate(parts, axis=1)


def _rmsnorm(x, g):
    ms = jnp.mean(x * x, axis=-1, keepdims=True)
    return x * lax.rsqrt(ms + NORM_EPS) * g


def _params(*sem):
    return pltpu.CompilerParams(dimension_semantics=sem, vmem_limit_bytes=VMEM_LIMIT)


def _const_spec(shape):
    nd = len(shape)
    return pl.BlockSpec(shape, lambda *_: (0,) * nd, pipeline_mode=pl.Buffered(1))


def _ada_kernel(c_ref, w_ref, b_ref, o_ref):
    c = c_ref[...]
    s = c * jax.nn.sigmoid(c)
    o_ref[...] = _dot(s.astype(BF16), w_ref[...].astype(BF16)) + b_ref[...]


def _ada_mod(cc, w_ada, b_ada):
    rows, d = cc.shape
    n = w_ada.shape[1]
    return pl.pallas_call(
        _ada_kernel,
        grid=(n // d,),
        in_specs=[pl.BlockSpec((rows, d), lambda j: (0, 0)),
                  pl.BlockSpec((d, d), lambda j: (0, j)),
                  pl.BlockSpec((1, d), lambda j: (0, j))],
        out_specs=pl.BlockSpec((rows, d), lambda j: (0, j)),
        out_shape=jax.ShapeDtypeStruct((rows, n), F32),
        compiler_params=_params("arbitrary"),
        name="ada_mod",
    )(cc, w_ada, b_ada.reshape(1, n))


def _rope(x, cos_t, sin_t):
    w = x.shape[1]
    half = HEAD_DIM // 2
    first = (_iota(x.shape, 1) % HEAD_DIM) < half
    swapped = jnp.where(first, pltpu.roll(x, w - half, 1), pltpu.roll(x, half, 1))
    reps = w // LANES
    c = jnp.concatenate([cos_t] * reps, axis=1)
    s = jnp.concatenate([sin_t] * reps, axis=1)
    return x * c + swapped * s


def _inproj_kernel(*refs, latent, widths):
    if latent:
        (x_ref, mod_ref, g_ref, w_ref, wg_ref, cos_ref, sin_ref,
         q_ref, k_ref, v_ref, rkv_ref, lora_ref, gate_ref) = refs
    else:
        x_ref, mod_ref, g_ref, w_ref, k_ref, v_ref, rkv_ref, lora_ref = refs
    x = x_ref[0]
    h = _rmsnorm(x, g_ref[...]) * (1.0 + mod_ref[0, 1:2, :]) + mod_ref[0, 0:1, :]
    hb = h.astype(BF16)
    off = widths["q"]

    def seg(name):
        nonlocal off
        lo = off
        off += widths[name]
        return _dot(hb, w_ref[:, lo:off])

    def dup_heads(t):
        first = _iota(t.shape, 1) < HEAD_DIM
        other = pltpu.roll(t, HEAD_DIM, 1)
        return jnp.concatenate([jnp.where(first, t, other), jnp.where(first, other, t)], axis=1)

    kv = seg("kv")
    k, v = dup_heads(kv[:, :LANES]), dup_heads(kv[:, LANES:])
    if latent:
        cos_t, sin_t = cos_ref[...], sin_ref[...]
        q = _dot(hb, w_ref[:, :widths["q"]])
        q_ref[0] = (_rope(q, cos_t, sin_t) * (LOG2_E * HEAD_DIM ** -0.5)).astype(BF16)
        k = _rope(k, cos_t, sin_t)
    k_ref[0] = k.astype(BF16)
    v_ref[0] = v.astype(BF16)
    rkv_ref[0] = seg("rkv")
    lora_ref[0] = seg("lora")
    if latent:
        gate_ref[0] = jax.nn.sigmoid(_dot(hb, wg_ref[...])).astype(BF16)


def _in_proj(x, mod, mod_row, norm_g, weights, widths, tables, *, latent, tile):
    b, l, d = x.shape
    nt = l // tile
    if latent:
        mod_map = lambda i, t: (i, 0, 0)
    else:
        mod_map = lambda i, t: (mod_row, 0, 0)
    tok = lambda w: pl.BlockSpec((1, tile, w), lambda i, t: (i, t, 0))
    in_specs = [tok(d),
                pl.BlockSpec((1,) + mod.shape[1:], mod_map),
                _const_spec((1, d))] + [_const_spec(w.shape) for w in weights]
    args = [x, mod, norm_g.reshape(1, d), *weights]
    out_specs, out_shape = [], []

    def out(w, dt):
        out_specs.append(tok(w))
        out_shape.append(jax.ShapeDtypeStruct((b, l, w), dt))

    if latent:
        in_specs += [pl.BlockSpec((tile, LANES), lambda i, t: (t, 0))] * 2
        args += list(tables)
        out(widths["q"], BF16)
    out(widths["k"], BF16)
    out(widths["v"], BF16)
    out(widths["rkv"], F32)
    out(widths["lora"], F32)
    if latent:
        out(widths["gate"], BF16)
    return pl.pallas_call(
        functools.partial(_inproj_kernel, latent=latent, widths=widths),
        grid=(b, nt),
        in_specs=in_specs,
        out_specs=out_specs,
        out_shape=out_shape,
        compiler_params=_params("parallel", "parallel"),
        name="in_proj_latent" if latent else "in_proj_context",
    )(*args)


def _attn_kernel(sink_ref, q_ref, kp_ref, kc_ref, kn_ref, vp_ref, vc_ref, vn_ref,
                 kx_ref, vx_ref, o_ref, *, n_kv):
    i = pl.program_id(1)
    last = pl.num_programs(1) - 1
    blk = ATTN_BLOCK
    qi = _iota((blk, blk), 0)
    kj = _iota((blk, blk), 1)
    left = _iota((blk, LANES), 1) < HEAD_DIM

    def key_block(refs, j, gs):
        ref_p, ref_c, ref_n = refs[:3]
        if j < 0:
            return ref_p[0, :, gs]
        if j >= ATTN_QB:
            return ref_n[0, :, gs]
        return ref_c[0, j * blk:(j + 1) * blk, gs]

    items = []
    for qb in range(ATTN_QB):
        lo_ok = kj >= qi
        hi_ok = kj <= qi
        if qb == 0:
            lo_ok = lo_ok & (i > 0)
        if qb == ATTN_QB - 1:
            hi_ok = hi_ok & (i < last)
        bias_lo = jnp.concatenate([jnp.where(lo_ok, 0.0, NEG)] * Q_PER_KV, axis=0)
        bias_hi = jnp.concatenate([jnp.where(hi_ok, 0.0, NEG)] * Q_PER_KV, axis=0)
        rows = slice(qb * blk, (qb + 1) * blk)
        q = q_ref[0, rows, :].astype(F32)
        for g in range(n_kv):
            gs = slice(g * LANES, (g + 1) * LANES)
            heads = range(g * Q_PER_KV, (g + 1) * Q_PER_KV)
            qs, sinks = [], []
            for hd in heads:
                qp = q[:, (hd // PAIR) * LANES:(hd // PAIR + 1) * LANES]
                keep = left if hd % PAIR == 0 else jnp.logical_not(left)
                qs.append(jnp.where(keep, qp, 0.0).astype(BF16))
                sinks.append(jnp.full((blk, 1), sink_ref[hd] * LOG2_E, F32))
            items.append({
                "qb": qb, "gs": gs, "rows": rows, "heads": heads, "bias": (bias_lo, bias_hi),
                "qs": jnp.concatenate(qs, axis=0), "sink": jnp.concatenate(sinks, axis=0)})

    def cat(refs, it):
        return jnp.concatenate([key_block(refs, it["qb"] + j, it["gs"]) for j in (-1, 0, 1)]
                               + [refs[3][0, :, it["gs"]]], axis=0)

    s = [_dot_nt(it["qs"], cat((kp_ref, kc_ref, kn_ref, kx_ref), it)) for it in items]
    s = [jnp.concatenate([x[:, :blk] + it["bias"][0], x[:, blk:2 * blk], x[:, 2 * blk:3 * blk] + it["bias"][1],
                          x[:, 3 * blk:]], axis=1) for x, it in zip(s, items)]
    m = [jnp.maximum(jnp.max(x, axis=1, keepdims=True), it["sink"]) for x, it in zip(s, items)]
    p = [jnp.exp2(x - mx) for x, mx in zip(s, m)]
    den = [jnp.sum(x, axis=1, keepdims=True) + jnp.exp2(it["sink"] - mx) for x, mx, it in zip(p, m, items)]
    o = [_dot(x.astype(BF16), cat((vp_ref, vc_ref, vn_ref, vx_ref), it)) / dn for x, dn, it in zip(p, den, items)]
    for x, it in zip(o, items):
        for hd in it["heads"][::PAIR]:
            j = hd - it["heads"][0]
            pair = jnp.where(left, x[j * blk:(j + 1) * blk], x[(j + 1) * blk:(j + 2) * blk])
            col = (hd // PAIR) * LANES
            o_ref[0, it["rows"], col:col + LANES] = pair.astype(BF16)


def _attention(sink, q, kd, vd, kxd, vxd):
    b, l, wq = q.shape
    wk = kd.shape[2]
    lc = kxd.shape[1]
    nb = l // ATTN_BLOCK
    span = ATTN_QB * ATTN_BLOCK
    blk = lambda w, f: pl.BlockSpec((1, ATTN_BLOCK, w), f)
    prev = lambda bi, i: (bi, jnp.maximum(i * ATTN_QB - 1, 0), 0)
    nxt = lambda bi, i: (bi, jnp.minimum((i + 1) * ATTN_QB, nb - 1), 0)
    cur = lambda w: pl.BlockSpec((1, span, w), lambda bi, i: (bi, i, 0))
    ctx = pl.BlockSpec((1, lc, wk), lambda bi, i: (bi, 0, 0))
    return pl.pallas_call(
        functools.partial(_attn_kernel, n_kv=wk // LANES),
        grid=(b, l // span),
        in_specs=[pl.BlockSpec(memory_space=pltpu.SMEM),
                  cur(wq),
                  blk(wk, prev), cur(wk), blk(wk, nxt),
                  blk(wk, prev), cur(wk), blk(wk, nxt),
                  ctx, ctx],
        out_specs=cur(wq),
        out_shape=jax.ShapeDtypeStruct((b, l, wq), BF16),
        compiler_params=_params("parallel", "parallel"),
        name="attention",
    )(sink, q, kd, kd, kd, vd, vd, vd, kxd, vxd)


def _conv3(x, prev_row, next_row, w):
    n = x.shape[0]
    row = _iota((8, x.shape[1]), 0)
    xm = pltpu.roll(x, 1, 0)
    xm = jnp.concatenate([jnp.where(row == 0, prev_row, xm[:8]), xm[8:]], axis=0)
    xp = pltpu.roll(x, n - 1, 0)
    xp = jnp.concatenate([xp[:n - 8], jnp.where(row == 7, next_row, xp[n - 8:])], axis=0)
    return xm * w[0:1] + x * w[1:2] + xp * w[2:3]


def _pair_masks():
    n = 2 * CHUNK
    row = _iota((n, n), 0)
    lane = _iota((n, n), 1)
    top, left = row < CHUNK, lane < CHUNK
    return {"row": row % CHUNK, "lane": lane % CHUNK, "top": top, "left": left, "bd": top == left,
            "left_h": _iota((CHUNK, n), 1) < CHUNK}


def _stack(a, b):
    return jnp.concatenate([a, b], axis=0)


def _fold(x, m):
    return jnp.where(m["left_h"], x[:CHUNK], x[CHUNK:])


def _chunk_local(inst, m):
    bf = lambda x: x.astype(BF16)
    diag = jnp.logical_not(m["top"]) & (m["lane"] == m["row"])
    masks = ((m["lane"] < m["row"]) | diag, (m["lane"] > m["row"]) | diag)
    mask_a = [masks[i["rev"]] for i in inst]
    lh = m["left_h"]
    half = CHUNK // 2
    keep_l = jnp.where(m["left"], 1.0, 0.0).astype(BF16)
    keep_r = jnp.where(m["left"], 0.0, 1.0).astype(BF16)
    keep_lh = jnp.where(lh, 1.0, 0.0).astype(BF16)
    keep_rh = jnp.where(lh, 0.0, 1.0).astype(BF16)

    def unfold(xb, anti=False):
        a, b = xb * keep_lh, xb * keep_rh
        return _stack(b, a) if anti else _stack(a, b)

    lhs = [bf(_stack(i["at"], i["rt"])) for i in inst]
    a01 = [_dot_nt(l, _stack(bf(_stack(i["bt"], i["kt"])) * keep_l, bf(_stack(i["kt"], i["bt"])) * keep_r))
           for l, i in zip(lhs, inst)]
    a0 = [jnp.where(ma, a[:, :LANES], 0.0) for ma, a in zip(mask_a, a01)]
    a1 = [jnp.where(ma, a[:, LANES:], 0.0) for ma, a in zip(mask_a, a01)]
    nc = [jnp.where(lh, x[:CHUNK], y[:CHUNK]) for x, y in zip(a0, a1)]
    arb = [jnp.where(lh, x[CHUNK:], y[CHUNK:]) for x, y in zip(a0, a1)]
    ak_ark_sw = [bf(jnp.where(m["left"], y, x)) for x, y in zip(a0, a1)]
    vh = [_dot(a, unfold(bf(i["v"]), anti=True)) for a, i in zip(ak_ark_sw, inst)]
    eye = jnp.where(_iota((CHUNK, LANES), 1) % CHUNK == _iota((CHUNK, LANES), 0), 1.0, 0.0)
    tc = [eye + n for n in nc]
    ncb = [bf(n) for n in nc]
    nc = [_dot(n, unfold(n)) for n in ncb]
    steps = CHUNK.bit_length() - 1
    for _ in range(steps - 2):
        ncb = [bf(n) for n in nc]
        both = [_dot(n, jnp.concatenate([unfold(bf(t)), unfold(n)], axis=1)) for n, t in zip(ncb, tc)]
        tc = [t + r[:, :LANES] for t, r in zip(tc, both)]
        nc = [r[:, LANES:] for r in both]
    inc = [_dot(bf(n[half:]), unfold(bf(t))) for n, t in zip(nc, tc)]
    tc = [jnp.concatenate([t[:half], t[half:] + d], axis=0) for t, d in zip(tc, inc)]
    pq = [_dot(bf(t), jnp.concatenate([unfold(l[:CHUNK]), unfold(bf(x[:CHUNK]))], axis=1))
          for t, l, x in zip(tc, lhs, vh)]
    return [{"pm": r[:, :LANES], "qm": r[:, LANES:], "arb": b, "hm": x[CHUNK:]} for r, b, x in zip(pq, arb, vh)]


def _prep_kernel(*refs, latent, width):
    (rkv_ref, rkv_p, rkv_n, lora_ref, lora_p, lora_n, cw_ref, cwl_ref, kk_ref, ka_ref,
     w0_ref, a0_ref, wl_ref, rk_ref) = refs[:14]
    outs = refs[14:]
    out_refs = dict(zip(("pm", "qm", "rt", "bt", "arb", "kt"), outs[:6]))
    wc_ref, v_ref = outs[6:8]
    if latent:
        bonus_ref, g_ref, hsum_ref = outs[8:]
    t = pl.program_id(1)
    nt = pl.num_programs(1)
    tile = rkv_ref.shape[1]
    w = width
    has_prev = (t > 0).astype(F32)
    has_next = (t < nt - 1).astype(F32)
    u = _conv3(rkv_ref[0], rkv_p[0, 7:8, :] * has_prev, rkv_n[0, 0:1, :] * has_next, cw_ref[...])
    ul = _conv3(lora_ref[0], lora_p[0, 7:8, :] * has_prev, lora_n[0, 0:1, :] * has_next, cwl_ref[...])
    r, k, v = u[:, :w], u[:, w:2 * w], u[:, 2 * w:]
    ones_bd = _head_ones(min(MXU_DIM, w))

    kk = k * kk_ref[...]
    kk = kk * lax.rsqrt(jnp.maximum(_head_sum(kk * kk, ones_bd), 1e-24))

    lane = _iota(ul.shape, 1)
    lin = jnp.where(lane < DECAY_LORA, jnp.tanh(ul),
                    jnp.where(lane < DECAY_LORA + ICLR_LORA, ul, jax.nn.sigmoid(ul)))
    proj = _dot(lin.astype(BF16), wl_ref[...])

    tr = _iota((tile, tile), 0)
    tc = _iota((tile, tile), 1)
    same = (tr // CHUNK) == (tc // CHUNK)
    tri = (jnp.where(same & (tc <= tr), 1.0, 0.0).astype(BF16),
           jnp.where(same & (tc >= tr), 1.0, 0.0).astype(BF16))

    def exact_dot(m, x):
        h1 = x.astype(BF16)
        h2 = (x - h1.astype(F32)).astype(BF16)
        return _dot(m, h1) + _dot(m, h2)

    masks = _pair_masks()
    k_sum = None
    work = []
    for d in range(2):
        z = w0_ref[d:d + 1, :] + proj[:, d * w:(d + 1) * w]
        lw = -math.exp(-0.5) * jax.nn.sigmoid(z)
        a = jax.nn.sigmoid(a0_ref[d:d + 1, :] + proj[:, (2 + d) * w:(3 + d) * w])
        kd = k * (1.0 + (a - 1.0) * ka_ref[...])
        k_sum = kd if k_sum is None else k_sum + kd
        cum = exact_dot(tri[d], lw)
        e_neg = jnp.exp(-cum)
        full = {"at": -kk * jnp.exp(cum - lw), "rt": r * jnp.exp(cum), "bt": kk * a * e_neg,
                "kt": kd * e_neg, "v": v}
        for name in ("rt", "bt", "kt"):
            out_refs[name][d, 0] = full[name].astype(BF16)
        for j in range(tile // CHUNK):
            edge = (j + 1) * CHUNK - 1 if d == 0 else j * CHUNK
            wc_ref[0, j, d:d + 1, :] = jnp.exp(cum[edge:edge + 1, :])
        work += [(d, slice(j * CHUNK, (j + 1) * CHUNK), slice(p * LANES, (p + 1) * LANES), full)
                 for j in range(tile // CHUNK) for p in range(w // LANES)]
    hsum = {}
    for g0 in range(0, len(work), PREP_GROUP):
        group = work[g0:g0 + PREP_GROUP]
        inst = [dict({name: val[rs, ls] for name, val in full.items()}, rev=d) for d, rs, ls, full in group]
        for (d, rs, ls, _), res in zip(group, _chunk_local(inst, masks)):
            seen = hsum.get((rs.start, ls.start))
            hsum[(rs.start, ls.start)] = (rs, ls, res["hm"] if seen is None else seen[2] + res["hm"])
            for name in res:
                if name in out_refs:
                    out_refs[name][d, 0, rs, ls] = res[name].astype(BF16)
    v_ref[0] = v.astype(BF16)
    if latent:
        for rs, ls, val in hsum.values():
            hsum_ref[0, rs, ls] = val
        bonus_ref[0] = _head_sum(r * k_sum * rk_ref[...], ones_bd) * v
        g_ref[0] = proj[:, 4 * w:5 * w].astype(BF16)


def _wkv_prep(rkv, lora, prm, *, latent, tile):
    b, l, w3 = rkv.shape
    w = w3 // 3
    nt = l // tile
    n8 = l // 8
    tok = lambda wd: pl.BlockSpec((1, tile, wd), lambda i, t: (i, t, 0))
    prev = lambda wd: pl.BlockSpec((1, 8, wd), lambda i, t: (i, jnp.maximum(t * (tile // 8) - 1, 0), 0))
    nxt = lambda wd: pl.BlockSpec((1, 8, wd), lambda i, t: (i, jnp.minimum((t + 1) * (tile // 8), n8 - 1), 0))
    wl = lora.shape[2]
    in_specs = [tok(w3), prev(w3), nxt(w3), tok(wl), prev(wl), nxt(wl)]
    consts = [prm["conv_rkv"], prm["conv_lora"], prm["k_k"], prm["k_a"], prm["decay_w0"],
              prm["iclr_a0"], prm["lora_w"], prm["r_k"]]
    in_specs += [_const_spec(c.shape) for c in consts]
    dirtok = pl.BlockSpec((2, 1, tile, w), lambda i, t: (0, i, t, 0))
    out_specs = [dirtok] * 6 + [pl.BlockSpec((1, tile // CHUNK, 2, w), lambda i, t: (i, t, 0, 0)), tok(w)]
    out_shape = [jax.ShapeDtypeStruct((2, b, l, w), BF16)] * 6 + [
        jax.ShapeDtypeStruct((b, l // CHUNK, 2, w), F32), jax.ShapeDtypeStruct((b, l, w), BF16)]
    if latent:
        out_specs += [tok(w), tok(w), tok(w)]
        out_shape += [jax.ShapeDtypeStruct((b, l, w), F32), jax.ShapeDtypeStruct((b, l, w), BF16),
                      jax.ShapeDtypeStruct((b, l, w), F32)]
    return pl.pallas_call(
        functools.partial(_prep_kernel, latent=latent, width=w),
        grid=(b, nt),
        in_specs=in_specs,
        out_specs=out_specs,
        out_shape=out_shape,
        compiler_params=_params("parallel", "parallel"),
        name="wkv_prep_latent" if latent else "wkv_prep_context",
    )(rkv, rkv, rkv, lora, lora, lora, *consts)


def _wkv_kernel(*refs, emit_y, n_pairs):
    names = ("pm", "qm", "rt", "bt", "arb", "kt", "wc", "v")
    n = len(names)
    dir_refs = (dict(zip(names, refs[0:n])), dict(zip(names, refs[n:2 * n])))
    z0_ref = refs[2 * n]
    if emit_y:
        y_refs = refs[2 * n + 1:2 * n + 3]
        z_scr = refs[2 * n + 3]
    else:
        zfin_ref = refs[2 * n + 1]
        z_scr = refs[2 * n + 2]
    c = pl.program_id(0)

    @pl.when(c == 0)
    def _():
        z_scr[...] = z0_ref[...]

    left_h = _iota((CHUNK, LANES), 1) < CHUNK
    keep_l = jnp.where(left_h, 1.0, 0.0).astype(BF16)
    keep_r = jnp.where(left_h, 0.0, 1.0).astype(BF16)

    def unfold(xb):
        return _stack(xb * keep_l, xb * keep_r)

    tiles = [(i, d, p, slice(p * LANES, (p + 1) * LANES))
             for i in range(z_scr.shape[0]) for d in range(2) for p in range(n_pairs)]
    state = [z_scr[i, d, p] for i, d, p, _ in tiles]
    n_sub = dir_refs[0]["wc"].shape[1]
    for step in range(n_sub):
        sub = (step, n_sub - 1 - step)
        rows = [slice(sub[d] * CHUNK, (sub[d] + 1) * CHUNK) for d in range(2)]
        ld = lambda name: [dir_refs[d][name][0, i, rows[d], sl] for i, d, _, sl in tiles]
        v = [dir_refs[d]["v"][i, rows[d], sl] for i, d, _, sl in tiles]
        sbd = [unfold(s.astype(BF16)) for s in state]
        if emit_y:
            ur = [_dot_nt(_stack(pm, rt), s) for pm, rt, s in zip(ld("pm"), ld("rt"), sbd)]
        else:
            ur = [_dot_nt(pm, s) for pm, s in zip(ld("pm"), sbd)]
        u = [x[:CHUNK] + q.astype(F32) for x, q in zip(ur, ld("qm"))]
        ub = [ui.astype(BF16) for ui in u]
        inc = [_dot_tn(_stack(ui, vi), _stack(bt, kt)) for ui, vi, bt, kt in zip(ub, v, ld("bt"), ld("kt"))]
        inc = [jnp.where(left_h, x[:CHUNK], x[CHUNK:]) for x in inc]
        if emit_y:
            yc = [_dot(a, unfold(ui)) for a, ui in zip(ld("arb"), ub)]
            for (i, d, _, sl), x, ys in zip(tiles, ur, yc):
                y_refs[d][i, rows[d], sl] = (x[CHUNK:] + ys).astype(BF16)
        state = [(s + dz) * dir_refs[d]["wc"][i, sub[d], d:d + 1, sl]
                 for (i, d, _, sl), s, dz in zip(tiles, state, inc)]
    for (i, d, p, _), s in zip(tiles, state):
        z_scr[i, d, p] = s

    if not emit_y:
        @pl.when(c == pl.num_programs(0) - 1)
        def _():
            zfin_ref[...] = z_scr[...]


def _wkv_scan(prep, z0, *, emit_y):
    wc = prep[6]
    _, b, l, w = prep[0].shape
    n_sub = min(SCAN_CHUNKS, l // CHUNK)
    blk = n_sub * CHUNK
    nc = l // blk
    n_pairs = w // LANES
    fwd = lambda c: c
    rev = lambda c: nc - 1 - c
    in_specs, args = [], []
    for d, cm in enumerate((fwd, rev)):
        for arr in prep[:6]:
            in_specs.append(pl.BlockSpec((1, b, blk, w), lambda c, d=d, cm=cm: (d, 0, cm(c), 0)))
            args.append(arr)
        in_specs.append(pl.BlockSpec((b, n_sub, 2, w), lambda c, cm=cm: (0, cm(c), 0, 0)))
        args.append(wc)
        in_specs.append(pl.BlockSpec((b, blk, w), lambda c, cm=cm: (0, cm(c), 0)))
        args.append(prep[7])
    zshape = (b, 2, n_pairs, CHUNK, LANES)
    zspec = pl.BlockSpec(zshape, lambda c: (0, 0, 0, 0, 0))
    in_specs.append(zspec)
    args.append(z0)
    if emit_y:
        out_specs = [pl.BlockSpec((b, blk, w), lambda c: (0, c, 0)),
                     pl.BlockSpec((b, blk, w), lambda c: (0, nc - 1 - c, 0))]
        out_shape = [jax.ShapeDtypeStruct((b, l, w), BF16)] * 2
    else:
        out_specs = zspec
        out_shape = jax.ShapeDtypeStruct(zshape, F32)
    return pl.pallas_call(
        functools.partial(_wkv_kernel, emit_y=emit_y, n_pairs=n_pairs),
        grid=(nc,),
        in_specs=in_specs,
        out_specs=out_specs,
        out_shape=out_shape,
        scratch_shapes=[pltpu.VMEM(zshape, F32)],
        compiler_params=_params("arbitrary"),
        name="wkv_scan_latent" if emit_y else "wkv_scan_context",
    )(*args)


def _merge_kernel(x_ref, mod_ref, ya_ref, yf_ref, yr_ref, yh_ref, bonus_ref, g_ref, gate_ref,
                  lnw_ref, lnb_ref, wba_ref, wbr_ref, wo_ref, n2_ref, wu_ref, wd_ref, nf_ref,
                  o_ref, *, ff_chunk):
    x = x_ref[0]
    d = x.shape[1]
    mod = lambda j: mod_ref[0, j:j + 1, :]
    y = yf_ref[0].astype(F32) + yr_ref[0].astype(F32) + yh_ref[0]
    gw = min(MXU_DIM, y.shape[1])
    ones_bd = _head_ones(gw)

    def head_mean(t):
        tb = t.astype(BF16)
        parts = [_dot(tb[:, j:j + gw], ones_bd) for j in range(0, t.shape[1], gw)]
        return jnp.concatenate(parts, axis=1) * (1.0 / HEAD_DIM)

    mu = head_mean(y)
    yc = y - mu
    var = head_mean(yc * yc)
    yn = yc * lax.rsqrt(var + LNX_EPS)
    yr = (yn * lnw_ref[...] + lnb_ref[...] + bonus_ref[0]) * g_ref[0].astype(F32)
    gate = gate_ref[0].astype(F32)
    merged = gate[:, :d] * _dot(ya_ref[0], wba_ref[...]) + gate[:, d:] * _dot(yr.astype(BF16), wbr_ref[...])
    x1 = x + mod(2) * _dot(merged.astype(BF16), wo_ref[...])
    h2 = (_rmsnorm(x1, n2_ref[...]) * (1.0 + mod(4)) + mod(3)).astype(BF16)
    acc = jnp.zeros_like(x1)
    for j in range(wu_ref.shape[1] // ff_chunk):
        cs = slice(j * ff_chunk, (j + 1) * ff_chunk)
        up = jnp.maximum(_dot(h2, wu_ref[:, cs]), 0.0)
        acc = acc + _dot((up * up).astype(BF16), wd_ref[cs, :])
    x2 = x1 + mod(5) * acc
    o_ref[0] = _rmsnorm(x2, nf_ref[...])


def _merge_mlp(x, mod, ya, yf, yr, yh, bonus, g, gate, prm, *, tile):
    b, l, d = x.shape
    tok = lambda arr: pl.BlockSpec((1, tile, arr.shape[2]), lambda i, t: (i, t, 0))
    consts = [prm["lnx_w"], prm["lnx_b"], prm["w_branch_attn"], prm["w_branch_rwkv"], prm["w_out"],
              prm["norm2_g"], prm["w_mlp_up"], prm["w_mlp_down"], prm["norm_f_g"]]
    toks = [ya, yf, yr, yh, bonus, g, gate]
    return pl.pallas_call(
        functools.partial(_merge_kernel, ff_chunk=min(1024, prm["w_mlp_up"].shape[1])),
        grid=(b, l // tile),
        in_specs=[tok(x), pl.BlockSpec((1,) + mod.shape[1:], lambda i, t: (i, 0, 0))]
        + [tok(a) for a in toks] + [_const_spec(c.shape) for c in consts],
        out_specs=tok(x),
        out_shape=jax.ShapeDtypeStruct(x.shape, x.dtype),
        compiler_params=_params("parallel", "parallel"),
        name="merge_mlp",
    )(x, mod, *toks, *consts)


def _rope_tables(l):
    n_freq = HEAD_DIM // 4
    inv_freq = np.power(np.float32(ROPE_BASE), -np.arange(n_freq, dtype=np.float32) / np.float32(n_freq))
    rows = l // GRID_W
    row = np.repeat(np.arange(rows, dtype=np.float32), GRID_W)
    col = np.tile(np.arange(GRID_W, dtype=np.float32), rows)
    ang = np.concatenate([row[:, None] * inv_freq, col[:, None] * inv_freq], axis=-1).astype(np.float32)
    cos, sin = np.cos(ang), np.sin(ang)
    reps = LANES // HEAD_DIM
    return (jnp.asarray(np.tile(np.concatenate([cos, cos], axis=1), (1, reps))),
            jnp.asarray(np.tile(np.concatenate([-sin, sin], axis=1), (1, reps))))


def _pad_cols(w, width):
    return jnp.pad(w, ((0, 0), (0, width - w.shape[1])))


def kernel(x, c, ctx, c_ctx, w_ada, b_ada, norm1_g, w_in, sink, conv_w, decay_w0, decay_w2, iclr_a0, iclr_a2, gate_g2, k_k, k_a, r_k, lnx_w, lnx_b, w_branch_attn, w_branch_rwkv, w_out, norm2_g, w_mlp_up, w_mlp_down, norm_f_g):
    assert w_in.shape[0] == 1, "single-layer block: context tokens are read, never updated"
    b, l, d = x.shape
    attn_w = w_branch_attn.shape[1]
    rw = w_branch_rwkv.shape[1]
    n_q = attn_w // HEAD_DIM
    n_kv = n_q // Q_PER_KV
    kv_w = n_kv * HEAD_DIM
    assert kv_w == LANES and rw % LANES == 0 and ctx.shape[1] % 256 == 0
    assert l % (ATTN_QB * ATTN_BLOCK) == 0 and l % 512 == 0

    w = w_in[0]
    o_k, o_r = attn_w, attn_w + 2 * kv_w
    o_l = o_r + 3 * rw
    o_g = o_l + DECAY_LORA + ICLR_LORA + GATE_LORA
    assert o_l + LORA_PAD <= w.shape[1]
    w_main = w[:, :o_l + LORA_PAD].astype(BF16)
    w_gate = w[:, o_g:].astype(BF16)
    widths = {"q": attn_w, "kv": 2 * kv_w, "k": PAIR * kv_w, "v": PAIR * kv_w, "rkv": 3 * rw,
              "lora": LORA_PAD, "gate": 2 * d}

    cw = conv_w[0]
    used = DECAY_LORA + ICLR_LORA + GATE_LORA
    lora_w = jnp.concatenate([
        jnp.pad(jnp.concatenate([decay_w2[0, 0], decay_w2[0, 1]], axis=1), ((0, 0), (0, 3 * rw))),
        jnp.pad(jnp.concatenate([iclr_a2[0, 0], iclr_a2[0, 1]], axis=1), ((0, 0), (2 * rw, rw))),
        jnp.pad(gate_g2[0], ((0, LORA_PAD - used), (4 * rw, 0)))], axis=0)
    prm = {
        "conv_rkv": cw[:, :3 * rw], "conv_lora": _pad_cols(cw[:, 3 * rw:], LORA_PAD),
        "k_k": k_k[0].reshape(1, rw), "k_a": k_a[0].reshape(1, rw),
        "decay_w0": decay_w0[0], "iclr_a0": iclr_a0[0], "lora_w": lora_w.astype(BF16),
        "r_k": r_k[0].reshape(1, rw),
        "lnx_w": lnx_w[0].reshape(1, rw), "lnx_b": lnx_b[0].reshape(1, rw),
        "w_branch_attn": w_branch_attn[0].astype(BF16), "w_branch_rwkv": w_branch_rwkv[0].astype(BF16),
        "w_out": w_out[0].astype(BF16), "norm2_g": norm2_g[0].reshape(1, d),
        "w_mlp_up": w_mlp_up[0].astype(BF16), "w_mlp_down": w_mlp_down[0].astype(BF16),
        "norm_f_g": norm_f_g.reshape(1, d),
    }

    rows = -(-(b + 1) // 8) * 8
    cc = jnp.concatenate([c, c_ctx[None, :], jnp.zeros((rows - b - 1, d), F32)], axis=0)
    mod = _ada_mod(cc, w_ada[0], b_ada[0]).reshape(rows, -1, d)

    q, kd, vd, rkv, lora, gate = _in_proj(x, mod, b, norm1_g[0], [w_main, w_gate], widths, _rope_tables(l),
                                          latent=True, tile=512)
    kxd, vxd, rkv_c, lora_c = _in_proj(ctx, mod, b, norm1_g[0], [w_main], widths, None,
                                       latent=False, tile=256)
    ya = _attention(sink[0], q, kd, vd, kxd, vxd)

    prep_c = _wkv_prep(rkv_c, lora_c, prm, latent=False, tile=256)
    z_ctx = _wkv_scan(prep_c, jnp.zeros((b, 2, rw // LANES, CHUNK, LANES), F32), emit_y=False)
    prep = _wkv_prep(rkv, lora, prm, latent=True, tile=512)
    yf, yr = _wkv_scan(prep, z_ctx, emit_y=True)
    bonus, g, yh = prep[8], prep[9], prep[10]

    return _merge_mlp(x, mod, ya, yf, yr, yh, bonus, g, gate, prm, tile=512)
```

```python
import functools
import math

import jax
import jax.numpy as jnp
import numpy as np
from jax import lax
from jax.experimental import pallas as pl
from jax.experimental.pallas import tpu as pltpu

F32 = jnp.float32
BF16 = jnp.bfloat16

GRID_W = 64
HEAD_DIM = 64
Q_PER_KV = 4
ATTN_BLOCK = 128
ATTN_QB = 8
ROPE_BASE = 10000.0
NORM_EPS = 1e-6
LNX_EPS = 1e-5 * HEAD_DIM
DECAY_LORA, ICLR_LORA, GATE_LORA = 32, 32, 96
LORA_PAD = 256
CHUNK = 64
SCAN_CHUNKS = 4
PREP_GROUP = 16
LANES = 128
MXU_DIM = 256
PAIR = LANES // HEAD_DIM
NEG = -1e30
LOG2_E = math.log2(math.e)
VMEM_LIMIT = 56 * 1024 * 1024


def _dot(a, b):
    return jnp.dot(a, b, preferred_element_type=F32)


def _dot_nt(a, b):
    return lax.dot_general(a, b, (((1,), (1,)), ((), ())), preferred_element_type=F32)


def _dot_tn(a, b):
    return lax.dot_general(a, b, (((0,), (0,)), ((), ())), preferred_element_type=F32)


def _iota(shape, dim):
    return lax.broadcasted_iota(jnp.int32, shape, dim)


def _head_ones(width):
    r = _iota((width, width), 0) // HEAD_DIM
    c = _iota((width, width), 1) // HEAD_DIM
    return jnp.where(r == c, 1.0, 0.0).astype(BF16)


def _head_sum(x, ones_bd):
    gw = ones_bd.shape[0]
    hi = x.astype(BF16)
    lo = (x - hi.astype(F32)).astype(BF16)
    parts = [_dot(hi[:, j:j + gw], ones_bd) + _dot(lo[:, j:j + gw], ones_bd) for j in range(0, x.shape[1], gw)]
    return jnp.concatenate(parts, axis=1)


def _rmsnorm(x, g):
    ms = jnp.mean(x * x, axis=-1, keepdims=True)
    return x * lax.rsqrt(ms + NORM_EPS) * g


def _params(*sem):
    return pltpu.CompilerParams(dimension_semantics=sem, vmem_limit_bytes=VMEM_LIMIT)


def _const_spec(shape):
    nd = len(shape)
    return pl.BlockSpec(shape, lambda *_: (0,) * nd, pipeline_mode=pl.Buffered(1))


def _ada_kernel(c_ref, w_ref, b_ref, o_ref):
    c = c_ref[...]
    s = c * jax.nn.sigmoid(c)
    o_ref[...] = _dot(s.astype(BF16), w_ref[...].astype(BF16)) + b_ref[...]


def _ada_mod(cc, w_ada, b_ada):
    rows, d = cc.shape
    n = w_ada.shape[1]
    return pl.pallas_call(
        _ada_kernel,
        grid=(n // d,),
        in_specs=[pl.BlockSpec((rows, d), lambda j: (0, 0)),
                  pl.BlockSpec((d, d), lambda j: (0, j)),
                  pl.BlockSpec((1, d), lambda j: (0, j))],
        out_specs=pl.BlockSpec((rows, d), lambda j: (0, j)),
        out_shape=jax.ShapeDtypeStruct((rows, n), F32),
        compiler_params=_params("arbitrary"),
        name="ada_mod",
    )(cc, w_ada, b_ada.reshape(1, n))


def _rope(x, cos_t, sin_t):
    w = x.shape[1]
    half = HEAD_DIM // 2
    first = (_iota(x.shape, 1) % HEAD_DIM) < half
    swapped = jnp.where(first, pltpu.roll(x, w - half, 1), pltpu.roll(x, half, 1))
    reps = w // LANES
    c = jnp.concatenate([cos_t] * reps, axis=1)
    s = jnp.concatenate([sin_t] * reps, axis=1)
    return x * c + swapped * s


def _inproj_kernel(*refs, latent, widths):
    if latent:
        (x_ref, mod_ref, g_ref, w_ref, wg_ref, cos_ref, sin_ref,
         q_ref, k_ref, v_ref, rkv_ref, lora_ref, gate_ref) = refs
    else:
        x_ref, mod_ref, g_ref, w_ref, k_ref, v_ref, rkv_ref, lora_ref = refs
    x = x_ref[0]
    h = _rmsnorm(x, g_ref[...]) * (1.0 + mod_ref[0, 1:2, :]) + mod_ref[0, 0:1, :]
    hb = h.astype(BF16)
    off = widths["q"]

    def seg(name):
        nonlocal off
        lo = off
        off += widths[name]
        return _dot(hb, w_ref[:, lo:off])

    def dup_heads(t):
        first = _iota(t.shape, 1) < HEAD_DIM
        other = pltpu.roll(t, HEAD_DIM, 1)
        return jnp.concatenate([jnp.where(first, t, other), jnp.where(first, other, t)], axis=1)

    kv = seg("kv")
    k, v = dup_heads(kv[:, :LANES]), dup_heads(kv[:, LANES:])
    if latent:
        cos_t, sin_t = cos_ref[...], sin_ref[...]
        q = _dot(hb, w_ref[:, :widths["q"]])
        q_ref[0] = (_rope(q, cos_t, sin_t) * (LOG2_E * HEAD_DIM ** -0.5)).astype(BF16)
        k = _rope(k, cos_t, sin_t)
    k_ref[0] = k.astype(BF16)
    v_ref[0] = v.astype(BF16)
    rkv_ref[0] = seg("rkv")
    lora_ref[0] = seg("lora")
    if latent:
        gate_ref[0] = jax.nn.sigmoid(_dot(hb, wg_ref[...])).astype(BF16)


def _in_proj(x, mod, mod_row, norm_g, weights, widths, tables, *, latent, tile):
    b, l, d = x.shape
    nt = l // tile
    if latent:
        mod_map = lambda i, t: (i, 0, 0)
    else:
        mod_map = lambda i, t: (mod_row, 0, 0)
    tok = lambda w: pl.BlockSpec((1, tile, w), lambda i, t: (i, t, 0))
    in_specs = [tok(d),
                pl.BlockSpec((1,) + mod.shape[1:], mod_map),
                _const_spec((1, d))] + [_const_spec(w.shape) for w in weights]
    args = [x, mod, norm_g.reshape(1, d), *weights]
    out_specs, out_shape = [], []

    def out(w, dt):
        out_specs.append(tok(w))
        out_shape.append(jax.ShapeDtypeStruct((b, l, w), dt))

    if latent:
        in_specs += [pl.BlockSpec((tile, LANES), lambda i, t: (t, 0))] * 2
        args += list(tables)
        out(widths["q"], BF16)
    out(widths["k"], BF16)
    out(widths["v"], BF16)
    out(widths["rkv"], F32)
    out(widths["lora"], F32)
    if latent:
        out(widths["gate"], BF16)
    return pl.pallas_call(
        functools.partial(_inproj_kernel, latent=latent, widths=widths),
        grid=(b, nt),
        in_specs=in_specs,
        out_specs=out_specs,
        out_shape=out_shape,
        compiler_params=_params("parallel", "parallel"),
        name="in_proj_latent" if latent else "in_proj_context",
    )(*args)


def _attn_kernel(sink_ref, q_ref, kp_ref, kc_ref, kn_ref, vp_ref, vc_ref, vn_ref,
                 kx_ref, vx_ref, o_ref, *, n_kv):
    i = pl.program_id(1)
    last = pl.num_programs(1) - 1
    blk = ATTN_BLOCK
    qi = _iota((blk, blk), 0)
    kj = _iota((blk, blk), 1)
    left = _iota((blk, LANES), 1) < HEAD_DIM

    def key_block(refs, j, gs):
        ref_p, ref_c, ref_n = refs[:3]
        if j < 0:
            return ref_p[0, :, gs]
        if j >= ATTN_QB:
            return ref_n[0, :, gs]
        return ref_c[0, j * blk:(j + 1) * blk, gs]

    items = []
    for qb in range(ATTN_QB):
        lo_ok = kj >= qi
        hi_ok = kj <= qi
        if qb == 0:
            lo_ok = lo_ok & (i > 0)
        if qb == ATTN_QB - 1:
            hi_ok = hi_ok & (i < last)
        bias_lo = jnp.concatenate([jnp.where(lo_ok, 0.0, NEG)] * Q_PER_KV, axis=0)
        bias_hi = jnp.concatenate([jnp.where(hi_ok, 0.0, NEG)] * Q_PER_KV, axis=0)
        rows = slice(qb * blk, (qb + 1) * blk)
        q = q_ref[0, rows, :].astype(F32)
        for g in range(n_kv):
            gs = slice(g * LANES, (g + 1) * LANES)
            heads = range(g * Q_PER_KV, (g + 1) * Q_PER_KV)
            qs, sinks = [], []
            for hd in heads:
                qp = q[:, (hd // PAIR) * LANES:(hd // PAIR + 1) * LANES]
                keep = left if hd % PAIR == 0 else jnp.logical_not(left)
                qs.append(jnp.where(keep, qp, 0.0).astype(BF16))
                sinks.append(jnp.full((blk, 1), sink_ref[hd] * LOG2_E, F32))
            items.append({
                "qb": qb, "gs": gs, "rows": rows, "heads": heads, "bias": (bias_lo, bias_hi),
                "qs": jnp.concatenate(qs, axis=0), "sink": jnp.concatenate(sinks, axis=0)})

    def cat(refs, it):
        return jnp.concatenate([key_block(refs, it["qb"] + j, it["gs"]) for j in (-1, 0, 1)]
                               + [refs[3][0, :, it["gs"]]], axis=0)

    s = [_dot_nt(it["qs"], cat((kp_ref, kc_ref, kn_ref, kx_ref), it)) for it in items]
    s = [jnp.concatenate([x[:, :blk] + it["bias"][0], x[:, blk:2 * blk], x[:, 2 * blk:3 * blk] + it["bias"][1],
                          x[:, 3 * blk:]], axis=1) for x, it in zip(s, items)]
    m = [jnp.maximum(jnp.max(x, axis=1, keepdims=True), it["sink"]) for x, it in zip(s, items)]
    p = [jnp.exp2(x - mx) for x, mx in zip(s, m)]
    den = [jnp.sum(x, axis=1, keepdims=True) + jnp.exp2(it["sink"] - mx) for x, mx, it in zip(p, m, items)]
    o = [_dot(x.astype(BF16), cat((vp_ref, vc_ref, vn_ref, vx_ref), it)) / dn for x, dn, it in zip(p, den, items)]
    for x, it in zip(o, items):
        for hd in it["heads"][::PAIR]:
            j = hd - it["heads"][0]
            pair = jnp.where(left, x[j * blk:(j + 1) * blk], x[(j + 1) * blk:(j + 2) * blk])
            col = (hd // PAIR) * LANES
            o_ref[0, it["rows"], col:col + LANES] = pair.astype(BF16)


def _attention(sink, q, kd, vd, kxd, vxd):
    b, l, wq = q.shape
    wk = kd.shape[2]
    lc = kxd.shape[1]
    nb = l // ATTN_BLOCK
    span = ATTN_QB * ATTN_BLOCK
    blk = lambda w, f: pl.BlockSpec((1, ATTN_BLOCK, w), f)
    prev = lambda bi, i: (bi, jnp.maximum(i * ATTN_QB - 1, 0), 0)
    nxt = lambda bi, i: (bi, jnp.minimum((i + 1) * ATTN_QB, nb - 1), 0)
    cur = lambda w: pl.BlockSpec((1, span, w), lambda bi, i: (bi, i, 0))
    ctx = pl.BlockSpec((1, lc, wk), lambda bi, i: (bi, 0, 0))
    return pl.pallas_call(
        functools.partial(_attn_kernel, n_kv=wk // LANES),
        grid=(b, l // span),
        in_specs=[pl.BlockSpec(memory_space=pltpu.SMEM),
                  cur(wq),
                  blk(wk, prev), cur(wk), blk(wk, nxt),
                  blk(wk, prev), cur(wk), blk(wk, nxt),
                  ctx, ctx],
        out_specs=cur(wq),
        out_shape=jax.ShapeDtypeStruct((b, l, wq), BF16),
        compiler_params=_params("parallel", "parallel"),
        name="attention",
    )(sink, q, kd, kd, kd, vd, vd, vd, kxd, vxd)


def _conv3(x, prev_row, next_row, w):
    n = x.shape[0]
    row = _iota((8, x.shape[1]), 0)
    xm = pltpu.roll(x, 1, 0)
    xm = jnp.concatenate([jnp.where(row == 0, prev_row, xm[:8]), xm[8:]], axis=0)
    xp = pltpu.roll(x, n - 1, 0)
    xp = jnp.concatenate([xp[:n - 8], jnp.where(row == 7, next_row, xp[n - 8:])], axis=0)
    return xm * w[0:1] + x * w[1:2] + xp * w[2:3]


def _pair_masks():
    n = 2 * CHUNK
    row = _iota((n, n), 0)
    lane = _iota((n, n), 1)
    top, left = row < CHUNK, lane < CHUNK
    return {"row": row % CHUNK, "lane": lane % CHUNK, "top": top, "left": left,
            "left_h": _iota((CHUNK, n), 1) < CHUNK}


def _stack(a, b):
    return jnp.concatenate([a, b], axis=0)


def _chunk_local(inst, m):
    bf = lambda x: x.astype(BF16)
    diag = jnp.logical_not(m["top"]) & (m["lane"] == m["row"])
    masks = ((m["lane"] < m["row"]) | diag, (m["lane"] > m["row"]) | diag)
    mask_a = [masks[i["rev"]] for i in inst]
    lh = m["left_h"]
    half = CHUNK // 2
    keep_l = jnp.where(m["left"], 1.0, 0.0).astype(BF16)
    keep_r = jnp.where(m["left"], 0.0, 1.0).astype(BF16)
    keep_lh = jnp.where(lh, 1.0, 0.0).astype(BF16)
    keep_rh = jnp.where(lh, 0.0, 1.0).astype(BF16)

    def unfold(xb, anti=False):
        a, b = xb * keep_lh, xb * keep_rh
        return _stack(b, a) if anti else _stack(a, b)

    lhs = [bf(_stack(i["at"], i["rt"])) for i in inst]
    a01 = [_dot_nt(l, _stack(bf(_stack(i["bt"], i["kt"])) * keep_l, bf(_stack(i["kt"], i["bt"])) * keep_r))
           for l, i in zip(lhs, inst)]
    a0 = [jnp.where(ma, a[:, :LANES], 0.0) for ma, a in zip(mask_a, a01)]
    a1 = [jnp.where(ma, a[:, LANES:], 0.0) for ma, a in zip(mask_a, a01)]
    nc = [jnp.where(lh, x[:CHUNK], y[:CHUNK]) for x, y in zip(a0, a1)]
    arb = [jnp.where(lh, x[CHUNK:], y[CHUNK:]) for x, y in zip(a0, a1)]
    ak_ark_sw = [bf(jnp.where(m["left"], y, x)) for x, y in zip(a0, a1)]
    vh = [_dot(a, unfold(bf(i["v"]), anti=True)) for a, i in zip(ak_ark_sw, inst)]
    eye = jnp.where(_iota((CHUNK, LANES), 1) % CHUNK == _iota((CHUNK, LANES), 0), 1.0, 0.0)
    tc = [eye + n for n in nc]
    ncb = [bf(n) for n in nc]
    nc = [_dot(n, unfold(n)) for n in ncb]
    steps = CHUNK.bit_length() - 1
    for _ in range(steps - 2):
        ncb = [bf(n) for n in nc]
        both = [_dot(n, jnp.concatenate([unfold(bf(t)), unfold(n)], axis=1)) for n, t in zip(ncb, tc)]
        tc = [t + r[:, :LANES] for t, r in zip(tc, both)]
        nc = [r[:, LANES:] for r in both]
    inc = [_dot(bf(n[half:]), unfold(bf(t))) for n, t in zip(nc, tc)]
    tc = [jnp.concatenate([t[:half], t[half:] + d], axis=0) for t, d in zip(tc, inc)]
    pq = [_dot(bf(t), jnp.concatenate([unfold(l[:CHUNK]), unfold(bf(x[:CHUNK]))], axis=1))
          for t, l, x in zip(tc, lhs, vh)]
    return [{"pm": r[:, :LANES], "qm": r[:, LANES:], "arb": b, "hm": x[CHUNK:]} for r, b, x in zip(pq, arb, vh)]


def _prep_kernel(*refs, latent, width):
    (rkv_ref, rkv_p, rkv_n, lora_ref, lora_p, lora_n, cw_ref, cwl_ref, kk_ref, ka_ref,
     w0_ref, a0_ref, wl_ref, rk_ref) = refs[:14]
    outs = refs[14:]
    out_refs = dict(zip(("pm", "qm", "rt", "bt", "arb", "kt"), outs[:6]))
    wc_ref, v_ref = outs[6:8]
    if latent:
        bonus_ref, g_ref, hsum_ref = outs[8:]
    t = pl.program_id(1)
    nt = pl.num_programs(1)
    tile = rkv_ref.shape[1]
    w = width
    has_prev = (t > 0).astype(F32)
    has_next = (t < nt - 1).astype(F32)
    u = _conv3(rkv_ref[0], rkv_p[0, 7:8, :] * has_prev, rkv_n[0, 0:1, :] * has_next, cw_ref[...])
    ul = _conv3(lora_ref[0], lora_p[0, 7:8, :] * has_prev, lora_n[0, 0:1, :] * has_next, cwl_ref[...])
    r, k, v = u[:, :w], u[:, w:2 * w], u[:, 2 * w:]
    ones_bd = _head_ones(min(MXU_DIM, w))

    kk = k * kk_ref[...]
    kk = kk * lax.rsqrt(jnp.maximum(_head_sum(kk * kk, ones_bd), 1e-24))

    lane = _iota(ul.shape, 1)
    lin = jnp.where(lane < DECAY_LORA, jnp.tanh(ul),
                    jnp.where(lane < DECAY_LORA + ICLR_LORA, ul, jax.nn.sigmoid(ul)))
    proj = _dot(lin.astype(BF16), wl_ref[...])

    tr = _iota((tile, tile), 0)
    tc = _iota((tile, tile), 1)
    same = (tr // CHUNK) == (tc // CHUNK)
    tri = (jnp.where(same & (tc <= tr), 1.0, 0.0).astype(BF16),
           jnp.where(same & (tc >= tr), 1.0, 0.0).astype(BF16))

    def exact_dot(m, x):
        h1 = x.astype(BF16)
        h2 = (x - h1.astype(F32)).astype(BF16)
        return _dot(m, h1) + _dot(m, h2)

    masks = _pair_masks()
    k_sum = None
    work = []
    for d in range(2):
        z = w0_ref[d:d + 1, :] + proj[:, d * w:(d + 1) * w]
        lw = -math.exp(-0.5) * jax.nn.sigmoid(z)
        a = jax.nn.sigmoid(a0_ref[d:d + 1, :] + proj[:, (2 + d) * w:(3 + d) * w])
        kd = k * (1.0 + (a - 1.0) * ka_ref[...])
        k_sum = kd if k_sum is None else k_sum + kd
        cum = exact_dot(tri[d], lw)
        e_neg = jnp.exp(-cum)
        full = {"at": -kk * jnp.exp(cum - lw), "rt": r * jnp.exp(cum), "bt": kk * a * e_neg,
                "kt": kd * e_neg, "v": v}
        for name in ("rt", "bt", "kt"):
            out_refs[name][d, 0] = full[name].astype(BF16)
        for j in range(tile // CHUNK):
            edge = (j + 1) * CHUNK - 1 if d == 0 else j * CHUNK
            wc_ref[0, j, d:d + 1, :] = jnp.exp(cum[edge:edge + 1, :])
        work += [(d, slice(j * CHUNK, (j + 1) * CHUNK), slice(p * LANES, (p + 1) * LANES), full)
                 for j in range(tile // CHUNK) for p in range(w // LANES)]
    hsum = {}
    for g0 in range(0, len(work), PREP_GROUP):
        group = work[g0:g0 + PREP_GROUP]
        inst = [dict({name: val[rs, ls] for name, val in full.items()}, rev=d) for d, rs, ls, full in group]
        for (d, rs, ls, _), res in zip(group, _chunk_local(inst, masks)):
            seen = hsum.get((rs.start, ls.start))
            hsum[(rs.start, ls.start)] = (rs, ls, res["hm"] if seen is None else seen[2] + res["hm"])
            for name in res:
                if name in out_refs:
                    out_refs[name][d, 0, rs, ls] = res[name].astype(BF16)
    v_ref[0] = v.astype(BF16)
    if latent:
        for rs, ls, val in hsum.values():
            hsum_ref[0, rs, ls] = val
        bonus_ref[0] = _head_sum(r * k_sum * rk_ref[...], ones_bd) * v
        g_ref[0] = proj[:, 4 * w:5 * w].astype(BF16)


def _wkv_prep(rkv, lora, prm, *, latent, tile):
    b, l, w3 = rkv.shape
    w = w3 // 3
    nt = l // tile
    n8 = l // 8
    tok = lambda wd: pl.BlockSpec((1, tile, wd), lambda i, t: (i, t, 0))
    prev = lambda wd: pl.BlockSpec((1, 8, wd), lambda i, t: (i, jnp.maximum(t * (tile // 8) - 1, 0), 0))
    nxt = lambda wd: pl.BlockSpec((1, 8, wd), lambda i, t: (i, jnp.minimum((t + 1) * (tile // 8), n8 - 1), 0))
    wl = lora.shape[2]
    in_specs = [tok(w3), prev(w3), nxt(w3), tok(wl), prev(wl), nxt(wl)]
    consts = [prm["conv_rkv"], prm["conv_lora"], prm["k_k"], prm["k_a"], prm["decay_w0"],
              prm["iclr_a0"], prm["lora_w"], prm["r_k"]]
    in_specs += [_const_spec(c.shape) for c in consts]
    dirtok = pl.BlockSpec((2, 1, tile, w), lambda i, t: (0, i, t, 0))
    out_specs = [dirtok] * 6 + [pl.BlockSpec((1, tile // CHUNK, 2, w), lambda i, t: (i, t, 0, 0)), tok(w)]
    out_shape = [jax.ShapeDtypeStruct((2, b, l, w), BF16)] * 6 + [
        jax.ShapeDtypeStruct((b, l // CHUNK, 2, w), F32), jax.ShapeDtypeStruct((b, l, w), BF16)]
    if latent:
        out_specs += [tok(w), tok(w), tok(w)]
        out_shape += [jax.ShapeDtypeStruct((b, l, w), F32), jax.ShapeDtypeStruct((b, l, w), BF16),
                      jax.ShapeDtypeStruct((b, l, w), F32)]
    return pl.pallas_call(
        functools.partial(_prep_kernel, latent=latent, width=w),
        grid=(b, nt),
        in_specs=in_specs,
        out_specs=out_specs,
        out_shape=out_shape,
        compiler_params=_params("parallel", "parallel"),
        name="wkv_prep_latent" if latent else "wkv_prep_context",
    )(rkv, rkv, rkv, lora, lora, lora, *consts)


def _wkv_kernel(*refs, emit_y, n_pairs):
    names = ("pm", "qm", "rt", "bt", "arb", "kt", "wc", "v")
    n = len(names)
    dir_refs = (dict(zip(names, refs[0:n])), dict(zip(names, refs[n:2 * n])))
    z0_ref = refs[2 * n]
    if emit_y:
        y_refs = refs[2 * n + 1:2 * n + 3]
        z_scr = refs[2 * n + 3]
    else:
        zfin_ref = refs[2 * n + 1]
        z_scr = refs[2 * n + 2]
    c = pl.program_id(0)

    @pl.when(c == 0)
    def _():
        z_scr[...] = z0_ref[...]

    left_h = _iota((CHUNK, LANES), 1) < CHUNK
    keep_l = jnp.where(left_h, 1.0, 0.0).astype(BF16)
    keep_r = jnp.where(left_h, 0.0, 1.0).astype(BF16)

    def unfold(xb):
        return _stack(xb * keep_l, xb * keep_r)

    tiles = [(i, d, p, slice(p * LANES, (p + 1) * LANES))
             for i in range(z_scr.shape[0]) for d in range(2) for p in range(n_pairs)]
    state = [z_scr[i, d, p] for i, d, p, _ in tiles]
    n_sub = dir_refs[0]["wc"].shape[1]
    for step in range(n_sub):
        sub = (step, n_sub - 1 - step)
        rows = [slice(sub[d] * CHUNK, (sub[d] + 1) * CHUNK) for d in range(2)]
        ld = lambda name: [dir_refs[d][name][0, i, rows[d], sl] for i, d, _, sl in tiles]
        v = [dir_refs[d]["v"][i, rows[d], sl] for i, d, _, sl in tiles]
        sbd = [unfold(s.astype(BF16)) for s in state]
        if emit_y:
            ur = [_dot_nt(_stack(pm, rt), s) for pm, rt, s in zip(ld("pm"), ld("rt"), sbd)]
        else:
            ur = [_dot_nt(pm, s) for pm, s in zip(ld("pm"), sbd)]
        u = [x[:CHUNK] + q.astype(F32) for x, q in zip(ur, ld("qm"))]
        ub = [ui.astype(BF16) for ui in u]
        inc = [_dot_tn(_stack(ui, vi), _stack(bt, kt)) for ui, vi, bt, kt in zip(ub, v, ld("bt"), ld("kt"))]
        inc = [jnp.where(left_h, x[:CHUNK], x[CHUNK:]) for x in inc]
        if emit_y:
            yc = [_dot(a, unfold(ui)) for a, ui in zip(ld("arb"), ub)]
            for (i, d, _, sl), x, ys in zip(tiles, ur, yc):
                y_refs[d][i, rows[d], sl] = (x[CHUNK:] + ys).astype(BF16)
        state = [(s + dz) * dir_refs[d]["wc"][i, sub[d], d:d + 1, sl]
                 for (i, d, _, sl), s, dz in zip(tiles, state, inc)]
    for (i, d, p, _), s in zip(tiles, state):
        z_scr[i, d, p] = s

    if not emit_y:
        @pl.when(c == pl.num_programs(0) - 1)
        def _():
            zfin_ref[...] = z_scr[...]


def _wkv_scan(prep, z0, *, emit_y):
    wc = prep[6]
    _, b, l, w = prep[0].shape
    n_sub = min(SCAN_CHUNKS, l // CHUNK)
    blk = n_sub * CHUNK
    nc = l // blk
    n_pairs = w // LANES
    fwd = lambda c: c
    rev = lambda c: nc - 1 - c
    in_specs, args = [], []
    for d, cm in enumerate((fwd, rev)):
        for arr in prep[:6]:
            in_specs.append(pl.BlockSpec((1, b, blk, w), lambda c, d=d, cm=cm: (d, 0, cm(c), 0)))
            args.append(arr)
        in_specs.append(pl.BlockSpec((b, n_sub, 2, w), lambda c, cm=cm: (0, cm(c), 0, 0)))
        args.append(wc)
        in_specs.append(pl.BlockSpec((b, blk, w), lambda c, cm=cm: (0, cm(c), 0)))
        args.append(prep[7])
    zshape = (b, 2, n_pairs, CHUNK, LANES)
    zspec = pl.BlockSpec(zshape, lambda c: (0, 0, 0, 0, 0))
    in_specs.append(zspec)
    args.append(z0)
    if emit_y:
        out_specs = [pl.BlockSpec((b, blk, w), lambda c: (0, c, 0)),
                     pl.BlockSpec((b, blk, w), lambda c: (0, nc - 1 - c, 0))]
        out_shape = [jax.ShapeDtypeStruct((b, l, w), BF16)] * 2
    else:
        out_specs = zspec
        out_shape = jax.ShapeDtypeStruct(zshape, F32)
    return pl.pallas_call(
        functools.partial(_wkv_kernel, emit_y=emit_y, n_pairs=n_pairs),
        grid=(nc,),
        in_specs=in_specs,
        out_specs=out_specs,
        out_shape=out_shape,
        scratch_shapes=[pltpu.VMEM(zshape, F32)],
        compiler_params=_params("arbitrary"),
        name="wkv_scan_latent" if emit_y else "wkv_scan_context",
    )(*args)


def _merge_kernel(x_ref, mod_ref, ya_ref, yf_ref, yr_ref, yh_ref, bonus_ref, g_ref, gate_ref,
                  lnw_ref, lnb_ref, wba_ref, wbr_ref, wo_ref, n2_ref, wu_ref, wd_ref, nf_ref,
                  o_ref, *, ff_chunk):
    x = x_ref[0]
    d = x.shape[1]
    mod = lambda j: mod_ref[0, j:j + 1, :]
    y = yf_ref[0].astype(F32) + yr_ref[0].astype(F32) + yh_ref[0]
    gw = min(MXU_DIM, y.shape[1])
    ones_bd = _head_ones(gw)

    def head_mean(t):
        tb = t.astype(BF16)
        parts = [_dot(tb[:, j:j + gw], ones_bd) for j in range(0, t.shape[1], gw)]
        return jnp.concatenate(parts, axis=1) * (1.0 / HEAD_DIM)

    mu = head_mean(y)
    yc = y - mu
    var = head_mean(yc * yc)
    yn = yc * lax.rsqrt(var + LNX_EPS)
    yr = (yn * lnw_ref[...] + lnb_ref[...] + bonus_ref[0]) * g_ref[0].astype(F32)
    gate = gate_ref[0].astype(F32)
    merged = gate[:, :d] * _dot(ya_ref[0], wba_ref[...]) + gate[:, d:] * _dot(yr.astype(BF16), wbr_ref[...])
    x1 = x + mod(2) * _dot(merged.astype(BF16), wo_ref[...])
    h2 = (_rmsnorm(x1, n2_ref[...]) * (1.0 + mod(4)) + mod(3)).astype(BF16)
    acc = jnp.zeros_like(x1)
    for j in range(wu_ref.shape[1] // ff_chunk):
        cs = slice(j * ff_chunk, (j + 1) * ff_chunk)
        up = jnp.maximum(_dot(h2, wu_ref[:, cs]), 0.0)
        acc = acc + _dot((up * up).astype(BF16), wd_ref[cs, :])
    x2 = x1 + mod(5) * acc
    o_ref[0] = _rmsnorm(x2, nf_ref[...])


def _merge_mlp(x, mod, ya, yf, yr, yh, bonus, g, gate, prm, *, tile):
    b, l, d = x.shape
    tok = lambda arr: pl.BlockSpec((1, tile, arr.shape[2]), lambda i, t: (i, t, 0))
    consts = [prm["lnx_w"], prm["lnx_b"], prm["w_branch_attn"], prm["w_branch_rwkv"], prm["w_out"],
              prm["norm2_g"], prm["w_mlp_up"], prm["w_mlp_down"], prm["norm_f_g"]]
    toks = [ya, yf, yr, yh, bonus, g, gate]
    return pl.pallas_call(
        functools.partial(_merge_kernel, ff_chunk=min(1024, prm["w_mlp_up"].shape[1])),
        grid=(b, l // tile),
        in_specs=[tok(x), pl.BlockSpec((1,) + mod.shape[1:], lambda i, t: (i, 0, 0))]
        + [tok(a) for a in toks] + [_const_spec(c.shape) for c in consts],
        out_specs=tok(x),
        out_shape=jax.ShapeDtypeStruct(x.shape, x.dtype),
        compiler_params=_params("parallel", "parallel"),
        name="merge_mlp",
    )(x, mod, *toks, *consts)


def _rope_tables(l):
    n_freq = HEAD_DIM // 4
    inv_freq = np.power(np.float32(ROPE_BASE), -np.arange(n_freq, dtype=np.float32) / np.float32(n_freq))
    rows = l // GRID_W
    row = np.repeat(np.arange(rows, dtype=np.float32), GRID_W)
    col = np.tile(np.arange(GRID_W, dtype=np.float32), rows)
    ang = np.concatenate([row[:, None] * inv_freq, col[:, None] * inv_freq], axis=-1).astype(np.float32)
    cos, sin = np.cos(ang), np.sin(ang)
    reps = LANES // HEAD_DIM
    return (jnp.asarray(np.tile(np.concatenate([cos, cos], axis=1), (1, reps))),
            jnp.asarray(np.tile(np.concatenate([-sin, sin], axis=1), (1, reps))))


def _pad_cols(w, width):
    return jnp.pad(w, ((0, 0), (0, width - w.shape[1])))


def kernel(x, c, ctx, c_ctx, w_ada, b_ada, norm1_g, w_in, sink, conv_w, decay_w0, decay_w2, iclr_a0, iclr_a2, gate_g2, k_k, k_a, r_k, lnx_w, lnx_b, w_branch_attn, w_branch_rwkv, w_out, norm2_g, w_mlp_up, w_mlp_down, norm_f_g):
    assert w_in.shape[0] == 1, "single-layer block: context tokens are read, never updated"
    b, l, d = x.shape
    attn_w = w_branch_attn.shape[1]
    rw = w_branch_rwkv.shape[1]
    n_q = attn_w // HEAD_DIM
    n_kv = n_q // Q_PER_KV
    kv_w = n_kv * HEAD_DIM
    assert kv_w == LANES and rw % LANES == 0 and ctx.shape[1] % 256 == 0
    assert l % (ATTN_QB * ATTN_BLOCK) == 0 and l % 512 == 0

    w = w_in[0]
    o_k, o_r = attn_w, attn_w + 2 * kv_w
    o_l = o_r + 3 * rw
    o_g = o_l + DECAY_LORA + ICLR_LORA + GATE_LORA
    assert o_l + LORA_PAD <= w.shape[1]
    w_main = w[:, :o_l + LORA_PAD].astype(BF16)
    w_gate = w[:, o_g:].astype(BF16)
    widths = {"q": attn_w, "kv": 2 * kv_w, "k": PAIR * kv_w, "v": PAIR * kv_w, "rkv": 3 * rw,
              "lora": LORA_PAD, "gate": 2 * d}

    cw = conv_w[0]
    used = DECAY_LORA + ICLR_LORA + GATE_LORA
    lora_w = jnp.concatenate([
        jnp.pad(jnp.concatenate([decay_w2[0, 0], decay_w2[0, 1]], axis=1), ((0, 0), (0, 3 * rw))),
        jnp.pad(jnp.concatenate([iclr_a2[0, 0], iclr_a2[0, 1]], axis=1), ((0, 0), (2 * rw, rw))),
        jnp.pad(gate_g2[0], ((0, LORA_PAD - used), (4 * rw, 0)))], axis=0)
    prm = {
        "conv_rkv": cw[:, :3 * rw], "conv_lora": _pad_cols(cw[:, 3 * rw:], LORA_PAD),
        "k_k": k_k[0].reshape(1, rw), "k_a": k_a[0].reshape(1, rw),
        "decay_w0": decay_w0[0], "iclr_a0": iclr_a0[0], "lora_w": lora_w.astype(BF16),
        "r_k": r_k[0].reshape(1, rw),
        "lnx_w": lnx_w[0].reshape(1, rw), "lnx_b": lnx_b[0].reshape(1, rw),
        "w_branch_attn": w_branch_attn[0].astype(BF16), "w_branch_rwkv": w_branch_rwkv[0].astype(BF16),
        "w_out": w_out[0].astype(BF16), "norm2_g": norm2_g[0].reshape(1, d),
        "w_mlp_up": w_mlp_up[0].astype(BF16), "w_mlp_down": w_mlp_down[0].astype(BF16),
        "norm_f_g": norm_f_g.reshape(1, d),
    }

    rows = -(-(b + 1) // 8) * 8
    cc = jnp.concatenate([c, c_ctx[None, :], jnp.zeros((rows - b - 1, d), F32)], axis=0)
    mod = _ada_mod(cc, w_ada[0], b_ada[0]).reshape(rows, -1, d)

    q, kd, vd, rkv, lora, gate = _in_proj(x, mod, b, norm1_g[0], [w_main, w_gate], widths, _rope_tables(l),
                                          latent=True, tile=512)
    kxd, vxd, rkv_c, lora_c = _in_proj(ctx, mod, b, norm1_g[0], [w_main], widths, None,
                                       latent=False, tile=256)
    ya = _attention(sink[0], q, kd, vd, kxd, vxd)

    prep_c = _wkv_prep(rkv_c, lora_c, prm, latent=False, tile=256)
    z_ctx = _wkv_scan(prep_c, jnp.zeros((b, 2, rw // LANES, CHUNK, LANES), F32), emit_y=False)
    prep = _wkv_prep(rkv, lora, prm, latent=True, tile=512)
    yf, yr = _wkv_scan(prep, z_ctx, emit_y=True)
    bonus, g, yh = prep[8], prep[9], prep[10]

    return _merge_mlp(x, mod, ya, yf, yr, yh, bonus, g, gate, prm, tile=512)
```

```python
import functools
import math

import jax
import jax.numpy as jnp
import numpy as np
from jax import lax
from jax.experimental import pallas as pl
from jax.experimental.pallas import tpu as pltpu

F32 = jnp.float32
BF16 = jnp.bfloat16

GRID_W = 64
HEAD_DIM = 64
Q_PER_KV = 4
ATTN_BLOCK = 128
ATTN_QB = 8
ROPE_BASE = 10000.0
NORM_EPS = 1e-6
LNX_EPS = 1e-5 * HEAD_DIM
DECAY_LORA, ICLR_LORA, GATE_LORA = 32, 32, 96
LORA_PAD = 256
CHUNK = 64
SCAN_CHUNKS = 4
PREP_GROUP = 16
LANES = 128
MXU_DIM = 256
PAIR = LANES // HEAD_DIM
NEG = -1e30
LOG2_E = math.log2(math.e)
VMEM_LIMIT = 56 * 1024 * 1024


def _dot(a, b):
    return jnp.dot(a, b, preferred_element_type=F32)


def _dot_nt(a, b):
    return lax.dot_general(a, b, (((1,), (1,)), ((), ())), preferred_element_type=F32)


def _dot_tn(a, b):
    return lax.dot_general(a, b, (((0,), (0,)), ((), ())), preferred_element_type=F32)


def _iota(shape, dim):
    return lax.broadcasted_iota(jnp.int32, shape, dim)


def _head_ones(width):
    r = _iota((width, width), 0) // HEAD_DIM
    c = _iota((width, width), 1) // HEAD_DIM
    return jnp.where(r == c, 1.0, 0.0).astype(BF16)


def _head_sum(x, ones_bd):
    gw = ones_bd.shape[0]
    hi = x.astype(BF16)
    lo = (x - hi.astype(F32)).astype(BF16)
    parts = [_dot(hi[:, j:j + gw], ones_bd) + _dot(lo[:, j:j + gw], ones_bd) for j in range(0, x.shape[1], gw)]
    return jnp.concatenate(parts, axis=1)


def _rmsnorm(x, g):
    ms = jnp.mean(x * x, axis=-1, keepdims=True)
    return x * lax.rsqrt(ms + NORM_EPS) * g


def _params(*sem):
    return pltpu.CompilerParams(dimension_semantics=sem, vmem_limit_bytes=VMEM_LIMIT)


def _const_spec(shape):
    nd = len(shape)
    return pl.BlockSpec(shape, lambda *_: (0,) * nd, pipeline_mode=pl.Buffered(1))


def _ada_kernel(c_ref, w_ref, b_ref, o_ref):
    c = c_ref[...]
    s = c * jax.nn.sigmoid(c)
    o_ref[...] = _dot(s.astype(BF16), w_ref[...].astype(BF16)) + b_ref[...]


def _ada_mod(cc, w_ada, b_ada):
    rows, d = cc.shape
    n = w_ada.shape[1]
    return pl.pallas_call(
        _ada_kernel,
        grid=(n // d,),
        in_specs=[pl.BlockSpec((rows, d), lambda j: (0, 0)),
                  pl.BlockSpec((d, d), lambda j: (0, j)),
                  pl.BlockSpec((1, d), lambda j: (0, j))],
        out_specs=pl.BlockSpec((rows, d), lambda j: (0, j)),
        out_shape=jax.ShapeDtypeStruct((rows, n), F32),
        compiler_params=_params("arbitrary"),
        name="ada_mod",
    )(cc, w_ada, b_ada.reshape(1, n))


def _rope(x, cos_t, sin_t):
    w = x.shape[1]
    half = HEAD_DIM // 2
    first = (_iota(x.shape, 1) % HEAD_DIM) < half
    swapped = jnp.where(first, pltpu.roll(x, w - half, 1), pltpu.roll(x, half, 1))
    reps = w // LANES
    c = jnp.concatenate([cos_t] * reps, axis=1)
    s = jnp.concatenate([sin_t] * reps, axis=1)
    return x * c + swapped * s


def _inproj_kernel(*refs, latent, widths):
    if latent:
        (x_ref, mod_ref, g_ref, w_ref, wg_ref, cos_ref, sin_ref,
         q_ref, k_ref, v_ref, rkv_ref, lora_ref, gate_ref) = refs
    else:
        x_ref, mod_ref, g_ref, w_ref, k_ref, v_ref, rkv_ref, lora_ref = refs
    x = x_ref[0]
    h = _rmsnorm(x, g_ref[...]) * (1.0 + mod_ref[0, 1:2, :]) + mod_ref[0, 0:1, :]
    hb = h.astype(BF16)
    off = widths["q"]

    def seg(name):
        nonlocal off
        lo = off
        off += widths[name]
        return _dot(hb, w_ref[:, lo:off])

    def dup_heads(t):
        first = _iota(t.shape, 1) < HEAD_DIM
        other = pltpu.roll(t, HEAD_DIM, 1)
        return jnp.concatenate([jnp.where(first, t, other), jnp.where(first, other, t)], axis=1)

    kv = seg("kv")
    k, v = dup_heads(kv[:, :LANES]), dup_heads(kv[:, LANES:])
    if latent:
        cos_t, sin_t = cos_ref[...], sin_ref[...]
        q = _dot(hb, w_ref[:, :widths["q"]])
        q_ref[0] = (_rope(q, cos_t, sin_t) * (LOG2_E * HEAD_DIM ** -0.5)).astype(BF16)
        k = _rope(k, cos_t, sin_t)
    k_ref[0] = k.astype(BF16)
    v_ref[0] = v.astype(BF16)
    rkv_ref[0] = seg("rkv")
    lora_ref[0] = seg("lora")
    if latent:
        gate_ref[0] = jax.nn.sigmoid(_dot(hb, wg_ref[...])).astype(BF16)


def _in_proj(x, mod, mod_row, norm_g, weights, widths, tables, *, latent, tile):
    b, l, d = x.shape
    nt = l // tile
    if latent:
        mod_map = lambda i, t: (i, 0, 0)
    else:
        mod_map = lambda i, t: (mod_row, 0, 0)
    tok = lambda w: pl.BlockSpec((1, tile, w), lambda i, t: (i, t, 0))
    in_specs = [tok(d),
                pl.BlockSpec((1,) + mod.shape[1:], mod_map),
                _const_spec((1, d))] + [_const_spec(w.shape) for w in weights]
    args = [x, mod, norm_g.reshape(1, d), *weights]
    out_specs, out_shape = [], []

    def out(w, dt):
        out_specs.append(tok(w))
        out_shape.append(jax.ShapeDtypeStruct((b, l, w), dt))

    if latent:
        in_specs += [pl.BlockSpec((tile, LANES), lambda i, t: (t, 0))] * 2
        args += list(tables)
        out(widths["q"], BF16)
    out(widths["k"], BF16)
    out(widths["v"], BF16)
    out(widths["rkv"], F32)
    out(widths["lora"], F32)
    if latent:
        out(widths["gate"], BF16)
    return pl.pallas_call(
        functools.partial(_inproj_kernel, latent=latent, widths=widths),
        grid=(b, nt),
        in_specs=in_specs,
        out_specs=out_specs,
        out_shape=out_shape,
        compiler_params=_params("parallel", "parallel"),
        name="in_proj_latent" if latent else "in_proj_context",
    )(*args)


def _attn_kernel(sink_ref, q_ref, kp_ref, kc_ref, kn_ref, vp_ref, vc_ref, vn_ref,
                 kx_ref, vx_ref, o_ref, *, n_kv):
    i = pl.program_id(1)
    last = pl.num_programs(1) - 1
    blk = ATTN_BLOCK
    qi = _iota((blk, blk), 0)
    kj = _iota((blk, blk), 1)
    left = _iota((blk, LANES), 1) < HEAD_DIM

    def key_block(refs, j, gs):
        ref_p, ref_c, ref_n = refs[:3]
        if j < 0:
            return ref_p[0, :, gs]
        if j >= ATTN_QB:
            return ref_n[0, :, gs]
        return ref_c[0, j * blk:(j + 1) * blk, gs]

    items = []
    for qb in range(ATTN_QB):
        lo_ok = kj >= qi
        hi_ok = kj <= qi
        if qb == 0:
            lo_ok = lo_ok & (i > 0)
        if qb == ATTN_QB - 1:
            hi_ok = hi_ok & (i < last)
        bias_lo = jnp.concatenate([jnp.where(lo_ok, 0.0, NEG)] * Q_PER_KV, axis=0)
        bias_hi = jnp.concatenate([jnp.where(hi_ok, 0.0, NEG)] * Q_PER_KV, axis=0)
        rows = slice(qb * blk, (qb + 1) * blk)
        q = q_ref[0, rows, :].astype(F32)
        for g in range(n_kv):
            gs = slice(g * LANES, (g + 1) * LANES)
            heads = range(g * Q_PER_KV, (g + 1) * Q_PER_KV)
            qs, sinks = [], []
            for hd in heads:
                qp = q[:, (hd // PAIR) * LANES:(hd // PAIR + 1) * LANES]
                keep = left if hd % PAIR == 0 else jnp.logical_not(left)
                qs.append(jnp.where(keep, qp, 0.0).astype(BF16))
                sinks.append(jnp.full((blk, 1), sink_ref[hd] * LOG2_E, F32))
            items.append({
                "qb": qb, "gs": gs, "rows": rows, "heads": heads, "bias": (bias_lo, bias_hi),
                "qs": jnp.concatenate(qs, axis=0), "sink": jnp.concatenate(sinks, axis=0)})

    def cat(refs, it):
        return jnp.concatenate([key_block(refs, it["qb"] + j, it["gs"]) for j in (-1, 0, 1)]
                               + [refs[3][0, :, it["gs"]]], axis=0)

    s = [_dot_nt(it["qs"], cat((kp_ref, kc_ref, kn_ref, kx_ref), it)) for it in items]
    s = [jnp.concatenate([x[:, :blk] + it["bias"][0], x[:, blk:2 * blk], x[:, 2 * blk:3 * blk] + it["bias"][1],
                          x[:, 3 * blk:]], axis=1) for x, it in zip(s, items)]
    m = [jnp.maximum(jnp.max(x, axis=1, keepdims=True), it["sink"]) for x, it in zip(s, items)]
    p = [jnp.exp2(x - mx) for x, mx in zip(s, m)]
    den = [jnp.sum(x, axis=1, keepdims=True) + jnp.exp2(it["sink"] - mx) for x, mx, it in zip(p, m, items)]
    o = [_dot(x.astype(BF16), cat((vp_ref, vc_ref, vn_ref, vx_ref), it)) / dn for x, dn, it in zip(p, den, items)]
    for x, it in zip(o, items):
        for hd in it["heads"][::PAIR]:
            j = hd - it["heads"][0]
            pair = jnp.where(left, x[j * blk:(j + 1) * blk], x[(j + 1) * blk:(j + 2) * blk])
            col = (hd // PAIR) * LANES
            o_ref[0, it["rows"], col:col + LANES] = pair.astype(BF16)


def _attention(sink, q, kd, vd, kxd, vxd):
    b, l, wq = q.shape
    wk = kd.shape[2]
    lc = kxd.shape[1]
    nb = l // ATTN_BLOCK
    span = ATTN_QB * ATTN_BLOCK
    blk = lambda w, f: pl.BlockSpec((1, ATTN_BLOCK, w), f)
    prev = lambda bi, i: (bi, jnp.maximum(i * ATTN_QB - 1, 0), 0)
    nxt = lambda bi, i: (bi, jnp.minimum((i + 1) * ATTN_QB, nb - 1), 0)
    cur = lambda w: pl.BlockSpec((1, span, w), lambda bi, i: (bi, i, 0))
    ctx = pl.BlockSpec((1, lc, wk), lambda bi, i: (bi, 0, 0))
    return pl.pallas_call(
        functools.partial(_attn_kernel, n_kv=wk // LANES),
        grid=(b, l // span),
        in_specs=[pl.BlockSpec(memory_space=pltpu.SMEM),
                  cur(wq),
                  blk(wk, prev), cur(wk), blk(wk, nxt),
                  blk(wk, prev), cur(wk), blk(wk, nxt),
                  ctx, ctx],
        out_specs=cur(wq),
        out_shape=jax.ShapeDtypeStruct((b, l, wq), BF16),
        compiler_params=_params("parallel", "parallel"),
        name="attention",
    )(sink, q, kd, kd, kd, vd, vd, vd, kxd, vxd)


def _conv3(x, prev_row, next_row, w):
    n = x.shape[0]
    row = _iota((8, x.shape[1]), 0)
    xm = pltpu.roll(x, 1, 0)
    xm = jnp.concatenate([jnp.where(row == 0, prev_row, xm[:8]), xm[8:]], axis=0)
    xp = pltpu.roll(x, n - 1, 0)
    xp = jnp.concatenate([xp[:n - 8], jnp.where(row == 7, next_row, xp[n - 8:])], axis=0)
    return xm * w[0:1] + x * w[1:2] + xp * w[2:3]


def _pair_masks():
    n = 2 * CHUNK
    row = _iota((n, n), 0)
    lane = _iota((n, n), 1)
    top, left = row < CHUNK, lane < CHUNK
    return {"row": row % CHUNK, "lane": lane % CHUNK, "top": top, "left": left,
            "left_h": _iota((CHUNK, n), 1) < CHUNK}


def _stack(a, b):
    return jnp.concatenate([a, b], axis=0)


def _chunk_local(inst, m):
    bf = lambda x: x.astype(BF16)
    diag = jnp.logical_not(m["top"]) & (m["lane"] == m["row"])
    masks = ((m["lane"] < m["row"]) | diag, (m["lane"] > m["row"]) | diag)
    mask_a = [masks[i["rev"]] for i in inst]
    lh = m["left_h"]
    half = CHUNK // 2
    keep_l = jnp.where(m["left"], 1.0, 0.0).astype(BF16)
    keep_r = jnp.where(m["left"], 0.0, 1.0).astype(BF16)
    keep_lh = jnp.where(lh, 1.0, 0.0).astype(BF16)
    keep_rh = jnp.where(lh, 0.0, 1.0).astype(BF16)

    def unfold(xb, anti=False):
        a, b = xb * keep_lh, xb * keep_rh
        return _stack(b, a) if anti else _stack(a, b)

    lhs = [bf(_stack(i["at"], i["rt"])) for i in inst]
    a01 = [_dot_nt(l, _stack(bf(_stack(i["bt"], i["kt"])) * keep_l, bf(_stack(i["kt"], i["bt"])) * keep_r))
           for l, i in zip(lhs, inst)]
    a0 = [jnp.where(ma, a[:, :LANES], 0.0) for ma, a in zip(mask_a, a01)]
    a1 = [jnp.where(ma, a[:, LANES:], 0.0) for ma, a in zip(mask_a, a01)]
    nc = [jnp.where(lh, x[:CHUNK], y[:CHUNK]) for x, y in zip(a0, a1)]
    arb = [jnp.where(lh, x[CHUNK:], y[CHUNK:]) for x, y in zip(a0, a1)]
    ak_ark_sw = [bf(jnp.where(m["left"], y, x)) for x, y in zip(a0, a1)]
    vh = [_dot(a, unfold(bf(i["v"]), anti=True)) for a, i in zip(ak_ark_sw, inst)]
    eye = jnp.where(_iota((CHUNK, LANES), 1) % CHUNK == _iota((CHUNK, LANES), 0), 1.0, 0.0)
    tc = [eye + n for n in nc]
    ncb = [bf(n) for n in nc]
    nc = [_dot(n, unfold(n)) for n in ncb]
    steps = CHUNK.bit_length() - 1
    for _ in range(steps - 2):
        ncb = [bf(n) for n in nc]
        both = [_dot(n, jnp.concatenate([unfold(bf(t)), unfold(n)], axis=1)) for n, t in zip(ncb, tc)]
        tc = [t + r[:, :LANES] for t, r in zip(tc, both)]
        nc = [r[:, LANES:] for r in both]
    inc = [_dot(bf(n[half:]), unfold(bf(t))) for n, t in zip(nc, tc)]
    tc = [jnp.concatenate([t[:half], t[half:] + d], axis=0) for t, d in zip(tc, inc)]
    pq = [_dot(bf(t), jnp.concatenate([unfold(l[:CHUNK]), unfold(bf(x[:CHUNK]))], axis=1))
          for t, l, x in zip(tc, lhs, vh)]
    return [{"pm": r[:, :LANES], "qm": r[:, LANES:], "arb": b, "hm": x[CHUNK:]} for r, b, x in zip(pq, arb, vh)]


def _prep_kernel(*refs, latent, width):
    (rkv_ref, rkv_p, rkv_n, lora_ref, lora_p, lora_n, cw_ref, cwl_ref, kk_ref, ka_ref,
     w0_ref, a0_ref, wl_ref, rk_ref) = refs[:14]
    outs = refs[14:]
    out_refs = dict(zip(("pm", "qm", "rt", "bt", "arb", "kt"), outs[:6]))
    wc_ref, v_ref = outs[6:8]
    if latent:
        bonus_ref, g_ref, hsum_ref = outs[8:]
    t = pl.program_id(1)
    nt = pl.num_programs(1)
    tile = rkv_ref.shape[1]
    w = width
    has_prev = (t > 0).astype(F32)
    has_next = (t < nt - 1).astype(F32)
    u = _conv3(rkv_ref[0], rkv_p[0, 7:8, :] * has_prev, rkv_n[0, 0:1, :] * has_next, cw_ref[...])
    ul = _conv3(lora_ref[0], lora_p[0, 7:8, :] * has_prev, lora_n[0, 0:1, :] * has_next, cwl_ref[...])
    r, k, v = u[:, :w], u[:, w:2 * w], u[:, 2 * w:]
    ones_bd = _head_ones(min(MXU_DIM, w))

    kk = k * kk_ref[...]
    kk = kk * lax.rsqrt(jnp.maximum(_head_sum(kk * kk, ones_bd), 1e-24))

    lane = _iota(ul.shape, 1)
    lin = jnp.where(lane < DECAY_LORA, jnp.tanh(ul),
                    jnp.where(lane < DECAY_LORA + ICLR_LORA, ul, jax.nn.sigmoid(ul)))
    proj = _dot(lin.astype(BF16), wl_ref[...])

    tr = _iota((tile, tile), 0)
    tc = _iota((tile, tile), 1)
    same = (tr // CHUNK) == (tc // CHUNK)
    tri = (jnp.where(same & (tc <= tr), 1.0, 0.0).astype(BF16),
           jnp.where(same & (tc >= tr), 1.0, 0.0).astype(BF16))

    def exact_dot(m, x):
        h1 = x.astype(BF16)
        h2 = (x - h1.astype(F32)).astype(BF16)
        return _dot(m, h1) + _dot(m, h2)

    masks = _pair_masks()
    k_sum = None
    work = []
    for d in range(2):
        z = w0_ref[d:d + 1, :] + proj[:, d * w:(d + 1) * w]
        lw = -math.exp(-0.5) * jax.nn.sigmoid(z)
        a = jax.nn.sigmoid(a0_ref[d:d + 1, :] + proj[:, (2 + d) * w:(3 + d) * w])
        kd = k * (1.0 + (a - 1.0) * ka_ref[...])
        k_sum = kd if k_sum is None else k_sum + kd
        cum = exact_dot(tri[d], lw)
        e_neg = jnp.exp(-cum)
        full = {"at": -kk * jnp.exp(cum - lw), "rt": r * jnp.exp(cum), "bt": kk * a * e_neg,
                "kt": kd * e_neg, "v": v}
        for name in ("rt", "bt", "kt"):
            out_refs[name][d, 0] = full[name].astype(BF16)
        for j in range(tile // CHUNK):
            edge = (j + 1) * CHUNK - 1 if d == 0 else j * CHUNK
            wc_ref[0, j, d:d + 1, :] = jnp.exp(cum[edge:edge + 1, :])
        work += [(d, slice(j * CHUNK, (j + 1) * CHUNK), slice(p * LANES, (p + 1) * LANES), full)
                 for j in range(tile // CHUNK) for p in range(w // LANES)]
    hsum = {}
    for g0 in range(0, len(work), PREP_GROUP):
        group = work[g0:g0 + PREP_GROUP]
        inst = [dict({name: val[rs, ls] for name, val in full.items()}, rev=d) for d, rs, ls, full in group]
        for (d, rs, ls, _), res in zip(group, _chunk_local(inst, masks)):
            seen = hsum.get((rs.start, ls.start))
            hsum[(rs.start, ls.start)] = (rs, ls, res["hm"] if seen is None else seen[2] + res["hm"])
            for name in res:
                if name in out_refs:
                    out_refs[name][d, 0, rs, ls] = res[name].astype(BF16)
    v_ref[0] = v.astype(BF16)
    if latent:
        for rs, ls, val in hsum.values():
            hsum_ref[0, rs, ls] = val
        bonus_ref[0] = _head_sum(r * k_sum * rk_ref[...], ones_bd) * v
        g_ref[0] = proj[:, 4 * w:5 * w].astype(BF16)


def _wkv_prep(rkv, lora, prm, *, latent, tile):
    b, l, w3 = rkv.shape
    w = w3 // 3
    nt = l // tile
    n8 = l // 8
    tok = lambda wd: pl.BlockSpec((1, tile, wd), lambda i, t: (i, t, 0))
    prev = lambda wd: pl.BlockSpec((1, 8, wd), lambda i, t: (i, jnp.maximum(t * (tile // 8) - 1, 0), 0))
    nxt = lambda wd: pl.BlockSpec((1, 8, wd), lambda i, t: (i, jnp.minimum((t + 1) * (tile // 8), n8 - 1), 0))
    wl = lora.shape[2]
    in_specs = [tok(w3), prev(w3), nxt(w3), tok(wl), prev(wl), nxt(wl)]
    consts = [prm["conv_rkv"], prm["conv_lora"], prm["k_k"], prm["k_a"], prm["decay_w0"],
              prm["iclr_a0"], prm["lora_w"], prm["r_k"]]
    in_specs += [_const_spec(c.shape) for c in consts]
    dirtok = pl.BlockSpec((2, 1, tile, w), lambda i, t: (0, i, t, 0))
    out_specs = [dirtok] * 6 + [pl.BlockSpec((1, tile // CHUNK, 2, w), lambda i, t: (i, t, 0, 0)), tok(w)]
    out_shape = [jax.ShapeDtypeStruct((2, b, l, w), BF16)] * 6 + [
        jax.ShapeDtypeStruct((b, l // CHUNK, 2, w), F32), jax.ShapeDtypeStruct((b, l, w), BF16)]
    if latent:
        out_specs += [tok(w), tok(w), tok(w)]
        out_shape += [jax.ShapeDtypeStruct((b, l, w), F32), jax.ShapeDtypeStruct((b, l, w), BF16),
                      jax.ShapeDtypeStruct((b, l, w), F32)]
    return pl.pallas_call(
        functools.partial(_prep_kernel, latent=latent, width=w),
        grid=(b, nt),
        in_specs=in_specs,
        out_specs=out_specs,
        out_shape=out_shape,
        compiler_params=_params("parallel", "parallel"),
        name="wkv_prep_latent" if latent else "wkv_prep_context",
    )(rkv, rkv, rkv, lora, lora, lora, *consts)


def _wkv_kernel(*refs, emit_y, n_pairs):
    names = ("pm", "qm", "rt", "bt", "arb", "kt", "wc", "v")
    n = len(names)
    dir_refs = (dict(zip(names, refs[0:n])), dict(zip(names, refs[n:2 * n])))
    z0_ref = refs[2 * n]
    if emit_y:
        y_refs = refs[2 * n + 1:2 * n + 3]
        z_scr = refs[2 * n + 3]
    else:
        zfin_ref = refs[2 * n + 1]
        z_scr = refs[2 * n + 2]
    c = pl.program_id(0)

    @pl.when(c == 0)
    def _():
        z_scr[...] = z0_ref[...]

    left_h = _iota((CHUNK, LANES), 1) < CHUNK
    keep_l = jnp.where(left_h, 1.0, 0.0).astype(BF16)
    keep_r = jnp.where(left_h, 0.0, 1.0).astype(BF16)

    def unfold(xb):
        return _stack(xb * keep_l, xb * keep_r)

    tiles = [(i, d, p, slice(p * LANES, (p + 1) * LANES))
             for i in range(z_scr.shape[0]) for d in range(2) for p in range(n_pairs)]
    state = [z_scr[i, d, p] for i, d, p, _ in tiles]
    n_sub = dir_refs[0]["wc"].shape[1]
    for step in range(n_sub):
        sub = (step, n_sub - 1 - step)
        rows = [slice(sub[d] * CHUNK, (sub[d] + 1) * CHUNK) for d in range(2)]
        ld = lambda name: [dir_refs[d][name][0, i, rows[d], sl] for i, d, _, sl in tiles]
        v = [dir_refs[d]["v"][i, rows[d], sl] for i, d, _, sl in tiles]
        sbd = [unfold(s.astype(BF16)) for s in state]
        if emit_y:
            ur = [_dot_nt(_stack(pm, rt), s) for pm, rt, s in zip(ld("pm"), ld("rt"), sbd)]
        else:
            ur = [_dot_nt(pm, s) for pm, s in zip(ld("pm"), sbd)]
        u = [x[:CHUNK] + q.astype(F32) for x, q in zip(ur, ld("qm"))]
        ub = [ui.astype(BF16) for ui in u]
        inc = [_dot_tn(_stack(ui, vi), _stack(bt, kt)) for ui, vi, bt, kt in zip(ub, v, ld("bt"), ld("kt"))]
        inc = [jnp.where(left_h, x[:CHUNK], x[CHUNK:]) for x in inc]
        if emit_y:
            yc = [_dot(a, unfold(ui)) for a, ui in zip(ld("arb"), ub)]
            for (i, d, _, sl), x, ys in zip(tiles, ur, yc):
                y_refs[d][i, rows[d], sl] = (x[CHUNK:] + ys).astype(BF16)
        state = [(s + dz) * dir_refs[d]["wc"][i, sub[d], d:d + 1, sl]
                 for (i, d, _, sl), s, dz in zip(tiles, state, inc)]
    for (i, d, p, _), s in zip(tiles, state):
        z_scr[i, d, p] = s

    if not emit_y:
        @pl.when(c == pl.num_programs(0) - 1)
        def _():
            zfin_ref[...] = z_scr[...]


def _wkv_scan(prep, z0, *, emit_y):
    wc = prep[6]
    _, b, l, w = prep[0].shape
    n_sub = min(SCAN_CHUNKS, l // CHUNK)
    blk = n_sub * CHUNK
    nc = l // blk
    n_pairs = w // LANES
    fwd = lambda c: c
    rev = lambda c: nc - 1 - c
    in_specs, args = [], []
    for d, cm in enumerate((fwd, rev)):
        for arr in prep[:6]:
            in_specs.append(pl.BlockSpec((1, b, blk, w), lambda c, d=d, cm=cm: (d, 0, cm(c), 0)))
            args.append(arr)
        in_specs.append(pl.BlockSpec((b, n_sub, 2, w), lambda c, cm=cm: (0, cm(c), 0, 0)))
        args.append(wc)
        in_specs.append(pl.BlockSpec((b, blk, w), lambda c, cm=cm: (0, cm(c), 0)))
        args.append(prep[7])
    zshape = (b, 2, n_pairs, CHUNK, LANES)
    zspec = pl.BlockSpec(zshape, lambda c: (0, 0, 0, 0, 0))
    in_specs.append(zspec)
    args.append(z0)
    if emit_y:
        out_specs = [pl.BlockSpec((b, blk, w), lambda c: (0, c, 0)),
                     pl.BlockSpec((b, blk, w), lambda c: (0, nc - 1 - c, 0))]
        out_shape = [jax.ShapeDtypeStruct((b, l, w), BF16)] * 2
    else:
        out_specs = zspec
        out_shape = jax.ShapeDtypeStruct(zshape, F32)
    return pl.pallas_call(
        functools.partial(_wkv_kernel, emit_y=emit_y, n_pairs=n_pairs),
        grid=(nc,),
        in_specs=in_specs,
        out_specs=out_specs,
        out_shape=out_shape,
        scratch_shapes=[pltpu.VMEM(zshape, F32)],
        compiler_params=_params("arbitrary"),
        name="wkv_scan_latent" if emit_y else "wkv_scan_context",
    )(*args)


def _merge_kernel(x_ref, mod_ref, ya_ref, yf_ref, yr_ref, yh_ref, bonus_ref, g_ref, gate_ref,
                  lnw_ref, lnb_ref, wba_ref, wbr_ref, wo_ref, n2_ref, wu_ref, wd_ref, nf_ref,
                  o_ref, *, ff_chunk):
    x = x_ref[0]
    d = x.shape[1]
    mod = lambda j: mod_ref[0, j:j + 1, :]
    y = yf_ref[0].astype(F32) + yr_ref[0].astype(F32) + yh_ref[0]
    gw = min(MXU_DIM, y.shape[1])
    ones_bd = _head_ones(gw)

    def head_mean(t):
        tb = t.astype(BF16)
        parts = [_dot(tb[:, j:j + gw], ones_bd) for j in range(0, t.shape[1], gw)]
        return jnp.concatenate(parts, axis=1) * (1.0 / HEAD_DIM)

    mu = head_mean(y)
    yc = y - mu
    var = head_mean(yc * yc)
    yn = yc * lax.rsqrt(var + LNX_EPS)
    yr = (yn * lnw_ref[...] + lnb_ref[...] + bonus_ref[0]) * g_ref[0].astype(F32)
    gate = gate_ref[0].astype(F32)
    merged = gate[:, :d] * _dot(ya_ref[0], wba_ref[...]) + gate[:, d:] * _dot(yr.astype(BF16), wbr_ref[...])
    x1 = x + mod(2) * _dot(merged.astype(BF16), wo_ref[...])
    h2 = (_rmsnorm(x1, n2_ref[...]) * (1.0 + mod(4)) + mod(3)).astype(BF16)
    acc = jnp.zeros_like(x1)
    for j in range(wu_ref.shape[1] // ff_chunk):
        cs = slice(j * ff_chunk, (j + 1) * ff_chunk)
        up = jnp.maximum(_dot(h2, wu_ref[:, cs]), 0.0)
        acc = acc + _dot((up * up).astype(BF16), wd_ref[cs, :])
    x2 = x1 + mod(5) * acc
    o_ref[0] = _rmsnorm(x2, nf_ref[...])


def _merge_mlp(x, mod, ya, yf, yr, yh, bonus, g, gate, prm, *, tile):
    b, l, d = x.shape
    tok = lambda arr: pl.BlockSpec((1, tile, arr.shape[2]), lambda i, t: (i, t, 0))
    consts = [prm["lnx_w"], prm["lnx_b"], prm["w_branch_attn"], prm["w_branch_rwkv"], prm["w_out"],
              prm["norm2_g"], prm["w_mlp_up"], prm["w_mlp_down"], prm["norm_f_g"]]
    toks = [ya, yf, yr, yh, bonus, g, gate]
    return pl.pallas_call(
        functools.partial(_merge_kernel, ff_chunk=min(1024, prm["w_mlp_up"].shape[1])),
        grid=(b, l // tile),
        in_specs=[tok(x), pl.BlockSpec((1,) + mod.shape[1:], lambda i, t: (i, 0, 0))]
        + [tok(a) for a in toks] + [_const_spec(c.shape) for c in consts],
        out_specs=tok(x),
        out_shape=jax.ShapeDtypeStruct(x.shape, x.dtype),
        compiler_params=_params("parallel", "parallel"),
        name="merge_mlp",
    )(x, mod, *toks, *consts)


def _rope_tables(l):
    n_freq = HEAD_DIM // 4
    inv_freq = np.power(np.float32(ROPE_BASE), -np.arange(n_freq, dtype=np.float32) / np.float32(n_freq))
    rows = l // GRID_W
    row = np.repeat(np.arange(rows, dtype=np.float32), GRID_W)
    col = np.tile(np.arange(GRID_W, dtype=np.float32), rows)
    ang = np.concatenate([row[:, None] * inv_freq, col[:, None] * inv_freq], axis=-1).astype(np.float32)
    cos, sin = np.cos(ang), np.sin(ang)
    reps = LANES // HEAD_DIM
    return (jnp.asarray(np.tile(np.concatenate([cos, cos], axis=1), (1, reps))),
            jnp.asarray(np.tile(np.concatenate([-sin, sin], axis=1), (1, reps))))


def _pad_cols(w, width):
    return jnp.pad(w, ((0, 0), (0, width - w.shape[1])))


def kernel(x, c, ctx, c_ctx, w_ada, b_ada, norm1_g, w_in, sink, conv_w, decay_w0, decay_w2, iclr_a0, iclr_a2, gate_g2, k_k, k_a, r_k, lnx_w, lnx_b, w_branch_attn, w_branch_rwkv, w_out, norm2_g, w_mlp_up, w_mlp_down, norm_f_g):
    assert w_in.shape[0] == 1, "single-layer block: context tokens are read, never updated"
    b, l, d = x.shape
    attn_w = w_branch_attn.shape[1]
    rw = w_branch_rwkv.shape[1]
    n_q = attn_w // HEAD_DIM
    n_kv = n_q // Q_PER_KV
    kv_w = n_kv * HEAD_DIM
    assert kv_w == LANES and rw % LANES == 0 and ctx.shape[1] % 256 == 0
    assert l % (ATTN_QB * ATTN_BLOCK) == 0 and l % 512 == 0

    w = w_in[0]
    o_k, o_r = attn_w, attn_w + 2 * kv_w
    o_l = o_r + 3 * rw
    o_g = o_l + DECAY_LORA + ICLR_LORA + GATE_LORA
    assert o_l + LORA_PAD <= w.shape[1]
    w_main = w[:, :o_l + LORA_PAD].astype(BF16)
    w_gate = w[:, o_g:].astype(BF16)
    widths = {"q": attn_w, "kv": 2 * kv_w, "k": PAIR * kv_w, "v": PAIR * kv_w, "rkv": 3 * rw,
              "lora": LORA_PAD, "gate": 2 * d}

    cw = conv_w[0]
    used = DECAY_LORA + ICLR_LORA + GATE_LORA
    lora_w = jnp.concatenate([
        jnp.pad(jnp.concatenate([decay_w2[0, 0], decay_w2[0, 1]], axis=1), ((0, 0), (0, 3 * rw))),
        jnp.pad(jnp.concatenate([iclr_a2[0, 0], iclr_a2[0, 1]], axis=1), ((0, 0), (2 * rw, rw))),
        jnp.pad(gate_g2[0], ((0, LORA_PAD - used), (4 * rw, 0)))], axis=0)
    prm = {
        "conv_rkv": cw[:, :3 * rw], "conv_lora": _pad_cols(cw[:, 3 * rw:], LORA_PAD),
        "k_k": k_k[0].reshape(1, rw), "k_a": k_a[0].reshape(1, rw),
        "decay_w0": decay_w0[0], "iclr_a0": iclr_a0[0], "lora_w": lora_w.astype(BF16),
        "r_k": r_k[0].reshape(1, rw),
        "lnx_w": lnx_w[0].reshape(1, rw), "lnx_b": lnx_b[0].reshape(1, rw),
        "w_branch_attn": w_branch_attn[0].astype(BF16), "w_branch_rwkv": w_branch_rwkv[0].astype(BF16),
        "w_out": w_out[0].astype(BF16), "norm2_g": norm2_g[0].reshape(1, d),
        "w_mlp_up": w_mlp_up[0].astype(BF16), "w_mlp_down": w_mlp_down[0].astype(BF16),
        "norm_f_g": norm_f_g.reshape(1, d),
    }

    rows = -(-(b + 1) // 8) * 8
    cc = jnp.concatenate([c, c_ctx[None, :], jnp.zeros((rows - b - 1, d), F32)], axis=0)
    mod = _ada_mod(cc, w_ada[0], b_ada[0]).reshape(rows, -1, d)

    q, kd, vd, rkv, lora, gate = _in_proj(x, mod, b, norm1_g[0], [w_main, w_gate], widths, _rope_tables(l),
                                          latent=True, tile=1024)
    kxd, vxd, rkv_c, lora_c = _in_proj(ctx, mod, b, norm1_g[0], [w_main], widths, None,
                                       latent=False, tile=256)
    ya = _attention(sink[0], q, kd, vd, kxd, vxd)

    prep_c = _wkv_prep(rkv_c, lora_c, prm, latent=False, tile=256)
    z_ctx = _wkv_scan(prep_c, jnp.zeros((b, 2, rw // LANES, CHUNK, LANES), F32), emit_y=False)
    prep = _wkv_prep(rkv, lora, prm, latent=True, tile=512)
    yf, yr = _wkv_scan(prep, z_ctx, emit_y=True)
    bonus, g, yh = prep[8], prep[9], prep[10]

    return _merge_mlp(x, mod, ya, yf, yr, yh, bonus, g, gate, prm, tile=512)
```

```python
import functools
import math

import jax
import jax.numpy as jnp
import numpy as np
from jax import lax
from jax.experimental import pallas as pl
from jax.experimental.pallas import tpu as pltpu

F32 = jnp.float32
BF16 = jnp.bfloat16

GRID_W = 64
HEAD_DIM = 64
Q_PER_KV = 4
ATTN_BLOCK = 128
ATTN_QB = 8
ROPE_BASE = 10000.0
NORM_EPS = 1e-6
LNX_EPS = 1e-5 * HEAD_DIM
DECAY_LORA, ICLR_LORA, GATE_LORA = 32, 32, 96
LORA_PAD = 256
CHUNK = 64
SCAN_CHUNKS = 4
PREP_GROUP = 16
LANES = 128
MXU_DIM = 256
PAIR = LANES // HEAD_DIM
NEG = -1e30
LOG2_E = math.log2(math.e)
VMEM_LIMIT = 56 * 1024 * 1024


def _dot(a, b):
    return jnp.dot(a, b, preferred_element_type=F32)


def _dot_nt(a, b):
    return lax.dot_general(a, b, (((1,), (1,)), ((), ())), preferred_element_type=F32)


def _dot_tn(a, b):
    return lax.dot_general(a, b, (((0,), (0,)), ((), ())), preferred_element_type=F32)


def _iota(shape, dim):
    return lax.broadcasted_iota(jnp.int32, shape, dim)


def _head_ones(width):
    r = _iota((width, width), 0) // HEAD_DIM
    c = _iota((width, width), 1) // HEAD_DIM
    return jnp.where(r == c, 1.0, 0.0).astype(BF16)


def _head_sum(x, ones_bd):
    gw = ones_bd.shape[0]
    hi = x.astype(BF16)
    lo = (x - hi.astype(F32)).astype(BF16)
    parts = [_dot(hi[:, j:j + gw], ones_bd) + _dot(lo[:, j:j + gw], ones_bd) for j in range(0, x.shape[1], gw)]
    return jnp.concatenate(parts, axis=1)


def _rmsnorm(x, g):
    ms = jnp.mean(x * x, axis=-1, keepdims=True)
    return x * lax.rsqrt(ms + NORM_EPS) * g


def _params(*sem):
    return pltpu.CompilerParams(dimension_semantics=sem, vmem_limit_bytes=VMEM_LIMIT)


def _const_spec(shape):
    nd = len(shape)
    return pl.BlockSpec(shape, lambda *_: (0,) * nd, pipeline_mode=pl.Buffered(1))


def _ada_kernel(c_ref, w_ref, b_ref, o_ref):
    c = c_ref[...]
    s = c * jax.nn.sigmoid(c)
    o_ref[...] = _dot(s.astype(BF16), w_ref[...].astype(BF16)) + b_ref[...]


def _ada_mod(cc, w_ada, b_ada):
    rows, d = cc.shape
    n = w_ada.shape[1]
    return pl.pallas_call(
        _ada_kernel,
        grid=(n // d,),
        in_specs=[pl.BlockSpec((rows, d), lambda j: (0, 0)),
                  pl.BlockSpec((d, d), lambda j: (0, j)),
                  pl.BlockSpec((1, d), lambda j: (0, j))],
        out_specs=pl.BlockSpec((rows, d), lambda j: (0, j)),
        out_shape=jax.ShapeDtypeStruct((rows, n), F32),
        compiler_params=_params("arbitrary"),
        name="ada_mod",
    )(cc, w_ada, b_ada.reshape(1, n))


def _rope(x, cos_t, sin_t):
    w = x.shape[1]
    half = HEAD_DIM // 2
    first = (_iota(x.shape, 1) % HEAD_DIM) < half
    swapped = jnp.where(first, pltpu.roll(x, w - half, 1), pltpu.roll(x, half, 1))
    reps = w // LANES
    c = jnp.concatenate([cos_t] * reps, axis=1)
    s = jnp.concatenate([sin_t] * reps, axis=1)
    return x * c + swapped * s


def _inproj_kernel(*refs, latent, widths):
    if latent:
        (x_ref, mod_ref, g_ref, w_ref, wg_ref, cos_ref, sin_ref,
         q_ref, k_ref, v_ref, rkv_ref, lora_ref, gate_ref) = refs
    else:
        x_ref, mod_ref, g_ref, w_ref, k_ref, v_ref, rkv_ref, lora_ref = refs
    x = x_ref[0]
    h = _rmsnorm(x, g_ref[...]) * (1.0 + mod_ref[0, 1:2, :]) + mod_ref[0, 0:1, :]
    hb = h.astype(BF16)
    off = widths["q"]

    def seg(name):
        nonlocal off
        lo = off
        off += widths[name]
        return _dot(hb, w_ref[:, lo:off])

    def dup_heads(t):
        first = _iota(t.shape, 1) < HEAD_DIM
        other = pltpu.roll(t, HEAD_DIM, 1)
        return jnp.concatenate([jnp.where(first, t, other), jnp.where(first, other, t)], axis=1)

    kv = seg("kv")
    k, v = dup_heads(kv[:, :LANES]), dup_heads(kv[:, LANES:])
    if latent:
        cos_t, sin_t = cos_ref[...], sin_ref[...]
        q = _dot(hb, w_ref[:, :widths["q"]])
        q_ref[0] = (_rope(q, cos_t, sin_t) * (LOG2_E * HEAD_DIM ** -0.5)).astype(BF16)
        k = _rope(k, cos_t, sin_t)
    k_ref[0] = k.astype(BF16)
    v_ref[0] = v.astype(BF16)
    rkv_ref[0] = seg("rkv")
    lora_ref[0] = seg("lora")
    if latent:
        gate_ref[0] = jax.nn.sigmoid(_dot(hb, wg_ref[...])).astype(BF16)


def _in_proj(x, mod, mod_row, norm_g, weights, widths, tables, *, latent, tile):
    b, l, d = x.shape
    nt = l // tile
    if latent:
        mod_map = lambda i, t: (i, 0, 0)
    else:
        mod_map = lambda i, t: (mod_row, 0, 0)
    tok = lambda w: pl.BlockSpec((1, tile, w), lambda i, t: (i, t, 0))
    in_specs = [tok(d),
                pl.BlockSpec((1,) + mod.shape[1:], mod_map),
                _const_spec((1, d))] + [_const_spec(w.shape) for w in weights]
    args = [x, mod, norm_g.reshape(1, d), *weights]
    out_specs, out_shape = [], []

    def out(w, dt):
        out_specs.append(tok(w))
        out_shape.append(jax.ShapeDtypeStruct((b, l, w), dt))

    if latent:
        in_specs += [pl.BlockSpec((tile, LANES), lambda i, t: (t, 0))] * 2
        args += list(tables)
        out(widths["q"], BF16)
    out(widths["k"], BF16)
    out(widths["v"], BF16)
    out(widths["rkv"], F32)
    out(widths["lora"], F32)
    if latent:
        out(widths["gate"], BF16)
    return pl.pallas_call(
        functools.partial(_inproj_kernel, latent=latent, widths=widths),
        grid=(b, nt),
        in_specs=in_specs,
        out_specs=out_specs,
        out_shape=out_shape,
        compiler_params=_params("parallel", "parallel"),
        name="in_proj_latent" if latent else "in_proj_context",
    )(*args)


def _attn_kernel(sink_ref, q_ref, kp_ref, kc_ref, kn_ref, vp_ref, vc_ref, vn_ref,
                 kx_ref, vx_ref, o_ref, *, n_kv):
    i = pl.program_id(1)
    last = pl.num_programs(1) - 1
    blk = ATTN_BLOCK
    qi = _iota((blk, blk), 0)
    kj = _iota((blk, blk), 1)
    left = _iota((blk, LANES), 1) < HEAD_DIM

    def key_block(refs, j, gs):
        ref_p, ref_c, ref_n = refs[:3]
        if j < 0:
            return ref_p[0, :, gs]
        if j >= ATTN_QB:
            return ref_n[0, :, gs]
        return ref_c[0, j * blk:(j + 1) * blk, gs]

    items = []
    for qb in range(ATTN_QB):
        lo_ok = kj >= qi
        hi_ok = kj <= qi
        if qb == 0:
            lo_ok = lo_ok & (i > 0)
        if qb == ATTN_QB - 1:
            hi_ok = hi_ok & (i < last)
        bias_lo = jnp.concatenate([jnp.where(lo_ok, 0.0, NEG)] * Q_PER_KV, axis=0)
        bias_hi = jnp.concatenate([jnp.where(hi_ok, 0.0, NEG)] * Q_PER_KV, axis=0)
        rows = slice(qb * blk, (qb + 1) * blk)
        q = q_ref[0, rows, :].astype(F32)
        for g in range(n_kv):
            gs = slice(g * LANES, (g + 1) * LANES)
            heads = range(g * Q_PER_KV, (g + 1) * Q_PER_KV)
            qs, sinks = [], []
            for hd in heads:
                qp = q[:, (hd // PAIR) * LANES:(hd // PAIR + 1) * LANES]
                keep = left if hd % PAIR == 0 else jnp.logical_not(left)
                qs.append(jnp.where(keep, qp, 0.0).astype(BF16))
                sinks.append(jnp.full((blk, 1), sink_ref[hd] * LOG2_E, F32))
            items.append({
                "qb": qb, "gs": gs, "rows": rows, "heads": heads, "bias": (bias_lo, bias_hi),
                "qs": jnp.concatenate(qs, axis=0), "sink": jnp.concatenate(sinks, axis=0)})

    def cat(refs, it):
        return jnp.concatenate([key_block(refs, it["qb"] + j, it["gs"]) for j in (-1, 0, 1)]
                               + [refs[3][0, :, it["gs"]]], axis=0)

    s = [_dot_nt(it["qs"], cat((kp_ref, kc_ref, kn_ref, kx_ref), it)) for it in items]
    s = [jnp.concatenate([x[:, :blk] + it["bias"][0], x[:, blk:2 * blk], x[:, 2 * blk:3 * blk] + it["bias"][1],
                          x[:, 3 * blk:]], axis=1) for x, it in zip(s, items)]
    m = [jnp.maximum(jnp.max(x, axis=1, keepdims=True), it["sink"]) for x, it in zip(s, items)]
    p = [jnp.exp2(x - mx) for x, mx in zip(s, m)]
    den = [jnp.sum(x, axis=1, keepdims=True) + jnp.exp2(it["sink"] - mx) for x, mx, it in zip(p, m, items)]
    o = [_dot(x.astype(BF16), cat((vp_ref, vc_ref, vn_ref, vx_ref), it)) / dn for x, dn, it in zip(p, den, items)]
    for x, it in zip(o, items):
        for hd in it["heads"][::PAIR]:
            j = hd - it["heads"][0]
            pair = jnp.where(left, x[j * blk:(j + 1) * blk], x[(j + 1) * blk:(j + 2) * blk])
            col = (hd // PAIR) * LANES
            o_ref[0, it["rows"], col:col + LANES] = pair.astype(BF16)


def _attention(sink, q, kd, vd, kxd, vxd):
    b, l, wq = q.shape
    wk = kd.shape[2]
    lc = kxd.shape[1]
    nb = l // ATTN_BLOCK
    span = ATTN_QB * ATTN_BLOCK
    blk = lambda w, f: pl.BlockSpec((1, ATTN_BLOCK, w), f)
    prev = lambda bi, i: (bi, jnp.maximum(i * ATTN_QB - 1, 0), 0)
    nxt = lambda bi, i: (bi, jnp.minimum((i + 1) * ATTN_QB, nb - 1), 0)
    cur = lambda w: pl.BlockSpec((1, span, w), lambda bi, i: (bi, i, 0))
    ctx = pl.BlockSpec((1, lc, wk), lambda bi, i: (bi, 0, 0))
    return pl.pallas_call(
        functools.partial(_attn_kernel, n_kv=wk // LANES),
        grid=(b, l // span),
        in_specs=[pl.BlockSpec(memory_space=pltpu.SMEM),
                  cur(wq),
                  blk(wk, prev), cur(wk), blk(wk, nxt),
                  blk(wk, prev), cur(wk), blk(wk, nxt),
                  ctx, ctx],
        out_specs=cur(wq),
        out_shape=jax.ShapeDtypeStruct((b, l, wq), BF16),
        compiler_params=_params("parallel", "parallel"),
        name="attention",
    )(sink, q, kd, kd, kd, vd, vd, vd, kxd, vxd)


def _conv3(x, prev_row, next_row, w):
    n = x.shape[0]
    row = _iota((8, x.shape[1]), 0)
    xm = pltpu.roll(x, 1, 0)
    xm = jnp.concatenate([jnp.where(row == 0, prev_row, xm[:8]), xm[8:]], axis=0)
    xp = pltpu.roll(x, n - 1, 0)
    xp = jnp.concatenate([xp[:n - 8], jnp.where(row == 7, next_row, xp[n - 8:])], axis=0)
    return xm * w[0:1] + x * w[1:2] + xp * w[2:3]


def _pair_masks():
    n = 2 * CHUNK
    row = _iota((n, n), 0)
    lane = _iota((n, n), 1)
    top, left = row < CHUNK, lane < CHUNK
    return {"row": row % CHUNK, "lane": lane % CHUNK, "top": top, "left": left,
            "left_h": _iota((CHUNK, n), 1) < CHUNK}


def _stack(a, b):
    return jnp.concatenate([a, b], axis=0)


def _chunk_local(inst, m):
    bf = lambda x: x.astype(BF16)
    diag = jnp.logical_not(m["top"]) & (m["lane"] == m["row"])
    masks = ((m["lane"] < m["row"]) | diag, (m["lane"] > m["row"]) | diag)
    mask_a = [masks[i["rev"]] for i in inst]
    lh = m["left_h"]
    half = CHUNK // 2
    keep_l = jnp.where(m["left"], 1.0, 0.0).astype(BF16)
    keep_r = jnp.where(m["left"], 0.0, 1.0).astype(BF16)
    keep_lh = jnp.where(lh, 1.0, 0.0).astype(BF16)
    keep_rh = jnp.where(lh, 0.0, 1.0).astype(BF16)

    def unfold(xb, anti=False):
        a, b = xb * keep_lh, xb * keep_rh
        return _stack(b, a) if anti else _stack(a, b)

    lhs = [bf(_stack(i["at"], i["rt"])) for i in inst]
    a01 = [_dot_nt(l, _stack(bf(_stack(i["bt"], i["kt"])) * keep_l, bf(_stack(i["kt"], i["bt"])) * keep_r))
           for l, i in zip(lhs, inst)]
    a0 = [jnp.where(ma, a[:, :LANES], 0.0) for ma, a in zip(mask_a, a01)]
    a1 = [jnp.where(ma, a[:, LANES:], 0.0) for ma, a in zip(mask_a, a01)]
    nc = [jnp.where(lh, x[:CHUNK], y[:CHUNK]) for x, y in zip(a0, a1)]
    arb = [jnp.where(lh, x[CHUNK:], y[CHUNK:]) for x, y in zip(a0, a1)]
    ak_ark_sw = [bf(jnp.where(m["left"], y, x)) for x, y in zip(a0, a1)]
    vh = [_dot(a, unfold(bf(i["v"]), anti=True)) for a, i in zip(ak_ark_sw, inst)]
    eye = jnp.where(_iota((CHUNK, LANES), 1) % CHUNK == _iota((CHUNK, LANES), 0), 1.0, 0.0)
    tc = [eye + n for n in nc]
    ncb = [bf(n) for n in nc]
    nc = [_dot(n, unfold(n)) for n in ncb]
    steps = CHUNK.bit_length() - 1
    for _ in range(steps - 2):
        ncb = [bf(n) for n in nc]
        both = [_dot(n, jnp.concatenate([unfold(bf(t)), unfold(n)], axis=1)) for n, t in zip(ncb, tc)]
        tc = [t + r[:, :LANES] for t, r in zip(tc, both)]
        nc = [r[:, LANES:] for r in both]
    inc = [_dot(bf(n[half:]), unfold(bf(t))) for n, t in zip(nc, tc)]
    tc = [jnp.concatenate([t[:half], t[half:] + d], axis=0) for t, d in zip(tc, inc)]
    pq = [_dot(bf(t), jnp.concatenate([unfold(l[:CHUNK]), unfold(bf(x[:CHUNK]))], axis=1))
          for t, l, x in zip(tc, lhs, vh)]
    return [{"pm": r[:, :LANES], "qm": r[:, LANES:], "arb": b, "hm": x[CHUNK:]} for r, b, x in zip(pq, arb, vh)]


def _prep_kernel(*refs, latent, width):
    (rkv_ref, rkv_p, rkv_n, lora_ref, lora_p, lora_n, cw_ref, cwl_ref, kk_ref, ka_ref,
     w0_ref, a0_ref, wl_ref, rk_ref) = refs[:14]
    outs = refs[14:]
    out_refs = dict(zip(("pm", "qm", "rt", "bt", "arb", "kt"), outs[:6]))
    wc_ref, v_ref = outs[6:8]
    if latent:
        bonus_ref, g_ref, hsum_ref = outs[8:]
    t = pl.program_id(1)
    nt = pl.num_programs(1)
    tile = rkv_ref.shape[1]
    w = width
    has_prev = (t > 0).astype(F32)
    has_next = (t < nt - 1).astype(F32)
    u = _conv3(rkv_ref[0], rkv_p[0, 7:8, :] * has_prev, rkv_n[0, 0:1, :] * has_next, cw_ref[...])
    ul = _conv3(lora_ref[0], lora_p[0, 7:8, :] * has_prev, lora_n[0, 0:1, :] * has_next, cwl_ref[...])
    r, k, v = u[:, :w], u[:, w:2 * w], u[:, 2 * w:]
    ones_bd = _head_ones(min(MXU_DIM, w))

    kk = k * kk_ref[...]
    kk = kk * lax.rsqrt(jnp.maximum(_head_sum(kk * kk, ones_bd), 1e-24))

    lane = _iota(ul.shape, 1)
    lin = jnp.where(lane < DECAY_LORA, jnp.tanh(ul),
                    jnp.where(lane < DECAY_LORA + ICLR_LORA, ul, jax.nn.sigmoid(ul)))
    proj = _dot(lin.astype(BF16), wl_ref[...])

    span = min(MXU_DIM, tile)
    tr = _iota((span, span), 0)
    tc = _iota((span, span), 1)
    same = (tr // CHUNK) == (tc // CHUNK)
    tri = (jnp.where(same & (tc <= tr), 1.0, 0.0).astype(BF16),
           jnp.where(same & (tc >= tr), 1.0, 0.0).astype(BF16))

    def exact_dot(m, x):
        h1 = x.astype(BF16)
        h2 = (x - h1.astype(F32)).astype(BF16)
        return jnp.concatenate([_dot(m, h1[j:j + span]) + _dot(m, h2[j:j + span]) for j in range(0, tile, span)],
                               axis=0)

    masks = _pair_masks()
    k_sum = None
    work = []
    for d in range(2):
        z = w0_ref[d:d + 1, :] + proj[:, d * w:(d + 1) * w]
        lw = -math.exp(-0.5) * jax.nn.sigmoid(z)
        a = jax.nn.sigmoid(a0_ref[d:d + 1, :] + proj[:, (2 + d) * w:(3 + d) * w])
        kd = k * (1.0 + (a - 1.0) * ka_ref[...])
        k_sum = kd if k_sum is None else k_sum + kd
        cum = exact_dot(tri[d], lw)
        e_neg = jnp.exp(-cum)
        full = {"at": -kk * jnp.exp(cum - lw), "rt": r * jnp.exp(cum), "bt": kk * a * e_neg,
                "kt": kd * e_neg, "v": v}
        for name in ("rt", "bt", "kt"):
            out_refs[name][d, 0] = full[name].astype(BF16)
        for j in range(tile // CHUNK):
            edge = (j + 1) * CHUNK - 1 if d == 0 else j * CHUNK
            wc_ref[0, j, d:d + 1, :] = jnp.exp(cum[edge:edge + 1, :])
        work += [(d, slice(j * CHUNK, (j + 1) * CHUNK), slice(p * LANES, (p + 1) * LANES), full)
                 for j in range(tile // CHUNK) for p in range(w // LANES)]
    hsum = {}
    for g0 in range(0, len(work), PREP_GROUP):
        group = work[g0:g0 + PREP_GROUP]
        inst = [dict({name: val[rs, ls] for name, val in full.items()}, rev=d) for d, rs, ls, full in group]
        for (d, rs, ls, _), res in zip(group, _chunk_local(inst, masks)):
            seen = hsum.get((rs.start, ls.start))
            hsum[(rs.start, ls.start)] = (rs, ls, res["hm"] if seen is None else seen[2] + res["hm"])
            for name in res:
                if name in out_refs:
                    out_refs[name][d, 0, rs, ls] = res[name].astype(BF16)
    v_ref[0] = v.astype(BF16)
    if latent:
        for rs, ls, val in hsum.values():
            hsum_ref[0, rs, ls] = val
        bonus_ref[0] = _head_sum(r * k_sum * rk_ref[...], ones_bd) * v
        g_ref[0] = proj[:, 4 * w:5 * w].astype(BF16)


def _wkv_prep(rkv, lora, prm, *, latent, tile):
    b, l, w3 = rkv.shape
    w = w3 // 3
    nt = l // tile
    n8 = l // 8
    tok = lambda wd: pl.BlockSpec((1, tile, wd), lambda i, t: (i, t, 0))
    prev = lambda wd: pl.BlockSpec((1, 8, wd), lambda i, t: (i, jnp.maximum(t * (tile // 8) - 1, 0), 0))
    nxt = lambda wd: pl.BlockSpec((1, 8, wd), lambda i, t: (i, jnp.minimum((t + 1) * (tile // 8), n8 - 1), 0))
    wl = lora.shape[2]
    in_specs = [tok(w3), prev(w3), nxt(w3), tok(wl), prev(wl), nxt(wl)]
    consts = [prm["conv_rkv"], prm["conv_lora"], prm["k_k"], prm["k_a"], prm["decay_w0"],
              prm["iclr_a0"], prm["lora_w"], prm["r_k"]]
    in_specs += [_const_spec(c.shape) for c in consts]
    dirtok = pl.BlockSpec((2, 1, tile, w), lambda i, t: (0, i, t, 0))
    out_specs = [dirtok] * 6 + [pl.BlockSpec((1, tile // CHUNK, 2, w), lambda i, t: (i, t, 0, 0)), tok(w)]
    out_shape = [jax.ShapeDtypeStruct((2, b, l, w), BF16)] * 6 + [
        jax.ShapeDtypeStruct((b, l // CHUNK, 2, w), F32), jax.ShapeDtypeStruct((b, l, w), BF16)]
    if latent:
        out_specs += [tok(w), tok(w), tok(w)]
        out_shape += [jax.ShapeDtypeStruct((b, l, w), F32), jax.ShapeDtypeStruct((b, l, w), BF16),
                      jax.ShapeDtypeStruct((b, l, w), F32)]
    return pl.pallas_call(
        functools.partial(_prep_kernel, latent=latent, width=w),
        grid=(b, nt),
        in_specs=in_specs,
        out_specs=out_specs,
        out_shape=out_shape,
        compiler_params=_params("parallel", "parallel"),
        name="wkv_prep_latent" if latent else "wkv_prep_context",
    )(rkv, rkv, rkv, lora, lora, lora, *consts)


def _wkv_kernel(*refs, emit_y, n_pairs):
    names = ("pm", "qm", "rt", "bt", "arb", "kt", "wc", "v")
    n = len(names)
    dir_refs = (dict(zip(names, refs[0:n])), dict(zip(names, refs[n:2 * n])))
    z0_ref = refs[2 * n]
    if emit_y:
        y_refs = refs[2 * n + 1:2 * n + 3]
        z_scr = refs[2 * n + 3]
    else:
        zfin_ref = refs[2 * n + 1]
        z_scr = refs[2 * n + 2]
    c = pl.program_id(0)

    @pl.when(c == 0)
    def _():
        z_scr[...] = z0_ref[...]

    left_h = _iota((CHUNK, LANES), 1) < CHUNK
    keep_l = jnp.where(left_h, 1.0, 0.0).astype(BF16)
    keep_r = jnp.where(left_h, 0.0, 1.0).astype(BF16)

    def unfold(xb):
        return _stack(xb * keep_l, xb * keep_r)

    tiles = [(i, d, p, slice(p * LANES, (p + 1) * LANES))
             for i in range(z_scr.shape[0]) for d in range(2) for p in range(n_pairs)]
    state = [z_scr[i, d, p] for i, d, p, _ in tiles]
    n_sub = dir_refs[0]["wc"].shape[1]
    for step in range(n_sub):
        sub = (step, n_sub - 1 - step)
        rows = [slice(sub[d] * CHUNK, (sub[d] + 1) * CHUNK) for d in range(2)]
        ld = lambda name: [dir_refs[d][name][0, i, rows[d], sl] for i, d, _, sl in tiles]
        v = [dir_refs[d]["v"][i, rows[d], sl] for i, d, _, sl in tiles]
        sbd = [unfold(s.astype(BF16)) for s in state]
        if emit_y:
            ur = [_dot_nt(_stack(pm, rt), s) for pm, rt, s in zip(ld("pm"), ld("rt"), sbd)]
        else:
            ur = [_dot_nt(pm, s) for pm, s in zip(ld("pm"), sbd)]
        u = [x[:CHUNK] + q.astype(F32) for x, q in zip(ur, ld("qm"))]
        ub = [ui.astype(BF16) for ui in u]
        inc = [_dot_tn(_stack(ui, vi), _stack(bt, kt)) for ui, vi, bt, kt in zip(ub, v, ld("bt"), ld("kt"))]
        inc = [jnp.where(left_h, x[:CHUNK], x[CHUNK:]) for x in inc]
        if emit_y:
            yc = [_dot(a, unfold(ui)) for a, ui in zip(ld("arb"), ub)]
            for (i, d, _, sl), x, ys in zip(tiles, ur, yc):
                y_refs[d][i, rows[d], sl] = (x[CHUNK:] + ys).astype(BF16)
        state = [(s + dz) * dir_refs[d]["wc"][i, sub[d], d:d + 1, sl]
                 for (i, d, _, sl), s, dz in zip(tiles, state, inc)]
    for (i, d, p, _), s in zip(tiles, state):
        z_scr[i, d, p] = s

    if not emit_y:
        @pl.when(c == pl.num_programs(0) - 1)
        def _():
            zfin_ref[...] = z_scr[...]


def _wkv_scan(prep, z0, *, emit_y):
    wc = prep[6]
    _, b, l, w = prep[0].shape
    n_sub = min(SCAN_CHUNKS, l // CHUNK)
    blk = n_sub * CHUNK
    nc = l // blk
    n_pairs = w // LANES
    fwd = lambda c: c
    rev = lambda c: nc - 1 - c
    in_specs, args = [], []
    for d, cm in enumerate((fwd, rev)):
        for arr in prep[:6]:
            in_specs.append(pl.BlockSpec((1, b, blk, w), lambda c, d=d, cm=cm: (d, 0, cm(c), 0)))
            args.append(arr)
        in_specs.append(pl.BlockSpec((b, n_sub, 2, w), lambda c, cm=cm: (0, cm(c), 0, 0)))
        args.append(wc)
        in_specs.append(pl.BlockSpec((b, blk, w), lambda c, cm=cm: (0, cm(c), 0)))
        args.append(prep[7])
    zshape = (b, 2, n_pairs, CHUNK, LANES)
    zspec = pl.BlockSpec(zshape, lambda c: (0, 0, 0, 0, 0))
    in_specs.append(zspec)
    args.append(z0)
    if emit_y:
        out_specs = [pl.BlockSpec((b, blk, w), lambda c: (0, c, 0)),
                     pl.BlockSpec((b, blk, w), lambda c: (0, nc - 1 - c, 0))]
        out_shape = [jax.ShapeDtypeStruct((b, l, w), BF16)] * 2
    else:
        out_specs = zspec
        out_shape = jax.ShapeDtypeStruct(zshape, F32)
    return pl.pallas_call(
        functools.partial(_wkv_kernel, emit_y=emit_y, n_pairs=n_pairs),
        grid=(nc,),
        in_specs=in_specs,
        out_specs=out_specs,
        out_shape=out_shape,
        scratch_shapes=[pltpu.VMEM(zshape, F32)],
        compiler_params=_params("arbitrary"),
        name="wkv_scan_latent" if emit_y else "wkv_scan_context",
    )(*args)


def _merge_kernel(x_ref, mod_ref, ya_ref, yf_ref, yr_ref, yh_ref, bonus_ref, g_ref, gate_ref,
                  lnw_ref, lnb_ref, wba_ref, wbr_ref, wo_ref, n2_ref, wu_ref, wd_ref, nf_ref,
                  o_ref, *, ff_chunk):
    x = x_ref[0]
    d = x.shape[1]
    mod = lambda j: mod_ref[0, j:j + 1, :]
    y = yf_ref[0].astype(F32) + yr_ref[0].astype(F32) + yh_ref[0]
    gw = min(MXU_DIM, y.shape[1])
    ones_bd = _head_ones(gw)

    def head_mean(t):
        tb = t.astype(BF16)
        parts = [_dot(tb[:, j:j + gw], ones_bd) for j in range(0, t.shape[1], gw)]
        return jnp.concatenate(parts, axis=1) * (1.0 / HEAD_DIM)

    mu = head_mean(y)
    yc = y - mu
    var = head_mean(yc * yc)
    yn = yc * lax.rsqrt(var + LNX_EPS)
    yr = (yn * lnw_ref[...] + lnb_ref[...] + bonus_ref[0]) * g_ref[0].astype(F32)
    gate = gate_ref[0].astype(F32)
    merged = gate[:, :d] * _dot(ya_ref[0], wba_ref[...]) + gate[:, d:] * _dot(yr.astype(BF16), wbr_ref[...])
    x1 = x + mod(2) * _dot(merged.astype(BF16), wo_ref[...])
    h2 = (_rmsnorm(x1, n2_ref[...]) * (1.0 + mod(4)) + mod(3)).astype(BF16)
    acc = jnp.zeros_like(x1)
    for j in range(wu_ref.shape[1] // ff_chunk):
        cs = slice(j * ff_chunk, (j + 1) * ff_chunk)
        up = jnp.maximum(_dot(h2, wu_ref[:, cs]), 0.0)
        acc = acc + _dot((up * up).astype(BF16), wd_ref[cs, :])
    x2 = x1 + mod(5) * acc
    o_ref[0] = _rmsnorm(x2, nf_ref[...])


def _merge_mlp(x, mod, ya, yf, yr, yh, bonus, g, gate, prm, *, tile):
    b, l, d = x.shape
    tok = lambda arr: pl.BlockSpec((1, tile, arr.shape[2]), lambda i, t: (i, t, 0))
    consts = [prm["lnx_w"], prm["lnx_b"], prm["w_branch_attn"], prm["w_branch_rwkv"], prm["w_out"],
              prm["norm2_g"], prm["w_mlp_up"], prm["w_mlp_down"], prm["norm_f_g"]]
    toks = [ya, yf, yr, yh, bonus, g, gate]
    return pl.pallas_call(
        functools.partial(_merge_kernel, ff_chunk=min(1024, prm["w_mlp_up"].shape[1])),
        grid=(b, l // tile),
        in_specs=[tok(x), pl.BlockSpec((1,) + mod.shape[1:], lambda i, t: (i, 0, 0))]
        + [tok(a) for a in toks] + [_const_spec(c.shape) for c in consts],
        out_specs=tok(x),
        out_shape=jax.ShapeDtypeStruct(x.shape, x.dtype),
        compiler_params=_params("parallel", "parallel"),
        name="merge_mlp",
    )(x, mod, *toks, *consts)


def _rope_tables(l):
    n_freq = HEAD_DIM // 4
    inv_freq = np.power(np.float32(ROPE_BASE), -np.arange(n_freq, dtype=np.float32) / np.float32(n_freq))
    rows = l // GRID_W
    row = np.repeat(np.arange(rows, dtype=np.float32), GRID_W)
    col = np.tile(np.arange(GRID_W, dtype=np.float32), rows)
    ang = np.concatenate([row[:, None] * inv_freq, col[:, None] * inv_freq], axis=-1).astype(np.float32)
    cos, sin = np.cos(ang), np.sin(ang)
    reps = LANES // HEAD_DIM
    return (jnp.asarray(np.tile(np.concatenate([cos, cos], axis=1), (1, reps))),
            jnp.asarray(np.tile(np.concatenate([-sin, sin], axis=1), (1, reps))))


def _pad_cols(w, width):
    return jnp.pad(w, ((0, 0), (0, width - w.shape[1])))


def kernel(x, c, ctx, c_ctx, w_ada, b_ada, norm1_g, w_in, sink, conv_w, decay_w0, decay_w2, iclr_a0, iclr_a2, gate_g2, k_k, k_a, r_k, lnx_w, lnx_b, w_branch_attn, w_branch_rwkv, w_out, norm2_g, w_mlp_up, w_mlp_down, norm_f_g):
    assert w_in.shape[0] == 1, "single-layer block: context tokens are read, never updated"
    b, l, d = x.shape
    attn_w = w_branch_attn.shape[1]
    rw = w_branch_rwkv.shape[1]
    n_q = attn_w // HEAD_DIM
    n_kv = n_q // Q_PER_KV
    kv_w = n_kv * HEAD_DIM
    assert kv_w == LANES and rw % LANES == 0 and ctx.shape[1] % 256 == 0
    assert l % (ATTN_QB * ATTN_BLOCK) == 0 and l % 512 == 0

    w = w_in[0]
    o_k, o_r = attn_w, attn_w + 2 * kv_w
    o_l = o_r + 3 * rw
    o_g = o_l + DECAY_LORA + ICLR_LORA + GATE_LORA
    assert o_l + LORA_PAD <= w.shape[1]
    w_main = w[:, :o_l + LORA_PAD].astype(BF16)
    w_gate = w[:, o_g:].astype(BF16)
    widths = {"q": attn_w, "kv": 2 * kv_w, "k": PAIR * kv_w, "v": PAIR * kv_w, "rkv": 3 * rw,
              "lora": LORA_PAD, "gate": 2 * d}

    cw = conv_w[0]
    used = DECAY_LORA + ICLR_LORA + GATE_LORA
    lora_w = jnp.concatenate([
        jnp.pad(jnp.concatenate([decay_w2[0, 0], decay_w2[0, 1]], axis=1), ((0, 0), (0, 3 * rw))),
        jnp.pad(jnp.concatenate([iclr_a2[0, 0], iclr_a2[0, 1]], axis=1), ((0, 0), (2 * rw, rw))),
        jnp.pad(gate_g2[0], ((0, LORA_PAD - used), (4 * rw, 0)))], axis=0)
    prm = {
        "conv_rkv": cw[:, :3 * rw], "conv_lora": _pad_cols(cw[:, 3 * rw:], LORA_PAD),
        "k_k": k_k[0].reshape(1, rw), "k_a": k_a[0].reshape(1, rw),
        "decay_w0": decay_w0[0], "iclr_a0": iclr_a0[0], "lora_w": lora_w.astype(BF16),
        "r_k": r_k[0].reshape(1, rw),
        "lnx_w": lnx_w[0].reshape(1, rw), "lnx_b": lnx_b[0].reshape(1, rw),
        "w_branch_attn": w_branch_attn[0].astype(BF16), "w_branch_rwkv": w_branch_rwkv[0].astype(BF16),
        "w_out": w_out[0].astype(BF16), "norm2_g": norm2_g[0].reshape(1, d),
        "w_mlp_up": w_mlp_up[0].astype(BF16), "w_mlp_down": w_mlp_down[0].astype(BF16),
        "norm_f_g": norm_f_g.reshape(1, d),
    }

    rows = -(-(b + 1) // 8) * 8
    cc = jnp.concatenate([c, c_ctx[None, :], jnp.zeros((rows - b - 1, d), F32)], axis=0)
    mod = _ada_mod(cc, w_ada[0], b_ada[0]).reshape(rows, -1, d)

    q, kd, vd, rkv, lora, gate = _in_proj(x, mod, b, norm1_g[0], [w_main, w_gate], widths, _rope_tables(l),
                                          latent=True, tile=1024)
    kxd, vxd, rkv_c, lora_c = _in_proj(ctx, mod, b, norm1_g[0], [w_main], widths, None,
                                       latent=False, tile=256)
    ya = _attention(sink[0], q, kd, vd, kxd, vxd)

    prep_c = _wkv_prep(rkv_c, lora_c, prm, latent=False, tile=256)
    z_ctx = _wkv_scan(prep_c, jnp.zeros((b, 2, rw // LANES, CHUNK, LANES), F32), emit_y=False)
    prep = _wkv_prep(rkv, lora, prm, latent=True, tile=512)
    yf, yr = _wkv_scan(prep, z_ctx, emit_y=True)
    bonus, g, yh = prep[8], prep[9], prep[10]

    return _merge_mlp(x, mod, ya, yf, yr, yh, bonus, g, gate, prm, tile=512)
```

```python
import functools
import math

import jax
import jax.numpy as jnp
import numpy as np
from jax import lax
from jax.experimental import pallas as pl
from jax.experimental.pallas import tpu as pltpu

F32 = jnp.float32
BF16 = jnp.bfloat16

GRID_W = 64
HEAD_DIM = 64
Q_PER_KV = 4
ATTN_BLOCK = 128
ATTN_QB = 8
ROPE_BASE = 10000.0
NORM_EPS = 1e-6
LNX_EPS = 1e-5 * HEAD_DIM
DECAY_LORA, ICLR_LORA, GATE_LORA = 32, 32, 96
LORA_PAD = 256
CHUNK = 64
SCAN_CHUNKS = 4
PREP_GROUP = 16
LANES = 128
MXU_DIM = 256
PAIR = LANES // HEAD_DIM
NEG = -1e30
LOG2_E = math.log2(math.e)
VMEM_LIMIT = 56 * 1024 * 1024


def _dot(a, b):
    return jnp.dot(a, b, preferred_element_type=F32)


def _dot_nt(a, b):
    return lax.dot_general(a, b, (((1,), (1,)), ((), ())), preferred_element_type=F32)


def _dot_tn(a, b):
    return lax.dot_general(a, b, (((0,), (0,)), ((), ())), preferred_element_type=F32)


def _iota(shape, dim):
    return lax.broadcasted_iota(jnp.int32, shape, dim)


def _head_ones(width):
    r = _iota((width, width), 0) // HEAD_DIM
    c = _iota((width, width), 1) // HEAD_DIM
    return jnp.where(r == c, 1.0, 0.0).astype(BF16)


def _head_sum(x, ones_bd, split=True):
    gw = ones_bd.shape[0]
    terms = [x.astype(BF16)]
    if split:
        terms.append((x - terms[0].astype(F32)).astype(BF16))
    parts = [functools.reduce(lambda a, b: a + b, [_dot(t[:, j:j + gw], ones_bd) for t in terms])
             for j in range(0, x.shape[1], gw)]
    return jnp.concatenate(parts, axis=1)


def _rmsnorm(x, g):
    ms = jnp.mean(x * x, axis=-1, keepdims=True)
    return x * lax.rsqrt(ms + NORM_EPS) * g


def _params(*sem):
    return pltpu.CompilerParams(dimension_semantics=sem, vmem_limit_bytes=VMEM_LIMIT)


def _const_spec(shape):
    nd = len(shape)
    return pl.BlockSpec(shape, lambda *_: (0,) * nd, pipeline_mode=pl.Buffered(1))


def _ada_kernel(c_ref, w_ref, b_ref, o_ref):
    c = c_ref[...]
    s = c * jax.nn.sigmoid(c)
    o_ref[...] = _dot(s.astype(BF16), w_ref[...].astype(BF16)) + b_ref[...]


def _ada_mod(cc, w_ada, b_ada):
    rows, d = cc.shape
    n = w_ada.shape[1]
    return pl.pallas_call(
        _ada_kernel,
        grid=(n // d,),
        in_specs=[pl.BlockSpec((rows, d), lambda j: (0, 0)),
                  pl.BlockSpec((d, d), lambda j: (0, j)),
                  pl.BlockSpec((1, d), lambda j: (0, j))],
        out_specs=pl.BlockSpec((rows, d), lambda j: (0, j)),
        out_shape=jax.ShapeDtypeStruct((rows, n), F32),
        compiler_params=_params("arbitrary"),
        name="ada_mod",
    )(cc, w_ada, b_ada.reshape(1, n))


def _rope(x, cos_t, sin_t):
    w = x.shape[1]
    half = HEAD_DIM // 2
    first = (_iota(x.shape, 1) % HEAD_DIM) < half
    swapped = jnp.where(first, pltpu.roll(x, w - half, 1), pltpu.roll(x, half, 1))
    reps = w // LANES
    c = jnp.concatenate([cos_t] * reps, axis=1)
    s = jnp.concatenate([sin_t] * reps, axis=1)
    return x * c + swapped * s


def _inproj_kernel(*refs, latent, widths):
    if latent:
        (x_ref, mod_ref, g_ref, w_ref, wg_ref, cos_ref, sin_ref,
         q_ref, k_ref, v_ref, rkv_ref, lora_ref, gate_ref) = refs
    else:
        x_ref, mod_ref, g_ref, w_ref, k_ref, v_ref, rkv_ref, lora_ref = refs
    x = x_ref[0]
    h = _rmsnorm(x, g_ref[...]) * (1.0 + mod_ref[0, 1:2, :]) + mod_ref[0, 0:1, :]
    hb = h.astype(BF16)
    off = widths["q"]

    def seg(name):
        nonlocal off
        lo = off
        off += widths[name]
        return _dot(hb, w_ref[:, lo:off])

    def dup_heads(t):
        first = _iota(t.shape, 1) < HEAD_DIM
        other = pltpu.roll(t, HEAD_DIM, 1)
        return jnp.concatenate([jnp.where(first, t, other), jnp.where(first, other, t)], axis=1)

    kv = seg("kv")
    k, v = dup_heads(kv[:, :LANES]), dup_heads(kv[:, LANES:])
    if latent:
        cos_t, sin_t = cos_ref[...], sin_ref[...]
        q = _dot(hb, w_ref[:, :widths["q"]])
        q_ref[0] = (_rope(q, cos_t, sin_t) * (LOG2_E * HEAD_DIM ** -0.5)).astype(BF16)
        k = _rope(k, cos_t, sin_t)
    k_ref[0] = k.astype(BF16)
    v_ref[0] = v.astype(BF16)
    rkv_ref[0] = seg("rkv")
    lora_ref[0] = seg("lora")
    if latent:
        gate_ref[0] = jax.nn.sigmoid(_dot(hb, wg_ref[...])).astype(BF16)


def _in_proj(x, mod, mod_row, norm_g, weights, widths, tables, *, latent, tile):
    b, l, d = x.shape
    nt = l // tile
    if latent:
        mod_map = lambda i, t: (i, 0, 0)
    else:
        mod_map = lambda i, t: (mod_row, 0, 0)
    tok = lambda w: pl.BlockSpec((1, tile, w), lambda i, t: (i, t, 0))
    in_specs = [tok(d),
                pl.BlockSpec((1,) + mod.shape[1:], mod_map),
                _const_spec((1, d))] + [_const_spec(w.shape) for w in weights]
    args = [x, mod, norm_g.reshape(1, d), *weights]
    out_specs, out_shape = [], []

    def out(w, dt):
        out_specs.append(tok(w))
        out_shape.append(jax.ShapeDtypeStruct((b, l, w), dt))

    if latent:
        in_specs += [pl.BlockSpec((tile, LANES), lambda i, t: (t, 0))] * 2
        args += list(tables)
        out(widths["q"], BF16)
    out(widths["k"], BF16)
    out(widths["v"], BF16)
    out(widths["rkv"], F32)
    out(widths["lora"], F32)
    if latent:
        out(widths["gate"], BF16)
    return pl.pallas_call(
        functools.partial(_inproj_kernel, latent=latent, widths=widths),
        grid=(b, nt),
        in_specs=in_specs,
        out_specs=out_specs,
        out_shape=out_shape,
        compiler_params=_params("parallel", "parallel"),
        name="in_proj_latent" if latent else "in_proj_context",
    )(*args)


def _attn_kernel(sink_ref, q_ref, kp_ref, kc_ref, kn_ref, vp_ref, vc_ref, vn_ref,
                 kx_ref, vx_ref, o_ref, *, n_kv):
    i = pl.program_id(1)
    last = pl.num_programs(1) - 1
    blk = ATTN_BLOCK
    qi = _iota((blk, blk), 0)
    kj = _iota((blk, blk), 1)
    left = _iota((blk, LANES), 1) < HEAD_DIM

    def key_block(refs, j, gs):
        ref_p, ref_c, ref_n = refs[:3]
        if j < 0:
            return ref_p[0, :, gs]
        if j >= ATTN_QB:
            return ref_n[0, :, gs]
        return ref_c[0, j * blk:(j + 1) * blk, gs]

    items = []
    for qb in range(ATTN_QB):
        lo_ok = kj >= qi
        hi_ok = kj <= qi
        if qb == 0:
            lo_ok = lo_ok & (i > 0)
        if qb == ATTN_QB - 1:
            hi_ok = hi_ok & (i < last)
        bias_lo = jnp.concatenate([jnp.where(lo_ok, 0.0, NEG)] * Q_PER_KV, axis=0)
        bias_hi = jnp.concatenate([jnp.where(hi_ok, 0.0, NEG)] * Q_PER_KV, axis=0)
        rows = slice(qb * blk, (qb + 1) * blk)
        q = q_ref[0, rows, :].astype(F32)
        for g in range(n_kv):
            gs = slice(g * LANES, (g + 1) * LANES)
            heads = range(g * Q_PER_KV, (g + 1) * Q_PER_KV)
            qs, sinks = [], []
            for hd in heads:
                qp = q[:, (hd // PAIR) * LANES:(hd // PAIR + 1) * LANES]
                keep = left if hd % PAIR == 0 else jnp.logical_not(left)
                qs.append(jnp.where(keep, qp, 0.0).astype(BF16))
                sinks.append(jnp.full((blk, 1), sink_ref[hd] * LOG2_E, F32))
            items.append({
                "qb": qb, "gs": gs, "rows": rows, "heads": heads, "bias": (bias_lo, bias_hi),
                "qs": jnp.concatenate(qs, axis=0), "sink": jnp.concatenate(sinks, axis=0)})

    def cat(refs, it):
        return jnp.concatenate([key_block(refs, it["qb"] + j, it["gs"]) for j in (-1, 0, 1)]
                               + [refs[3][0, :, it["gs"]]], axis=0)

    s = [_dot_nt(it["qs"], cat((kp_ref, kc_ref, kn_ref, kx_ref), it)) for it in items]
    s = [jnp.concatenate([x[:, :blk] + it["bias"][0], x[:, blk:2 * blk], x[:, 2 * blk:3 * blk] + it["bias"][1],
                          x[:, 3 * blk:]], axis=1) for x, it in zip(s, items)]
    m = [jnp.maximum(jnp.max(x, axis=1, keepdims=True), it["sink"]) for x, it in zip(s, items)]
    p = [jnp.exp2(x - mx) for x, mx in zip(s, m)]
    den = [jnp.sum(x, axis=1, keepdims=True) + jnp.exp2(it["sink"] - mx) for x, mx, it in zip(p, m, items)]
    o = [_dot(x.astype(BF16), cat((vp_ref, vc_ref, vn_ref, vx_ref), it)) / dn for x, dn, it in zip(p, den, items)]
    for x, it in zip(o, items):
        for hd in it["heads"][::PAIR]:
            j = hd - it["heads"][0]
            pair = jnp.where(left, x[j * blk:(j + 1) * blk], x[(j + 1) * blk:(j + 2) * blk])
            col = (hd // PAIR) * LANES
            o_ref[0, it["rows"], col:col + LANES] = pair.astype(BF16)


def _attention(sink, q, kd, vd, kxd, vxd):
    b, l, wq = q.shape
    wk = kd.shape[2]
    lc = kxd.shape[1]
    nb = l // ATTN_BLOCK
    span = ATTN_QB * ATTN_BLOCK
    blk = lambda w, f: pl.BlockSpec((1, ATTN_BLOCK, w), f)
    prev = lambda bi, i: (bi, jnp.maximum(i * ATTN_QB - 1, 0), 0)
    nxt = lambda bi, i: (bi, jnp.minimum((i + 1) * ATTN_QB, nb - 1), 0)
    cur = lambda w: pl.BlockSpec((1, span, w), lambda bi, i: (bi, i, 0))
    ctx = pl.BlockSpec((1, lc, wk), lambda bi, i: (bi, 0, 0))
    return pl.pallas_call(
        functools.partial(_attn_kernel, n_kv=wk // LANES),
        grid=(b, l // span),
        in_specs=[pl.BlockSpec(memory_space=pltpu.SMEM),
                  cur(wq),
                  blk(wk, prev), cur(wk), blk(wk, nxt),
                  blk(wk, prev), cur(wk), blk(wk, nxt),
                  ctx, ctx],
        out_specs=cur(wq),
        out_shape=jax.ShapeDtypeStruct((b, l, wq), BF16),
        compiler_params=_params("parallel", "parallel"),
        name="attention",
    )(sink, q, kd, kd, kd, vd, vd, vd, kxd, vxd)


def _conv3(x, prev_row, next_row, w):
    n = x.shape[0]
    row = _iota((8, x.shape[1]), 0)
    xm = pltpu.roll(x, 1, 0)
    xm = jnp.concatenate([jnp.where(row == 0, prev_row, xm[:8]), xm[8:]], axis=0)
    xp = pltpu.roll(x, n - 1, 0)
    xp = jnp.concatenate([xp[:n - 8], jnp.where(row == 7, next_row, xp[n - 8:])], axis=0)
    return xm * w[0:1] + x * w[1:2] + xp * w[2:3]


def _pair_masks():
    n = 2 * CHUNK
    row = _iota((n, n), 0)
    lane = _iota((n, n), 1)
    top, left = row < CHUNK, lane < CHUNK
    return {"row": row % CHUNK, "lane": lane % CHUNK, "top": top, "left": left,
            "left_h": _iota((CHUNK, n), 1) < CHUNK}


def _stack(a, b):
    return jnp.concatenate([a, b], axis=0)


def _chunk_local(inst, m):
    bf = lambda x: x.astype(BF16)
    diag = jnp.logical_not(m["top"]) & (m["lane"] == m["row"])
    masks = ((m["lane"] < m["row"]) | diag, (m["lane"] > m["row"]) | diag)
    mask_a = [masks[i["rev"]] for i in inst]
    lh = m["left_h"]
    half = CHUNK // 2
    keep_l = jnp.where(m["left"], 1.0, 0.0).astype(BF16)
    keep_r = jnp.where(m["left"], 0.0, 1.0).astype(BF16)
    keep_lh = jnp.where(lh, 1.0, 0.0).astype(BF16)
    keep_rh = jnp.where(lh, 0.0, 1.0).astype(BF16)

    def unfold(xb, anti=False):
        a, b = xb * keep_lh, xb * keep_rh
        return _stack(b, a) if anti else _stack(a, b)

    lhs = [bf(_stack(i["at"], i["rt"])) for i in inst]
    a01 = [_dot_nt(l, _stack(bf(_stack(i["bt"], i["kt"])) * keep_l, bf(_stack(i["kt"], i["bt"])) * keep_r))
           for l, i in zip(lhs, inst)]
    a0 = [jnp.where(ma, a[:, :LANES], 0.0) for ma, a in zip(mask_a, a01)]
    a1 = [jnp.where(ma, a[:, LANES:], 0.0) for ma, a in zip(mask_a, a01)]
    nc = [jnp.where(lh, x[:CHUNK], y[:CHUNK]) for x, y in zip(a0, a1)]
    arb = [jnp.where(lh, x[CHUNK:], y[CHUNK:]) for x, y in zip(a0, a1)]
    ak_ark_sw = [bf(jnp.where(m["left"], y, x)) for x, y in zip(a0, a1)]
    vh = [_dot(a, unfold(bf(i["v"]), anti=True)) for a, i in zip(ak_ark_sw, inst)]
    eye = jnp.where(_iota((CHUNK, LANES), 1) % CHUNK == _iota((CHUNK, LANES), 0), 1.0, 0.0)
    tc = [eye + n for n in nc]
    ncb = [bf(n) for n in nc]
    nc = [_dot(n, unfold(n)) for n in ncb]
    steps = CHUNK.bit_length() - 1
    for _ in range(steps - 2):
        ncb = [bf(n) for n in nc]
        both = [_dot(n, jnp.concatenate([unfold(bf(t)), unfold(n)], axis=1)) for n, t in zip(ncb, tc)]
        tc = [t + r[:, :LANES] for t, r in zip(tc, both)]
        nc = [r[:, LANES:] for r in both]
    inc = [_dot(bf(n[half:]), unfold(bf(t))) for n, t in zip(nc, tc)]
    tc = [jnp.concatenate([t[:half], t[half:] + d], axis=0) for t, d in zip(tc, inc)]
    pq = [_dot(bf(t), jnp.concatenate([unfold(l[:CHUNK]), unfold(bf(x[:CHUNK]))], axis=1))
          for t, l, x in zip(tc, lhs, vh)]
    return [{"pm": r[:, :LANES], "qm": r[:, LANES:], "arb": b, "hm": x[CHUNK:]} for r, b, x in zip(pq, arb, vh)]


def _prep_kernel(*refs, latent, width):
    (rkv_ref, rkv_p, rkv_n, lora_ref, lora_p, lora_n, cw_ref, cwl_ref, kk_ref, ka_ref,
     w0_ref, a0_ref, wl_ref, rk_ref) = refs[:14]
    outs = refs[14:]
    out_refs = dict(zip(("pm", "qm", "rt", "bt", "arb", "kt"), outs[:6]))
    wc_ref, v_ref = outs[6:8]
    if latent:
        bonus_ref, g_ref, hsum_ref = outs[8:]
    t = pl.program_id(1)
    nt = pl.num_programs(1)
    tile = rkv_ref.shape[1]
    w = width
    has_prev = (t > 0).astype(F32)
    has_next = (t < nt - 1).astype(F32)
    u = _conv3(rkv_ref[0], rkv_p[0, 7:8, :] * has_prev, rkv_n[0, 0:1, :] * has_next, cw_ref[...])
    ul = _conv3(lora_ref[0], lora_p[0, 7:8, :] * has_prev, lora_n[0, 0:1, :] * has_next, cwl_ref[...])
    r, k, v = u[:, :w], u[:, w:2 * w], u[:, 2 * w:]
    ones_bd = _head_ones(min(MXU_DIM, w))

    kk = k * kk_ref[...]
    kk = kk * lax.rsqrt(jnp.maximum(_head_sum(kk * kk, ones_bd), 1e-24))

    lane = _iota(ul.shape, 1)
    lin = jnp.where(lane < DECAY_LORA, jnp.tanh(ul),
                    jnp.where(lane < DECAY_LORA + ICLR_LORA, ul, jax.nn.sigmoid(ul)))
    proj = _dot(lin.astype(BF16), wl_ref[...])

    span = min(MXU_DIM, tile)
    tr = _iota((span, span), 0)
    tc = _iota((span, span), 1)
    same = (tr // CHUNK) == (tc // CHUNK)
    tri = (jnp.where(same & (tc <= tr), 1.0, 0.0).astype(BF16),
           jnp.where(same & (tc >= tr), 1.0, 0.0).astype(BF16))

    def exact_dot(m, x):
        h1 = x.astype(BF16)
        h2 = (x - h1.astype(F32)).astype(BF16)
        return jnp.concatenate([_dot(m, h1[j:j + span]) + _dot(m, h2[j:j + span]) for j in range(0, tile, span)],
                               axis=0)

    masks = _pair_masks()
    k_sum = None
    work = []
    for d in range(2):
        z = w0_ref[d:d + 1, :] + proj[:, d * w:(d + 1) * w]
        lw = -math.exp(-0.5) * jax.nn.sigmoid(z)
        a = jax.nn.sigmoid(a0_ref[d:d + 1, :] + proj[:, (2 + d) * w:(3 + d) * w])
        kd = k * (1.0 + (a - 1.0) * ka_ref[...])
        k_sum = kd if k_sum is None else k_sum + kd
        cum = exact_dot(tri[d], lw)
        e_neg = jnp.exp(-cum)
        full = {"at": -kk * jnp.exp(cum - lw), "rt": r * jnp.exp(cum), "bt": kk * a * e_neg,
                "kt": kd * e_neg, "v": v}
        for name in ("rt", "bt", "kt"):
            out_refs[name][d, 0] = full[name].astype(BF16)
        for j in range(tile // CHUNK):
            edge = (j + 1) * CHUNK - 1 if d == 0 else j * CHUNK
            wc_ref[0, j, d:d + 1, :] = jnp.exp(cum[edge:edge + 1, :])
        work += [(d, slice(j * CHUNK, (j + 1) * CHUNK), slice(p * LANES, (p + 1) * LANES), full)
                 for j in range(tile // CHUNK) for p in range(w // LANES)]
    hsum = {}
    for g0 in range(0, len(work), PREP_GROUP):
        group = work[g0:g0 + PREP_GROUP]
        inst = [dict({name: val[rs, ls] for name, val in full.items()}, rev=d) for d, rs, ls, full in group]
        for (d, rs, ls, _), res in zip(group, _chunk_local(inst, masks)):
            seen = hsum.get((rs.start, ls.start))
            hsum[(rs.start, ls.start)] = (rs, ls, res["hm"] if seen is None else seen[2] + res["hm"])
            for name in res:
                if name in out_refs:
                    out_refs[name][d, 0, rs, ls] = res[name].astype(BF16)
    v_ref[0] = v.astype(BF16)
    if latent:
        for rs, ls, val in hsum.values():
            hsum_ref[0, rs, ls] = val
        bonus_ref[0] = _head_sum(r * k_sum * rk_ref[...], ones_bd, split=False) * v
        g_ref[0] = proj[:, 4 * w:5 * w].astype(BF16)


def _wkv_prep(rkv, lora, prm, *, latent, tile):
    b, l, w3 = rkv.shape
    w = w3 // 3
    nt = l // tile
    n8 = l // 8
    tok = lambda wd: pl.BlockSpec((1, tile, wd), lambda i, t: (i, t, 0))
    prev = lambda wd: pl.BlockSpec((1, 8, wd), lambda i, t: (i, jnp.maximum(t * (tile // 8) - 1, 0), 0))
    nxt = lambda wd: pl.BlockSpec((1, 8, wd), lambda i, t: (i, jnp.minimum((t + 1) * (tile // 8), n8 - 1), 0))
    wl = lora.shape[2]
    in_specs = [tok(w3), prev(w3), nxt(w3), tok(wl), prev(wl), nxt(wl)]
    consts = [prm["conv_rkv"], prm["conv_lora"], prm["k_k"], prm["k_a"], prm["decay_w0"],
              prm["iclr_a0"], prm["lora_w"], prm["r_k"]]
    in_specs += [_const_spec(c.shape) for c in consts]
    dirtok = pl.BlockSpec((2, 1, tile, w), lambda i, t: (0, i, t, 0))
    out_specs = [dirtok] * 6 + [pl.BlockSpec((1, tile // CHUNK, 2, w), lambda i, t: (i, t, 0, 0)), tok(w)]
    out_shape = [jax.ShapeDtypeStruct((2, b, l, w), BF16)] * 6 + [
        jax.ShapeDtypeStruct((b, l // CHUNK, 2, w), F32), jax.ShapeDtypeStruct((b, l, w), BF16)]
    if latent:
        out_specs += [tok(w), tok(w), tok(w)]
        out_shape += [jax.ShapeDtypeStruct((b, l, w), F32), jax.ShapeDtypeStruct((b, l, w), BF16),
                      jax.ShapeDtypeStruct((b, l, w), F32)]
    return pl.pallas_call(
        functools.partial(_prep_kernel, latent=latent, width=w),
        grid=(b, nt),
        in_specs=in_specs,
        out_specs=out_specs,
        out_shape=out_shape,
        compiler_params=_params("parallel", "parallel"),
        name="wkv_prep_latent" if latent else "wkv_prep_context",
    )(rkv, rkv, rkv, lora, lora, lora, *consts)


def _wkv_kernel(*refs, emit_y, n_pairs):
    names = ("pm", "qm", "rt", "bt", "arb", "kt", "wc", "v")
    n = len(names)
    dir_refs = (dict(zip(names, refs[0:n])), dict(zip(names, refs[n:2 * n])))
    z0_ref = refs[2 * n]
    if emit_y:
        y_refs = refs[2 * n + 1:2 * n + 3]
        z_scr = refs[2 * n + 3]
    else:
        zfin_ref = refs[2 * n + 1]
        z_scr = refs[2 * n + 2]
    c = pl.program_id(0)

    @pl.when(c == 0)
    def _():
        z_scr[...] = z0_ref[...]

    left_h = _iota((CHUNK, LANES), 1) < CHUNK
    keep_l = jnp.where(left_h, 1.0, 0.0).astype(BF16)
    keep_r = jnp.where(left_h, 0.0, 1.0).astype(BF16)

    def unfold(xb):
        return _stack(xb * keep_l, xb * keep_r)

    tiles = [(i, d, p, slice(p * LANES, (p + 1) * LANES))
             for i in range(z_scr.shape[0]) for d in range(2) for p in range(n_pairs)]
    state = [z_scr[i, d, p] for i, d, p, _ in tiles]
    n_sub = dir_refs[0]["wc"].shape[1]
    for step in range(n_sub):
        sub = (step, n_sub - 1 - step)
        rows = [slice(sub[d] * CHUNK, (sub[d] + 1) * CHUNK) for d in range(2)]
        ld = lambda name: [dir_refs[d][name][0, i, rows[d], sl] for i, d, _, sl in tiles]
        v = [dir_refs[d]["v"][i, rows[d], sl] for i, d, _, sl in tiles]
        sbd = [unfold(s.astype(BF16)) for s in state]
        if emit_y:
            ur = [_dot_nt(_stack(pm, rt), s) for pm, rt, s in zip(ld("pm"), ld("rt"), sbd)]
        else:
            ur = [_dot_nt(pm, s) for pm, s in zip(ld("pm"), sbd)]
        u = [x[:CHUNK] + q.astype(F32) for x, q in zip(ur, ld("qm"))]
        ub = [ui.astype(BF16) for ui in u]
        inc = [_dot_tn(_stack(ui, vi), _stack(bt, kt)) for ui, vi, bt, kt in zip(ub, v, ld("bt"), ld("kt"))]
        inc = [jnp.where(left_h, x[:CHUNK], x[CHUNK:]) for x in inc]
        if emit_y:
            yc = [_dot(a, unfold(ui)) for a, ui in zip(ld("arb"), ub)]
            for (i, d, _, sl), x, ys in zip(tiles, ur, yc):
                y_refs[d][i, rows[d], sl] = (x[CHUNK:] + ys).astype(BF16)
        state = [(s + dz) * dir_refs[d]["wc"][i, sub[d], d:d + 1, sl]
                 for (i, d, _, sl), s, dz in zip(tiles, state, inc)]
    for (i, d, p, _), s in zip(tiles, state):
        z_scr[i, d, p] = s

    if not emit_y:
        @pl.when(c == pl.num_programs(0) - 1)
        def _():
            zfin_ref[...] = z_scr[...]


def _wkv_scan(prep, z0, *, emit_y):
    wc = prep[6]
    _, b, l, w = prep[0].shape
    n_sub = min(SCAN_CHUNKS, l // CHUNK)
    blk = n_sub * CHUNK
    nc = l // blk
    n_pairs = w // LANES
    fwd = lambda c: c
    rev = lambda c: nc - 1 - c
    in_specs, args = [], []
    for d, cm in enumerate((fwd, rev)):
        for arr in prep[:6]:
            in_specs.append(pl.BlockSpec((1, b, blk, w), lambda c, d=d, cm=cm: (d, 0, cm(c), 0)))
            args.append(arr)
        in_specs.append(pl.BlockSpec((b, n_sub, 2, w), lambda c, cm=cm: (0, cm(c), 0, 0)))
        args.append(wc)
        in_specs.append(pl.BlockSpec((b, blk, w), lambda c, cm=cm: (0, cm(c), 0)))
        args.append(prep[7])
    zshape = (b, 2, n_pairs, CHUNK, LANES)
    zspec = pl.BlockSpec(zshape, lambda c: (0, 0, 0, 0, 0))
    in_specs.append(zspec)
    args.append(z0)
    if emit_y:
        out_specs = [pl.BlockSpec((b, blk, w), lambda c: (0, c, 0)),
                     pl.BlockSpec((b, blk, w), lambda c: (0, nc - 1 - c, 0))]
        out_shape = [jax.ShapeDtypeStruct((b, l, w), BF16)] * 2
    else:
        out_specs = zspec
        out_shape = jax.ShapeDtypeStruct(zshape, F32)
    return pl.pallas_call(
        functools.partial(_wkv_kernel, emit_y=emit_y, n_pairs=n_pairs),
        grid=(nc,),
        in_specs=in_specs,
        out_specs=out_specs,
        out_shape=out_shape,
        scratch_shapes=[pltpu.VMEM(zshape, F32)],
        compiler_params=_params("arbitrary"),
        name="wkv_scan_latent" if emit_y else "wkv_scan_context",
    )(*args)


def _merge_kernel(x_ref, mod_ref, ya_ref, yf_ref, yr_ref, yh_ref, bonus_ref, g_ref, gate_ref,
                  lnw_ref, lnb_ref, wba_ref, wbr_ref, wo_ref, n2_ref, wu_ref, wd_ref, nf_ref,
                  o_ref, *, ff_chunk):
    x = x_ref[0]
    d = x.shape[1]
    mod = lambda j: mod_ref[0, j:j + 1, :]
    y = yf_ref[0].astype(F32) + yr_ref[0].astype(F32) + yh_ref[0]
    gw = min(MXU_DIM, y.shape[1])
    ones_bd = _head_ones(gw)

    def head_mean(t):
        tb = t.astype(BF16)
        parts = [_dot(tb[:, j:j + gw], ones_bd) for j in range(0, t.shape[1], gw)]
        return jnp.concatenate(parts, axis=1) * (1.0 / HEAD_DIM)

    mu = head_mean(y)
    yc = y - mu
    var = head_mean(yc * yc)
    yn = yc * lax.rsqrt(var + LNX_EPS)
    yr = (yn * lnw_ref[...] + lnb_ref[...] + bonus_ref[0]) * g_ref[0].astype(F32)
    gate = gate_ref[0].astype(F32)
    merged = gate[:, :d] * _dot(ya_ref[0], wba_ref[...]) + gate[:, d:] * _dot(yr.astype(BF16), wbr_ref[...])
    x1 = x + mod(2) * _dot(merged.astype(BF16), wo_ref[...])
    h2 = (_rmsnorm(x1, n2_ref[...]) * (1.0 + mod(4)) + mod(3)).astype(BF16)
    acc = jnp.zeros_like(x1)
    for j in range(wu_ref.shape[1] // ff_chunk):
        cs = slice(j * ff_chunk, (j + 1) * ff_chunk)
        up = jnp.maximum(_dot(h2, wu_ref[:, cs]), 0.0)
        acc = acc + _dot((up * up).astype(BF16), wd_ref[cs, :])
    x2 = x1 + mod(5) * acc
    o_ref[0] = _rmsnorm(x2, nf_ref[...])


def _merge_mlp(x, mod, ya, yf, yr, yh, bonus, g, gate, prm, *, tile):
    b, l, d = x.shape
    tok = lambda arr: pl.BlockSpec((1, tile, arr.shape[2]), lambda i, t: (i, t, 0))
    consts = [prm["lnx_w"], prm["lnx_b"], prm["w_branch_attn"], prm["w_branch_rwkv"], prm["w_out"],
              prm["norm2_g"], prm["w_mlp_up"], prm["w_mlp_down"], prm["norm_f_g"]]
    toks = [ya, yf, yr, yh, bonus, g, gate]
    return pl.pallas_call(
        functools.partial(_merge_kernel, ff_chunk=min(1024, prm["w_mlp_up"].shape[1])),
        grid=(b, l // tile),
        in_specs=[tok(x), pl.BlockSpec((1,) + mod.shape[1:], lambda i, t: (i, 0, 0))]
        + [tok(a) for a in toks] + [_const_spec(c.shape) for c in consts],
        out_specs=tok(x),
        out_shape=jax.ShapeDtypeStruct(x.shape, x.dtype),
        compiler_params=_params("parallel", "parallel"),
        name="merge_mlp",
    )(x, mod, *toks, *consts)


def _rope_tables(l):
    n_freq = HEAD_DIM // 4
    inv_freq = np.power(np.float32(ROPE_BASE), -np.arange(n_freq, dtype=np.float32) / np.float32(n_freq))
    rows = l // GRID_W
    row = np.repeat(np.arange(rows, dtype=np.float32), GRID_W)
    col = np.tile(np.arange(GRID_W, dtype=np.float32), rows)
    ang = np.concatenate([row[:, None] * inv_freq, col[:, None] * inv_freq], axis=-1).astype(np.float32)
    cos, sin = np.cos(ang), np.sin(ang)
    reps = LANES // HEAD_DIM
    return (jnp.asarray(np.tile(np.concatenate([cos, cos], axis=1), (1, reps))),
            jnp.asarray(np.tile(np.concatenate([-sin, sin], axis=1), (1, reps))))


def _pad_cols(w, width):
    return jnp.pad(w, ((0, 0), (0, width - w.shape[1])))


def kernel(x, c, ctx, c_ctx, w_ada, b_ada, norm1_g, w_in, sink, conv_w, decay_w0, decay_w2, iclr_a0, iclr_a2, gate_g2, k_k, k_a, r_k, lnx_w, lnx_b, w_branch_attn, w_branch_rwkv, w_out, norm2_g, w_mlp_up, w_mlp_down, norm_f_g):
    assert w_in.shape[0] == 1, "single-layer block: context tokens are read, never updated"
    b, l, d = x.shape
    attn_w = w_branch_attn.shape[1]
    rw = w_branch_rwkv.shape[1]
    n_q = attn_w // HEAD_DIM
    n_kv = n_q // Q_PER_KV
    kv_w = n_kv * HEAD_DIM
    assert kv_w == LANES and rw % LANES == 0 and ctx.shape[1] % 256 == 0
    assert l % (ATTN_QB * ATTN_BLOCK) == 0 and l % 512 == 0

    w = w_in[0]
    o_k, o_r = attn_w, attn_w + 2 * kv_w
    o_l = o_r + 3 * rw
    o_g = o_l + DECAY_LORA + ICLR_LORA + GATE_LORA
    assert o_l + LORA_PAD <= w.shape[1]
    w_main = w[:, :o_l + LORA_PAD].astype(BF16)
    w_gate = w[:, o_g:].astype(BF16)
    widths = {"q": attn_w, "kv": 2 * kv_w, "k": PAIR * kv_w, "v": PAIR * kv_w, "rkv": 3 * rw,
              "lora": LORA_PAD, "gate": 2 * d}

    cw = conv_w[0]
    used = DECAY_LORA + ICLR_LORA + GATE_LORA
    lora_w = jnp.concatenate([
        jnp.pad(jnp.concatenate([decay_w2[0, 0], decay_w2[0, 1]], axis=1), ((0, 0), (0, 3 * rw))),
        jnp.pad(jnp.concatenate([iclr_a2[0, 0], iclr_a2[0, 1]], axis=1), ((0, 0), (2 * rw, rw))),
        jnp.pad(gate_g2[0], ((0, LORA_PAD - used), (4 * rw, 0)))], axis=0)
    prm = {
        "conv_rkv": cw[:, :3 * rw], "conv_lora": _pad_cols(cw[:, 3 * rw:], LORA_PAD),
        "k_k": k_k[0].reshape(1, rw), "k_a": k_a[0].reshape(1, rw),
        "decay_w0": decay_w0[0], "iclr_a0": iclr_a0[0], "lora_w": lora_w.astype(BF16),
        "r_k": r_k[0].reshape(1, rw),
        "lnx_w": lnx_w[0].reshape(1, rw), "lnx_b": lnx_b[0].reshape(1, rw),
        "w_branch_attn": w_branch_attn[0].astype(BF16), "w_branch_rwkv": w_branch_rwkv[0].astype(BF16),
        "w_out": w_out[0].astype(BF16), "norm2_g": norm2_g[0].reshape(1, d),
        "w_mlp_up": w_mlp_up[0].astype(BF16), "w_mlp_down": w_mlp_down[0].astype(BF16),
        "norm_f_g": norm_f_g.reshape(1, d),
    }

    rows = -(-(b + 1) // 8) * 8
    cc = jnp.concatenate([c, c_ctx[None, :], jnp.zeros((rows - b - 1, d), F32)], axis=0)
    mod = _ada_mod(cc, w_ada[0], b_ada[0]).reshape(rows, -1, d)

    q, kd, vd, rkv, lora, gate = _in_proj(x, mod, b, norm1_g[0], [w_main, w_gate], widths, _rope_tables(l),
                                          latent=True, tile=1024)
    kxd, vxd, rkv_c, lora_c = _in_proj(ctx, mod, b, norm1_g[0], [w_main], widths, None,
                                       latent=False, tile=256)
    ya = _attention(sink[0], q, kd, vd, kxd, vxd)

    prep_c = _wkv_prep(rkv_c, lora_c, prm, latent=False, tile=256)
    z_ctx = _wkv_scan(prep_c, jnp.zeros((b, 2, rw // LANES, CHUNK, LANES), F32), emit_y=False)
    prep = _wkv_prep(rkv, lora, prm, latent=True, tile=512)
    yf, yr = _wkv_scan(prep, z_ctx, emit_y=True)
    bonus, g, yh = prep[8], prep[9], prep[10]

    return _merge_mlp(x, mod, ya, yf, yr, yh, bonus, g, gate, prm, tile=512)
```

```python
import functools
import math

import jax
import jax.numpy as jnp
import numpy as np
from jax import lax
from jax.experimental import pallas as pl
from jax.experimental.pallas import tpu as pltpu

F32 = jnp.float32
BF16 = jnp.bfloat16

GRID_W = 64
HEAD_DIM = 64
Q_PER_KV = 4
ATTN_BLOCK = 128
ATTN_QB = 8
ROPE_BASE = 10000.0
NORM_EPS = 1e-6
LNX_EPS = 1e-5 * HEAD_DIM
DECAY_LORA, ICLR_LORA, GATE_LORA = 32, 32, 96
LORA_PAD = 256
CHUNK = 64
SCAN_CHUNKS = 4
PREP_GROUP = 16
LANES = 128
MXU_DIM = 256
PAIR = LANES // HEAD_DIM
NEG = -1e30
LOG2_E = math.log2(math.e)
VMEM_LIMIT = 56 * 1024 * 1024


def _dot(a, b):
    return jnp.dot(a, b, preferred_element_type=F32)


def _dot_nt(a, b):
    return lax.dot_general(a, b, (((1,), (1,)), ((), ())), preferred_element_type=F32)


def _dot_tn(a, b):
    return lax.dot_general(a, b, (((0,), (0,)), ((), ())), preferred_element_type=F32)


def _iota(shape, dim):
    return lax.broadcasted_iota(jnp.int32, shape, dim)


def _head_ones(width):
    r = _iota((width, width), 0) // HEAD_DIM
    c = _iota((width, width), 1) // HEAD_DIM
    return jnp.where(r == c, 1.0, 0.0).astype(BF16)


def _head_sum(x, ones_bd):
    gw = ones_bd.shape[0]
    xb = x.astype(BF16)
    return jnp.concatenate([_dot(xb[:, j:j + gw], ones_bd) for j in range(0, x.shape[1], gw)], axis=1)


def _rmsnorm(x, g):
    ms = jnp.mean(x * x, axis=-1, keepdims=True)
    return x * lax.rsqrt(ms + NORM_EPS) * g


def _params(*sem):
    return pltpu.CompilerParams(dimension_semantics=sem, vmem_limit_bytes=VMEM_LIMIT)


def _const_spec(shape):
    nd = len(shape)
    return pl.BlockSpec(shape, lambda *_: (0,) * nd, pipeline_mode=pl.Buffered(1))


def _ada_kernel(c_ref, w_ref, b_ref, o_ref):
    c = c_ref[...]
    s = c * jax.nn.sigmoid(c)
    o_ref[...] = _dot(s.astype(BF16), w_ref[...].astype(BF16)) + b_ref[...]


def _ada_mod(cc, w_ada, b_ada):
    rows, d = cc.shape
    n = w_ada.shape[1]
    return pl.pallas_call(
        _ada_kernel,
        grid=(n // d,),
        in_specs=[pl.BlockSpec((rows, d), lambda j: (0, 0)),
                  pl.BlockSpec((d, d), lambda j: (0, j)),
                  pl.BlockSpec((1, d), lambda j: (0, j))],
        out_specs=pl.BlockSpec((rows, d), lambda j: (0, j)),
        out_shape=jax.ShapeDtypeStruct((rows, n), F32),
        compiler_params=_params("arbitrary"),
        name="ada_mod",
    )(cc, w_ada, b_ada.reshape(1, n))


def _rope(x, cos_t, sin_t):
    w = x.shape[1]
    half = HEAD_DIM // 2
    first = (_iota(x.shape, 1) % HEAD_DIM) < half
    swapped = jnp.where(first, pltpu.roll(x, w - half, 1), pltpu.roll(x, half, 1))
    reps = w // LANES
    c = jnp.concatenate([cos_t] * reps, axis=1)
    s = jnp.concatenate([sin_t] * reps, axis=1)
    return x * c + swapped * s


def _inproj_kernel(*refs, latent, widths):
    if latent:
        (x_ref, mod_ref, g_ref, w_ref, wg_ref, cos_ref, sin_ref,
         q_ref, k_ref, v_ref, rkv_ref, lora_ref, gate_ref) = refs
    else:
        x_ref, mod_ref, g_ref, w_ref, k_ref, v_ref, rkv_ref, lora_ref = refs
    x = x_ref[0]
    h = _rmsnorm(x, g_ref[...]) * (1.0 + mod_ref[0, 1:2, :]) + mod_ref[0, 0:1, :]
    hb = h.astype(BF16)
    off = widths["q"]

    def seg(name):
        nonlocal off
        lo = off
        off += widths[name]
        return _dot(hb, w_ref[:, lo:off])

    def dup_heads(t):
        first = _iota(t.shape, 1) < HEAD_DIM
        other = pltpu.roll(t, HEAD_DIM, 1)
        return jnp.concatenate([jnp.where(first, t, other), jnp.where(first, other, t)], axis=1)

    kv = seg("kv")
    k, v = dup_heads(kv[:, :LANES]), dup_heads(kv[:, LANES:])
    if latent:
        cos_t, sin_t = cos_ref[...], sin_ref[...]
        q = _dot(hb, w_ref[:, :widths["q"]])
        q_ref[0] = (_rope(q, cos_t, sin_t) * (LOG2_E * HEAD_DIM ** -0.5)).astype(BF16)
        k = _rope(k, cos_t, sin_t)
    k_ref[0] = k.astype(BF16)
    v_ref[0] = v.astype(BF16)
    rkv_ref[0] = seg("rkv")
    lora_ref[0] = seg("lora")
    if latent:
        gate_ref[0] = jax.nn.sigmoid(_dot(hb, wg_ref[...])).astype(BF16)


def _in_proj(x, mod, mod_row, norm_g, weights, widths, tables, *, latent, tile):
    b, l, d = x.shape
    nt = l // tile
    if latent:
        mod_map = lambda i, t: (i, 0, 0)
    else:
        mod_map = lambda i, t: (mod_row, 0, 0)
    tok = lambda w: pl.BlockSpec((1, tile, w), lambda i, t: (i, t, 0))
    in_specs = [tok(d),
                pl.BlockSpec((1,) + mod.shape[1:], mod_map),
                _const_spec((1, d))] + [_const_spec(w.shape) for w in weights]
    args = [x, mod, norm_g.reshape(1, d), *weights]
    out_specs, out_shape = [], []

    def out(w, dt):
        out_specs.append(tok(w))
        out_shape.append(jax.ShapeDtypeStruct((b, l, w), dt))

    if latent:
        in_specs += [pl.BlockSpec((tile, LANES), lambda i, t: (t, 0))] * 2
        args += list(tables)
        out(widths["q"], BF16)
    out(widths["k"], BF16)
    out(widths["v"], BF16)
    out(widths["rkv"], F32)
    out(widths["lora"], F32)
    if latent:
        out(widths["gate"], BF16)
    return pl.pallas_call(
        functools.partial(_inproj_kernel, latent=latent, widths=widths),
        grid=(b, nt),
        in_specs=in_specs,
        out_specs=out_specs,
        out_shape=out_shape,
        compiler_params=_params("parallel", "parallel"),
        name="in_proj_latent" if latent else "in_proj_context",
    )(*args)


def _attn_kernel(sink_ref, q_ref, kp_ref, kc_ref, kn_ref, vp_ref, vc_ref, vn_ref,
                 kx_ref, vx_ref, o_ref, *, n_kv):
    i = pl.program_id(1)
    last = pl.num_programs(1) - 1
    blk = ATTN_BLOCK
    qi = _iota((blk, blk), 0)
    kj = _iota((blk, blk), 1)
    left = _iota((blk, LANES), 1) < HEAD_DIM

    def key_block(refs, j, gs):
        ref_p, ref_c, ref_n = refs[:3]
        if j < 0:
            return ref_p[0, :, gs]
        if j >= ATTN_QB:
            return ref_n[0, :, gs]
        return ref_c[0, j * blk:(j + 1) * blk, gs]

    items = []
    for qb in range(ATTN_QB):
        lo_ok = kj >= qi
        hi_ok = kj <= qi
        if qb == 0:
            lo_ok = lo_ok & (i > 0)
        if qb == ATTN_QB - 1:
            hi_ok = hi_ok & (i < last)
        bias_lo = jnp.concatenate([jnp.where(lo_ok, 0.0, NEG)] * Q_PER_KV, axis=0)
        bias_hi = jnp.concatenate([jnp.where(hi_ok, 0.0, NEG)] * Q_PER_KV, axis=0)
        rows = slice(qb * blk, (qb + 1) * blk)
        q = q_ref[0, rows, :].astype(F32)
        for g in range(n_kv):
            gs = slice(g * LANES, (g + 1) * LANES)
            heads = range(g * Q_PER_KV, (g + 1) * Q_PER_KV)
            qs, sinks = [], []
            for hd in heads:
                qp = q[:, (hd // PAIR) * LANES:(hd // PAIR + 1) * LANES]
                keep = left if hd % PAIR == 0 else jnp.logical_not(left)
                qs.append(jnp.where(keep, qp, 0.0).astype(BF16))
                sinks.append(jnp.full((blk, 1), sink_ref[hd] * LOG2_E, F32))
            items.append({
                "qb": qb, "gs": gs, "rows": rows, "heads": heads, "bias": (bias_lo, bias_hi),
                "qs": jnp.concatenate(qs, axis=0), "sink": jnp.concatenate(sinks, axis=0)})

    def cat(refs, it):
        return jnp.concatenate([key_block(refs, it["qb"] + j, it["gs"]) for j in (-1, 0, 1)]
                               + [refs[3][0, :, it["gs"]]], axis=0)

    s = [_dot_nt(it["qs"], cat((kp_ref, kc_ref, kn_ref, kx_ref), it)) for it in items]
    s = [jnp.concatenate([x[:, :blk] + it["bias"][0], x[:, blk:2 * blk], x[:, 2 * blk:3 * blk] + it["bias"][1],
                          x[:, 3 * blk:]], axis=1) for x, it in zip(s, items)]
    m = [jnp.maximum(jnp.max(x, axis=1, keepdims=True), it["sink"]) for x, it in zip(s, items)]
    p = [jnp.exp2(x - mx) for x, mx in zip(s, m)]
    den = [jnp.sum(x, axis=1, keepdims=True) + jnp.exp2(it["sink"] - mx) for x, mx, it in zip(p, m, items)]
    o = [_dot(x.astype(BF16), cat((vp_ref, vc_ref, vn_ref, vx_ref), it)) / dn for x, dn, it in zip(p, den, items)]
    for x, it in zip(o, items):
        for hd in it["heads"][::PAIR]:
            j = hd - it["heads"][0]
            pair = jnp.where(left, x[j * blk:(j + 1) * blk], x[(j + 1) * blk:(j + 2) * blk])
            col = (hd // PAIR) * LANES
            o_ref[0, it["rows"], col:col + LANES] = pair.astype(BF16)


def _attention(sink, q, kd, vd, kxd, vxd):
    b, l, wq = q.shape
    wk = kd.shape[2]
    lc = kxd.shape[1]
    nb = l // ATTN_BLOCK
    span = ATTN_QB * ATTN_BLOCK
    blk = lambda w, f: pl.BlockSpec((1, ATTN_BLOCK, w), f)
    prev = lambda bi, i: (bi, jnp.maximum(i * ATTN_QB - 1, 0), 0)
    nxt = lambda bi, i: (bi, jnp.minimum((i + 1) * ATTN_QB, nb - 1), 0)
    cur = lambda w: pl.BlockSpec((1, span, w), lambda bi, i: (bi, i, 0))
    ctx = pl.BlockSpec((1, lc, wk), lambda bi, i: (bi, 0, 0))
    return pl.pallas_call(
        functools.partial(_attn_kernel, n_kv=wk // LANES),
        grid=(b, l // span),
        in_specs=[pl.BlockSpec(memory_space=pltpu.SMEM),
                  cur(wq),
                  blk(wk, prev), cur(wk), blk(wk, nxt),
                  blk(wk, prev), cur(wk), blk(wk, nxt),
                  ctx, ctx],
        out_specs=cur(wq),
        out_shape=jax.ShapeDtypeStruct((b, l, wq), BF16),
        compiler_params=_params("parallel", "parallel"),
        name="attention",
    )(sink, q, kd, kd, kd, vd, vd, vd, kxd, vxd)


def _conv3(x, prev_row, next_row, w):
    n = x.shape[0]
    row = _iota((8, x.shape[1]), 0)
    xm = pltpu.roll(x, 1, 0)
    xm = jnp.concatenate([jnp.where(row == 0, prev_row, xm[:8]), xm[8:]], axis=0)
    xp = pltpu.roll(x, n - 1, 0)
    xp = jnp.concatenate([xp[:n - 8], jnp.where(row == 7, next_row, xp[n - 8:])], axis=0)
    return xm * w[0:1] + x * w[1:2] + xp * w[2:3]


def _pair_masks():
    n = 2 * CHUNK
    row = _iota((n, n), 0)
    lane = _iota((n, n), 1)
    top, left = row < CHUNK, lane < CHUNK
    return {"row": row % CHUNK, "lane": lane % CHUNK, "top": top, "left": left,
            "left_h": _iota((CHUNK, n), 1) < CHUNK}


def _stack(a, b):
    return jnp.concatenate([a, b], axis=0)


def _chunk_local(inst, m):
    bf = lambda x: x.astype(BF16)
    diag = jnp.logical_not(m["top"]) & (m["lane"] == m["row"])
    masks = ((m["lane"] < m["row"]) | diag, (m["lane"] > m["row"]) | diag)
    mask_a = [masks[i["rev"]] for i in inst]
    lh = m["left_h"]
    half = CHUNK // 2
    keep_l = jnp.where(m["left"], 1.0, 0.0).astype(BF16)
    keep_r = jnp.where(m["left"], 0.0, 1.0).astype(BF16)
    keep_lh = jnp.where(lh, 1.0, 0.0).astype(BF16)
    keep_rh = jnp.where(lh, 0.0, 1.0).astype(BF16)

    def unfold(xb, anti=False):
        a, b = xb * keep_lh, xb * keep_rh
        return _stack(b, a) if anti else _stack(a, b)

    lhs = [bf(_stack(i["at"], i["rt"])) for i in inst]
    a01 = [_dot_nt(l, _stack(bf(_stack(i["bt"], i["kt"])) * keep_l, bf(_stack(i["kt"], i["bt"])) * keep_r))
           for l, i in zip(lhs, inst)]
    a0 = [jnp.where(ma, a[:, :LANES], 0.0) for ma, a in zip(mask_a, a01)]
    a1 = [jnp.where(ma, a[:, LANES:], 0.0) for ma, a in zip(mask_a, a01)]
    nc = [jnp.where(lh, x[:CHUNK], y[:CHUNK]) for x, y in zip(a0, a1)]
    arb = [jnp.where(lh, x[CHUNK:], y[CHUNK:]) for x, y in zip(a0, a1)]
    ak_ark_sw = [bf(jnp.where(m["left"], y, x)) for x, y in zip(a0, a1)]
    vh = [_dot(a, unfold(bf(i["v"]), anti=True)) for a, i in zip(ak_ark_sw, inst)]
    eye = jnp.where(_iota((CHUNK, LANES), 1) % CHUNK == _iota((CHUNK, LANES), 0), 1.0, 0.0)
    tc = [eye + n for n in nc]
    ncb = [bf(n) for n in nc]
    nc = [_dot(n, unfold(n)) for n in ncb]
    steps = CHUNK.bit_length() - 1
    for _ in range(steps - 2):
        ncb = [bf(n) for n in nc]
        both = [_dot(n, jnp.concatenate([unfold(bf(t)), unfold(n)], axis=1)) for n, t in zip(ncb, tc)]
        tc = [t + r[:, :LANES] for t, r in zip(tc, both)]
        nc = [r[:, LANES:] for r in both]
    inc = [_dot(bf(n[half:]), unfold(bf(t))) for n, t in zip(nc, tc)]
    tc = [jnp.concatenate([t[:half], t[half:] + d], axis=0) for t, d in zip(tc, inc)]
    pq = [_dot(bf(t), jnp.concatenate([unfold(l[:CHUNK]), unfold(bf(x[:CHUNK]))], axis=1))
          for t, l, x in zip(tc, lhs, vh)]
    return [{"pm": r[:, :LANES], "qm": r[:, LANES:], "arb": b, "hm": x[CHUNK:]} for r, b, x in zip(pq, arb, vh)]


def _prep_kernel(*refs, latent, width):
    (rkv_ref, rkv_p, rkv_n, lora_ref, lora_p, lora_n, cw_ref, cwl_ref, kk_ref, ka_ref,
     w0_ref, a0_ref, wl_ref, rk_ref) = refs[:14]
    outs = refs[14:]
    out_refs = dict(zip(("pm", "qm", "rt", "bt", "arb", "kt"), outs[:6]))
    wc_ref, v_ref = outs[6:8]
    if latent:
        bonus_ref, g_ref, hsum_ref = outs[8:]
    t = pl.program_id(1)
    nt = pl.num_programs(1)
    tile = rkv_ref.shape[1]
    w = width
    has_prev = (t > 0).astype(F32)
    has_next = (t < nt - 1).astype(F32)
    u = _conv3(rkv_ref[0], rkv_p[0, 7:8, :] * has_prev, rkv_n[0, 0:1, :] * has_next, cw_ref[...])
    ul = _conv3(lora_ref[0], lora_p[0, 7:8, :] * has_prev, lora_n[0, 0:1, :] * has_next, cwl_ref[...])
    r, k, v = u[:, :w], u[:, w:2 * w], u[:, 2 * w:]
    ones_bd = _head_ones(min(MXU_DIM, w))

    kk = k * kk_ref[...]
    kk = kk * lax.rsqrt(jnp.maximum(_head_sum(kk * kk, ones_bd), 1e-24))

    lane = _iota(ul.shape, 1)
    lin = jnp.where(lane < DECAY_LORA, jnp.tanh(ul),
                    jnp.where(lane < DECAY_LORA + ICLR_LORA, ul, jax.nn.sigmoid(ul)))
    proj = _dot(lin.astype(BF16), wl_ref[...])

    span = min(MXU_DIM, tile)
    tr = _iota((span, span), 0)
    tc = _iota((span, span), 1)
    same = (tr // CHUNK) == (tc // CHUNK)
    tri = (jnp.where(same & (tc <= tr), 1.0, 0.0).astype(BF16),
           jnp.where(same & (tc >= tr), 1.0, 0.0).astype(BF16))

    def exact_dot(m, x):
        h1 = x.astype(BF16)
        h2 = (x - h1.astype(F32)).astype(BF16)
        return jnp.concatenate([_dot(m, h1[j:j + span]) + _dot(m, h2[j:j + span]) for j in range(0, tile, span)],
                               axis=0)

    masks = _pair_masks()
    k_sum = None
    work = []
    for d in range(2):
        z = w0_ref[d:d + 1, :] + proj[:, d * w:(d + 1) * w]
        lw = -math.exp(-0.5) * jax.nn.sigmoid(z)
        a = jax.nn.sigmoid(a0_ref[d:d + 1, :] + proj[:, (2 + d) * w:(3 + d) * w])
        kd = k * (1.0 + (a - 1.0) * ka_ref[...])
        k_sum = kd if k_sum is None else k_sum + kd
        cum = exact_dot(tri[d], lw)
        e_neg = jnp.exp(-cum)
        full = {"at": -kk * jnp.exp(cum - lw), "rt": r * jnp.exp(cum), "bt": kk * a * e_neg,
                "kt": kd * e_neg, "v": v}
        for name in ("rt", "bt", "kt"):
            out_refs[name][d, 0] = full[name].astype(BF16)
        for j in range(tile // CHUNK):
            edge = (j + 1) * CHUNK - 1 if d == 0 else j * CHUNK
            wc_ref[0, j, d:d + 1, :] = jnp.exp(cum[edge:edge + 1, :])
        work += [(d, slice(j * CHUNK, (j + 1) * CHUNK), slice(p * LANES, (p + 1) * LANES), full)
                 for j in range(tile // CHUNK) for p in range(w // LANES)]
    hsum = {}
    for g0 in range(0, len(work), PREP_GROUP):
        group = work[g0:g0 + PREP_GROUP]
        inst = [dict({name: val[rs, ls] for name, val in full.items()}, rev=d) for d, rs, ls, full in group]
        for (d, rs, ls, _), res in zip(group, _chunk_local(inst, masks)):
            seen = hsum.get((rs.start, ls.start))
            hsum[(rs.start, ls.start)] = (rs, ls, res["hm"] if seen is None else seen[2] + res["hm"])
            for name in res:
                if name in out_refs:
                    out_refs[name][d, 0, rs, ls] = res[name].astype(BF16)
    v_ref[0] = v.astype(BF16)
    if latent:
        for rs, ls, val in hsum.values():
            hsum_ref[0, rs, ls] = val
        bonus_ref[0] = _head_sum(r * k_sum * rk_ref[...], ones_bd) * v
        g_ref[0] = proj[:, 4 * w:5 * w].astype(BF16)


def _wkv_prep(rkv, lora, prm, *, latent, tile):
    b, l, w3 = rkv.shape
    w = w3 // 3
    nt = l // tile
    n8 = l // 8
    tok = lambda wd: pl.BlockSpec((1, tile, wd), lambda i, t: (i, t, 0))
    prev = lambda wd: pl.BlockSpec((1, 8, wd), lambda i, t: (i, jnp.maximum(t * (tile // 8) - 1, 0), 0))
    nxt = lambda wd: pl.BlockSpec((1, 8, wd), lambda i, t: (i, jnp.minimum((t + 1) * (tile // 8), n8 - 1), 0))
    wl = lora.shape[2]
    in_specs = [tok(w3), prev(w3), nxt(w3), tok(wl), prev(wl), nxt(wl)]
    consts = [prm["conv_rkv"], prm["conv_lora"], prm["k_k"], prm["k_a"], prm["decay_w0"],
              prm["iclr_a0"], prm["lora_w"], prm["r_k"]]
    in_specs += [_const_spec(c.shape) for c in consts]
    dirtok = pl.BlockSpec((2, 1, tile, w), lambda i, t: (0, i, t, 0))
    out_specs = [dirtok] * 6 + [pl.BlockSpec((1, tile // CHUNK, 2, w), lambda i, t: (i, t, 0, 0)), tok(w)]
    out_shape = [jax.ShapeDtypeStruct((2, b, l, w), BF16)] * 6 + [
        jax.ShapeDtypeStruct((b, l // CHUNK, 2, w), F32), jax.ShapeDtypeStruct((b, l, w), BF16)]
    if latent:
        out_specs += [tok(w), tok(w), tok(w)]
        out_shape += [jax.ShapeDtypeStruct((b, l, w), F32), jax.ShapeDtypeStruct((b, l, w), BF16),
                      jax.ShapeDtypeStruct((b, l, w), F32)]
    return pl.pallas_call(
        functools.partial(_prep_kernel, latent=latent, width=w),
        grid=(b, nt),
        in_specs=in_specs,
        out_specs=out_specs,
        out_shape=out_shape,
        compiler_params=_params("parallel", "parallel"),
        name="wkv_prep_latent" if latent else "wkv_prep_context",
    )(rkv, rkv, rkv, lora, lora, lora, *consts)


def _wkv_kernel(*refs, emit_y, n_pairs):
    names = ("pm", "qm", "rt", "bt", "arb", "kt", "wc", "v")
    n = len(names)
    dir_refs = (dict(zip(names, refs[0:n])), dict(zip(names, refs[n:2 * n])))
    z0_ref = refs[2 * n]
    if emit_y:
        y_refs = refs[2 * n + 1:2 * n + 3]
        z_scr = refs[2 * n + 3]
    else:
        zfin_ref = refs[2 * n + 1]
        z_scr = refs[2 * n + 2]
    c = pl.program_id(0)

    @pl.when(c == 0)
    def _():
        z_scr[...] = z0_ref[...]

    left_h = _iota((CHUNK, LANES), 1) < CHUNK
    keep_l = jnp.where(left_h, 1.0, 0.0).astype(BF16)
    keep_r = jnp.where(left_h, 0.0, 1.0).astype(BF16)

    def unfold(xb):
        return _stack(xb * keep_l, xb * keep_r)

    tiles = [(i, d, p, slice(p * LANES, (p + 1) * LANES))
             for i in range(z_scr.shape[0]) for d in range(2) for p in range(n_pairs)]
    state = [z_scr[i, d, p] for i, d, p, _ in tiles]
    n_sub = dir_refs[0]["wc"].shape[1]
    for step in range(n_sub):
        sub = (step, n_sub - 1 - step)
        rows = [slice(sub[d] * CHUNK, (sub[d] + 1) * CHUNK) for d in range(2)]
        ld = lambda name: [dir_refs[d][name][0, i, rows[d], sl] for i, d, _, sl in tiles]
        v = [dir_refs[d]["v"][i, rows[d], sl] for i, d, _, sl in tiles]
        sbd = [unfold(s.astype(BF16)) for s in state]
        if emit_y:
            ur = [_dot_nt(_stack(pm, rt), s) for pm, rt, s in zip(ld("pm"), ld("rt"), sbd)]
        else:
            ur = [_dot_nt(pm, s) for pm, s in zip(ld("pm"), sbd)]
        u = [x[:CHUNK] + q.astype(F32) for x, q in zip(ur, ld("qm"))]
        ub = [ui.astype(BF16) for ui in u]
        inc = [_dot_tn(_stack(ui, vi), _stack(bt, kt)) for ui, vi, bt, kt in zip(ub, v, ld("bt"), ld("kt"))]
        inc = [jnp.where(left_h, x[:CHUNK], x[CHUNK:]) for x in inc]
        if emit_y:
            yc = [_dot(a, unfold(ui)) for a, ui in zip(ld("arb"), ub)]
            for (i, d, _, sl), x, ys in zip(tiles, ur, yc):
                y_refs[d][i, rows[d], sl] = (x[CHUNK:] + ys).astype(BF16)
        state = [(s + dz) * dir_refs[d]["wc"][i, sub[d], d:d + 1, sl]
                 for (i, d, _, sl), s, dz in zip(tiles, state, inc)]
    for (i, d, p, _), s in zip(tiles, state):
        z_scr[i, d, p] = s

    if not emit_y:
        @pl.when(c == pl.num_programs(0) - 1)
        def _():
            zfin_ref[...] = z_scr[...]


def _wkv_scan(prep, z0, *, emit_y):
    wc = prep[6]
    _, b, l, w = prep[0].shape
    n_sub = min(SCAN_CHUNKS, l // CHUNK)
    blk = n_sub * CHUNK
    nc = l // blk
    n_pairs = w // LANES
    fwd = lambda c: c
    rev = lambda c: nc - 1 - c
    in_specs, args = [], []
    for d, cm in enumerate((fwd, rev)):
        for arr in prep[:6]:
            in_specs.append(pl.BlockSpec((1, b, blk, w), lambda c, d=d, cm=cm: (d, 0, cm(c), 0)))
            args.append(arr)
        in_specs.append(pl.BlockSpec((b, n_sub, 2, w), lambda c, cm=cm: (0, cm(c), 0, 0)))
        args.append(wc)
        in_specs.append(pl.BlockSpec((b, blk, w), lambda c, cm=cm: (0, cm(c), 0)))
        args.append(prep[7])
    zshape = (b, 2, n_pairs, CHUNK, LANES)
    zspec = pl.BlockSpec(zshape, lambda c: (0, 0, 0, 0, 0))
    in_specs.append(zspec)
    args.append(z0)
    if emit_y:
        out_specs = [pl.BlockSpec((b, blk, w), lambda c: (0, c, 0)),
                     pl.BlockSpec((b, blk, w), lambda c: (0, nc - 1 - c, 0))]
        out_shape = [jax.ShapeDtypeStruct((b, l, w), BF16)] * 2
    else:
        out_specs = zspec
        out_shape = jax.ShapeDtypeStruct(zshape, F32)
    return pl.pallas_call(
        functools.partial(_wkv_kernel, emit_y=emit_y, n_pairs=n_pairs),
        grid=(nc,),
        in_specs=in_specs,
        out_specs=out_specs,
        out_shape=out_shape,
        scratch_shapes=[pltpu.VMEM(zshape, F32)],
        compiler_params=_params("arbitrary"),
        name="wkv_scan_latent" if emit_y else "wkv_scan_context",
    )(*args)


def _merge_kernel(x_ref, mod_ref, ya_ref, yf_ref, yr_ref, yh_ref, bonus_ref, g_ref, gate_ref,
                  lnw_ref, lnb_ref, wba_ref, wbr_ref, wo_ref, n2_ref, wu_ref, wd_ref, nf_ref,
                  o_ref, *, ff_chunk):
    x = x_ref[0]
    d = x.shape[1]
    mod = lambda j: mod_ref[0, j:j + 1, :]
    y = yf_ref[0].astype(F32) + yr_ref[0].astype(F32) + yh_ref[0]
    gw = min(MXU_DIM, y.shape[1])
    ones_bd = _head_ones(gw)

    def head_mean(t):
        tb = t.astype(BF16)
        parts = [_dot(tb[:, j:j + gw], ones_bd) for j in range(0, t.shape[1], gw)]
        return jnp.concatenate(parts, axis=1) * (1.0 / HEAD_DIM)

    mu = head_mean(y)
    yc = y - mu
    var = head_mean(yc * yc)
    yn = yc * lax.rsqrt(var + LNX_EPS)
    yr = (yn * lnw_ref[...] + lnb_ref[...] + bonus_ref[0]) * g_ref[0].astype(F32)
    gate = gate_ref[0].astype(F32)
    merged = gate[:, :d] * _dot(ya_ref[0], wba_ref[...]) + gate[:, d:] * _dot(yr.astype(BF16), wbr_ref[...])
    x1 = x + mod(2) * _dot(merged.astype(BF16), wo_ref[...])
    h2 = (_rmsnorm(x1, n2_ref[...]) * (1.0 + mod(4)) + mod(3)).astype(BF16)
    acc = jnp.zeros_like(x1)
    for j in range(wu_ref.shape[1] // ff_chunk):
        cs = slice(j * ff_chunk, (j + 1) * ff_chunk)
        up = jnp.maximum(_dot(h2, wu_ref[:, cs]), 0.0)
        acc = acc + _dot((up * up).astype(BF16), wd_ref[cs, :])
    x2 = x1 + mod(5) * acc
    o_ref[0] = _rmsnorm(x2, nf_ref[...])


def _merge_mlp(x, mod, ya, yf, yr, yh, bonus, g, gate, prm, *, tile):
    b, l, d = x.shape
    tok = lambda arr: pl.BlockSpec((1, tile, arr.shape[2]), lambda i, t: (i, t, 0))
    consts = [prm["lnx_w"], prm["lnx_b"], prm["w_branch_attn"], prm["w_branch_rwkv"], prm["w_out"],
              prm["norm2_g"], prm["w_mlp_up"], prm["w_mlp_down"], prm["norm_f_g"]]
    toks = [ya, yf, yr, yh, bonus, g, gate]
    return pl.pallas_call(
        functools.partial(_merge_kernel, ff_chunk=min(1024, prm["w_mlp_up"].shape[1])),
        grid=(b, l // tile),
        in_specs=[tok(x), pl.BlockSpec((1,) + mod.shape[1:], lambda i, t: (i, 0, 0))]
        + [tok(a) for a in toks] + [_const_spec(c.shape) for c in consts],
        out_specs=tok(x),
        out_shape=jax.ShapeDtypeStruct(x.shape, x.dtype),
        compiler_params=_params("parallel", "parallel"),
        name="merge_mlp",
    )(x, mod, *toks, *consts)


def _rope_tables(l):
    n_freq = HEAD_DIM // 4
    inv_freq = np.power(np.float32(ROPE_BASE), -np.arange(n_freq, dtype=np.float32) / np.float32(n_freq))
    rows = l // GRID_W
    row = np.repeat(np.arange(rows, dtype=np.float32), GRID_W)
    col = np.tile(np.arange(GRID_W, dtype=np.float32), rows)
    ang = np.concatenate([row[:, None] * inv_freq, col[:, None] * inv_freq], axis=-1).astype(np.float32)
    cos, sin = np.cos(ang), np.sin(ang)
    reps = LANES // HEAD_DIM
    return (jnp.asarray(np.tile(np.concatenate([cos, cos], axis=1), (1, reps))),
            jnp.asarray(np.tile(np.concatenate([-sin, sin], axis=1), (1, reps))))


def _pad_cols(w, width):
    return jnp.pad(w, ((0, 0), (0, width - w.shape[1])))


def kernel(x, c, ctx, c_ctx, w_ada, b_ada, norm1_g, w_in, sink, conv_w, decay_w0, decay_w2, iclr_a0, iclr_a2, gate_g2, k_k, k_a, r_k, lnx_w, lnx_b, w_branch_attn, w_branch_rwkv, w_out, norm2_g, w_mlp_up, w_mlp_down, norm_f_g):
    assert w_in.shape[0] == 1, "single-layer block: context tokens are read, never updated"
    b, l, d = x.shape
    attn_w = w_branch_attn.shape[1]
    rw = w_branch_rwkv.shape[1]
    n_q = attn_w // HEAD_DIM
    n_kv = n_q // Q_PER_KV
    kv_w = n_kv * HEAD_DIM
    assert kv_w == LANES and rw % LANES == 0 and ctx.shape[1] % 256 == 0
    assert l % (ATTN_QB * ATTN_BLOCK) == 0 and l % 512 == 0

    w = w_in[0]
    o_k, o_r = attn_w, attn_w + 2 * kv_w
    o_l = o_r + 3 * rw
    o_g = o_l + DECAY_LORA + ICLR_LORA + GATE_LORA
    assert o_l + LORA_PAD <= w.shape[1]
    w_main = w[:, :o_l + LORA_PAD].astype(BF16)
    w_gate = w[:, o_g:].astype(BF16)
    widths = {"q": attn_w, "kv": 2 * kv_w, "k": PAIR * kv_w, "v": PAIR * kv_w, "rkv": 3 * rw,
              "lora": LORA_PAD, "gate": 2 * d}

    cw = conv_w[0]
    used = DECAY_LORA + ICLR_LORA + GATE_LORA
    lora_w = jnp.concatenate([
        jnp.pad(jnp.concatenate([decay_w2[0, 0], decay_w2[0, 1]], axis=1), ((0, 0), (0, 3 * rw))),
        jnp.pad(jnp.concatenate([iclr_a2[0, 0], iclr_a2[0, 1]], axis=1), ((0, 0), (2 * rw, rw))),
        jnp.pad(gate_g2[0], ((0, LORA_PAD - used), (4 * rw, 0)))], axis=0)
    prm = {
        "conv_rkv": cw[:, :3 * rw], "conv_lora": _pad_cols(cw[:, 3 * rw:], LORA_PAD),
        "k_k": k_k[0].reshape(1, rw), "k_a": k_a[0].reshape(1, rw),
        "decay_w0": decay_w0[0], "iclr_a0": iclr_a0[0], "lora_w": lora_w.astype(BF16),
        "r_k": r_k[0].reshape(1, rw),
        "lnx_w": lnx_w[0].reshape(1, rw), "lnx_b": lnx_b[0].reshape(1, rw),
        "w_branch_attn": w_branch_attn[0].astype(BF16), "w_branch_rwkv": w_branch_rwkv[0].astype(BF16),
        "w_out": w_out[0].astype(BF16), "norm2_g": norm2_g[0].reshape(1, d),
        "w_mlp_up": w_mlp_up[0].astype(BF16), "w_mlp_down": w_mlp_down[0].astype(BF16),
        "norm_f_g": norm_f_g.reshape(1, d),
    }

    rows = -(-(b + 1) // 8) * 8
    cc = jnp.concatenate([c, c_ctx[None, :], jnp.zeros((rows - b - 1, d), F32)], axis=0)
    mod = _ada_mod(cc, w_ada[0], b_ada[0]).reshape(rows, -1, d)

    q, kd, vd, rkv, lora, gate = _in_proj(x, mod, b, norm1_g[0], [w_main, w_gate], widths, _rope_tables(l),
                                          latent=True, tile=1024)
    kxd, vxd, rkv_c, lora_c = _in_proj(ctx, mod, b, norm1_g[0], [w_main], widths, None,
                                       latent=False, tile=256)
    ya = _attention(sink[0], q, kd, vd, kxd, vxd)

    prep_c = _wkv_prep(rkv_c, lora_c, prm, latent=False, tile=256)
    z_ctx = _wkv_scan(prep_c, jnp.zeros((b, 2, rw // LANES, CHUNK, LANES), F32), emit_y=False)
    prep = _wkv_prep(rkv, lora, prm, latent=True, tile=512)
    yf, yr = _wkv_scan(prep, z_ctx, emit_y=True)
    bonus, g, yh = prep[8], prep[9], prep[10]

    return _merge_mlp(x, mod, ya, yf, yr, yh, bonus, g, gate, prm, tile=512)
```
